```python
import math
import jax, jax.numpy as jnp
from jax import lax
import numpy as np

D_MODEL = 1024
BATCH = 4
SEQ = 8192
DEPTH = 2

CTX_LEN = 256
GRID_W = 64
EPS = 1e-6
ROPE_THETA = 10000.0
NEG_INF = -1e30
Q_BLOCK = 128

MLA_HEADS = 8
MLA_NOPE = 64
MLA_ROPE = 32
MLA_V = 64
KV_RANK = 256
MLA_SCALE = (MLA_NOPE + MLA_ROPE) ** -0.5
SWA_HEADS = 8
SWA_KV_HEADS = 2
SWA_HEAD_DIM = 64
WINDOW = 128
BLOCK = 128
SWA_SCALE = SWA_HEAD_DIM ** -0.5
HYENA_WIDTH = 512
HYENA_ORDER = 2
HYENA_BANDS = 16
HYENA_EMB = 2 * HYENA_BANDS + 1
HYENA_HIDDEN = 64
HYENA_TARGET = 1e-2
HYENA_FAST_PCT = 0.3
HYENA_SLOW_PCT = 1.5
N_BRANCH = 3
BRANCH_WIDTH = 512
D_FF = 2816

COLS = (
    MLA_HEADS * (MLA_NOPE + MLA_ROPE),
    KV_RANK,
    MLA_ROPE,
    SWA_HEADS * SWA_HEAD_DIM,
    SWA_KV_HEADS * SWA_HEAD_DIM,
    SWA_KV_HEADS * SWA_HEAD_DIM,
    (HYENA_ORDER + 1) * HYENA_WIDTH,
    N_BRANCH * D_MODEL,
)
IN_WIDTH = sum(COLS)

kernel_name = "hybrid_mla_swa_hyena_dit_block"


def split_cols(p):
    outs, start = [], 0
    for w in COLS:
        outs.append(p[..., start:start + w])
        start += w
    return outs


def rms_norm(x, g):
    xf = x.astype(jnp.float32)
    y = xf * lax.rsqrt(jnp.mean(jnp.square(xf), axis=-1, keepdims=True) + EPS)
    return (y * g.astype(jnp.float32)).astype(x.dtype)


def modulate(x, g, shift, scale):
    return rms_norm(x, g) * (1 + scale) + shift


def axial_rope_tables(rows, rot_dim):
    row = jnp.repeat(jnp.arange(rows, dtype=jnp.float32), GRID_W)
    col = jnp.tile(jnp.arange(GRID_W, dtype=jnp.float32), rows)
    n_freq = rot_dim // 4
    inv_freq = ROPE_THETA ** (-jnp.arange(n_freq, dtype=jnp.float32) / n_freq)
    ang = jnp.concatenate([row[:, None] * inv_freq, col[:, None] * inv_freq], axis=-1)
    return jnp.cos(ang), jnp.sin(ang)


def apply_rope(x, cos, sin):
    x1, x2 = jnp.split(x, 2, axis=-1)
    return jnp.concatenate([x1 * cos - x2 * sin, x2 * cos + x1 * sin], axis=-1).astype(x.dtype)


def dwconv3(x, w, b):
    xp = jnp.pad(x, ((0, 0), (1, 1), (0, 0)))
    return xp[:, :-2] * w[0] + xp[:, 1:-1] * w[1] + xp[:, 2:] * w[2] + b


def sweep_query_blocks(fn, *qs):
    B, S = qs[0].shape[:2]
    nb = S // Q_BLOCK
    blocks = tuple(jnp.moveaxis(t.reshape(B, nb, Q_BLOCK, *t.shape[2:]), 1, 0) for t in qs)
    out = lax.map(lambda args: fn(*args), blocks)
    return jnp.moveaxis(out, 0, 1).reshape(B, S, -1)


def mla_kv(ckv, kv_norm_g, w_kv_up):
    B, L, _ = ckv.shape
    kv = (rms_norm(ckv, kv_norm_g) @ w_kv_up).reshape(B, L, MLA_HEADS, MLA_NOPE + MLA_V)
    return kv[..., :MLA_NOPE], kv[..., MLA_NOPE:]


def mla_attend(q_nope, q_rope, k_nope, k_rope, v):
    s = (jnp.einsum('bqhd,bkhd->bhqk', q_nope, k_nope, preferred_element_type=jnp.float32)
         + jnp.einsum('bqhr,bkr->bhqk', q_rope, k_rope, preferred_element_type=jnp.float32))
    p = jax.nn.softmax(s * MLA_SCALE, axis=-1).astype(v.dtype)
    return jnp.einsum('bhqk,bkhd->bqhd', p, v)


def swa_latent_attend(q, k, v, k_ctx, v_ctx, sink):
    B, S = q.shape[:2]
    C = k_ctx.shape[1]
    nb = S // BLOCK
    grp = SWA_HEADS // SWA_KV_HEADS
    qb = q.reshape(B, nb, BLOCK, SWA_KV_HEADS, grp, SWA_HEAD_DIM)

    def band(t):
        tp = jnp.pad(t, ((0, 0), (BLOCK, BLOCK), (0, 0), (0, 0))).reshape(B, nb + 2, BLOCK, SWA_KV_HEADS, SWA_HEAD_DIM)
        return jnp.concatenate([tp[:, :-2], tp[:, 1:-1], tp[:, 2:]], axis=2)

    kw, vw = band(k), band(v)
    s_loc = jnp.einsum('bnqhgd,bnkhd->bnhgqk', qb, kw, preferred_element_type=jnp.float32)
    s_ctx = jnp.einsum('bnqhgd,bchd->bnhgqc', qb, k_ctx, preferred_element_type=jnp.float32)
    q_pos = jnp.arange(nb)[:, None, None] * BLOCK + jnp.arange(BLOCK)[None, :, None]
    k_pos = (jnp.arange(nb)[:, None, None] - 1) * BLOCK + jnp.arange(3 * BLOCK)[None, None, :]
    valid = (jnp.abs(q_pos - k_pos) <= WINDOW) & (k_pos >= 0) & (k_pos < S)
    s_loc = jnp.where(valid[None, :, None, None], s_loc * SWA_SCALE, NEG_INF)
    sink_l = jnp.broadcast_to(sink.astype(jnp.float32).reshape(1, 1, SWA_KV_HEADS, grp, 1, 1), s_ctx.shape[:-1] + (1,))
    p = jax.nn.softmax(jnp.concatenate([s_ctx * SWA_SCALE, s_loc, sink_l], axis=-1), axis=-1).astype(v.dtype)
    out = (jnp.einsum('bnhgqc,bchd->bnqhgd', p[..., :C], v_ctx)
           + jnp.einsum('bnhgqk,bnkhd->bnqhgd', p[..., C:C + 3 * BLOCK], vw))
    return out.reshape(B, S, SWA_HEADS * SWA_HEAD_DIM)


def swa_context_attend(q, k, v, sink):
    B, C = q.shape[:2]
    grp = SWA_HEADS // SWA_KV_HEADS
    qg = q.reshape(B, C, SWA_KV_HEADS, grp, SWA_HEAD_DIM)
    s = jnp.einsum('bqhgd,bkhd->bhgqk', qg, k, preferred_element_type=jnp.float32) * SWA_SCALE
    sink_l = jnp.broadcast_to(sink.astype(jnp.float32).reshape(1, SWA_KV_HEADS, grp, 1, 1), s.shape[:-1] + (1,))
    p = jax.nn.softmax(jnp.concatenate([s, sink_l], axis=-1), axis=-1)[..., :-1].astype(v.dtype)
    return jnp.einsum('bhgqk,bkhd->bqhgd', p, v).reshape(B, C, SWA_HEADS * SWA_HEAD_DIM)


def hyena_filters(n_tokens, w1, b1, w2, b2, freq, w3):
    L = n_tokens
    t_idx = jnp.arange(L, dtype=jnp.float32)
    t = t_idx / max(L - 1, 1)
    bands = jnp.linspace(1e-4, HYENA_BANDS - 1, HYENA_BANDS, dtype=jnp.float32)
    ang = (2.0 * math.pi / L) * t_idx[:, None] * bands
    feats = jnp.concatenate([t[:, None], jnp.cos(ang), -jnp.sin(ang)], axis=-1)
    hid = jnp.sin(freq[0] * (feats @ w1 + b1))
    hid = jnp.sin(freq[1] * (hid @ w2 + b2))
    h = (hid @ w3).astype(jnp.float32).reshape(L, 2, HYENA_ORDER, HYENA_WIDTH)
    deltas = jnp.abs(jnp.linspace(math.log(HYENA_TARGET) / HYENA_SLOW_PCT, math.log(HYENA_TARGET) / HYENA_FAST_PCT,
                                  HYENA_WIDTH, dtype=jnp.float32))
    h = h * jnp.exp(-t[:, None] * deltas)[:, None, None, :]
    kern = jnp.concatenate([h[:, 0], jnp.zeros((1, HYENA_ORDER, HYENA_WIDTH), jnp.float32), h[:0:-1, 1]], axis=0)
    kern = kern / jnp.sum(jnp.abs(kern), axis=0, keepdims=True)
    return jnp.fft.rfft(kern, axis=0)


def hyena_branch(proj, conv_w, conv_b, filt_f, skip):
    L = proj.shape[1]
    u = dwconv3(proj, conv_w, conv_b)
    z, *gates = jnp.split(u, HYENA_ORDER + 1, axis=-1)
    z = z.astype(jnp.float32)
    for o in range(HYENA_ORDER):
        conv = jnp.fft.irfft(jnp.fft.rfft(z, n=2 * L, axis=1) * filt_f[:, o], n=2 * L, axis=1)[:, :L]
        z = gates[o] * (conv + z * skip[o])
    return z.astype(proj.dtype)


def merge_branches(y_a, y_b, y_c, gate_logits, w_branch, w_out):
    g_a, g_b, g_c = jnp.split(jax.nn.sigmoid(gate_logits), N_BRANCH, axis=-1)
    merged = g_a * (y_a @ w_branch[0]) + g_b * (y_b @ w_branch[1]) + g_c * (y_c @ w_branch[2])
    return merged @ w_out


def token_mixer(h, hc, with_ctx, rope_mla, rope_swa, w_in, kv_norm_g, w_kv_up, swa_sink,
                hy_conv_w, hy_conv_b, hy_w1, hy_b1, hy_w2, hy_b2, hy_freq, hy_w3, hy_skip, w_branch, w_out):
    B, S, _ = h.shape
    C = hc.shape[1]
    mq, mckv, mkr, sq, sk, sv, hy, gt = split_cols(h @ w_in)
    mq_c, mckv_c, mkr_c, sq_c, sk_c, sv_c, hy_c, gt_c = split_cols(hc @ w_in)
    cos_m, sin_m = rope_mla
    cos_s, sin_s = rope_swa

    q = mq.reshape(B, S, MLA_HEADS, MLA_NOPE + MLA_ROPE)
    q_nope = q[..., :MLA_NOPE]
    q_rope = apply_rope(q[..., MLA_NOPE:], cos_m[:, None], sin_m[:, None])
    k_nope, v = mla_kv(mckv, kv_norm_g, w_kv_up)
    kc_nope, vc = mla_kv(mckv_c, kv_norm_g, w_kv_up)
    keys_nope = jnp.concatenate([kc_nope, k_nope], axis=1)
    keys_rope = jnp.concatenate([mkr_c, apply_rope(mkr, cos_m, sin_m)], axis=1)
    vals = jnp.concatenate([vc, v], axis=1)
    y_a = sweep_query_blocks(lambda qn, qr: mla_attend(qn, qr, keys_nope, keys_rope, vals), q_nope, q_rope)

    q_s = apply_rope(sq.reshape(B, S, SWA_HEADS, SWA_HEAD_DIM), cos_s[:, None], sin_s[:, None])
    k_s = apply_rope(sk.reshape(B, S, SWA_KV_HEADS, SWA_HEAD_DIM), cos_s[:, None], sin_s[:, None])
    v_s = sv.reshape(B, S, SWA_KV_HEADS, SWA_HEAD_DIM)
    k_sc = sk_c.reshape(B, C, SWA_KV_HEADS, SWA_HEAD_DIM)
    v_sc = sv_c.reshape(B, C, SWA_KV_HEADS, SWA_HEAD_DIM)
    y_b = swa_latent_attend(q_s, k_s, v_s, k_sc, v_sc, swa_sink)

    y_c = hyena_branch(hy, hy_conv_w, hy_conv_b, hyena_filters(S, hy_w1, hy_b1, hy_w2, hy_b2, hy_freq, hy_w3), hy_skip)

    y = merge_branches(y_a, y_b, y_c, gt, w_branch, w_out)
    if not with_ctx:
        return y, None

    q_c = mq_c.reshape(B, C, MLA_HEADS, MLA_NOPE + MLA_ROPE)
    yc_a = mla_attend(q_c[..., :MLA_NOPE], q_c[..., MLA_NOPE:], kc_nope, mkr_c, vc).reshape(B, C, -1)
    yc_b = swa_context_attend(sq_c.reshape(B, C, SWA_HEADS, SWA_HEAD_DIM), k_sc, v_sc, swa_sink)
    yc_c = hyena_branch(hy_c, hy_conv_w, hy_conv_b, hyena_filters(C, hy_w1, hy_b1, hy_w2, hy_b2, hy_freq, hy_w3), hy_skip)
    y_ctx = merge_branches(yc_a, yc_b, yc_c, gt_c, w_branch, w_out)
    return y, y_ctx


def conv_ffn(h, w_up, conv_w, conv_b, w_down):
    u = dwconv3(h @ w_up, conv_w, conv_b)
    a, b = jnp.split(u, 2, axis=-1)
    return (jax.nn.silu(a) * b) @ w_down


def setup_inputs(seed: int = 0) -> dict:
    key = jax.random.key(seed)
    ks = iter(jax.random.split(key, 32))

    def nrm(shape, scale):
        return jax.random.normal(next(ks), shape, jnp.float32) * scale

    D = D_MODEL
    return {
        "x": nrm((BATCH, SEQ, D), 1.0),
        "c": nrm((BATCH, D), 1.0),
        "ctx": nrm((BATCH, CTX_LEN, D), 1.0),
        "c_ctx": nrm((D,), 1.0),
        "w_mod": nrm((DEPTH, D, 6 * D), D ** -0.5),
        "b_mod": nrm((DEPTH, 6 * D), 0.02),
        "norm_g": 1.0 + nrm((DEPTH, 4, D), 0.05),
        "w_in": nrm((DEPTH, D, IN_WIDTH), D ** -0.5),
        "kv_norm_g": 1.0 + nrm((DEPTH, KV_RANK), 0.05),
        "w_kv_up": nrm((DEPTH, KV_RANK, MLA_HEADS * (MLA_NOPE + MLA_V)), KV_RANK ** -0.5),
        "swa_sink": nrm((DEPTH, SWA_HEADS), 0.5),
        "hy_conv_w": nrm((DEPTH, 3, (HYENA_ORDER + 1) * HYENA_WIDTH), 3 ** -0.5),
        "hy_conv_b": nrm((DEPTH, (HYENA_ORDER + 1) * HYENA_WIDTH), 0.02),
        "hy_w1": nrm((DEPTH, HYENA_EMB, HYENA_HIDDEN), HYENA_EMB ** -0.5),
        "hy_b1": nrm((DEPTH, HYENA_HIDDEN), 0.02),
        "hy_w2": nrm((DEPTH, HYENA_HIDDEN, HYENA_HIDDEN), HYENA_HIDDEN ** -0.5),
        "hy_b2": nrm((DEPTH, HYENA_HIDDEN), 0.02),
        "hy_freq": 1.0 + nrm((DEPTH, 2, HYENA_HIDDEN), 0.05),
        "hy_w3": nrm((DEPTH, HYENA_HIDDEN, 2 * HYENA_ORDER * HYENA_WIDTH), HYENA_HIDDEN ** -0.5),
        "hy_skip": nrm((DEPTH, HYENA_ORDER, HYENA_WIDTH), 0.5),
        "w_branch": nrm((DEPTH, N_BRANCH, BRANCH_WIDTH, D), BRANCH_WIDTH ** -0.5),
        "w_out": nrm((DEPTH, D, D), D ** -0.5),
        "w_up": nrm((DEPTH, D, 2 * D_FF), D ** -0.5),
        "ffn_conv_w": nrm((DEPTH, 3, 2 * D_FF), 3 ** -0.5),
        "ffn_conv_b": nrm((DEPTH, 2 * D_FF), 0.02),
        "w_down": nrm((DEPTH, D_FF, D), D_FF ** -0.5),
    }


def reference(x, c, ctx, c_ctx, w_mod, b_mod, norm_g, w_in, kv_norm_g, w_kv_up, swa_sink,
              hy_conv_w, hy_conv_b, hy_w1, hy_b1, hy_w2, hy_b2, hy_freq, hy_w3, hy_skip,
              w_branch, w_out, w_up, ffn_conv_w, ffn_conv_b, w_down):
    S = x.shape[1]
    rows = S // GRID_W
    rope_mla = axial_rope_tables(rows, MLA_ROPE)
    rope_swa = axial_rope_tables(rows, SWA_HEAD_DIM)
    x_lat, x_ctx = x, ctx
    for l in range(DEPTH):
        with_ctx = l < DEPTH - 1
        mod = (jax.nn.silu(c) @ w_mod[l] + b_mod[l])[:, None, :]
        mod_c = (jax.nn.silu(c_ctx) @ w_mod[l] + b_mod[l])[None, None, :]
        sh1, sc1, g1, sh2, sc2, g2 = jnp.split(mod, 6, axis=-1)
        csh1, csc1, cg1, csh2, csc2, cg2 = jnp.split(mod_c, 6, axis=-1)

        y, y_ctx = token_mixer(
            modulate(x_lat, norm_g[l, 0], sh1, sc1), modulate(x_ctx, norm_g[l, 0], csh1, csc1), with_ctx,
            rope_mla, rope_swa, w_in[l], kv_norm_g[l], w_kv_up[l], swa_sink[l],
            hy_conv_w[l], hy_conv_b[l], hy_w1[l], hy_b1[l], hy_w2[l], hy_b2[l], hy_freq[l], hy_w3[l], hy_skip[l],
            w_branch[l], w_out[l])
        x_lat = x_lat + g1 * rms_norm(y, norm_g[l, 1])

        f = conv_ffn(modulate(x_lat, norm_g[l, 2], sh2, sc2), w_up[l], ffn_conv_w[l], ffn_conv_b[l], w_down[l])
        x_lat = x_lat + g2 * rms_norm(f, norm_g[l, 3])

        if with_ctx:
            x_ctx = x_ctx + cg1 * rms_norm(y_ctx, norm_g[l, 1])
            fc = conv_ffn(modulate(x_ctx, norm_g[l, 2], csh2, csc2), w_up[l], ffn_conv_w[l], ffn_conv_b[l], w_down[l])
            x_ctx = x_ctx + cg2 * rms_norm(fc, norm_g[l, 3])
    return x_lat
```

```python
import functools
import math

import numpy as np
import jax
import jax.numpy as jnp
from jax import lax
from jax.experimental import pallas as pl
from jax.experimental.pallas import tpu as pltpu

F32 = jnp.float32
BF16 = jnp.bfloat16

GRID_W = 64
EPS = 1e-6
ROPE_THETA = 10000.0
NEG_INF = -1e30
MLA_HEADS = 8
MLA_NOPE = 64
MLA_ROPE = 32
MLA_V = 64
KV_RANK = 256
MLA_SCALE = (MLA_NOPE + MLA_ROPE) ** -0.5
SWA_HEADS = 8
SWA_KV_HEADS = 2
SWA_HEAD_DIM = 64
SWA_BLOCK = 128
SWA_SCALE = SWA_HEAD_DIM ** -0.5
HY_WIDTH = 512
HY_ORDER = 2
HY_BANDS = 16
HY_HIDDEN = 64
HY_TARGET = 1e-2
HY_FAST_PCT = 0.3
HY_SLOW_PCT = 1.5
N_BRANCH = 3
BRANCH_WIDTH = 512
LANE = 128
HEAD_PAD = 128

VMEM_LIMIT = 56 * 1024 * 1024

C_Q = 0
C_CKV = C_Q + MLA_HEADS * HEAD_PAD
C_KR = C_CKV + KV_RANK
C_SQ = C_KR + HEAD_PAD
C_SK = C_SQ + SWA_HEADS * SWA_HEAD_DIM
C_SV = C_SK + 2 * SWA_KV_HEADS * SWA_HEAD_DIM
C_HY = C_SV + 2 * SWA_KV_HEADS * SWA_HEAD_DIM
C_GT = C_HY + (HY_ORDER + 1) * HY_WIDTH


def _cparams(sem):
    return pltpu.CompilerParams(dimension_semantics=sem, vmem_limit_bytes=VMEM_LIMIT)


def _rms(xf, g):
    return xf * lax.rsqrt(jnp.mean(xf * xf, axis=-1, keepdims=True) + EPS) * g


def _rope(x, cos, sa, sb, half):
    return x * cos + pltpu.roll(x, LANE - half, 1) * sa + pltpu.roll(x, half, 1) * sb


def _dwconv3(p, prev_row, next_row, cw_ref, cb_ref):
    tm = p.shape[0]
    rows = lax.broadcasted_iota(jnp.int32, (tm, 1), 0)
    up = jnp.where(rows == 0, prev_row, pltpu.roll(p, 1, 0))
    dn = jnp.where(rows == tm - 1, next_row, pltpu.roll(p, tm - 1, 0))
    return up * cw_ref[0:1, :] + p * cw_ref[1:2, :] + dn * cw_ref[2:3, :] + cb_ref[...]


def _mod_kernel(c_ref, w_ref, b_ref, o_ref):
    c = c_ref[...]
    a = c * jax.nn.sigmoid(c)
    a_hi = a.astype(BF16)
    a_lo = (a - a_hi.astype(F32)).astype(BF16)
    w = w_ref[0]
    w_hi = w.astype(BF16)
    w_lo = (w - w_hi.astype(F32)).astype(BF16)
    acc = jnp.dot(a_hi, w_hi, preferred_element_type=F32)
    acc += jnp.dot(a_lo, w_hi, preferred_element_type=F32)
    acc += jnp.dot(a_hi, w_lo, preferred_element_type=F32)
    o_ref[0] = acc + b_ref[0]


def _modulation(cvec, w_mod, b_mod):
    depth, d, n = w_mod.shape
    tn = 1536
    return pl.pallas_call(
        _mod_kernel,
        grid=(depth, n // tn),
        in_specs=[pl.BlockSpec((8, d), lambda l, j: (0, 0)),
                  pl.BlockSpec((1, d, tn), lambda l, j: (l, 0, j)),
                  pl.BlockSpec((1, 1, tn), lambda l, j: (l, 0, j))],
        out_specs=pl.BlockSpec((1, 8, tn), lambda l, j: (l, 0, j)),
        out_shape=jax.ShapeDtypeStruct((depth, 8, n), F32),
        compiler_params=_cparams(("arbitrary", "arbitrary")),
        name="adaln_mod",
    )(cvec, w_mod, b_mod.reshape(depth, 1, n))


def _in_proj_kernel(x_ref, xp_ref, xn_ref, sh_ref, sc_ref, g_ref, w_ref, wkv_ref, kvg_ref,
                    cm_ref, sma_ref, smb_ref, cs_ref, ssa_ref, ssb_ref, cw_ref, cb_ref,
                    q_ref, kk_ref, vv_ref, sq_ref, sk_ref, sv_ref, z_ref, x1_ref, x2_ref, gt_ref):
    i = pl.program_id(1)
    nt = pl.num_programs(1)
    g = g_ref[...]
    sh = sh_ref[0]
    sc = sc_ref[0]

    def mod(xf):
        return (_rms(xf, g) * (1.0 + sc) + sh).astype(BF16)

    hb = mod(x_ref[0])

    def proj(lo, hi, lhs=hb):
        return jnp.dot(lhs, w_ref[:, lo:hi], preferred_element_type=F32)

    cm, sma, smb = cm_ref[...], sma_ref[...], smb_ref[...]
    cs, ssa, ssb = cs_ref[...], ssa_ref[...], ssb_ref[...]

    pq = proj(C_Q, C_CKV)
    for h in range(MLA_HEADS):
        xh = pq[:, HEAD_PAD * h:HEAD_PAD * (h + 1)]
        q_ref[0, :, HEAD_PAD * h:HEAD_PAD * (h + 1)] = (
            _rope(xh, cm, sma, smb, MLA_ROPE // 2) * MLA_SCALE).astype(BF16)

    ckv = proj(C_CKV, C_KR)
    cn = _rms(ckv, kvg_ref[...]).astype(BF16)
    kv = jnp.dot(cn, wkv_ref[...], preferred_element_type=F32)
    krr = _rope(proj(C_KR, C_SQ), cm, sma, smb, MLA_ROPE // 2)
    for h in range(MLA_HEADS):
        kk_ref[0, :, HEAD_PAD * h:HEAD_PAD * (h + 1)] = (
            kv[:, HEAD_PAD * h:HEAD_PAD * (h + 1)] + krr).astype(BF16)
    vv_ref[0] = kv[:, MLA_HEADS * HEAD_PAD:].astype(BF16)

    psq = proj(C_SQ, C_SK)
    for j in range(SWA_HEADS // 2):
        xh = psq[:, LANE * j:LANE * (j + 1)]
        sq_ref[0, :, LANE * j:LANE * (j + 1)] = (
            _rope(xh, cs, ssa, ssb, SWA_HEAD_DIM // 2) * SWA_SCALE).astype(BF16)
    psk = proj(C_SK, C_SV)
    lane = lax.broadcasted_iota(jnp.int32, (hb.shape[0], LANE), 1)
    for k in range(SWA_KV_HEADS):
        rk = _rope(psk[:, LANE * k:LANE * (k + 1)], cs, ssa, ssb, SWA_HEAD_DIM // 2)
        sk_ref[0, :, 2 * LANE * k:2 * LANE * k + LANE] = jnp.where(lane < 64, rk, 0.0).astype(BF16)
        sk_ref[0, :, 2 * LANE * k + LANE:2 * LANE * (k + 1)] = jnp.where(lane >= 64, rk, 0.0).astype(BF16)
    sv_ref[0] = proj(C_SV, C_HY).astype(BF16)

    ph = proj(C_HY, C_GT)
    pprev = proj(C_HY, C_GT, mod(xp_ref[0]))[7:8, :]
    pnext = proj(C_HY, C_GT, mod(xn_ref[0]))[0:1, :]
    pprev = jnp.where(i > 0, pprev, 0.0)
    pnext = jnp.where(i < nt - 1, pnext, 0.0)
    u = _dwconv3(ph, pprev, pnext, cw_ref, cb_ref)
    z_ref[0] = u[:, :HY_WIDTH]
    x1_ref[0] = u[:, HY_WIDTH:2 * HY_WIDTH]
    x2_ref[0] = u[:, 2 * HY_WIDTH:]

    gt_ref[0] = jax.nn.sigmoid(proj(C_GT, w_ref.shape[1])).astype(BF16)


def _in_proj(x, sh, sc, g, w, wkv, kvg, rope_m, rope_s, cw, cb):
    b, s, d = x.shape
    tm = min(256, s)
    nt = s // tm
    nw = w.shape[1]
    r8 = tm // 8
    row = lambda width: pl.BlockSpec((1, tm, width), lambda bi, i: (bi, i, 0))
    const2 = lambda a: pl.BlockSpec(a.shape, lambda bi, i: (0, 0))
    tab = pl.BlockSpec((tm, LANE), lambda bi, i: (i, 0))
    in_specs = [
        row(d),
        pl.BlockSpec((1, 8, d), lambda bi, i: (bi, jnp.maximum(i * r8 - 1, 0), 0)),
        pl.BlockSpec((1, 8, d), lambda bi, i: (bi, jnp.minimum((i + 1) * r8, s // 8 - 1), 0)),
        pl.BlockSpec((1, 1, d), lambda bi, i: (bi, 0, 0)),
        pl.BlockSpec((1, 1, d), lambda bi, i: (bi, 0, 0)),
        const2(g), const2(w), const2(wkv), const2(kvg),
        tab, tab, tab, tab, tab, tab,
        const2(cw), const2(cb),
    ]
    widths = [(MLA_HEADS * HEAD_PAD, BF16), (MLA_HEADS * HEAD_PAD, BF16), (MLA_HEADS * MLA_V, BF16),
              (SWA_HEADS * SWA_HEAD_DIM, BF16), (4 * LANE, BF16), (2 * LANE, BF16),
              (HY_WIDTH, F32), (HY_WIDTH, F32), (HY_WIDTH, F32), (N_BRANCH * d, BF16)]
    return pl.pallas_call(
        _in_proj_kernel,
        grid=(b, nt),
        in_specs=in_specs,
        out_specs=[row(wd) for wd, _ in widths],
        out_shape=[jax.ShapeDtypeStruct((b, s, wd), dt) for wd, dt in widths],
        compiler_params=_cparams(("arbitrary", "arbitrary")),
        name="in_proj",
    )(x, x, x, sh, sc, g, w, wkv, kvg, *rope_m, *rope_s, cw, cb)


def _mla_kernel(*refs, tk, n_lat):
    if n_lat:
        q_ref, kc_ref, vc_ref, kl_ref, vl_ref, o_ref = refs
    else:
        q_ref, kc_ref, vc_ref, o_ref = refs
    tq = q_ref.shape[1]
    lane = lax.broadcasted_iota(jnp.int32, (tq, LANE), 1)
    outs = []
    for e in range(2):
        q = q_ref[0, :, HEAD_PAD * e:HEAD_PAD * (e + 1)]

        def step(k, v, carry):
            m, l, acc = carry
            s = lax.dot_general(q, k, (((1,), (1,)), ((), ())), preferred_element_type=F32)
            m_new = jnp.maximum(m, jnp.max(s, axis=1, keepdims=True))
            alpha = jnp.exp(m - m_new)
            p = jnp.exp(s - m_new)
            l = alpha * l + jnp.sum(p, axis=1, keepdims=True)
            acc = alpha * acc + jnp.dot(p.astype(BF16), v, preferred_element_type=F32)
            return m_new, l, acc

        carry = (jnp.full((tq, 1), NEG_INF, F32), jnp.zeros((tq, 1), F32), jnp.zeros((tq, LANE), F32))
        carry = step(kc_ref[0, :, HEAD_PAD * e:HEAD_PAD * (e + 1)], vc_ref[0], carry)
        if n_lat:
            def body(j, carry):
                off = pl.multiple_of(j * tk, tk)
                return step(kl_ref[0, pl.ds(off, tk), HEAD_PAD * e:HEAD_PAD * (e + 1)],
                            vl_ref[0, pl.ds(off, tk), :], carry)
            carry = lax.fori_loop(0, n_lat, body, carry)
        _, l, acc = carry
        outs.append(acc / l)
    o_ref[0] = jnp.where(lane < MLA_V, outs[0], outs[1]).astype(o_ref.dtype)


def _mla(q, kc, vc, kl=None, vl=None):
    b, sq, _ = q.shape
    c = kc.shape[1]
    tq = min(256, sq)
    tk = 512
    hp = MLA_HEADS // 2
    in_specs = [pl.BlockSpec((1, tq, 2 * HEAD_PAD), lambda bi, h, i: (bi, i, h)),
                pl.BlockSpec((1, c, 2 * HEAD_PAD), lambda bi, h, i: (bi, 0, h)),
                pl.BlockSpec((1, c, LANE), lambda bi, h, i: (bi, 0, h))]
    args = [q, kc, vc]
    n_lat = 0
    if kl is not None:
        s = kl.shape[1]
        tk = min(tk, s)
        n_lat = s // tk
        in_specs += [pl.BlockSpec((1, s, 2 * HEAD_PAD), lambda bi, h, i: (bi, 0, h)),
                     pl.BlockSpec((1, s, LANE), lambda bi, h, i: (bi, 0, h))]
        args += [kl, vl]
    return pl.pallas_call(
        functools.partial(_mla_kernel, tk=tk, n_lat=n_lat),
        grid=(b, hp, sq // tq),
        in_specs=in_specs,
        out_specs=pl.BlockSpec((1, tq, LANE), lambda bi, h, i: (bi, i, h)),
        out_shape=jax.ShapeDtypeStruct((b, sq, MLA_HEADS * MLA_V), BF16),
        compiler_params=_cparams(("arbitrary", "arbitrary", "arbitrary")),
        name="mla_attn",
    )(*args)


def _swa_kernel(*refs, band):
    if band:
        sink_ref, q_ref, kc_ref, vc_ref, kp_ref, k0_ref, kn_ref, vp_ref, v0_ref, vn_ref, o_ref = refs
    else:
        sink_ref, q_ref, kc_ref, vc_ref, o_ref = refs
    i = pl.program_id(1)
    nb = pl.num_programs(1)
    tq = q_ref.shape[1]
    lane = lax.broadcasted_iota(jnp.int32, (tq, LANE), 1)
    if band:
        r = lax.broadcasted_iota(jnp.int32, (tq, SWA_BLOCK), 0)
        c = lax.broadcasted_iota(jnp.int32, (tq, SWA_BLOCK), 1)
        ok_prev = (r <= c) & (i > 0)
        ok_next = (c <= r) & (i < nb - 1)
    grp = SWA_HEADS // SWA_KV_HEADS
    nt = (((1,), (1,)), ((), ()))
    for j in range(SWA_HEADS // 2):
        qp = q_ref[0, :, LANE * j:LANE * (j + 1)]
        kvh = (2 * j) // grp
        vsl = slice(LANE * kvh, LANE * (kvh + 1))
        outs = []
        for e in range(2):
            ksl = slice(LANE * (2 * kvh + e), LANE * (2 * kvh + e + 1))
            parts = [lax.dot_general(qp, kc_ref[0, :, ksl], nt, preferred_element_type=F32)]
            vals = [vc_ref[0, :, vsl]]
            if band:
                sp = lax.dot_general(qp, kp_ref[0, :, ksl], nt, preferred_element_type=F32)
                s0 = lax.dot_general(qp, k0_ref[0, :, ksl], nt, preferred_element_type=F32)
                sn = lax.dot_general(qp, kn_ref[0, :, ksl], nt, preferred_element_type=F32)
                parts += [jnp.where(ok_prev, sp, NEG_INF), s0, jnp.where(ok_next, sn, NEG_INF)]
                vals += [vp_ref[0, :, vsl], v0_ref[0, :, vsl], vn_ref[0, :, vsl]]
            s = jnp.concatenate(parts, axis=1)
            v = jnp.concatenate(vals, axis=0)
            sink = sink_ref[2 * j + e]
            m = jnp.maximum(jnp.max(s, axis=1, keepdims=True), sink)
            p = jnp.exp(s - m)
            den = jnp.sum(p, axis=1, keepdims=True) + jnp.exp(sink - m)
            outs.append(jnp.dot(p.astype(BF16), v, preferred_element_type=F32) / den)
        o_ref[0, :, LANE * j:LANE * (j + 1)] = jnp.where(lane < 64, outs[0], outs[1]).astype(o_ref.dtype)


def _swa(sink, q, kc, vc, k=None, v=None):
    b, sq, _ = q.shape
    c = kc.shape[1]
    tq = SWA_BLOCK
    nb = sq // tq
    band = k is not None
    blk = lambda width, f: pl.BlockSpec((1, tq, width), f)
    in_specs = [pl.BlockSpec(memory_space=pltpu.SMEM),
                blk(SWA_HEADS * SWA_HEAD_DIM, lambda bi, i: (bi, i, 0)),
                pl.BlockSpec((1, c, 4 * LANE), lambda bi, i: (bi, 0, 0)),
                pl.BlockSpec((1, c, 2 * LANE), lambda bi, i: (bi, 0, 0))]
    args = [sink, q, kc, vc]
    if band:
        prev = lambda bi, i: (bi, jnp.maximum(i - 1, 0), 0)
        cur = lambda bi, i: (bi, i, 0)
        nxt = lambda bi, i: (bi, jnp.minimum(i + 1, nb - 1), 0)
        in_specs += [blk(4 * LANE, prev), blk(4 * LANE, cur), blk(4 * LANE, nxt),
                     blk(2 * LANE, prev), blk(2 * LANE, cur), blk(2 * LANE, nxt)]
        args += [k, k, k, v, v, v]
    return pl.pallas_call(
        functools.partial(_swa_kernel, band=band),
        grid=(b, nb),
        in_specs=in_specs,
        out_specs=blk(SWA_HEADS * SWA_HEAD_DIM, lambda bi, i: (bi, i, 0)),
        out_shape=jax.ShapeDtypeStruct((b, sq, SWA_HEADS * SWA_HEAD_DIM), BF16),
        compiler_params=_cparams(("arbitrary", "arbitrary")),
        name="swa_attn",
    )(*args)


def _merge_kernel(x_ref, ya_ref, yb_ref, yc_ref, gt_ref, wb_ref, wo_ref, ng_ref, g1_ref, o_ref):
    d = x_ref.shape[2]
    merged = None
    for k, y_ref in enumerate((ya_ref, yb_ref, yc_ref)):
        t = jnp.dot(y_ref[0].astype(BF16), wb_ref[k], preferred_element_type=F32)
        t = gt_ref[0, :, d * k:d * (k + 1)].astype(F32) * t
        merged = t if merged is None else merged + t
    y = jnp.dot(merged.astype(BF16), wo_ref[...], preferred_element_type=F32)
    o_ref[0] = x_ref[0] + g1_ref[0] * _rms(y, ng_ref[...])


def _merge(x, ya, yb, yc, gt, wb, wo, ng, g1):
    b, s, d = x.shape
    tm = min(512, s)
    row = lambda width: pl.BlockSpec((1, tm, width), lambda bi, i: (bi, i, 0))
    return pl.pallas_call(
        _merge_kernel,
        grid=(b, s // tm),
        in_specs=[row(d), row(BRANCH_WIDTH), row(BRANCH_WIDTH), row(BRANCH_WIDTH), row(N_BRANCH * d),
                  pl.BlockSpec(wb.shape, lambda bi, i: (0, 0, 0)),
                  pl.BlockSpec(wo.shape, lambda bi, i: (0, 0)),
                  pl.BlockSpec(ng.shape, lambda bi, i: (0, 0)),
                  pl.BlockSpec((1, 1, d), lambda bi, i: (bi, 0, 0))],
        out_specs=row(d),
        out_shape=jax.ShapeDtypeStruct((b, s, d), F32),
        compiler_params=_cparams(("arbitrary", "arbitrary")),
        name="merge",
    )(x, ya, yb, yc, gt, wb, wo, ng, g1)


def _ffn_kernel(x_ref, xp_ref, xn_ref, sh_ref, sc_ref, g2_ref, ng_in_ref, ng_out_ref,
                wu_ref, cw_ref, cb_ref, wd_ref, o_ref, *, chunk):
    i = pl.program_id(1)
    nt = pl.num_programs(1)
    g = ng_in_ref[...]
    sh = sh_ref[0]
    sc = sc_ref[0]

    def mod(xf):
        return (_rms(xf, g) * (1.0 + sc) + sh).astype(BF16)

    x = x_ref[0]
    hb = mod(x)
    hp = mod(xp_ref[0])
    hn = mod(xn_ref[0])
    dff = wd_ref.shape[0]
    f = None
    for c0 in range(0, dff, chunk):
        halves = []
        for off in (c0, dff + c0):
            w = wu_ref[:, off:off + chunk]
            p = jnp.dot(hb, w, preferred_element_type=F32)
            pprev = jnp.where(i > 0, jnp.dot(hp, w, preferred_element_type=F32)[7:8, :], 0.0)
            pnext = jnp.where(i < nt - 1, jnp.dot(hn, w, preferred_element_type=F32)[0:1, :], 0.0)
            halves.append(_dwconv3(p, pprev, pnext, cw_ref.at[:, off:off + chunk], cb_ref.at[:, off:off + chunk]))
        a, bb = halves
        act = (a * jax.nn.sigmoid(a) * bb).astype(BF16)
        t = jnp.dot(act, wd_ref[c0:c0 + chunk, :], preferred_element_type=F32)
        f = t if f is None else f + t
    o_ref[0] = x + g2_ref[0] * _rms(f, ng_out_ref[...])


def _ffn(x, sh, sc, g2, ng_in, ng_out, wu, cw, cb, wd):
    b, s, d = x.shape
    tm = min(256, s)
    r8 = tm // 8
    dff = wd.shape[0]
    chunk = dff // 2 if (dff // 2) % LANE == 0 else dff
    row = pl.BlockSpec((1, tm, d), lambda bi, i: (bi, i, 0))
    vec = pl.BlockSpec((1, 1, d), lambda bi, i: (bi, 0, 0))
    const2 = lambda a: pl.BlockSpec(a.shape, lambda bi, i: (0, 0))
    return pl.pallas_call(
        functools.partial(_ffn_kernel, chunk=chunk),
        grid=(b, s // tm),
        in_specs=[row,
                  pl.BlockSpec((1, 8, d), lambda bi, i: (bi, jnp.maximum(i * r8 - 1, 0), 0)),
                  pl.BlockSpec((1, 8, d), lambda bi, i: (bi, jnp.minimum((i + 1) * r8, s // 8 - 1), 0)),
                  vec, vec, vec, const2(ng_in), const2(ng_out),
                  const2(wu), const2(cw), const2(cb), const2(wd)],
        out_specs=row,
        out_shape=jax.ShapeDtypeStruct((b, s, d), F32),
        compiler_params=_cparams(("arbitrary", "arbitrary")),
        name="conv_ffn",
    )(x, x, x, sh, sc, g2, ng_in, ng_out, wu, cw, cb, wd)


def _hyena_filters_jnp(n_tokens, w1, b1, w2, b2, freq, w3):
    L = n_tokens
    t_idx = jnp.arange(L, dtype=F32)
    t = t_idx / max(L - 1, 1)
    bands = jnp.linspace(1e-4, HY_BANDS - 1, HY_BANDS, dtype=F32)
    ang = (2.0 * math.pi / L) * t_idx[:, None] * bands
    feats = jnp.concatenate([t[:, None], jnp.cos(ang), -jnp.sin(ang)], axis=-1)
    hid = jnp.sin(freq[0] * (feats @ w1 + b1))
    hid = jnp.sin(freq[1] * (hid @ w2 + b2))
    h = (hid @ w3).astype(F32).reshape(L, 2, HY_ORDER, HY_WIDTH)
    deltas = jnp.abs(jnp.linspace(math.log(HY_TARGET) / HY_SLOW_PCT, math.log(HY_TARGET) / HY_FAST_PCT,
                                  HY_WIDTH, dtype=F32))
    h = h * jnp.exp(-t[:, None] * deltas)[:, None, None, :]
    kern = jnp.concatenate([h[:, 0], jnp.zeros((1, HY_ORDER, HY_WIDTH), F32), h[:0:-1, 1]], axis=0)
    kern = kern / jnp.sum(jnp.abs(kern), axis=0, keepdims=True)
    return jnp.fft.rfft(kern, axis=0)


def _hyena_jnp(z, x1, x2, filt_f, skip):
    L = z.shape[1]
    for o, gate in enumerate((x1, x2)):
        conv = jnp.fft.irfft(jnp.fft.rfft(z, n=2 * L, axis=1) * filt_f[:, o], n=2 * L, axis=1)[:, :L]
        z = gate * (conv + z * skip[o])
    return z


def _rope_tables(rows, rot_dim, head_lanes, rope_off, identity_rows):
    half = rot_dim // 2
    n_freq = rot_dim // 4
    pos = np.arange(rows * GRID_W)
    inv_freq = ROPE_THETA ** (-np.arange(n_freq, dtype=np.float64) / n_freq)
    ang = np.concatenate([(pos // GRID_W)[:, None] * inv_freq, (pos % GRID_W)[:, None] * inv_freq], axis=-1)
    cos, sin = np.cos(ang), np.sin(ang)
    n = pos.shape[0]
    ct = np.ones((n, LANE))
    sa = np.zeros((n, LANE))
    sb = np.zeros((n, LANE))
    for h0 in range(0, LANE, head_lanes):
        lo = h0 + rope_off
        ct[:, lo:lo + half] = cos
        ct[:, lo + half:lo + rot_dim] = cos
        sa[:, lo:lo + half] = -sin
        sb[:, lo + half:lo + rot_dim] = sin
    ident = (np.ones((identity_rows, LANE)), np.zeros((identity_rows, LANE)), np.zeros((identity_rows, LANE)))
    lat = tuple(jnp.asarray(t, F32) for t in (ct, sa, sb))
    ctx = tuple(jnp.asarray(t, F32) for t in ident)
    return lat, ctx


def _pack_w_in(w):
    d = w.shape[0]
    o = 0
    mq = w[:, o:o + MLA_HEADS * (MLA_NOPE + MLA_ROPE)]; o += MLA_HEADS * (MLA_NOPE + MLA_ROPE)
    mckv = w[:, o:o + KV_RANK]; o += KV_RANK
    mkr = w[:, o:o + MLA_ROPE]; o += MLA_ROPE
    sq = w[:, o:o + SWA_HEADS * SWA_HEAD_DIM]; o += SWA_HEADS * SWA_HEAD_DIM
    sk = w[:, o:o + SWA_KV_HEADS * SWA_HEAD_DIM]; o += SWA_KV_HEADS * SWA_HEAD_DIM
    sv = w[:, o:o + SWA_KV_HEADS * SWA_HEAD_DIM]; o += SWA_KV_HEADS * SWA_HEAD_DIM
    hy = w[:, o:o + (HY_ORDER + 1) * HY_WIDTH]; o += (HY_ORDER + 1) * HY_WIDTH
    gt = w[:, o:]
    pad_q = HEAD_PAD - MLA_NOPE - MLA_ROPE
    mq = jnp.pad(mq.reshape(d, MLA_HEADS, MLA_NOPE + MLA_ROPE), ((0, 0), (0, 0), (0, pad_q))).reshape(d, -1)
    mkr = jnp.pad(mkr, ((0, 0), (MLA_NOPE, pad_q)))
    dup = lambda t: jnp.repeat(t.reshape(d, SWA_KV_HEADS, 1, SWA_HEAD_DIM), 2, axis=2).reshape(d, -1)
    return jnp.concatenate([mq, mckv, mkr, sq, dup(sk), dup(sv), hy, gt], axis=1).astype(BF16)


def _pack_w_kv(w):
    r = w.shape[0]
    w = w.reshape(r, MLA_HEADS, MLA_NOPE + MLA_V)
    k = jnp.pad(w[..., :MLA_NOPE], ((0, 0), (0, 0), (0, HEAD_PAD - MLA_NOPE))).reshape(r, -1)
    v = w[..., MLA_NOPE:].reshape(r, -1)
    return jnp.concatenate([k, v], axis=1).astype(BF16)


def kernel(x, c, ctx, c_ctx, w_mod, b_mod, norm_g, w_in, kv_norm_g, w_kv_up, swa_sink, hy_conv_w, hy_conv_b,
           hy_w1, hy_b1, hy_w2, hy_b2, hy_freq, hy_w3, hy_skip, w_branch, w_out, w_up, ffn_conv_w, ffn_conv_b,
           w_down):
    b, s, d = x.shape
    n_ctx = ctx.shape[1]
    depth = w_mod.shape[0]
    rows = s // GRID_W
    rope_m, rope_m_ctx = _rope_tables(rows, MLA_ROPE, HEAD_PAD, MLA_NOPE, n_ctx)
    rope_s, rope_s_ctx = _rope_tables(rows, SWA_HEAD_DIM, SWA_HEAD_DIM, 0, n_ctx)

    cvec = jnp.concatenate([c, c_ctx[None, :], jnp.zeros((8 - b - 1, d), F32)], axis=0)
    mod_all = _modulation(cvec, w_mod, b_mod)

    x_lat, x_ctx = x, ctx
    for l in range(depth):
        with_ctx = l < depth - 1
        m = mod_all[l].reshape(8, 6, d)
        lat = [m[:b, k][:, None, :] for k in range(6)]
        cx = [jnp.broadcast_to(m[b, k][None, None, :], (b, 1, d)) for k in range(6)]
        ng = [norm_g[l, k][None, :] for k in range(4)]
        w_pack = _pack_w_in(w_in[l])
        wkv_pack = _pack_w_kv(w_kv_up[l])
        kvg = kv_norm_g[l][None, :]
        cw, cb = hy_conv_w[l], hy_conv_b[l][None, :]

        q, kk, vv, sq, sk, sv, z, x1, x2, gt = _in_proj(
            x_lat, lat[0], lat[1], ng[0], w_pack, wkv_pack, kvg, rope_m, rope_s, cw, cb)
        qc, kkc, vvc, sqc, skc, svc, zc, x1c, x2c, gtc = _in_proj(
            x_ctx, cx[0], cx[1], ng[0], w_pack, wkv_pack, kvg, rope_m_ctx, rope_s_ctx, cw, cb)

        y_a = _mla(q, kkc, vvc, kk, vv)
        y_b = _swa(swa_sink[l], sq, skc, svc, sk, sv)
        filt = _hyena_filters_jnp(s, hy_w1[l], hy_b1[l], hy_w2[l], hy_b2[l], hy_freq[l], hy_w3[l])
        y_c = _hyena_jnp(z, x1, x2, filt, hy_skip[l])

        wb = w_branch[l].astype(BF16)
        wo = w_out[l].astype(BF16)
        wu = w_up[l].astype(BF16)
        wd = w_down[l].astype(BF16)
        fcw, fcb = ffn_conv_w[l], ffn_conv_b[l][None, :]

        x_lat = _merge(x_lat, y_a, y_b, y_c, gt, wb, wo, ng[1], lat[2])
        x_lat = _ffn(x_lat, lat[3], lat[4], lat[5], ng[2], ng[3], wu, fcw, fcb, wd)

        if with_ctx:
            yc_a = _mla(qc, kkc, vvc)
            yc_b = _swa(swa_sink[l], sqc, skc, svc)
            filt_c = _hyena_filters_jnp(n_ctx, hy_w1[l], hy_b1[l], hy_w2[l], hy_b2[l], hy_freq[l], hy_w3[l])
            yc_c = _hyena_jnp(zc, x1c, x2c, filt_c, hy_skip[l])
            x_ctx = _merge(x_ctx, yc_a, yc_b, yc_c, gtc, wb, wo, ng[1], cx[2])
            x_ctx = _ffn(x_ctx, cx[3], cx[4], cx[5], ng[2], ng[3], wu, fcw, fcb, wd)
    return x_lat
```

```python
import functools
import math

import numpy as np
import jax
import jax.numpy as jnp
from jax import lax
from jax.experimental import pallas as pl
from jax.experimental.pallas import tpu as pltpu

F32 = jnp.float32
BF16 = jnp.bfloat16

GRID_W = 64
EPS = 1e-6
ROPE_THETA = 10000.0
NEG_INF = -1e30
MLA_HEADS = 8
MLA_NOPE = 64
MLA_ROPE = 32
MLA_V = 64
KV_RANK = 256
MLA_SCALE = (MLA_NOPE + MLA_ROPE) ** -0.5
LOG2E = math.log2(math.e)
SWA_HEADS = 8
SWA_KV_HEADS = 2
SWA_HEAD_DIM = 64
SWA_BLOCK = 128
SWA_SCALE = SWA_HEAD_DIM ** -0.5
HY_WIDTH = 512
HY_ORDER = 2
HY_BANDS = 16
HY_HIDDEN = 64
HY_TARGET = 1e-2
HY_FAST_PCT = 0.3
HY_SLOW_PCT = 1.5
N_BRANCH = 3
BRANCH_WIDTH = 512
LANE = 128
HEAD_PAD = 128

VMEM_LIMIT = 56 * 1024 * 1024

C_Q = 0
C_CKV = C_Q + MLA_HEADS * HEAD_PAD
C_KR = C_CKV + KV_RANK
C_SQ = C_KR + HEAD_PAD
C_SK = C_SQ + SWA_HEADS * SWA_HEAD_DIM
C_SV = C_SK + 2 * SWA_KV_HEADS * SWA_HEAD_DIM
C_HY = C_SV + 2 * SWA_KV_HEADS * SWA_HEAD_DIM
C_GT = C_HY + (HY_ORDER + 1) * HY_WIDTH


def _cparams(sem):
    return pltpu.CompilerParams(dimension_semantics=sem, vmem_limit_bytes=VMEM_LIMIT)


def _rms(xf, g):
    return xf * lax.rsqrt(jnp.mean(xf * xf, axis=-1, keepdims=True) + EPS) * g


def _rope(x, cos, sa, sb, half):
    return x * cos + pltpu.roll(x, LANE - half, 1) * sa + pltpu.roll(x, half, 1) * sb


def _dwconv3(p, prev_row, next_row, cw_ref, cb_ref):
    tm = p.shape[0]
    rows = lax.broadcasted_iota(jnp.int32, (tm, 1), 0)
    up = jnp.where(rows == 0, prev_row, pltpu.roll(p, 1, 0))
    dn = jnp.where(rows == tm - 1, next_row, pltpu.roll(p, tm - 1, 0))
    return up * cw_ref[0:1, :] + p * cw_ref[1:2, :] + dn * cw_ref[2:3, :] + cb_ref[...]


def _mod_kernel(c_ref, w_ref, b_ref, o_ref):
    c = c_ref[...]
    a = c * jax.nn.sigmoid(c)
    a_hi = a.astype(BF16)
    a_lo = (a - a_hi.astype(F32)).astype(BF16)
    w = w_ref[0]
    w_hi = w.astype(BF16)
    w_lo = (w - w_hi.astype(F32)).astype(BF16)
    acc = jnp.dot(a_hi, w_hi, preferred_element_type=F32)
    acc += jnp.dot(a_lo, w_hi, preferred_element_type=F32)
    acc += jnp.dot(a_hi, w_lo, preferred_element_type=F32)
    o_ref[0] = acc + b_ref[0]


def _modulation(cvec, w_mod, b_mod):
    depth, d, n = w_mod.shape
    tn = 1536
    return pl.pallas_call(
        _mod_kernel,
        grid=(depth, n // tn),
        in_specs=[pl.BlockSpec((8, d), lambda l, j: (0, 0)),
                  pl.BlockSpec((1, d, tn), lambda l, j: (l, 0, j)),
                  pl.BlockSpec((1, 1, tn), lambda l, j: (l, 0, j))],
        out_specs=pl.BlockSpec((1, 8, tn), lambda l, j: (l, 0, j)),
        out_shape=jax.ShapeDtypeStruct((depth, 8, n), F32),
        compiler_params=_cparams(("arbitrary", "arbitrary")),
        name="adaln_mod",
    )(cvec, w_mod, b_mod.reshape(depth, 1, n))


def _in_proj_kernel(x_ref, xp_ref, xn_ref, sh_ref, sc_ref, g_ref, w_ref, wkv_ref, kvg_ref,
                    cm_ref, sma_ref, smb_ref, cs_ref, ssa_ref, ssb_ref, cw_ref, cb_ref,
                    q_ref, kk_ref, vv_ref, sq_ref, sk_ref, sv_ref, z_ref, x1_ref, x2_ref, gt_ref):
    i = pl.program_id(1)
    nt = pl.num_programs(1)
    g = g_ref[...]
    sh = sh_ref[0]
    sc = sc_ref[0]

    def mod(xf):
        return (_rms(xf, g) * (1.0 + sc) + sh).astype(BF16)

    hb = mod(x_ref[0])

    def proj(lo, hi, lhs=hb):
        return jnp.dot(lhs, w_ref[:, lo:hi], preferred_element_type=F32)

    cm, sma, smb = cm_ref[...], sma_ref[...], smb_ref[...]
    cs, ssa, ssb = cs_ref[...], ssa_ref[...], ssb_ref[...]

    pq = proj(C_Q, C_CKV)
    for h in range(MLA_HEADS):
        xh = pq[:, HEAD_PAD * h:HEAD_PAD * (h + 1)]
        q_ref[0, :, HEAD_PAD * h:HEAD_PAD * (h + 1)] = (
            _rope(xh, cm, sma, smb, MLA_ROPE // 2) * (MLA_SCALE * LOG2E)).astype(BF16)

    ckv = proj(C_CKV, C_KR)
    cn = _rms(ckv, kvg_ref[...]).astype(BF16)
    kv = jnp.dot(cn, wkv_ref[...], preferred_element_type=F32)
    krr = _rope(proj(C_KR, C_SQ), cm, sma, smb, MLA_ROPE // 2)
    for h in range(MLA_HEADS):
        kk_ref[0, :, HEAD_PAD * h:HEAD_PAD * (h + 1)] = (
            kv[:, HEAD_PAD * h:HEAD_PAD * (h + 1)] + krr).astype(BF16)
    ones = jnp.ones((hb.shape[0], LANE), BF16)
    for j in range(MLA_HEADS // 2):
        v0 = MLA_HEADS * HEAD_PAD + LANE * j
        vv_ref[0, :, 2 * LANE * j:2 * LANE * j + LANE] = kv[:, v0:v0 + LANE].astype(BF16)
        vv_ref[0, :, 2 * LANE * j + LANE:2 * LANE * (j + 1)] = ones

    psq = proj(C_SQ, C_SK)
    for j in range(SWA_HEADS // 2):
        xh = psq[:, LANE * j:LANE * (j + 1)]
        sq_ref[0, :, LANE * j:LANE * (j + 1)] = (
            _rope(xh, cs, ssa, ssb, SWA_HEAD_DIM // 2) * SWA_SCALE).astype(BF16)
    psk = proj(C_SK, C_SV)
    lane = lax.broadcasted_iota(jnp.int32, (hb.shape[0], LANE), 1)
    for k in range(SWA_KV_HEADS):
        rk = _rope(psk[:, LANE * k:LANE * (k + 1)], cs, ssa, ssb, SWA_HEAD_DIM // 2)
        sk_ref[0, :, 2 * LANE * k:2 * LANE * k + LANE] = jnp.where(lane < 64, rk, 0.0).astype(BF16)
        sk_ref[0, :, 2 * LANE * k + LANE:2 * LANE * (k + 1)] = jnp.where(lane >= 64, rk, 0.0).astype(BF16)
    sv_ref[0] = proj(C_SV, C_HY).astype(BF16)

    ph = proj(C_HY, C_GT)
    pprev = proj(C_HY, C_GT, mod(xp_ref[0]))[7:8, :]
    pnext = proj(C_HY, C_GT, mod(xn_ref[0]))[0:1, :]
    pprev = jnp.where(i > 0, pprev, 0.0)
    pnext = jnp.where(i < nt - 1, pnext, 0.0)
    u = _dwconv3(ph, pprev, pnext, cw_ref, cb_ref)
    z_ref[0] = u[:, :HY_WIDTH]
    x1_ref[0] = u[:, HY_WIDTH:2 * HY_WIDTH]
    x2_ref[0] = u[:, 2 * HY_WIDTH:]

    gt_ref[0] = jax.nn.sigmoid(proj(C_GT, w_ref.shape[1])).astype(BF16)


def _in_proj(x, sh, sc, g, w, wkv, kvg, rope_m, rope_s, cw, cb):
    b, s, d = x.shape
    tm = min(256, s)
    nt = s // tm
    nw = w.shape[1]
    r8 = tm // 8
    row = lambda width: pl.BlockSpec((1, tm, width), lambda bi, i: (bi, i, 0))
    const2 = lambda a: pl.BlockSpec(a.shape, lambda bi, i: (0, 0))
    tab = pl.BlockSpec((tm, LANE), lambda bi, i: (i, 0))
    in_specs = [
        row(d),
        pl.BlockSpec((1, 8, d), lambda bi, i: (bi, jnp.maximum(i * r8 - 1, 0), 0)),
        pl.BlockSpec((1, 8, d), lambda bi, i: (bi, jnp.minimum((i + 1) * r8, s // 8 - 1), 0)),
        pl.BlockSpec((1, 1, d), lambda bi, i: (bi, 0, 0)),
        pl.BlockSpec((1, 1, d), lambda bi, i: (bi, 0, 0)),
        const2(g), const2(w), const2(wkv), const2(kvg),
        tab, tab, tab, tab, tab, tab,
        const2(cw), const2(cb),
    ]
    widths = [(MLA_HEADS * HEAD_PAD, BF16), (MLA_HEADS * HEAD_PAD, BF16), (MLA_HEADS * LANE, BF16),
              (SWA_HEADS * SWA_HEAD_DIM, BF16), (4 * LANE, BF16), (2 * LANE, BF16),
              (HY_WIDTH, F32), (HY_WIDTH, F32), (HY_WIDTH, F32), (N_BRANCH * d, BF16)]
    return pl.pallas_call(
        _in_proj_kernel,
        grid=(b, nt),
        in_specs=in_specs,
        out_specs=[row(wd) for wd, _ in widths],
        out_shape=[jax.ShapeDtypeStruct((b, s, wd), dt) for wd, dt in widths],
        compiler_params=_cparams(("arbitrary", "arbitrary")),
        name="in_proj",
    )(x, x, x, sh, sc, g, w, wkv, kvg, *rope_m, *rope_s, cw, cb)


def _mla_kernel(*refs, tk, n_lat, rb):
    if n_lat:
        q_ref, kc_ref, vc_ref, kl_ref, vl_ref, o_ref, m_scr, acc_scr = refs
    else:
        q_ref, kc_ref, vc_ref, o_ref, m_scr, acc_scr = refs
    tq = q_ref.shape[1]
    nt = (((1,), (1,)), ((), ()))
    m_scr[...] = jnp.full(m_scr.shape, NEG_INF, F32)
    acc_scr[...] = jnp.zeros(acc_scr.shape, F32)

    def step(k_ref, v_ref, off, n):
        v = v_ref[0, pl.ds(off, n), :]
        for e in range(2):
            k = k_ref[0, pl.ds(off, n), HEAD_PAD * e:HEAD_PAD * (e + 1)]
            for r in range(tq // rb):
                rows = pl.ds(r * rb, rb)
                s = lax.dot_general(q_ref[0, rows, HEAD_PAD * e:HEAD_PAD * (e + 1)], k, nt,
                                    preferred_element_type=F32)
                m_old = m_scr[e, rows, :]
                m_new = jnp.maximum(m_old, jnp.max(s, axis=1, keepdims=True))
                alpha = jnp.exp2(m_old - m_new)
                p = jnp.exp2(s - jnp.tile(m_new, (1, n // LANE))).astype(BF16)
                acc_scr[e, rows, :] = (jnp.tile(alpha, (1, 2)) * acc_scr[e, rows, :]
                                       + jnp.dot(p, v, preferred_element_type=F32))
                m_scr[e, rows, :] = m_new

    step(kc_ref, vc_ref, 0, kc_ref.shape[1])
    if n_lat:
        def body(j, carry):
            step(kl_ref, vl_ref, pl.multiple_of(j * tk, tk), tk)
            return carry
        lax.fori_loop(0, n_lat, body, 0)
    lane = lax.broadcasted_iota(jnp.int32, (tq, LANE), 1)
    o0 = acc_scr[0, :, :LANE] / acc_scr[0, :, LANE:]
    o1 = acc_scr[1, :, :LANE] / acc_scr[1, :, LANE:]
    o_ref[0] = jnp.where(lane < MLA_V, o0, o1).astype(o_ref.dtype)


def _mla(q, kc, vc, kl=None, vl=None, tq=2048, tk=512, rb=128):
    b, sq, _ = q.shape
    c = kc.shape[1]
    tq = min(tq, sq)
    rb = min(rb, tq)
    hp = MLA_HEADS // 2
    in_specs = [pl.BlockSpec((1, tq, 2 * HEAD_PAD), lambda bi, h, i: (bi, i, h)),
                pl.BlockSpec((1, c, 2 * HEAD_PAD), lambda bi, h, i: (bi, 0, h)),
                pl.BlockSpec((1, c, 2 * LANE), lambda bi, h, i: (bi, 0, h))]
    args = [q, kc, vc]
    n_lat = 0
    if kl is not None:
        s = kl.shape[1]
        tk = min(tk, s)
        n_lat = s // tk
        in_specs += [pl.BlockSpec((1, s, 2 * HEAD_PAD), lambda bi, h, i: (bi, 0, h)),
                     pl.BlockSpec((1, s, 2 * LANE), lambda bi, h, i: (bi, 0, h))]
        args += [kl, vl]
    return pl.pallas_call(
        functools.partial(_mla_kernel, tk=tk, n_lat=n_lat, rb=rb),
        grid=(b, hp, sq // tq),
        in_specs=in_specs,
        out_specs=pl.BlockSpec((1, tq, LANE), lambda bi, h, i: (bi, i, h)),
        out_shape=jax.ShapeDtypeStruct((b, sq, MLA_HEADS * MLA_V), BF16),
        scratch_shapes=[pltpu.VMEM((2, tq, LANE), F32), pltpu.VMEM((2, tq, 2 * LANE), F32)],
        compiler_params=_cparams(("arbitrary", "arbitrary", "arbitrary")),
        name="mla_attn",
    )(*args)


def _swa_kernel(*refs, band):
    if band:
        sink_ref, q_ref, kc_ref, vc_ref, kp_ref, k0_ref, kn_ref, vp_ref, v0_ref, vn_ref, o_ref = refs
    else:
        sink_ref, q_ref, kc_ref, vc_ref, o_ref = refs
    i = pl.program_id(1)
    nb = pl.num_programs(1)
    tq = q_ref.shape[1]
    lane = lax.broadcasted_iota(jnp.int32, (tq, LANE), 1)
    if band:
        r = lax.broadcasted_iota(jnp.int32, (tq, SWA_BLOCK), 0)
        c = lax.broadcasted_iota(jnp.int32, (tq, SWA_BLOCK), 1)
        ok_prev = (r <= c) & (i > 0)
        ok_next = (c <= r) & (i < nb - 1)
    grp = SWA_HEADS // SWA_KV_HEADS
    nt = (((1,), (1,)), ((), ()))
    for j in range(SWA_HEADS // 2):
        qp = q_ref[0, :, LANE * j:LANE * (j + 1)]
        kvh = (2 * j) // grp
        vsl = slice(LANE * kvh, LANE * (kvh + 1))
        outs = []
        for e in range(2):
            ksl = slice(LANE * (2 * kvh + e), LANE * (2 * kvh + e + 1))
            parts = [lax.dot_general(qp, kc_ref[0, :, ksl], nt, preferred_element_type=F32)]
            vals = [vc_ref[0, :, vsl]]
            if band:
                sp = lax.dot_general(qp, kp_ref[0, :, ksl], nt, preferred_element_type=F32)
                s0 = lax.dot_general(qp, k0_ref[0, :, ksl], nt, preferred_element_type=F32)
                sn = lax.dot_general(qp, kn_ref[0, :, ksl], nt, preferred_element_type=F32)
                parts += [jnp.where(ok_prev, sp, NEG_INF), s0, jnp.where(ok_next, sn, NEG_INF)]
                vals += [vp_ref[0, :, vsl], v0_ref[0, :, vsl], vn_ref[0, :, vsl]]
            s = jnp.concatenate(parts, axis=1)
            v = jnp.concatenate(vals, axis=0)
            sink = sink_ref[2 * j + e]
            m = jnp.maximum(jnp.max(s, axis=1, keepdims=True), sink)
            p = jnp.exp(s - m)
            den = jnp.sum(p, axis=1, keepdims=True) + jnp.exp(sink - m)
            outs.append(jnp.dot(p.astype(BF16), v, preferred_element_type=F32) / den)
        o_ref[0, :, LANE * j:LANE * (j + 1)] = jnp.where(lane < 64, outs[0], outs[1]).astype(o_ref.dtype)


def _swa(sink, q, kc, vc, k=None, v=None):
    b, sq, _ = q.shape
    c = kc.shape[1]
    tq = SWA_BLOCK
    nb = sq // tq
    band = k is not None
    blk = lambda width, f: pl.BlockSpec((1, tq, width), f)
    in_specs = [pl.BlockSpec(memory_space=pltpu.SMEM),
                blk(SWA_HEADS * SWA_HEAD_DIM, lambda bi, i: (bi, i, 0)),
                pl.BlockSpec((1, c, 4 * LANE), lambda bi, i: (bi, 0, 0)),
                pl.BlockSpec((1, c, 2 * LANE), lambda bi, i: (bi, 0, 0))]
    args = [sink, q, kc, vc]
    if band:
        prev = lambda bi, i: (bi, jnp.maximum(i - 1, 0), 0)
        cur = lambda bi, i: (bi, i, 0)
        nxt = lambda bi, i: (bi, jnp.minimum(i + 1, nb - 1), 0)
        in_specs += [blk(4 * LANE, prev), blk(4 * LANE, cur), blk(4 * LANE, nxt),
                     blk(2 * LANE, prev), blk(2 * LANE, cur), blk(2 * LANE, nxt)]
        args += [k, k, k, v, v, v]
    return pl.pallas_call(
        functools.partial(_swa_kernel, band=band),
        grid=(b, nb),
        in_specs=in_specs,
        out_specs=blk(SWA_HEADS * SWA_HEAD_DIM, lambda bi, i: (bi, i, 0)),
        out_shape=jax.ShapeDtypeStruct((b, sq, SWA_HEADS * SWA_HEAD_DIM), BF16),
        compiler_params=_cparams(("arbitrary", "arbitrary")),
        name="swa_attn",
    )(*args)


def _merge_kernel(x_ref, ya_ref, yb_ref, yc_ref, gt_ref, wb_ref, wo_ref, ng_ref, g1_ref, o_ref):
    d = x_ref.shape[2]
    merged = None
    for k, y_ref in enumerate((ya_ref, yb_ref, yc_ref)):
        t = jnp.dot(y_ref[0].astype(BF16), wb_ref[k], preferred_element_type=F32)
        t = gt_ref[0, :, d * k:d * (k + 1)].astype(F32) * t
        merged = t if merged is None else merged + t
    y = jnp.dot(merged.astype(BF16), wo_ref[...], preferred_element_type=F32)
    o_ref[0] = x_ref[0] + g1_ref[0] * _rms(y, ng_ref[...])


def _merge(x, ya, yb, yc, gt, wb, wo, ng, g1):
    b, s, d = x.shape
    tm = min(512, s)
    row = lambda width: pl.BlockSpec((1, tm, width), lambda bi, i: (bi, i, 0))
    return pl.pallas_call(
        _merge_kernel,
        grid=(b, s // tm),
        in_specs=[row(d), row(BRANCH_WIDTH), row(BRANCH_WIDTH), row(BRANCH_WIDTH), row(N_BRANCH * d),
                  pl.BlockSpec(wb.shape, lambda bi, i: (0, 0, 0)),
                  pl.BlockSpec(wo.shape, lambda bi, i: (0, 0)),
                  pl.BlockSpec(ng.shape, lambda bi, i: (0, 0)),
                  pl.BlockSpec((1, 1, d), lambda bi, i: (bi, 0, 0))],
        out_specs=row(d),
        out_shape=jax.ShapeDtypeStruct((b, s, d), F32),
        compiler_params=_cparams(("arbitrary", "arbitrary")),
        name="merge",
    )(x, ya, yb, yc, gt, wb, wo, ng, g1)


def _ffn_kernel(x_ref, xp_ref, xn_ref, sh_ref, sc_ref, g2_ref, ng_in_ref, ng_out_ref,
                wu_ref, cw_ref, cb_ref, wd_ref, o_ref, *, chunk):
    i = pl.program_id(1)
    nt = pl.num_programs(1)
    g = ng_in_ref[...]
    sh = sh_ref[0]
    sc = sc_ref[0]

    def mod(xf):
        return (_rms(xf, g) * (1.0 + sc) + sh).astype(BF16)

    x = x_ref[0]
    hb = mod(x)
    hp = mod(xp_ref[0])
    hn = mod(xn_ref[0])
    dff = wd_ref.shape[0]
    f = None
    for c0 in range(0, dff, chunk):
        halves = []
        for off in (c0, dff + c0):
            w = wu_ref[:, off:off + chunk]
            p = jnp.dot(hb, w, preferred_element_type=F32)
            pprev = jnp.where(i > 0, jnp.dot(hp, w, preferred_element_type=F32)[7:8, :], 0.0)
            pnext = jnp.where(i < nt - 1, jnp.dot(hn, w, preferred_element_type=F32)[0:1, :], 0.0)
            halves.append(_dwconv3(p, pprev, pnext, cw_ref.at[:, off:off + chunk], cb_ref.at[:, off:off + chunk]))
        a, bb = halves
        act = (a * jax.nn.sigmoid(a) * bb).astype(BF16)
        t = jnp.dot(act, wd_ref[c0:c0 + chunk, :], preferred_element_type=F32)
        f = t if f is None else f + t
    o_ref[0] = x + g2_ref[0] * _rms(f, ng_out_ref[...])


def _ffn(x, sh, sc, g2, ng_in, ng_out, wu, cw, cb, wd):
    b, s, d = x.shape
    tm = min(256, s)
    r8 = tm // 8
    dff = wd.shape[0]
    chunk = dff // 2 if (dff // 2) % LANE == 0 else dff
    row = pl.BlockSpec((1, tm, d), lambda bi, i: (bi, i, 0))
    vec = pl.BlockSpec((1, 1, d), lambda bi, i: (bi, 0, 0))
    const2 = lambda a: pl.BlockSpec(a.shape, lambda bi, i: (0, 0))
    return pl.pallas_call(
        functools.partial(_ffn_kernel, chunk=chunk),
        grid=(b, s // tm),
        in_specs=[row,
                  pl.BlockSpec((1, 8, d), lambda bi, i: (bi, jnp.maximum(i * r8 - 1, 0), 0)),
                  pl.BlockSpec((1, 8, d), lambda bi, i: (bi, jnp.minimum((i + 1) * r8, s // 8 - 1), 0)),
                  vec, vec, vec, const2(ng_in), const2(ng_out),
                  const2(wu), const2(cw), const2(cb), const2(wd)],
        out_specs=row,
        out_shape=jax.ShapeDtypeStruct((b, s, d), F32),
        compiler_params=_cparams(("arbitrary", "arbitrary")),
        name="conv_ffn",
    )(x, x, x, sh, sc, g2, ng_in, ng_out, wu, cw, cb, wd)


HY_PASSES = 1
HY_UNROLL = 8
HY_UNROLL2 = 5


def _split(x):
    hi = x.astype(BF16)
    return hi, (x - hi.astype(F32)).astype(BF16)


def _cdot(c_hi, c_lo, x, passes=HY_PASSES):
    x_hi, x_lo = _split(x)
    acc = jnp.dot(c_hi, x_hi, preferred_element_type=F32)
    if passes > 1:
        acc = acc + jnp.dot(c_lo, x_hi, preferred_element_type=F32)
        acc = acc + jnp.dot(c_hi, x_lo, preferred_element_type=F32)
    return acc


def _np_split(m):
    m = jnp.asarray(m, F32)
    hi = m.astype(BF16)
    return hi, (m - hi.astype(F32)).astype(BF16)


def _dft_consts(n_blocks, nonzero_blocks):
    nb = n_blocks
    n = LANE * nb
    dh = nb // 2 + 1
    dhp = -(-dh // 8) * 8
    d = np.arange(dh)[:, None]
    b = np.arange(nonzero_blocks)[None, :]
    ang = 2 * np.pi * ((d * b) % nb) / nb
    f1 = np.zeros((2 * dhp, nonzero_blocks))
    f1[:dh] = np.cos(ang)
    f1[dhp:dhp + dh] = -np.sin(ang)
    a = np.arange(LANE)
    ang2 = 2 * np.pi * ((a[:, None] * a[None, :]) % LANE) / LANE
    ar, ai = np.cos(ang2), -np.sin(ang2)
    f2 = np.block([[ar, -ai], [ai, ar]])
    f2inv = np.block([[ar, ai], [-ai, ar]])
    bo = np.arange(nb // 2)[:, None]
    do = np.arange(dh)[None, :]
    wd = np.where((do == 0) | (do == nb // 2), 1.0, 2.0) / n
    ang3 = 2 * np.pi * ((bo * do) % nb) / nb
    f3 = np.zeros((nb // 2, 2 * dhp))
    f3[:, :dh] = wd * np.cos(ang3)
    f3[:, dhp:dhp + dh] = -wd * np.sin(ang3)
    step_ang = 2 * np.pi * a / n
    tw_step = np.stack([np.broadcast_to(np.cos(step_ang)[:, None], (LANE, LANE)),
                        np.broadcast_to(-np.sin(step_ang)[:, None], (LANE, LANE))])
    return dict(dh=dh, dhp=dhp, f1=_np_split(f1), f2=_np_split(f2), f2inv=_np_split(f2inv),
                f3=_np_split(f3), tw_step=jnp.asarray(tw_step, F32))


def _stage1(x_ref, f1_hi, f1_lo, g_scr, n_in, n_out):
    def body(a, carry):
        xa = x_ref[pl.ds(a, n_in, stride=LANE), :]
        g_scr[pl.ds(a, n_out, stride=LANE), :] = _cdot(f1_hi, f1_lo, xa)
        return carry
    lax.fori_loop(0, LANE, body, 0, unroll=HY_UNROLL)


def _twiddle_init(tw_scr):
    tw_scr[0] = jnp.ones((LANE, LANE), F32)
    tw_scr[1] = jnp.zeros((LANE, LANE), F32)


def _twiddle_next(tw_scr, step_ref):
    twr, twi = tw_scr[0], tw_scr[1]
    sr, si = step_ref[0], step_ref[1]
    tw_scr[0] = twr * sr - twi * si
    tw_scr[1] = twr * si + twi * sr


def _spectrum_kernel(k_ref, inv_ref, f1h_ref, f1l_ref, f2h_ref, f2l_ref, step_ref, hf_ref, g_scr, tw_scr,
                     *, dh, dhp):
    nb = k_ref.shape[0] // LANE
    _stage1(k_ref, f1h_ref[...], f1l_ref[...], g_scr, nb, 2 * dhp)
    _twiddle_init(tw_scr)
    inv = inv_ref[...]

    def body(d, carry):
        gr = g_scr[pl.ds(pl.multiple_of(d * LANE, LANE), LANE), :]
        gi = g_scr[pl.ds(pl.multiple_of((dhp + d) * LANE, LANE), LANE), :]
        twr, twi = tw_scr[0], tw_scr[1]
        t = jnp.concatenate([gr * twr - gi * twi, gr * twi + gi * twr], axis=0)
        y = _cdot(f2h_ref[...], f2l_ref[...], t, passes=3)
        hf_ref[0, d] = y[:LANE] * inv
        hf_ref[1, d] = y[LANE:] * inv
        _twiddle_next(tw_scr, step_ref)
        return carry
    lax.fori_loop(0, dh, body, 0, unroll=HY_UNROLL2)


def _hyena_spectrum(kern, inv_norm):
    n, c = kern.shape
    nb = n // LANE
    cst = _dft_consts(nb, nb)
    dh, dhp = cst["dh"], cst["dhp"]
    const = lambda a: pl.BlockSpec(a.shape, lambda j: (0,) * a.ndim)
    consts = [*cst["f1"], *cst["f2"], cst["tw_step"]]
    return pl.pallas_call(
        functools.partial(_spectrum_kernel, dh=dh, dhp=dhp),
        grid=(c // LANE,),
        in_specs=[pl.BlockSpec((n, LANE), lambda j: (0, j)), pl.BlockSpec((1, LANE), lambda j: (0, j))]
        + [const(a) for a in consts],
        out_specs=pl.BlockSpec((None, 2, dh, LANE, LANE), lambda j: (j, 0, 0, 0, 0)),
        out_shape=jax.ShapeDtypeStruct((c // LANE, 2, dh, LANE, LANE), F32),
        scratch_shapes=[pltpu.VMEM((2 * dhp * LANE, LANE), F32), pltpu.VMEM((2, LANE, LANE), F32)],
        compiler_params=_cparams(("arbitrary",)),
        name="hyena_spectrum",
    )(kern, inv_norm, *consts)


def _conv_kernel(z_ref, gate_ref, skip_ref, hf_ref, f1h_ref, f1l_ref, f2h_ref, f2l_ref, f2ih_ref, f2il_ref,
                 f3h_ref, f3l_ref, step_ref, o_ref, g_scr, tw_scr, *, dh, dhp):
    nbh = z_ref.shape[1] // LANE
    zr = z_ref.at[0]
    _stage1(zr, f1h_ref[...], f1l_ref[...], g_scr, nbh, 2 * dhp)
    _twiddle_init(tw_scr)

    def body(d, carry):
        r0 = pl.multiple_of(d * LANE, LANE)
        r1 = pl.multiple_of((dhp + d) * LANE, LANE)
        gr = g_scr[pl.ds(r0, LANE), :]
        gi = g_scr[pl.ds(r1, LANE), :]
        twr, twi = tw_scr[0], tw_scr[1]
        t = jnp.concatenate([gr * twr - gi * twi, gr * twi + gi * twr], axis=0)
        y = _cdot(f2h_ref[...], f2l_ref[...], t)
        yr, yi = y[:LANE], y[LANE:]
        hr, hi = hf_ref[0, d], hf_ref[1, d]
        zz = jnp.concatenate([yr * hr - yi * hi, yr * hi + yi * hr], axis=0)
        u = _cdot(f2ih_ref[...], f2il_ref[...], zz)
        ur, ui = u[:LANE], u[LANE:]
        g_scr[pl.ds(r0, LANE), :] = ur * twr + ui * twi
        g_scr[pl.ds(r1, LANE), :] = ui * twr - ur * twi
        _twiddle_next(tw_scr, step_ref)
        return carry
    lax.fori_loop(0, dh, body, 0, unroll=HY_UNROLL2)

    skip = skip_ref[...]
    f3h, f3l = f3h_ref[...], f3l_ref[...]

    def out_body(a, carry):
        ga = g_scr[pl.ds(a, 2 * dhp, stride=LANE), :]
        conv = _cdot(f3h, f3l, ga)
        rows = pl.ds(a, nbh, stride=LANE)
        o_ref[0, rows, :] = gate_ref[0, rows, :] * (conv + z_ref[0, rows, :] * skip)
        return carry
    lax.fori_loop(0, LANE, out_body, 0, unroll=HY_UNROLL)


def _hyena_conv(z, gate, skip, hf, cst):
    b, l, w = z.shape
    dh, dhp = cst["dh"], cst["dhp"]
    consts = [*cst["f1"], *cst["f2"], *cst["f2inv"], *cst["f3"], cst["tw_step"]]
    const = lambda a: pl.BlockSpec(a.shape, lambda j, bi: (0,) * a.ndim)
    tile = pl.BlockSpec((1, l, LANE), lambda j, bi: (bi, 0, j))
    return pl.pallas_call(
        functools.partial(_conv_kernel, dh=dh, dhp=dhp),
        grid=(w // LANE, b),
        in_specs=[tile, tile, pl.BlockSpec((1, LANE), lambda j, bi: (0, j)),
                  pl.BlockSpec((None, 2, dh, LANE, LANE), lambda j, bi: (j, 0, 0, 0, 0))]
        + [const(a) for a in consts],
        out_specs=tile,
        out_shape=jax.ShapeDtypeStruct((b, l, w), F32),
        scratch_shapes=[pltpu.VMEM((2 * dhp * LANE, LANE), F32), pltpu.VMEM((2, LANE, LANE), F32)],
        compiler_params=_cparams(("arbitrary", "arbitrary")),
        name="hyena_conv",
    )(z, gate, skip, hf, *consts)


def _filter_mlp_kernel(ft_ref, w1_ref, b1_ref, w2_ref, b2_ref, fr_ref, w3_ref, dl_ref, k_ref, s_ref):
    i = pl.program_id(0)
    ft = ft_ref[...]

    def mm(x, w):
        x_hi, x_lo = _split(x)
        w_hi, w_lo = _split(w)
        return (jnp.dot(x_hi, w_hi, preferred_element_type=F32) + jnp.dot(x_lo, w_hi, preferred_element_type=F32)
                + jnp.dot(x_hi, w_lo, preferred_element_type=F32))

    hid = jnp.sin(fr_ref[0:1, :] * (mm(ft, w1_ref[...]) + b1_ref[...]))
    hid = jnp.sin(fr_ref[1:2, :] * (mm(hid, w2_ref[...]) + b2_ref[...]))
    h = mm(hid, w3_ref[0])
    k = h * jnp.exp(-ft[:, 0:1] * dl_ref[...]) * ft[:, LANE - 1:LANE]
    k_ref[...] = k

    @pl.when(i == 0)
    def _():
        s_ref[...] = jnp.zeros(s_ref.shape, F32)
    s_ref[...] += jnp.sum(jnp.abs(k), axis=0, keepdims=True)


def _hyena_filter_taps(n_tokens, w1, b1, w2, b2, freq, w3):
    L = n_tokens
    n = 2 * L
    idx = jnp.arange(n)
    m = jnp.where(idx < L, idx, n - idx).astype(F32)
    t = m / max(L - 1, 1)
    bands = jnp.linspace(1e-4, HY_BANDS - 1, HY_BANDS, dtype=F32)
    ang = (2.0 * math.pi / L) * m[:, None] * bands
    n_feat = 2 * HY_BANDS + 1
    feats = jnp.concatenate([t[:, None], jnp.cos(ang), -jnp.sin(ang),
                             jnp.zeros((n, LANE - n_feat - 1), F32),
                             (idx != L).astype(F32)[:, None]], axis=-1)
    w1p = jnp.pad(w1, ((0, LANE - n_feat), (0, 0)))
    cw = HY_ORDER * HY_WIDTH
    w3d = jnp.moveaxis(w3.reshape(HY_HIDDEN, 2, cw), 1, 0)
    deltas = jnp.abs(jnp.linspace(math.log(HY_TARGET) / HY_SLOW_PCT, math.log(HY_TARGET) / HY_FAST_PCT,
                                  HY_WIDTH, dtype=F32))
    dl = jnp.tile(deltas, HY_ORDER)[None, :]
    tr = min(1024, L)
    nt = n // tr
    const = lambda a: pl.BlockSpec(a.shape, lambda i: (0,) * a.ndim)
    b1r, b2r = b1[None, :], b2[None, :]
    return pl.pallas_call(
        _filter_mlp_kernel,
        grid=(nt,),
        in_specs=[pl.BlockSpec((tr, LANE), lambda i: (i, 0)), const(w1p), const(b1r), const(w2), const(b2r),
                  const(freq), pl.BlockSpec((1, HY_HIDDEN, cw), lambda i: (i // (nt // 2), 0, 0)), const(dl)],
        out_specs=[pl.BlockSpec((tr, cw), lambda i: (i, 0)), pl.BlockSpec((1, cw), lambda i: (0, 0))],
        out_shape=[jax.ShapeDtypeStruct((n, cw), F32), jax.ShapeDtypeStruct((1, cw), F32)],
        compiler_params=_cparams(("arbitrary",)),
        name="hyena_filter_mlp",
    )(feats, w1p, b1r, w2, b2r, freq, w3d, dl)


def _hyena_long(z, x1, x2, taps, norms, skip):
    b, l, w = z.shape
    nb = 2 * l // LANE
    hf = _hyena_spectrum(taps, 1.0 / norms)
    cst = _dft_consts(nb, nb // 2)
    tiles = w // LANE
    for o, gate in enumerate((x1, x2)):
        z = _hyena_conv(z, gate, skip[o][None, :], hf[o * tiles:(o + 1) * tiles], cst)
    return z


def _dense_dft_consts(length):
    n = 2 * length
    h = length + 1
    hp = -(-h // 8) * 8
    k = np.arange(h)[:, None]
    pos = np.arange(n)[None, :]
    ang = 2 * np.pi * ((k * pos) % n) / n
    fwd = np.zeros((2 * hp, n))
    fwd[:h] = np.cos(ang)
    fwd[hp:hp + h] = -np.sin(ang)
    wk = np.where((k == 0) | (k == length), 1.0, 2.0) / n
    inv = np.zeros((length, 2 * hp))
    inv[:, :h] = (wk * np.cos(ang[:, :length])).T
    inv[:, hp:hp + h] = (-wk * np.sin(ang[:, :length])).T
    return dict(hp=hp, full=_np_split(fwd), fwd=_np_split(fwd[:, :length]), inv=_np_split(inv))


def _hyena_short_kernel(z_ref, x1_ref, x2_ref, k_ref, inv_ref, skip_ref, ffh_ref, ffl_ref, fh_ref, fl_ref,
                        fih_ref, fil_ref, o_ref, *, hp):
    z = z_ref[0]
    w = z.shape[1]
    for o, gate_ref in enumerate((x1_ref, x2_ref)):
        cols = slice(o * w, (o + 1) * w)
        hf = _cdot(ffh_ref[...], ffl_ref[...], k_ref[:, cols], passes=3) * inv_ref[:, cols]
        x = _cdot(fh_ref[...], fl_ref[...], z, passes=3)
        xr, xi, hr, hi = x[:hp], x[hp:], hf[:hp], hf[hp:]
        zz = jnp.concatenate([xr * hr - xi * hi, xr * hi + xi * hr], axis=0)
        conv = _cdot(fih_ref[...], fil_ref[...], zz, passes=3)
        z = gate_ref[0] * (conv + z * skip_ref[o:o + 1, :])
    o_ref[0] = z


def _hyena_short(z, x1, x2, taps, norms, skip):
    b, l, w = z.shape
    cst = _dense_dft_consts(l)
    consts = [*cst["full"], *cst["fwd"], *cst["inv"]]
    inv_norm = 1.0 / norms
    const = lambda a: pl.BlockSpec(a.shape, lambda bi: (0,) * a.ndim)
    tile = pl.BlockSpec((1, l, w), lambda bi: (bi, 0, 0))
    return pl.pallas_call(
        functools.partial(_hyena_short_kernel, hp=cst["hp"]),
        grid=(b,),
        in_specs=[tile, tile, tile, const(taps), const(inv_norm), const(skip)] + [const(a) for a in consts],
        out_specs=tile,
        out_shape=jax.ShapeDtypeStruct((b, l, w), F32),
        compiler_params=_cparams(("arbitrary",)),
        name="hyena_short",
    )(z, x1, x2, taps, inv_norm, skip, *consts)


def _rope_tables(rows, rot_dim, head_lanes, rope_off, identity_rows):
    half = rot_dim // 2
    n_freq = rot_dim // 4
    pos = np.arange(rows * GRID_W)
    inv_freq = ROPE_THETA ** (-np.arange(n_freq, dtype=np.float64) / n_freq)
    ang = np.concatenate([(pos // GRID_W)[:, None] * inv_freq, (pos % GRID_W)[:, None] * inv_freq], axis=-1)
    cos, sin = np.cos(ang), np.sin(ang)
    n = pos.shape[0]
    ct = np.ones((n, LANE))
    sa = np.zeros((n, LANE))
    sb = np.zeros((n, LANE))
    for h0 in range(0, LANE, head_lanes):
        lo = h0 + rope_off
        ct[:, lo:lo + half] = cos
        ct[:, lo + half:lo + rot_dim] = cos
        sa[:, lo:lo + half] = -sin
        sb[:, lo + half:lo + rot_dim] = sin
    ident = (np.ones((identity_rows, LANE)), np.zeros((identity_rows, LANE)), np.zeros((identity_rows, LANE)))
    lat = tuple(jnp.asarray(t, F32) for t in (ct, sa, sb))
    ctx = tuple(jnp.asarray(t, F32) for t in ident)
    return lat, ctx


def _pack_w_in(w):
    d = w.shape[0]
    o = 0
    mq = w[:, o:o + MLA_HEADS * (MLA_NOPE + MLA_ROPE)]; o += MLA_HEADS * (MLA_NOPE + MLA_ROPE)
    mckv = w[:, o:o + KV_RANK]; o += KV_RANK
    mkr = w[:, o:o + MLA_ROPE]; o += MLA_ROPE
    sq = w[:, o:o + SWA_HEADS * SWA_HEAD_DIM]; o += SWA_HEADS * SWA_HEAD_DIM
    sk = w[:, o:o + SWA_KV_HEADS * SWA_HEAD_DIM]; o += SWA_KV_HEADS * SWA_HEAD_DIM
    sv = w[:, o:o + SWA_KV_HEADS * SWA_HEAD_DIM]; o += SWA_KV_HEADS * SWA_HEAD_DIM
    hy = w[:, o:o + (HY_ORDER + 1) * HY_WIDTH]; o += (HY_ORDER + 1) * HY_WIDTH
    gt = w[:, o:]
    pad_q = HEAD_PAD - MLA_NOPE - MLA_ROPE
    mq = jnp.pad(mq.reshape(d, MLA_HEADS, MLA_NOPE + MLA_ROPE), ((0, 0), (0, 0), (0, pad_q))).reshape(d, -1)
    mkr = jnp.pad(mkr, ((0, 0), (MLA_NOPE, pad_q)))
    dup = lambda t: jnp.repeat(t.reshape(d, SWA_KV_HEADS, 1, SWA_HEAD_DIM), 2, axis=2).reshape(d, -1)
    return jnp.concatenate([mq, mckv, mkr, sq, dup(sk), dup(sv), hy, gt], axis=1).astype(BF16)


def _pack_w_kv(w):
    r = w.shape[0]
    w = w.reshape(r, MLA_HEADS, MLA_NOPE + MLA_V)
    k = jnp.pad(w[..., :MLA_NOPE], ((0, 0), (0, 0), (0, HEAD_PAD - MLA_NOPE))).reshape(r, -1)
    v = w[..., MLA_NOPE:].reshape(r, -1)
    return jnp.concatenate([k, v], axis=1).astype(BF16)


def kernel(x, c, ctx, c_ctx, w_mod, b_mod, norm_g, w_in, kv_norm_g, w_kv_up, swa_sink, hy_conv_w, hy_conv_b,
           hy_w1, hy_b1, hy_w2, hy_b2, hy_freq, hy_w3, hy_skip, w_branch, w_out, w_up, ffn_conv_w, ffn_conv_b,
           w_down):
    b, s, d = x.shape
    n_ctx = ctx.shape[1]
    depth = w_mod.shape[0]
    rows = s // GRID_W
    rope_m, rope_m_ctx = _rope_tables(rows, MLA_ROPE, HEAD_PAD, MLA_NOPE, n_ctx)
    rope_s, rope_s_ctx = _rope_tables(rows, SWA_HEAD_DIM, SWA_HEAD_DIM, 0, n_ctx)

    cvec = jnp.concatenate([c, c_ctx[None, :], jnp.zeros((8 - b - 1, d), F32)], axis=0)
    mod_all = _modulation(cvec, w_mod, b_mod)

    x_lat, x_ctx = x, ctx
    for l in range(depth):
        with_ctx = l < depth - 1
        m = mod_all[l].reshape(8, 6, d)
        lat = [m[:b, k][:, None, :] for k in range(6)]
        cx = [jnp.broadcast_to(m[b, k][None, None, :], (b, 1, d)) for k in range(6)]
        ng = [norm_g[l, k][None, :] for k in range(4)]
        w_pack = _pack_w_in(w_in[l])
        wkv_pack = _pack_w_kv(w_kv_up[l])
        kvg = kv_norm_g[l][None, :]
        cw, cb = hy_conv_w[l], hy_conv_b[l][None, :]

        q, kk, vv, sq, sk, sv, z, x1, x2, gt = _in_proj(
            x_lat, lat[0], lat[1], ng[0], w_pack, wkv_pack, kvg, rope_m, rope_s, cw, cb)
        qc, kkc, vvc, sqc, skc, svc, zc, x1c, x2c, gtc = _in_proj(
            x_ctx, cx[0], cx[1], ng[0], w_pack, wkv_pack, kvg, rope_m_ctx, rope_s_ctx, cw, cb)

        y_a = _mla(q, kkc, vvc, kk, vv)
        y_b = _swa(swa_sink[l], sq, skc, svc, sk, sv)
        hy_mlp = (hy_w1[l], hy_b1[l], hy_w2[l], hy_b2[l], hy_freq[l], hy_w3[l])
        y_c = _hyena_long(z, x1, x2, *_hyena_filter_taps(s, *hy_mlp), hy_skip[l])

        wb = w_branch[l].astype(BF16)
        wo = w_out[l].astype(BF16)
        wu = w_up[l].astype(BF16)
        wd = w_down[l].astype(BF16)
        fcw, fcb = ffn_conv_w[l], ffn_conv_b[l][None, :]

        x_lat = _merge(x_lat, y_a, y_b, y_c, gt, wb, wo, ng[1], lat[2])
        x_lat = _ffn(x_lat, lat[3], lat[4], lat[5], ng[2], ng[3], wu, fcw, fcb, wd)

        if with_ctx:
            yc_a = _mla(qc, kkc, vvc)
            yc_b = _swa(swa_sink[l], sqc, skc, svc)
            yc_c = _hyena_short(zc, x1c, x2c, *_hyena_filter_taps(n_ctx, *hy_mlp), hy_skip[l])
            x_ctx = _merge(x_ctx, yc_a, yc_b, yc_c, gtc, wb, wo, ng[1], cx[2])
            x_ctx = _ffn(x_ctx, cx[3], cx[4], cx[5], ng[2], ng[3], wu, fcw, fcb, wd)
    return x_lat
```

```python
import functools
import math

import numpy as np
import jax
import jax.numpy as jnp
from jax import lax
from jax.experimental import pallas as pl
from jax.experimental.pallas import tpu as pltpu

F32 = jnp.float32
BF16 = jnp.bfloat16

GRID_W = 64
EPS = 1e-6
ROPE_THETA = 10000.0
NEG_INF = -1e30
MLA_HEADS = 8
MLA_NOPE = 64
MLA_ROPE = 32
MLA_V = 64
KV_RANK = 256
MLA_SCALE = (MLA_NOPE + MLA_ROPE) ** -0.5
LOG2E = math.log2(math.e)
SWA_HEADS = 8
SWA_KV_HEADS = 2
SWA_HEAD_DIM = 64
SWA_BLOCK = 128
SWA_SCALE = SWA_HEAD_DIM ** -0.5
HY_WIDTH = 512
HY_ORDER = 2
HY_BANDS = 16
HY_HIDDEN = 64
HY_TARGET = 1e-2
HY_FAST_PCT = 0.3
HY_SLOW_PCT = 1.5
N_BRANCH = 3
BRANCH_WIDTH = 512
LANE = 128
HEAD_PAD = 128

VMEM_LIMIT = 56 * 1024 * 1024
IN_PROJ_ROWS = 512
FFN_ROWS = 512
FFN_CHUNKS = 2

C_Q = 0
C_CKV = C_Q + MLA_HEADS * HEAD_PAD
C_KR = C_CKV + KV_RANK
C_SQ = C_KR + HEAD_PAD
C_SK = C_SQ + SWA_HEADS * SWA_HEAD_DIM
C_SV = C_SK + 2 * SWA_KV_HEADS * SWA_HEAD_DIM
C_HY = C_SV + 2 * SWA_KV_HEADS * SWA_HEAD_DIM
C_GT = C_HY + (HY_ORDER + 1) * HY_WIDTH


def _cparams(sem):
    return pltpu.CompilerParams(dimension_semantics=sem, vmem_limit_bytes=VMEM_LIMIT)


def _rms(xf, g):
    return xf * lax.rsqrt(jnp.mean(xf * xf, axis=-1, keepdims=True) + EPS) * g


def _rope(x, cos, sa, sb, half):
    return x * cos + pltpu.roll(x, LANE - half, 1) * sa + pltpu.roll(x, half, 1) * sb


def _dwconv3(p, prev_row, next_row, cw_ref, cb_ref):
    tm = p.shape[0]
    rows = lax.broadcasted_iota(jnp.int32, (tm, 1), 0)
    up = jnp.where(rows == 0, prev_row, pltpu.roll(p, 1, 0))
    dn = jnp.where(rows == tm - 1, next_row, pltpu.roll(p, tm - 1, 0))
    return up * cw_ref[0:1, :] + p * cw_ref[1:2, :] + dn * cw_ref[2:3, :] + cb_ref[...]


def _mod_kernel(c_ref, w_ref, b_ref, o_ref):
    c = c_ref[...]
    a = c * jax.nn.sigmoid(c)
    a_hi = a.astype(BF16)
    a_lo = (a - a_hi.astype(F32)).astype(BF16)
    w = w_ref[0]
    w_hi = w.astype(BF16)
    w_lo = (w - w_hi.astype(F32)).astype(BF16)
    acc = jnp.dot(a_hi, w_hi, preferred_element_type=F32)
    acc += jnp.dot(a_lo, w_hi, preferred_element_type=F32)
    acc += jnp.dot(a_hi, w_lo, preferred_element_type=F32)
    o_ref[0] = acc + b_ref[0]


def _modulation(cvec, w_mod, b_mod):
    depth, d, n = w_mod.shape
    tn = 1536
    return pl.pallas_call(
        _mod_kernel,
        grid=(depth, n // tn),
        in_specs=[pl.BlockSpec((8, d), lambda l, j: (0, 0)),
                  pl.BlockSpec((1, d, tn), lambda l, j: (l, 0, j)),
                  pl.BlockSpec((1, 1, tn), lambda l, j: (l, 0, j))],
        out_specs=pl.BlockSpec((1, 8, tn), lambda l, j: (l, 0, j)),
        out_shape=jax.ShapeDtypeStruct((depth, 8, n), F32),
        compiler_params=_cparams(("arbitrary", "arbitrary")),
        name="adaln_mod",
    )(cvec, w_mod, b_mod.reshape(depth, 1, n))


def _in_proj_kernel(x_ref, xp_ref, xn_ref, sh_ref, sc_ref, g_ref, w_ref, wkv_ref, kvg_ref,
                    cm_ref, sma_ref, smb_ref, cs_ref, ssa_ref, ssb_ref, cw_ref, cb_ref,
                    q_ref, kk_ref, vv_ref, sq_ref, sk_ref, sv_ref, z_ref, x1_ref, x2_ref, gt_ref):
    i = pl.program_id(1)
    nt = pl.num_programs(1)
    g = g_ref[...]
    sh = sh_ref[0]
    sc = sc_ref[0]

    def mod(xf):
        return (_rms(xf, g) * (1.0 + sc) + sh).astype(BF16)

    tm = x_ref.shape[1]
    hb_ext = mod(jnp.concatenate([x_ref[0], xp_ref[0], xn_ref[0]], axis=0))
    hb = hb_ext[:tm]

    def proj(lo, hi, lhs=hb):
        return jnp.dot(lhs, w_ref[:, lo:hi], preferred_element_type=F32)

    cm, sma, smb = cm_ref[...], sma_ref[...], smb_ref[...]
    cs, ssa, ssb = cs_ref[...], ssa_ref[...], ssb_ref[...]

    pq = proj(C_Q, C_CKV)
    for h in range(MLA_HEADS):
        xh = pq[:, HEAD_PAD * h:HEAD_PAD * (h + 1)]
        q_ref[0, :, HEAD_PAD * h:HEAD_PAD * (h + 1)] = (
            _rope(xh, cm, sma, smb, MLA_ROPE // 2) * (MLA_SCALE * LOG2E)).astype(BF16)

    ckv = proj(C_CKV, C_KR)
    cn = _rms(ckv, kvg_ref[...]).astype(BF16)
    kv = jnp.dot(cn, wkv_ref[...], preferred_element_type=F32)
    krr = _rope(proj(C_KR, C_SQ), cm, sma, smb, MLA_ROPE // 2)
    for h in range(MLA_HEADS):
        kk_ref[0, :, HEAD_PAD * h:HEAD_PAD * (h + 1)] = (
            kv[:, HEAD_PAD * h:HEAD_PAD * (h + 1)] + krr).astype(BF16)
    ones = jnp.ones((hb.shape[0], LANE), BF16)
    for j in range(MLA_HEADS // 2):
        v0 = MLA_HEADS * HEAD_PAD + LANE * j
        vv_ref[0, :, 2 * LANE * j:2 * LANE * j + LANE] = kv[:, v0:v0 + LANE].astype(BF16)
        vv_ref[0, :, 2 * LANE * j + LANE:2 * LANE * (j + 1)] = ones

    psq = proj(C_SQ, C_SK)
    for j in range(SWA_HEADS // 2):
        xh = psq[:, LANE * j:LANE * (j + 1)]
        sq_ref[0, :, LANE * j:LANE * (j + 1)] = (
            _rope(xh, cs, ssa, ssb, SWA_HEAD_DIM // 2) * (SWA_SCALE * LOG2E)).astype(BF16)
    psk = proj(C_SK, C_SV)
    lane = lax.broadcasted_iota(jnp.int32, (hb.shape[0], LANE), 1)
    for k in range(SWA_KV_HEADS):
        rk = _rope(psk[:, LANE * k:LANE * (k + 1)], cs, ssa, ssb, SWA_HEAD_DIM // 2)
        sk_ref[0, :, 2 * LANE * k:2 * LANE * k + LANE] = jnp.where(lane < 64, rk, 0.0).astype(BF16)
        sk_ref[0, :, 2 * LANE * k + LANE:2 * LANE * (k + 1)] = jnp.where(lane >= 64, rk, 0.0).astype(BF16)
    sv_ref[0] = proj(C_SV, C_HY).astype(BF16)

    ph = proj(C_HY, C_GT, hb_ext)
    pprev = jnp.where(i > 0, ph[tm + 7:tm + 8, :], 0.0)
    pnext = jnp.where(i < nt - 1, ph[tm + 8:tm + 9, :], 0.0)
    u = _dwconv3(ph[:tm], pprev, pnext, cw_ref, cb_ref)
    z_ref[0] = u[:, :HY_WIDTH]
    x1_ref[0] = u[:, HY_WIDTH:2 * HY_WIDTH]
    x2_ref[0] = u[:, 2 * HY_WIDTH:]

    gt_ref[0] = jax.nn.sigmoid(proj(C_GT, w_ref.shape[1])).astype(BF16)


def _in_proj(x, sh, sc, g, w, wkv, kvg, rope_m, rope_s, cw, cb):
    b, s, d = x.shape
    tm = min(IN_PROJ_ROWS, s)
    nt = s // tm
    r8 = tm // 8
    row = lambda width: pl.BlockSpec((1, tm, width), lambda bi, i: (bi, i, 0))
    const2 = lambda a: pl.BlockSpec(a.shape, lambda bi, i: (0, 0), pipeline_mode=pl.Buffered(1))
    tab = pl.BlockSpec((tm, LANE), lambda bi, i: (i, 0))
    in_specs = [
        row(d),
        pl.BlockSpec((1, 8, d), lambda bi, i: (bi, jnp.maximum(i * r8 - 1, 0), 0)),
        pl.BlockSpec((1, 8, d), lambda bi, i: (bi, jnp.minimum((i + 1) * r8, s // 8 - 1), 0)),
        pl.BlockSpec((1, 1, d), lambda bi, i: (bi, 0, 0)),
        pl.BlockSpec((1, 1, d), lambda bi, i: (bi, 0, 0)),
        const2(g), const2(w), const2(wkv), const2(kvg),
        tab, tab, tab, tab, tab, tab,
        const2(cw), const2(cb),
    ]
    widths = [(MLA_HEADS * HEAD_PAD, BF16), (MLA_HEADS * HEAD_PAD, BF16), (MLA_HEADS * LANE, BF16),
              (SWA_HEADS * SWA_HEAD_DIM, BF16), (4 * LANE, BF16), (2 * LANE, BF16),
              (HY_WIDTH, F32), (HY_WIDTH, F32), (HY_WIDTH, F32), (N_BRANCH * d, BF16)]
    return pl.pallas_call(
        _in_proj_kernel,
        grid=(b, nt),
        in_specs=in_specs,
        out_specs=[row(wd) for wd, _ in widths],
        out_shape=[jax.ShapeDtypeStruct((b, s, wd), dt) for wd, dt in widths],
        compiler_params=_cparams(("arbitrary", "arbitrary")),
        name="in_proj",
    )(x, x, x, sh, sc, g, w, wkv, kvg, *rope_m, *rope_s, cw, cb)


def _mla_kernel(*refs, tk, n_lat, rb):
    if n_lat:
        q_ref, kc_ref, vc_ref, kl_ref, vl_ref, o_ref, m_scr, acc_scr = refs
    else:
        q_ref, kc_ref, vc_ref, o_ref, m_scr, acc_scr = refs
    tq = q_ref.shape[1]
    nt = (((1,), (1,)), ((), ()))
    m_scr[...] = jnp.full(m_scr.shape, NEG_INF, F32)
    acc_scr[...] = jnp.zeros(acc_scr.shape, F32)

    def step(k_ref, v_ref, off, n):
        v = v_ref[0, pl.ds(off, n), :]
        for e in range(2):
            k = k_ref[0, pl.ds(off, n), HEAD_PAD * e:HEAD_PAD * (e + 1)]
            for r in range(tq // rb):
                rows = pl.ds(r * rb, rb)
                s = lax.dot_general(q_ref[0, rows, HEAD_PAD * e:HEAD_PAD * (e + 1)], k, nt,
                                    preferred_element_type=F32)
                m_old = m_scr[e, rows, :]
                m_new = jnp.maximum(m_old, jnp.max(s, axis=1, keepdims=True))
                alpha = jnp.exp2(m_old - m_new)
                p = jnp.exp2(s - jnp.tile(m_new, (1, n // LANE))).astype(BF16)
                acc_scr[e, rows, :] = (jnp.tile(alpha, (1, 2)) * acc_scr[e, rows, :]
                                       + jnp.dot(p, v, preferred_element_type=F32))
                m_scr[e, rows, :] = m_new

    step(kc_ref, vc_ref, 0, kc_ref.shape[1])
    if n_lat:
        def body(j, carry):
            step(kl_ref, vl_ref, pl.multiple_of(j * tk, tk), tk)
            return carry
        lax.fori_loop(0, n_lat, body, 0)
    lane = lax.broadcasted_iota(jnp.int32, (tq, LANE), 1)
    o0 = acc_scr[0, :, :LANE] / acc_scr[0, :, LANE:]
    o1 = acc_scr[1, :, :LANE] / acc_scr[1, :, LANE:]
    o_ref[0] = jnp.where(lane < MLA_V, o0, o1).astype(o_ref.dtype)


def _mla(q, kc, vc, kl=None, vl=None, tq=2048, tk=512, rb=128):
    b, sq, _ = q.shape
    c = kc.shape[1]
    tq = min(tq, sq)
    rb = min(rb, tq)
    hp = MLA_HEADS // 2
    in_specs = [pl.BlockSpec((1, tq, 2 * HEAD_PAD), lambda bi, h, i: (bi, i, h)),
                pl.BlockSpec((1, c, 2 * HEAD_PAD), lambda bi, h, i: (bi, 0, h)),
                pl.BlockSpec((1, c, 2 * LANE), lambda bi, h, i: (bi, 0, h))]
    args = [q, kc, vc]
    n_lat = 0
    if kl is not None:
        s = kl.shape[1]
        tk = min(tk, s)
        n_lat = s // tk
        in_specs += [pl.BlockSpec((1, s, 2 * HEAD_PAD), lambda bi, h, i: (bi, 0, h)),
                     pl.BlockSpec((1, s, 2 * LANE), lambda bi, h, i: (bi, 0, h))]
        args += [kl, vl]
    return pl.pallas_call(
        functools.partial(_mla_kernel, tk=tk, n_lat=n_lat, rb=rb),
        grid=(b, hp, sq // tq),
        in_specs=in_specs,
        out_specs=pl.BlockSpec((1, tq, LANE), lambda bi, h, i: (bi, i, h)),
        out_shape=jax.ShapeDtypeStruct((b, sq, MLA_HEADS * MLA_V), BF16),
        scratch_shapes=[pltpu.VMEM((2, tq, LANE), F32), pltpu.VMEM((2, tq, 2 * LANE), F32)],
        compiler_params=_cparams(("arbitrary", "arbitrary", "arbitrary")),
        name="mla_attn",
    )(*args)


def _swa_kernel(*refs, band):
    if band:
        sink_ref, q_ref, kc_ref, vc_ref, kp_ref, k0_ref, kn_ref, vp_ref, v0_ref, vn_ref, o_ref = refs
        kband = jnp.concatenate([kp_ref[0], k0_ref[0], kn_ref[0]], axis=0)
        vband = jnp.concatenate([vp_ref[0], v0_ref[0], vn_ref[0]], axis=0)
    else:
        sink_ref, q_ref, kc_ref, vc_ref, o_ref = refs
    i = pl.program_id(1)
    nb = pl.num_programs(1)
    tq = q_ref.shape[1]
    nsub = tq // SWA_BLOCK
    n_ctx = kc_ref.shape[1]
    n_keys = n_ctx + (3 * SWA_BLOCK if band else 0)
    rows2 = 2 * SWA_BLOCK
    nt = (((1,), (1,)), ((), ()))
    lane = lax.broadcasted_iota(jnp.int32, (SWA_BLOCK, LANE), 1)
    row = lax.broadcasted_iota(jnp.int32, (rows2, 1), 0)
    if band:
        r_loc = lax.broadcasted_iota(jnp.int32, (rows2, n_keys), 0) % SWA_BLOCK
        col = lax.broadcasted_iota(jnp.int32, (rows2, n_keys), 1) - n_ctx
        prev_bias = jnp.where((col >= 0) & (col < SWA_BLOCK), NEG_INF, 0.0)
        next_bias = jnp.where(col >= 2 * SWA_BLOCK, NEG_INF, 0.0)
        base_bias = (jnp.where(r_loc > col, prev_bias, 0.0)
                     + jnp.where(col - 2 * SWA_BLOCK > r_loc, next_bias, 0.0))
    for r in range(nsub):
        rs = slice(r * SWA_BLOCK, (r + 1) * SWA_BLOCK)
        if band:
            bias = base_bias
            if r == 0:
                bias = jnp.minimum(bias, jnp.where(i == 0, prev_bias, 0.0))
            if r == nsub - 1:
                bias = jnp.minimum(bias, jnp.where(i == nb - 1, next_bias, 0.0))
            bs = slice(r * SWA_BLOCK, (r + 3) * SWA_BLOCK)
        for kvh in range(SWA_KV_HEADS):
            p0, p1 = 2 * kvh, 2 * kvh + 1
            qq = jnp.concatenate([q_ref[0, rs, LANE * p0:LANE * (p0 + 1)],
                                  q_ref[0, rs, LANE * p1:LANE * (p1 + 1)]], axis=0)
            vsl = slice(LANE * kvh, LANE * (kvh + 1))
            v = vc_ref[0, :, vsl]
            if band:
                v = jnp.concatenate([v, vband[bs, vsl]], axis=0)
            outs = []
            for e in range(2):
                ksl = slice(LANE * (2 * kvh + e), LANE * (2 * kvh + e + 1))
                k = kc_ref[0, :, ksl]
                if band:
                    k = jnp.concatenate([k, kband[bs, ksl]], axis=0)
                s = lax.dot_general(qq, k, nt, preferred_element_type=F32)
                if band:
                    s = s + bias
                sink = jnp.where(row < SWA_BLOCK, sink_ref[2 * p0 + e], sink_ref[2 * p1 + e]) * LOG2E
                m = jnp.maximum(jnp.max(s, axis=1, keepdims=True), sink)
                p = jnp.exp2(s - m)
                den = jnp.sum(p, axis=1, keepdims=True) + jnp.exp2(sink - m)
                outs.append(jnp.dot(p.astype(BF16), v, preferred_element_type=F32) / den)
            for t, pair in enumerate((p0, p1)):
                ts = slice(t * SWA_BLOCK, (t + 1) * SWA_BLOCK)
                o_ref[0, rs, LANE * pair:LANE * (pair + 1)] = jnp.where(
                    lane < SWA_HEAD_DIM, outs[0][ts], outs[1][ts]).astype(o_ref.dtype)


def _swa(sink, q, kc, vc, k=None, v=None, tq=512):
    b, sq, _ = q.shape
    c = kc.shape[1]
    tq = min(tq, sq)
    nb = sq // tq
    band = k is not None
    blk = lambda rows, width, f: pl.BlockSpec((1, rows, width), f)
    in_specs = [pl.BlockSpec(memory_space=pltpu.SMEM),
                blk(tq, SWA_HEADS * SWA_HEAD_DIM, lambda bi, i: (bi, i, 0)),
                pl.BlockSpec((1, c, 4 * LANE), lambda bi, i: (bi, 0, 0)),
                pl.BlockSpec((1, c, 2 * LANE), lambda bi, i: (bi, 0, 0))]
    args = [sink, q, kc, vc]
    if band:
        per = tq // SWA_BLOCK
        prev = lambda bi, i: (bi, jnp.maximum(i * per - 1, 0), 0)
        cur = lambda bi, i: (bi, i, 0)
        nxt = lambda bi, i: (bi, jnp.minimum((i + 1) * per, sq // SWA_BLOCK - 1), 0)
        in_specs += [blk(SWA_BLOCK, 4 * LANE, prev), blk(tq, 4 * LANE, cur), blk(SWA_BLOCK, 4 * LANE, nxt),
                     blk(SWA_BLOCK, 2 * LANE, prev), blk(tq, 2 * LANE, cur), blk(SWA_BLOCK, 2 * LANE, nxt)]
        args += [k, k, k, v, v, v]
    return pl.pallas_call(
        functools.partial(_swa_kernel, band=band),
        grid=(b, nb),
        in_specs=in_specs,
        out_specs=blk(tq, SWA_HEADS * SWA_HEAD_DIM, lambda bi, i: (bi, i, 0)),
        out_shape=jax.ShapeDtypeStruct((b, sq, SWA_HEADS * SWA_HEAD_DIM), BF16),
        compiler_params=_cparams(("arbitrary", "arbitrary")),
        name="swa_attn",
    )(*args)


def _merge_kernel(x_ref, ya_ref, yb_ref, yc_ref, gt_ref, wb_ref, wo_ref, ng_ref, g1_ref, o_ref):
    d = x_ref.shape[2]
    merged = None
    for k, y_ref in enumerate((ya_ref, yb_ref, yc_ref)):
        t = jnp.dot(y_ref[0].astype(BF16), wb_ref[k], preferred_element_type=F32)
        t = gt_ref[0, :, d * k:d * (k + 1)].astype(F32) * t
        merged = t if merged is None else merged + t
    y = jnp.dot(merged.astype(BF16), wo_ref[...], preferred_element_type=F32)
    o_ref[0] = x_ref[0] + g1_ref[0] * _rms(y, ng_ref[...])


def _merge(x, ya, yb, yc, gt, wb, wo, ng, g1):
    b, s, d = x.shape
    tm = min(512, s)
    row = lambda width: pl.BlockSpec((1, tm, width), lambda bi, i: (bi, i, 0))
    return pl.pallas_call(
        _merge_kernel,
        grid=(b, s // tm),
        in_specs=[row(d), row(BRANCH_WIDTH), row(BRANCH_WIDTH), row(BRANCH_WIDTH), row(N_BRANCH * d),
                  pl.BlockSpec(wb.shape, lambda bi, i: (0, 0, 0)),
                  pl.BlockSpec(wo.shape, lambda bi, i: (0, 0)),
                  pl.BlockSpec(ng.shape, lambda bi, i: (0, 0)),
                  pl.BlockSpec((1, 1, d), lambda bi, i: (bi, 0, 0))],
        out_specs=row(d),
        out_shape=jax.ShapeDtypeStruct((b, s, d), F32),
        compiler_params=_cparams(("arbitrary", "arbitrary")),
        name="merge",
    )(x, ya, yb, yc, gt, wb, wo, ng, g1)


def _ffn_kernel(x_ref, xp_ref, xn_ref, sh_ref, sc_ref, g2_ref, ng_in_ref, ng_out_ref,
                wu_ref, cw_ref, cb_ref, wd_ref, o_ref, *, chunk):
    i = pl.program_id(1)
    nt = pl.num_programs(1)
    g = ng_in_ref[...]
    sh = sh_ref[0]
    sc = sc_ref[0]

    def mod(xf):
        return (_rms(xf, g) * (1.0 + sc) + sh).astype(BF16)

    x = x_ref[0]
    tm = x.shape[0]
    hb = mod(jnp.concatenate([x, xp_ref[0], xn_ref[0]], axis=0))
    dff = wd_ref.shape[0]
    f = None
    for c0 in range(0, dff, chunk):
        halves = []
        for off in (c0, dff + c0):
            p = jnp.dot(hb, wu_ref[:, off:off + chunk], preferred_element_type=F32)
            pprev = jnp.where(i > 0, p[tm + 7:tm + 8, :], 0.0)
            pnext = jnp.where(i < nt - 1, p[tm + 8:tm + 9, :], 0.0)
            halves.append(_dwconv3(p[:tm], pprev, pnext, cw_ref.at[:, off:off + chunk],
                                   cb_ref.at[:, off:off + chunk]))
        a, bb = halves
        act = (a * jax.nn.sigmoid(a) * bb).astype(BF16)
        t = jnp.dot(act, wd_ref[c0:c0 + chunk, :], preferred_element_type=F32)
        f = t if f is None else f + t
    o_ref[0] = x + g2_ref[0] * _rms(f, ng_out_ref[...])


def _ffn(x, sh, sc, g2, ng_in, ng_out, wu, cw, cb, wd):
    b, s, d = x.shape
    tm = min(FFN_ROWS, s)
    r8 = tm // 8
    dff = wd.shape[0]
    chunk = dff // FFN_CHUNKS if dff % (FFN_CHUNKS * LANE) == 0 else dff
    row = pl.BlockSpec((1, tm, d), lambda bi, i: (bi, i, 0))
    vec = pl.BlockSpec((1, 1, d), lambda bi, i: (bi, 0, 0))
    const2 = lambda a: pl.BlockSpec(a.shape, lambda bi, i: (0, 0), pipeline_mode=pl.Buffered(1))
    return pl.pallas_call(
        functools.partial(_ffn_kernel, chunk=chunk),
        grid=(b, s // tm),
        in_specs=[row,
                  pl.BlockSpec((1, 8, d), lambda bi, i: (bi, jnp.maximum(i * r8 - 1, 0), 0)),
                  pl.BlockSpec((1, 8, d), lambda bi, i: (bi, jnp.minimum((i + 1) * r8, s // 8 - 1), 0)),
                  vec, vec, vec, const2(ng_in), const2(ng_out),
                  const2(wu), const2(cw), const2(cb), const2(wd)],
        out_specs=row,
        out_shape=jax.ShapeDtypeStruct((b, s, d), F32),
        compiler_params=_cparams(("arbitrary", "arbitrary")),
        name="conv_ffn",
    )(x, x, x, sh, sc, g2, ng_in, ng_out, wu, cw, cb, wd)


HY_PASSES = 1
HY_UNROLL = 8
HY_UNROLL2 = 5


def _split(x):
    hi = x.astype(BF16)
    return hi, (x - hi.astype(F32)).astype(BF16)


def _cdot(c_hi, c_lo, x, passes=HY_PASSES):
    x_hi, x_lo = _split(x)
    acc = jnp.dot(c_hi, x_hi, preferred_element_type=F32)
    if passes > 1:
        acc = acc + jnp.dot(c_lo, x_hi, preferred_element_type=F32)
        acc = acc + jnp.dot(c_hi, x_lo, preferred_element_type=F32)
    return acc


def _np_split(m):
    m = jnp.asarray(m, F32)
    hi = m.astype(BF16)
    return hi, (m - hi.astype(F32)).astype(BF16)


def _dft_consts(n_blocks, nonzero_blocks):
    nb = n_blocks
    n = LANE * nb
    dh = nb // 2 + 1
    dhp = -(-dh // 8) * 8
    d = np.arange(dh)[:, None]
    b = np.arange(nonzero_blocks)[None, :]
    ang = 2 * np.pi * ((d * b) % nb) / nb
    f1 = np.zeros((2 * dhp, nonzero_blocks))
    f1[:dh] = np.cos(ang)
    f1[dhp:dhp + dh] = -np.sin(ang)
    a = np.arange(LANE)
    ang2 = 2 * np.pi * ((a[:, None] * a[None, :]) % LANE) / LANE
    ar, ai = np.cos(ang2), -np.sin(ang2)
    f2 = np.block([[ar, -ai], [ai, ar]])
    f2inv = np.block([[ar, ai], [-ai, ar]])
    bo = np.arange(nb // 2)[:, None]
    do = np.arange(dh)[None, :]
    wd = np.where((do == 0) | (do == nb // 2), 1.0, 2.0) / n
    ang3 = 2 * np.pi * ((bo * do) % nb) / nb
    f3 = np.zeros((nb // 2, 2 * dhp))
    f3[:, :dh] = wd * np.cos(ang3)
    f3[:, dhp:dhp + dh] = -wd * np.sin(ang3)
    step_ang = 2 * np.pi * a / n
    tw_step = np.stack([np.broadcast_to(np.cos(step_ang)[:, None], (LANE, LANE)),
                        np.broadcast_to(-np.sin(step_ang)[:, None], (LANE, LANE))])
    return dict(dh=dh, dhp=dhp, f1=_np_split(f1), f2=_np_split(f2), f2inv=_np_split(f2inv),
                f3=_np_split(f3), tw_step=jnp.asarray(tw_step, F32))


def _stage1(x_ref, f1_hi, f1_lo, g_scr, n_in, n_out):
    def body(a, carry):
        xa = x_ref[pl.ds(a, n_in, stride=LANE), :]
        g_scr[pl.ds(a, n_out, stride=LANE), :] = _cdot(f1_hi, f1_lo, xa)
        return carry
    lax.fori_loop(0, LANE, body, 0, unroll=HY_UNROLL)


def _twiddle_init(tw_scr):
    tw_scr[0] = jnp.ones((LANE, LANE), F32)
    tw_scr[1] = jnp.zeros((LANE, LANE), F32)


def _twiddle_next(tw_scr, step_ref):
    twr, twi = tw_scr[0], tw_scr[1]
    sr, si = step_ref[0], step_ref[1]
    tw_scr[0] = twr * sr - twi * si
    tw_scr[1] = twr * si + twi * sr


def _spectrum_kernel(k_ref, inv_ref, f1h_ref, f1l_ref, f2h_ref, f2l_ref, step_ref, hf_ref, g_scr, tw_scr,
                     *, dh, dhp):
    nb = k_ref.shape[0] // LANE
    _stage1(k_ref, f1h_ref[...], f1l_ref[...], g_scr, nb, 2 * dhp)
    _twiddle_init(tw_scr)
    inv = inv_ref[...]

    def body(d, carry):
        gr = g_scr[pl.ds(pl.multiple_of(d * LANE, LANE), LANE), :]
        gi = g_scr[pl.ds(pl.multiple_of((dhp + d) * LANE, LANE), LANE), :]
        twr, twi = tw_scr[0], tw_scr[1]
        t = jnp.concatenate([gr * twr - gi * twi, gr * twi + gi * twr], axis=0)
        y = _cdot(f2h_ref[...], f2l_ref[...], t, passes=3)
        hf_ref[0, d] = y[:LANE] * inv
        hf_ref[1, d] = y[LANE:] * inv
        _twiddle_next(tw_scr, step_ref)
        return carry
    lax.fori_loop(0, dh, body, 0, unroll=HY_UNROLL2)


def _hyena_spectrum(kern, inv_norm):
    n, c = kern.shape
    nb = n // LANE
    cst = _dft_consts(nb, nb)
    dh, dhp = cst["dh"], cst["dhp"]
    const = lambda a: pl.BlockSpec(a.shape, lambda j: (0,) * a.ndim)
    consts = [*cst["f1"], *cst["f2"], cst["tw_step"]]
    return pl.pallas_call(
        functools.partial(_spectrum_kernel, dh=dh, dhp=dhp),
        grid=(c // LANE,),
        in_specs=[pl.BlockSpec((n, LANE), lambda j: (0, j)), pl.BlockSpec((1, LANE), lambda j: (0, j))]
        + [const(a) for a in consts],
        out_specs=pl.BlockSpec((None, 2, dh, LANE, LANE), lambda j: (j, 0, 0, 0, 0)),
        out_shape=jax.ShapeDtypeStruct((c // LANE, 2, dh, LANE, LANE), F32),
        scratch_shapes=[pltpu.VMEM((2 * dhp * LANE, LANE), F32), pltpu.VMEM((2, LANE, LANE), F32)],
        compiler_params=_cparams(("arbitrary",)),
        name="hyena_spectrum",
    )(kern, inv_norm, *consts)


def _conv_kernel(z_ref, gate_ref, skip_ref, hf_ref, f1h_ref, f1l_ref, f2h_ref, f2l_ref, f2ih_ref, f2il_ref,
                 f3h_ref, f3l_ref, step_ref, o_ref, g_scr, tw_scr, *, dh, dhp):
    nbh = z_ref.shape[1] // LANE
    zr = z_ref.at[0]
    _stage1(zr, f1h_ref[...], f1l_ref[...], g_scr, nbh, 2 * dhp)
    _twiddle_init(tw_scr)

    def body(d, carry):
        r0 = pl.multiple_of(d * LANE, LANE)
        r1 = pl.multiple_of((dhp + d) * LANE, LANE)
        gr = g_scr[pl.ds(r0, LANE), :]
        gi = g_scr[pl.ds(r1, LANE), :]
        twr, twi = tw_scr[0], tw_scr[1]
        t = jnp.concatenate([gr * twr - gi * twi, gr * twi + gi * twr], axis=0)
        y = _cdot(f2h_ref[...], f2l_ref[...], t)
        yr, yi = y[:LANE], y[LANE:]
        hr, hi = hf_ref[0, d], hf_ref[1, d]
        zz = jnp.concatenate([yr * hr - yi * hi, yr * hi + yi * hr], axis=0)
        u = _cdot(f2ih_ref[...], f2il_ref[...], zz)
        ur, ui = u[:LANE], u[LANE:]
        g_scr[pl.ds(r0, LANE), :] = ur * twr + ui * twi
        g_scr[pl.ds(r1, LANE), :] = ui * twr - ur * twi
        _twiddle_next(tw_scr, step_ref)
        return carry
    lax.fori_loop(0, dh, body, 0, unroll=HY_UNROLL2)

    skip = skip_ref[...]
    f3h, f3l = f3h_ref[...], f3l_ref[...]

    def out_body(a, carry):
        ga = g_scr[pl.ds(a, 2 * dhp, stride=LANE), :]
        conv = _cdot(f3h, f3l, ga)
        rows = pl.ds(a, nbh, stride=LANE)
        o_ref[0, rows, :] = gate_ref[0, rows, :] * (conv + z_ref[0, rows, :] * skip)
        return carry
    lax.fori_loop(0, LANE, out_body, 0, unroll=HY_UNROLL)


def _hyena_conv(z, gate, skip, hf, cst):
    b, l, w = z.shape
    dh, dhp = cst["dh"], cst["dhp"]
    consts = [*cst["f1"], *cst["f2"], *cst["f2inv"], *cst["f3"], cst["tw_step"]]
    const = lambda a: pl.BlockSpec(a.shape, lambda j, bi: (0,) * a.ndim)
    tile = pl.BlockSpec((1, l, LANE), lambda j, bi: (bi, 0, j))
    return pl.pallas_call(
        functools.partial(_conv_kernel, dh=dh, dhp=dhp),
        grid=(w // LANE, b),
        in_specs=[tile, tile, pl.BlockSpec((1, LANE), lambda j, bi: (0, j)),
                  pl.BlockSpec((None, 2, dh, LANE, LANE), lambda j, bi: (j, 0, 0, 0, 0))]
        + [const(a) for a in consts],
        out_specs=tile,
        out_shape=jax.ShapeDtypeStruct((b, l, w), F32),
        scratch_shapes=[pltpu.VMEM((2 * dhp * LANE, LANE), F32), pltpu.VMEM((2, LANE, LANE), F32)],
        compiler_params=_cparams(("arbitrary", "arbitrary")),
        name="hyena_conv",
    )(z, gate, skip, hf, *consts)


def _filter_mlp_kernel(ft_ref, w1_ref, b1_ref, w2_ref, b2_ref, fr_ref, w3_ref, dl_ref, k_ref, s_ref):
    i = pl.program_id(0)
    ft = ft_ref[...]

    def mm(x, w):
        x_hi, x_lo = _split(x)
        w_hi, w_lo = _split(w)
        return (jnp.dot(x_hi, w_hi, preferred_element_type=F32) + jnp.dot(x_lo, w_hi, preferred_element_type=F32)
                + jnp.dot(x_hi, w_lo, preferred_element_type=F32))

    hid = jnp.sin(fr_ref[0:1, :] * (mm(ft, w1_ref[...]) + b1_ref[...]))
    hid = jnp.sin(fr_ref[1:2, :] * (mm(hid, w2_ref[...]) + b2_ref[...]))
    h = mm(hid, w3_ref[0])
    k = h * jnp.exp(-ft[:, 0:1] * dl_ref[...]) * ft[:, LANE - 1:LANE]
    k_ref[...] = k

    @pl.when(i == 0)
    def _():
        s_ref[...] = jnp.zeros(s_ref.shape, F32)
    s_ref[...] += jnp.sum(jnp.abs(k), axis=0, keepdims=True)


def _hyena_filter_taps(n_tokens, w1, b1, w2, b2, freq, w3):
    L = n_tokens
    n = 2 * L
    idx = jnp.arange(n)
    m = jnp.where(idx < L, idx, n - idx).astype(F32)
    t = m / max(L - 1, 1)
    bands = jnp.linspace(1e-4, HY_BANDS - 1, HY_BANDS, dtype=F32)
    ang = (2.0 * math.pi / L) * m[:, None] * bands
    n_feat = 2 * HY_BANDS + 1
    feats = jnp.concatenate([t[:, None], jnp.cos(ang), -jnp.sin(ang),
                             jnp.zeros((n, LANE - n_feat - 1), F32),
                             (idx != L).astype(F32)[:, None]], axis=-1)
    w1p = jnp.pad(w1, ((0, LANE - n_feat), (0, 0)))
    cw = HY_ORDER * HY_WIDTH
    w3d = jnp.moveaxis(w3.reshape(HY_HIDDEN, 2, cw), 1, 0)
    deltas = jnp.abs(jnp.linspace(math.log(HY_TARGET) / HY_SLOW_PCT, math.log(HY_TARGET) / HY_FAST_PCT,
                                  HY_WIDTH, dtype=F32))
    dl = jnp.tile(deltas, HY_ORDER)[None, :]
    tr = min(1024, L)
    nt = n // tr
    const = lambda a: pl.BlockSpec(a.shape, lambda i: (0,) * a.ndim)
    b1r, b2r = b1[None, :], b2[None, :]
    return pl.pallas_call(
        _filter_mlp_kernel,
        grid=(nt,),
        in_specs=[pl.BlockSpec((tr, LANE), lambda i: (i, 0)), const(w1p), const(b1r), const(w2), const(b2r),
                  const(freq), pl.BlockSpec((1, HY_HIDDEN, cw), lambda i: (i // (nt // 2), 0, 0)), const(dl)],
        out_specs=[pl.BlockSpec((tr, cw), lambda i: (i, 0)), pl.BlockSpec((1, cw), lambda i: (0, 0))],
        out_shape=[jax.ShapeDtypeStruct((n, cw), F32), jax.ShapeDtypeStruct((1, cw), F32)],
        compiler_params=_cparams(("arbitrary",)),
        name="hyena_filter_mlp",
    )(feats, w1p, b1r, w2, b2r, freq, w3d, dl)


def _hyena_long(z, x1, x2, taps, norms, skip):
    b, l, w = z.shape
    nb = 2 * l // LANE
    hf = _hyena_spectrum(taps, 1.0 / norms)
    cst = _dft_consts(nb, nb // 2)
    tiles = w // LANE
    for o, gate in enumerate((x1, x2)):
        z = _hyena_conv(z, gate, skip[o][None, :], hf[o * tiles:(o + 1) * tiles], cst)
    return z


def _dense_dft_consts(length):
    n = 2 * length
    h = length + 1
    hp = -(-h // 8) * 8
    k = np.arange(h)[:, None]
    pos = np.arange(n)[None, :]
    ang = 2 * np.pi * ((k * pos) % n) / n
    fwd = np.zeros((2 * hp, n))
    fwd[:h] = np.cos(ang)
    fwd[hp:hp + h] = -np.sin(ang)
    wk = np.where((k == 0) | (k == length), 1.0, 2.0) / n
    inv = np.zeros((length, 2 * hp))
    inv[:, :h] = (wk * np.cos(ang[:, :length])).T
    inv[:, hp:hp + h] = (-wk * np.sin(ang[:, :length])).T
    return dict(hp=hp, full=_np_split(fwd), fwd=_np_split(fwd[:, :length]), inv=_np_split(inv))


def _hyena_short_kernel(z_ref, x1_ref, x2_ref, k_ref, inv_ref, skip_ref, ffh_ref, ffl_ref, fh_ref, fl_ref,
                        fih_ref, fil_ref, o_ref, *, hp):
    z = z_ref[0]
    w = z.shape[1]
    for o, gate_ref in enumerate((x1_ref, x2_ref)):
        cols = slice(o * w, (o + 1) * w)
        hf = _cdot(ffh_ref[...], ffl_ref[...], k_ref[:, cols], passes=3) * inv_ref[:, cols]
        x = _cdot(fh_ref[...], fl_ref[...], z, passes=3)
        xr, xi, hr, hi = x[:hp], x[hp:], hf[:hp], hf[hp:]
        zz = jnp.concatenate([xr * hr - xi * hi, xr * hi + xi * hr], axis=0)
        conv = _cdot(fih_ref[...], fil_ref[...], zz, passes=3)
        z = gate_ref[0] * (conv + z * skip_ref[o:o + 1, :])
    o_ref[0] = z


def _hyena_short(z, x1, x2, taps, norms, skip):
    b, l, w = z.shape
    cst = _dense_dft_consts(l)
    consts = [*cst["full"], *cst["fwd"], *cst["inv"]]
    inv_norm = 1.0 / norms
    const = lambda a: pl.BlockSpec(a.shape, lambda bi: (0,) * a.ndim)
    tile = pl.BlockSpec((1, l, w), lambda bi: (bi, 0, 0))
    return pl.pallas_call(
        functools.partial(_hyena_short_kernel, hp=cst["hp"]),
        grid=(b,),
        in_specs=[tile, tile, tile, const(taps), const(inv_norm), const(skip)] + [const(a) for a in consts],
        out_specs=tile,
        out_shape=jax.ShapeDtypeStruct((b, l, w), F32),
        compiler_params=_cparams(("arbitrary",)),
        name="hyena_short",
    )(z, x1, x2, taps, inv_norm, skip, *consts)


def _rope_tables(rows, rot_dim, head_lanes, rope_off, identity_rows):
    half = rot_dim // 2
    n_freq = rot_dim // 4
    pos = np.arange(rows * GRID_W)
    inv_freq = ROPE_THETA ** (-np.arange(n_freq, dtype=np.float64) / n_freq)
    ang = np.concatenate([(pos // GRID_W)[:, None] * inv_freq, (pos % GRID_W)[:, None] * inv_freq], axis=-1)
    cos, sin = np.cos(ang), np.sin(ang)
    n = pos.shape[0]
    ct = np.ones((n, LANE))
    sa = np.zeros((n, LANE))
    sb = np.zeros((n, LANE))
    for h0 in range(0, LANE, head_lanes):
        lo = h0 + rope_off
        ct[:, lo:lo + half] = cos
        ct[:, lo + half:lo + rot_dim] = cos
        sa[:, lo:lo + half] = -sin
        sb[:, lo + half:lo + rot_dim] = sin
    ident = (np.ones((identity_rows, LANE)), np.zeros((identity_rows, LANE)), np.zeros((identity_rows, LANE)))
    lat = tuple(jnp.asarray(t, F32) for t in (ct, sa, sb))
    ctx = tuple(jnp.asarray(t, F32) for t in ident)
    return lat, ctx


def _pack_w_in(w):
    d = w.shape[0]
    o = 0
    mq = w[:, o:o + MLA_HEADS * (MLA_NOPE + MLA_ROPE)]; o += MLA_HEADS * (MLA_NOPE + MLA_ROPE)
    mckv = w[:, o:o + KV_RANK]; o += KV_RANK
    mkr = w[:, o:o + MLA_ROPE]; o += MLA_ROPE
    sq = w[:, o:o + SWA_HEADS * SWA_HEAD_DIM]; o += SWA_HEADS * SWA_HEAD_DIM
    sk = w[:, o:o + SWA_KV_HEADS * SWA_HEAD_DIM]; o += SWA_KV_HEADS * SWA_HEAD_DIM
    sv = w[:, o:o + SWA_KV_HEADS * SWA_HEAD_DIM]; o += SWA_KV_HEADS * SWA_HEAD_DIM
    hy = w[:, o:o + (HY_ORDER + 1) * HY_WIDTH]; o += (HY_ORDER + 1) * HY_WIDTH
    gt = w[:, o:]
    pad_q = HEAD_PAD - MLA_NOPE - MLA_ROPE
    mq = jnp.pad(mq.reshape(d, MLA_HEADS, MLA_NOPE + MLA_ROPE), ((0, 0), (0, 0), (0, pad_q))).reshape(d, -1)
    mkr = jnp.pad(mkr, ((0, 0), (MLA_NOPE, pad_q)))
    dup = lambda t: jnp.repeat(t.reshape(d, SWA_KV_HEADS, 1, SWA_HEAD_DIM), 2, axis=2).reshape(d, -1)
    return jnp.concatenate([mq, mckv, mkr, sq, dup(sk), dup(sv), hy, gt], axis=1).astype(BF16)


def _pack_w_kv(w):
    r = w.shape[0]
    w = w.reshape(r, MLA_HEADS, MLA_NOPE + MLA_V)
    k = jnp.pad(w[..., :MLA_NOPE], ((0, 0), (0, 0), (0, HEAD_PAD - MLA_NOPE))).reshape(r, -1)
    v = w[..., MLA_NOPE:].reshape(r, -1)
    return jnp.concatenate([k, v], axis=1).astype(BF16)


def kernel(x, c, ctx, c_ctx, w_mod, b_mod, norm_g, w_in, kv_norm_g, w_kv_up, swa_sink, hy_conv_w, hy_conv_b,
           hy_w1, hy_b1, hy_w2, hy_b2, hy_freq, hy_w3, hy_skip, w_branch, w_out, w_up, ffn_conv_w, ffn_conv_b,
           w_down):
    b, s, d = x.shape
    n_ctx = ctx.shape[1]
    depth = w_mod.shape[0]
    rows = s // GRID_W
    rope_m, rope_m_ctx = _rope_tables(rows, MLA_ROPE, HEAD_PAD, MLA_NOPE, n_ctx)
    rope_s, rope_s_ctx = _rope_tables(rows, SWA_HEAD_DIM, SWA_HEAD_DIM, 0, n_ctx)

    cvec = jnp.concatenate([c, c_ctx[None, :], jnp.zeros((8 - b - 1, d), F32)], axis=0)
    mod_all = _modulation(cvec, w_mod, b_mod)

    x_lat, x_ctx = x, ctx
    for l in range(depth):
        with_ctx = l < depth - 1
        m = mod_all[l].reshape(8, 6, d)
        lat = [m[:b, k][:, None, :] for k in range(6)]
        cx = [jnp.broadcast_to(m[b, k][None, None, :], (b, 1, d)) for k in range(6)]
        ng = [norm_g[l, k][None, :] for k in range(4)]
        w_pack = _pack_w_in(w_in[l])
        wkv_pack = _pack_w_kv(w_kv_up[l])
        kvg = kv_norm_g[l][None, :]
        cw, cb = hy_conv_w[l], hy_conv_b[l][None, :]

        q, kk, vv, sq, sk, sv, z, x1, x2, gt = _in_proj(
            x_lat, lat[0], lat[1], ng[0], w_pack, wkv_pack, kvg, rope_m, rope_s, cw, cb)
        qc, kkc, vvc, sqc, skc, svc, zc, x1c, x2c, gtc = _in_proj(
            x_ctx, cx[0], cx[1], ng[0], w_pack, wkv_pack, kvg, rope_m_ctx, rope_s_ctx, cw, cb)

        y_a = _mla(q, kkc, vvc, kk, vv)
        y_b = _swa(swa_sink[l], sq, skc, svc, sk, sv)
        hy_mlp = (hy_w1[l], hy_b1[l], hy_w2[l], hy_b2[l], hy_freq[l], hy_w3[l])
        y_c = _hyena_long(z, x1, x2, *_hyena_filter_taps(s, *hy_mlp), hy_skip[l])

        wb = w_branch[l].astype(BF16)
        wo = w_out[l].astype(BF16)
        wu = w_up[l].astype(BF16)
        wd = w_down[l].astype(BF16)
        fcw, fcb = ffn_conv_w[l], ffn_conv_b[l][None, :]

        x_lat = _merge(x_lat, y_a, y_b, y_c, gt, wb, wo, ng[1], lat[2])
        x_lat = _ffn(x_lat, lat[3], lat[4], lat[5], ng[2], ng[3], wu, fcw, fcb, wd)

        if with_ctx:
            yc_a = _mla(qc, kkc, vvc)
            yc_b = _swa(swa_sink[l], sqc, skc, svc)
            yc_c = _hyena_short(zc, x1c, x2c, *_hyena_filter_taps(n_ctx, *hy_mlp), hy_skip[l])
            x_ctx = _merge(x_ctx, yc_a, yc_b, yc_c, gtc, wb, wo, ng[1], cx[2])
            x_ctx = _ffn(x_ctx, cx[3], cx[4], cx[5], ng[2], ng[3], wu, fcw, fcb, wd)
    return x_lat
```

```python
import functools
import math

import numpy as np
import jax
import jax.numpy as jnp
from jax import lax
from jax.experimental import pallas as pl
from jax.experimental.pallas import tpu as pltpu

F32 = jnp.float32
BF16 = jnp.bfloat16

GRID_W = 64
EPS = 1e-6
ROPE_THETA = 10000.0
NEG_INF = -1e30
MLA_HEADS = 8
MLA_NOPE = 64
MLA_ROPE = 32
MLA_V = 64
KV_RANK = 256
MLA_SCALE = (MLA_NOPE + MLA_ROPE) ** -0.5
LOG2E = math.log2(math.e)
SWA_HEADS = 8
SWA_KV_HEADS = 2
SWA_HEAD_DIM = 64
SWA_BLOCK = 128
SWA_SCALE = SWA_HEAD_DIM ** -0.5
HY_WIDTH = 512
HY_ORDER = 2
HY_BANDS = 16
HY_HIDDEN = 64
HY_TARGET = 1e-2
HY_FAST_PCT = 0.3
HY_SLOW_PCT = 1.5
N_BRANCH = 3
BRANCH_WIDTH = 512
LANE = 128
SUBLANES = 8
HEAD_PAD = 128

VMEM_LIMIT = 56 * 1024 * 1024
IN_PROJ_ROWS = 512
FFN_ROWS = 512
FFN_CHUNKS = 2

C_Q = 0
C_CKV = C_Q + MLA_HEADS * HEAD_PAD
C_KR = C_CKV + KV_RANK
C_SQ = C_KR + HEAD_PAD
C_SK = C_SQ + SWA_HEADS * SWA_HEAD_DIM
C_SV = C_SK + 2 * SWA_KV_HEADS * SWA_HEAD_DIM
C_HY = C_SV + 2 * SWA_KV_HEADS * SWA_HEAD_DIM
C_GT = C_HY + (HY_ORDER + 1) * HY_WIDTH


def _cparams(sem):
    return pltpu.CompilerParams(dimension_semantics=sem, vmem_limit_bytes=VMEM_LIMIT)


def _rms(xf, g):
    return xf * lax.rsqrt(jnp.mean(xf * xf, axis=-1, keepdims=True) + EPS) * g


def _rope(x, cos, sa, sb, half):
    return x * cos + pltpu.roll(x, LANE - half, 1) * sa + pltpu.roll(x, half, 1) * sb


def _dwconv3(p, prev_row, next_row, cw_ref, cb_ref):
    tm = p.shape[0]
    rows = lax.broadcasted_iota(jnp.int32, (tm, 1), 0)
    up = jnp.where(rows == 0, prev_row, pltpu.roll(p, 1, 0))
    dn = jnp.where(rows == tm - 1, next_row, pltpu.roll(p, tm - 1, 0))
    return up * cw_ref[0:1, :] + p * cw_ref[1:2, :] + dn * cw_ref[2:3, :] + cb_ref[...]


def _mod_kernel(c_ref, w_ref, b_ref, o_ref):
    c = c_ref[...]
    a = c * jax.nn.sigmoid(c)
    a_hi = a.astype(BF16)
    a_lo = (a - a_hi.astype(F32)).astype(BF16)
    w = w_ref[0]
    w_hi = w.astype(BF16)
    w_lo = (w - w_hi.astype(F32)).astype(BF16)
    acc = jnp.dot(a_hi, w_hi, preferred_element_type=F32)
    acc += jnp.dot(a_lo, w_hi, preferred_element_type=F32)
    acc += jnp.dot(a_hi, w_lo, preferred_element_type=F32)
    o_ref[0] = acc + b_ref[0]


def _modulation(cvec, w_mod, b_mod):
    depth, d, n = w_mod.shape
    tn = 1536
    return pl.pallas_call(
        _mod_kernel,
        grid=(depth, n // tn),
        in_specs=[pl.BlockSpec((8, d), lambda l, j: (0, 0)),
                  pl.BlockSpec((1, d, tn), lambda l, j: (l, 0, j)),
                  pl.BlockSpec((1, 1, tn), lambda l, j: (l, 0, j))],
        out_specs=pl.BlockSpec((1, 8, tn), lambda l, j: (l, 0, j)),
        out_shape=jax.ShapeDtypeStruct((depth, 8, n), F32),
        compiler_params=_cparams(("arbitrary", "arbitrary")),
        name="adaln_mod",
    )(cvec, w_mod, b_mod.reshape(depth, 1, n))


def _in_proj_kernel(x_ref, xp_ref, xn_ref, sh_ref, sc_ref, g_ref, w_ref, wkv_ref, kvg_ref,
                    cm_ref, sma_ref, smb_ref, cs_ref, ssa_ref, ssb_ref, cw_ref, cb_ref,
                    q_ref, kk_ref, vv_ref, sq_ref, sk_ref, sv_ref, z_ref, x1_ref, x2_ref, gt_ref):
    i = pl.program_id(1)
    nt = pl.num_programs(1)
    g = g_ref[...]
    sh = sh_ref[0]
    sc = sc_ref[0]

    def mod(xf):
        return (_rms(xf, g) * (1.0 + sc) + sh).astype(BF16)

    tm = x_ref.shape[1]
    hb_ext = mod(jnp.concatenate([x_ref[0], xp_ref[0], xn_ref[0]], axis=0))
    hb = hb_ext[:tm]

    def proj(lo, hi, lhs=hb):
        return jnp.dot(lhs, w_ref[:, lo:hi], preferred_element_type=F32)

    cm, sma, smb = cm_ref[...], sma_ref[...], smb_ref[...]
    cs, ssa, ssb = cs_ref[...], ssa_ref[...], ssb_ref[...]

    pq = proj(C_Q, C_CKV)
    for h in range(MLA_HEADS):
        xh = pq[:, HEAD_PAD * h:HEAD_PAD * (h + 1)]
        q_ref[0, :, HEAD_PAD * h:HEAD_PAD * (h + 1)] = (
            _rope(xh, cm, sma, smb, MLA_ROPE // 2) * (MLA_SCALE * LOG2E)).astype(BF16)

    ckv = proj(C_CKV, C_KR)
    cn = _rms(ckv, kvg_ref[...]).astype(BF16)
    kv = jnp.dot(cn, wkv_ref[...], preferred_element_type=F32)
    krr = _rope(proj(C_KR, C_SQ), cm, sma, smb, MLA_ROPE // 2)
    for h in range(MLA_HEADS):
        kk_ref[0, :, HEAD_PAD * h:HEAD_PAD * (h + 1)] = (
            kv[:, HEAD_PAD * h:HEAD_PAD * (h + 1)] + krr).astype(BF16)
    ones = jnp.ones((hb.shape[0], LANE), BF16)
    for j in range(MLA_HEADS // 2):
        v0 = MLA_HEADS * HEAD_PAD + LANE * j
        vv_ref[0, :, 2 * LANE * j:2 * LANE * j + LANE] = kv[:, v0:v0 + LANE].astype(BF16)
        vv_ref[0, :, 2 * LANE * j + LANE:2 * LANE * (j + 1)] = ones

    psq = proj(C_SQ, C_SK)
    for j in range(SWA_HEADS // 2):
        xh = psq[:, LANE * j:LANE * (j + 1)]
        sq_ref[0, :, LANE * j:LANE * (j + 1)] = (
            _rope(xh, cs, ssa, ssb, SWA_HEAD_DIM // 2) * (SWA_SCALE * LOG2E)).astype(BF16)
    psk = proj(C_SK, C_SV)
    lane = lax.broadcasted_iota(jnp.int32, (hb.shape[0], LANE), 1)
    for k in range(SWA_KV_HEADS):
        rk = _rope(psk[:, LANE * k:LANE * (k + 1)], cs, ssa, ssb, SWA_HEAD_DIM // 2)
        sk_ref[0, :, 2 * LANE * k:2 * LANE * k + LANE] = jnp.where(lane < 64, rk, 0.0).astype(BF16)
        sk_ref[0, :, 2 * LANE * k + LANE:2 * LANE * (k + 1)] = jnp.where(lane >= 64, rk, 0.0).astype(BF16)
    sv_ref[0] = proj(C_SV, C_HY).astype(BF16)

    ph = proj(C_HY, C_GT, hb_ext)
    pprev = jnp.where(i > 0, ph[tm + 7:tm + 8, :], 0.0)
    pnext = jnp.where(i < nt - 1, ph[tm + 8:tm + 9, :], 0.0)
    u = _dwconv3(ph[:tm], pprev, pnext, cw_ref, cb_ref)
    z_ref[0] = u[:, :HY_WIDTH]
    x1_ref[0] = u[:, HY_WIDTH:2 * HY_WIDTH]
    x2_ref[0] = u[:, 2 * HY_WIDTH:]

    gt_ref[0] = jax.nn.sigmoid(proj(C_GT, w_ref.shape[1])).astype(BF16)


def _in_proj(x, sh, sc, g, w, wkv, kvg, rope_m, rope_s, cw, cb):
    b, s, d = x.shape
    tm = min(IN_PROJ_ROWS, s)
    nt = s // tm
    r8 = tm // 8
    row = lambda width: pl.BlockSpec((1, tm, width), lambda bi, i: (bi, i, 0))
    const2 = lambda a: pl.BlockSpec(a.shape, lambda bi, i: (0, 0), pipeline_mode=pl.Buffered(1))
    tab = pl.BlockSpec((tm, LANE), lambda bi, i: (i, 0))
    in_specs = [
        row(d),
        pl.BlockSpec((1, 8, d), lambda bi, i: (bi, jnp.maximum(i * r8 - 1, 0), 0)),
        pl.BlockSpec((1, 8, d), lambda bi, i: (bi, jnp.minimum((i + 1) * r8, s // 8 - 1), 0)),
        pl.BlockSpec((1, 1, d), lambda bi, i: (bi, 0, 0)),
        pl.BlockSpec((1, 1, d), lambda bi, i: (bi, 0, 0)),
        const2(g), const2(w), const2(wkv), const2(kvg),
        tab, tab, tab, tab, tab, tab,
        const2(cw), const2(cb),
    ]
    widths = [(MLA_HEADS * HEAD_PAD, BF16), (MLA_HEADS * HEAD_PAD, BF16), (MLA_HEADS * LANE, BF16),
              (SWA_HEADS * SWA_HEAD_DIM, BF16), (4 * LANE, BF16), (2 * LANE, BF16),
              (HY_WIDTH, F32), (HY_WIDTH, F32), (HY_WIDTH, F32), (N_BRANCH * d, BF16)]
    return pl.pallas_call(
        _in_proj_kernel,
        grid=(b, nt),
        in_specs=in_specs,
        out_specs=[row(wd) for wd, _ in widths],
        out_shape=[jax.ShapeDtypeStruct((b, s, wd), dt) for wd, dt in widths],
        compiler_params=_cparams(("arbitrary", "arbitrary")),
        name="in_proj",
    )(x, x, x, sh, sc, g, w, wkv, kvg, *rope_m, *rope_s, cw, cb)


def _mla_kernel(*refs, tk, n_lat, rb):
    if n_lat:
        q_ref, kc_ref, vc_ref, kl_ref, vl_ref, o_ref, m_scr, acc_scr = refs
    else:
        q_ref, kc_ref, vc_ref, o_ref, m_scr, acc_scr = refs
    tq = q_ref.shape[1]
    nt = (((1,), (1,)), ((), ()))
    m_scr[...] = jnp.full(m_scr.shape, NEG_INF, F32)
    acc_scr[...] = jnp.zeros(acc_scr.shape, F32)

    def step(k_ref, v_ref, off, n):
        v = v_ref[0, pl.ds(off, n), :]
        for e in range(2):
            k = k_ref[0, pl.ds(off, n), HEAD_PAD * e:HEAD_PAD * (e + 1)]
            for r in range(tq // rb):
                rows = pl.ds(r * rb, rb)
                s = lax.dot_general(q_ref[0, rows, HEAD_PAD * e:HEAD_PAD * (e + 1)], k, nt,
                                    preferred_element_type=F32)
                m_old = m_scr[e, rows, :]
                m_new = jnp.maximum(m_old, jnp.max(s, axis=1, keepdims=True))
                alpha = jnp.exp2(m_old - m_new)
                p = jnp.exp2(s - jnp.tile(m_new, (1, n // LANE))).astype(BF16)
                acc_scr[e, rows, :] = (jnp.tile(alpha, (1, 2)) * acc_scr[e, rows, :]
                                       + jnp.dot(p, v, preferred_element_type=F32))
                m_scr[e, rows, :] = m_new

    step(kc_ref, vc_ref, 0, kc_ref.shape[1])
    if n_lat:
        def body(j, carry):
            step(kl_ref, vl_ref, pl.multiple_of(j * tk, tk), tk)
            return carry
        lax.fori_loop(0, n_lat, body, 0)
    lane = lax.broadcasted_iota(jnp.int32, (tq, LANE), 1)
    o0 = acc_scr[0, :, :LANE] / acc_scr[0, :, LANE:]
    o1 = acc_scr[1, :, :LANE] / acc_scr[1, :, LANE:]
    o_ref[0] = jnp.where(lane < MLA_V, o0, o1).astype(o_ref.dtype)


def _mla(q, kc, vc, kl=None, vl=None, tq=4096, tk=512, rb=128):
    b, sq, _ = q.shape
    c = kc.shape[1]
    tq = min(tq, sq)
    rb = min(rb, tq)
    hp = MLA_HEADS // 2
    in_specs = [pl.BlockSpec((1, tq, 2 * HEAD_PAD), lambda bi, h, i: (bi, i, h)),
                pl.BlockSpec((1, c, 2 * HEAD_PAD), lambda bi, h, i: (bi, 0, h)),
                pl.BlockSpec((1, c, 2 * LANE), lambda bi, h, i: (bi, 0, h))]
    args = [q, kc, vc]
    n_lat = 0
    if kl is not None:
        s = kl.shape[1]
        tk = min(tk, s)
        n_lat = s // tk
        in_specs += [pl.BlockSpec((1, s, 2 * HEAD_PAD), lambda bi, h, i: (bi, 0, h)),
                     pl.BlockSpec((1, s, 2 * LANE), lambda bi, h, i: (bi, 0, h))]
        args += [kl, vl]
    return pl.pallas_call(
        functools.partial(_mla_kernel, tk=tk, n_lat=n_lat, rb=rb),
        grid=(b, hp, sq // tq),
        in_specs=in_specs,
        out_specs=pl.BlockSpec((1, tq, LANE), lambda bi, h, i: (bi, i, h)),
        out_shape=jax.ShapeDtypeStruct((b, sq, MLA_HEADS * MLA_V), BF16),
        scratch_shapes=[pltpu.VMEM((2, tq, LANE), F32), pltpu.VMEM((2, tq, 2 * LANE), F32)],
        compiler_params=_cparams(("arbitrary", "arbitrary", "arbitrary")),
        name="mla_attn",
    )(*args)


def _swa_kernel(*refs, band):
    if band:
        sink_ref, q_ref, kc_ref, vc_ref, kp_ref, k0_ref, kn_ref, vp_ref, v0_ref, vn_ref, o_ref = refs
        kband = jnp.concatenate([kp_ref[0], k0_ref[0], kn_ref[0]], axis=0)
        vband = jnp.concatenate([vp_ref[0], v0_ref[0], vn_ref[0]], axis=0)
    else:
        sink_ref, q_ref, kc_ref, vc_ref, o_ref = refs
    i = pl.program_id(1)
    nb = pl.num_programs(1)
    tq = q_ref.shape[1]
    nsub = tq // SWA_BLOCK
    n_ctx = kc_ref.shape[1]
    n_keys = n_ctx + (3 * SWA_BLOCK if band else 0)
    rows2 = 2 * SWA_BLOCK
    nt = (((1,), (1,)), ((), ()))
    lane = lax.broadcasted_iota(jnp.int32, (SWA_BLOCK, LANE), 1)
    row = lax.broadcasted_iota(jnp.int32, (rows2, 1), 0)
    if band:
        r_loc = lax.broadcasted_iota(jnp.int32, (rows2, n_keys), 0) % SWA_BLOCK
        col = lax.broadcasted_iota(jnp.int32, (rows2, n_keys), 1) - n_ctx
        prev_bias = jnp.where((col >= 0) & (col < SWA_BLOCK), NEG_INF, 0.0)
        next_bias = jnp.where(col >= 2 * SWA_BLOCK, NEG_INF, 0.0)
        base_bias = (jnp.where(r_loc > col, prev_bias, 0.0)
                     + jnp.where(col - 2 * SWA_BLOCK > r_loc, next_bias, 0.0))
    for r in range(nsub):
        rs = slice(r * SWA_BLOCK, (r + 1) * SWA_BLOCK)
        if band:
            bias = base_bias
            if r == 0:
                bias = jnp.minimum(bias, jnp.where(i == 0, prev_bias, 0.0))
            if r == nsub - 1:
                bias = jnp.minimum(bias, jnp.where(i == nb - 1, next_bias, 0.0))
            bs = slice(r * SWA_BLOCK, (r + 3) * SWA_BLOCK)
        for kvh in range(SWA_KV_HEADS):
            p0, p1 = 2 * kvh, 2 * kvh + 1
            qq = jnp.concatenate([q_ref[0, rs, LANE * p0:LANE * (p0 + 1)],
                                  q_ref[0, rs, LANE * p1:LANE * (p1 + 1)]], axis=0)
            vsl = slice(LANE * kvh, LANE * (kvh + 1))
            v = vc_ref[0, :, vsl]
            if band:
                v = jnp.concatenate([v, vband[bs, vsl]], axis=0)
            outs = []
            for e in range(2):
                ksl = slice(LANE * (2 * kvh + e), LANE * (2 * kvh + e + 1))
                k = kc_ref[0, :, ksl]
                if band:
                    k = jnp.concatenate([k, kband[bs, ksl]], axis=0)
                s = lax.dot_general(qq, k, nt, preferred_element_type=F32)
                if band:
                    s = s + bias
                sink = jnp.where(row < SWA_BLOCK, sink_ref[2 * p0 + e], sink_ref[2 * p1 + e]) * LOG2E
                m = jnp.maximum(jnp.max(s, axis=1, keepdims=True), sink)
                p = jnp.exp2(s - m)
                den = jnp.sum(p, axis=1, keepdims=True) + jnp.exp2(sink - m)
                outs.append(jnp.dot(p.astype(BF16), v, preferred_element_type=F32) / den)
            for t, pair in enumerate((p0, p1)):
                ts = slice(t * SWA_BLOCK, (t + 1) * SWA_BLOCK)
                o_ref[0, rs, LANE * pair:LANE * (pair + 1)] = jnp.where(
                    lane < SWA_HEAD_DIM, outs[0][ts], outs[1][ts]).astype(o_ref.dtype)


def _swa(sink, q, kc, vc, k=None, v=None, tq=512):
    b, sq, _ = q.shape
    c = kc.shape[1]
    tq = min(tq, sq)
    nb = sq // tq
    band = k is not None
    blk = lambda rows, width, f: pl.BlockSpec((1, rows, width), f)
    in_specs = [pl.BlockSpec(memory_space=pltpu.SMEM),
                blk(tq, SWA_HEADS * SWA_HEAD_DIM, lambda bi, i: (bi, i, 0)),
                pl.BlockSpec((1, c, 4 * LANE), lambda bi, i: (bi, 0, 0)),
                pl.BlockSpec((1, c, 2 * LANE), lambda bi, i: (bi, 0, 0))]
    args = [sink, q, kc, vc]
    if band:
        per = tq // SWA_BLOCK
        prev = lambda bi, i: (bi, jnp.maximum(i * per - 1, 0), 0)
        cur = lambda bi, i: (bi, i, 0)
        nxt = lambda bi, i: (bi, jnp.minimum((i + 1) * per, sq // SWA_BLOCK - 1), 0)
        in_specs += [blk(SWA_BLOCK, 4 * LANE, prev), blk(tq, 4 * LANE, cur), blk(SWA_BLOCK, 4 * LANE, nxt),
                     blk(SWA_BLOCK, 2 * LANE, prev), blk(tq, 2 * LANE, cur), blk(SWA_BLOCK, 2 * LANE, nxt)]
        args += [k, k, k, v, v, v]
    return pl.pallas_call(
        functools.partial(_swa_kernel, band=band),
        grid=(b, nb),
        in_specs=in_specs,
        out_specs=blk(tq, SWA_HEADS * SWA_HEAD_DIM, lambda bi, i: (bi, i, 0)),
        out_shape=jax.ShapeDtypeStruct((b, sq, SWA_HEADS * SWA_HEAD_DIM), BF16),
        compiler_params=_cparams(("arbitrary", "arbitrary")),
        name="swa_attn",
    )(*args)


def _merge_kernel(x_ref, ya_ref, yb_ref, yc_ref, gt_ref, wb_ref, wo_ref, ng_ref, g1_ref, o_ref):
    d = x_ref.shape[2]
    merged = None
    for k, y_ref in enumerate((ya_ref, yb_ref, yc_ref)):
        t = jnp.dot(y_ref[0].astype(BF16), wb_ref[k], preferred_element_type=F32)
        t = gt_ref[0, :, d * k:d * (k + 1)].astype(F32) * t
        merged = t if merged is None else merged + t
    y = jnp.dot(merged.astype(BF16), wo_ref[...], preferred_element_type=F32)
    o_ref[0] = x_ref[0] + g1_ref[0] * _rms(y, ng_ref[...])


def _merge(x, ya, yb, yc, gt, wb, wo, ng, g1):
    b, s, d = x.shape
    tm = min(512, s)
    row = lambda width: pl.BlockSpec((1, tm, width), lambda bi, i: (bi, i, 0))
    return pl.pallas_call(
        _merge_kernel,
        grid=(b, s // tm),
        in_specs=[row(d), row(BRANCH_WIDTH), row(BRANCH_WIDTH), row(BRANCH_WIDTH), row(N_BRANCH * d),
                  pl.BlockSpec(wb.shape, lambda bi, i: (0, 0, 0)),
                  pl.BlockSpec(wo.shape, lambda bi, i: (0, 0)),
                  pl.BlockSpec(ng.shape, lambda bi, i: (0, 0)),
                  pl.BlockSpec((1, 1, d), lambda bi, i: (bi, 0, 0))],
        out_specs=row(d),
        out_shape=jax.ShapeDtypeStruct((b, s, d), F32),
        compiler_params=_cparams(("arbitrary", "arbitrary")),
        name="merge",
    )(x, ya, yb, yc, gt, wb, wo, ng, g1)


def _ffn_kernel(x_ref, xp_ref, xn_ref, sh_ref, sc_ref, g2_ref, ng_in_ref, ng_out_ref,
                wu_ref, cw_ref, cb_ref, wd_ref, o_ref, *, chunk):
    i = pl.program_id(1)
    nt = pl.num_programs(1)
    g = ng_in_ref[...]
    sh = sh_ref[0]
    sc = sc_ref[0]

    def mod(xf):
        return (_rms(xf, g) * (1.0 + sc) + sh).astype(BF16)

    x = x_ref[0]
    tm = x.shape[0]
    hb = mod(jnp.concatenate([x, xp_ref[0], xn_ref[0]], axis=0))
    dff = wd_ref.shape[0]
    f = None
    for c0 in range(0, dff, chunk):
        halves = []
        for off in (c0, dff + c0):
            p = jnp.dot(hb, wu_ref[:, off:off + chunk], preferred_element_type=F32)
            pprev = jnp.where(i > 0, p[tm + 7:tm + 8, :], 0.0)
            pnext = jnp.where(i < nt - 1, p[tm + 8:tm + 9, :], 0.0)
            halves.append(_dwconv3(p[:tm], pprev, pnext, cw_ref.at[:, off:off + chunk],
                                   cb_ref.at[:, off:off + chunk]))
        a, bb = halves
        act = (a * jax.nn.sigmoid(a) * bb).astype(BF16)
        t = jnp.dot(act, wd_ref[c0:c0 + chunk, :], preferred_element_type=F32)
        f = t if f is None else f + t
    o_ref[0] = x + g2_ref[0] * _rms(f, ng_out_ref[...])


def _ffn(x, sh, sc, g2, ng_in, ng_out, wu, cw, cb, wd):
    b, s, d = x.shape
    tm = min(FFN_ROWS, s)
    r8 = tm // 8
    dff = wd.shape[0]
    chunk = dff // FFN_CHUNKS if dff % (FFN_CHUNKS * LANE) == 0 else dff
    row = pl.BlockSpec((1, tm, d), lambda bi, i: (bi, i, 0))
    vec = pl.BlockSpec((1, 1, d), lambda bi, i: (bi, 0, 0))
    const2 = lambda a: pl.BlockSpec(a.shape, lambda bi, i: (0, 0), pipeline_mode=pl.Buffered(1))
    return pl.pallas_call(
        functools.partial(_ffn_kernel, chunk=chunk),
        grid=(b, s // tm),
        in_specs=[row,
                  pl.BlockSpec((1, 8, d), lambda bi, i: (bi, jnp.maximum(i * r8 - 1, 0), 0)),
                  pl.BlockSpec((1, 8, d), lambda bi, i: (bi, jnp.minimum((i + 1) * r8, s // 8 - 1), 0)),
                  vec, vec, vec, const2(ng_in), const2(ng_out),
                  const2(wu), const2(cw), const2(cb), const2(wd)],
        out_specs=row,
        out_shape=jax.ShapeDtypeStruct((b, s, d), F32),
        compiler_params=_cparams(("arbitrary", "arbitrary")),
        name="conv_ffn",
    )(x, x, x, sh, sc, g2, ng_in, ng_out, wu, cw, cb, wd)


HY_PASSES = 1
HY_UNROLL = 2
HY_UNROLL2 = 3


def _split(x):
    hi = x.astype(BF16)
    return hi, (x - hi.astype(F32)).astype(BF16)


def _cdot(c_hi, c_lo, x, passes=HY_PASSES):
    x_hi, x_lo = _split(x)
    acc = jnp.dot(c_hi, x_hi, preferred_element_type=F32)
    if passes > 1:
        acc = acc + jnp.dot(c_lo, x_hi, preferred_element_type=F32)
        acc = acc + jnp.dot(c_hi, x_lo, preferred_element_type=F32)
    return acc


def _np_split(m):
    m = jnp.asarray(m, F32)
    hi = m.astype(BF16)
    return hi, (m - hi.astype(F32)).astype(BF16)


def _dft_consts(n_blocks, nonzero_blocks):
    nb = n_blocks
    n = LANE * nb
    dh = nb // 2 + 1
    dhp = -(-dh // 8) * 8
    d = np.arange(dh)[:, None]
    b = np.arange(nonzero_blocks)[None, :]
    ang = 2 * np.pi * ((d * b) % nb) / nb
    f1 = np.zeros((2 * dhp, nonzero_blocks))
    f1[:dh] = np.cos(ang)
    f1[dhp:dhp + dh] = -np.sin(ang)
    a = np.arange(LANE)
    ang2 = 2 * np.pi * ((a[:, None] * a[None, :]) % LANE) / LANE
    ar, ai = np.cos(ang2), -np.sin(ang2)
    f2 = np.block([[ar, -ai], [ai, ar]])
    f2inv = np.block([[ar, ai], [-ai, ar]])
    bo = np.arange(nb // 2)[:, None]
    do = np.arange(dh)[None, :]
    wd = np.where((do == 0) | (do == nb // 2), 1.0, 2.0) / n
    ang3 = 2 * np.pi * ((bo * do) % nb) / nb
    f3 = np.zeros((nb // 2, 2 * dhp))
    f3[:, :dh] = wd * np.cos(ang3)
    f3[:, dhp:dhp + dh] = -wd * np.sin(ang3)
    step_ang = 2 * np.pi * a / n
    tw_step = np.stack([np.broadcast_to(np.cos(step_ang)[:, None], (LANE, LANE)),
                        np.broadcast_to(-np.sin(step_ang)[:, None], (LANE, LANE))])
    eye = np.eye(SUBLANES)
    return dict(dh=dh, dhp=dhp, f1=_np_split(np.kron(f1, eye)), f2=_np_split(f2), f2inv=_np_split(f2inv),
                f3=_np_split(np.kron(f3, eye)), tw_step=jnp.asarray(tw_step, F32))


def _stage1(x_ref, f1_hi, f1_lo, g_scr, passes):
    n_in, n_out = x_ref.shape[0], g_scr.shape[0]

    def body(t, carry):
        r0 = pl.ds(pl.multiple_of(2 * t * SUBLANES, SUBLANES), SUBLANES)
        r1 = pl.ds(pl.multiple_of((2 * t + 1) * SUBLANES, SUBLANES), SUBLANES)
        xa = jnp.concatenate([x_ref[:, r0, :].reshape(n_in * SUBLANES, LANE),
                              x_ref[:, r1, :].reshape(n_in * SUBLANES, LANE)], axis=1)
        g = _cdot(f1_hi, f1_lo, xa, passes)
        g_scr[:, r0, :] = g[:, :LANE].reshape(n_out, SUBLANES, LANE)
        g_scr[:, r1, :] = g[:, LANE:].reshape(n_out, SUBLANES, LANE)
        return carry
    lax.fori_loop(0, LANE // (2 * SUBLANES), body, 0, unroll=HY_UNROLL)


def _twiddle_init(tw_scr):
    tw_scr[0] = jnp.ones((LANE, LANE), F32)
    tw_scr[1] = jnp.zeros((LANE, LANE), F32)


def _twiddle_next(twr, twi, step_ref):
    sr, si = step_ref[0], step_ref[1]
    return twr * sr - twi * si, twr * si + twi * sr


def _twiddle_pair(tw_scr, step_ref):
    t0r, t0i = tw_scr[0], tw_scr[1]
    t1r, t1i = _twiddle_next(t0r, t0i, step_ref)
    t2r, t2i = _twiddle_next(t1r, t1i, step_ref)
    tw_scr[0] = t2r
    tw_scr[1] = t2i
    return jnp.concatenate([t0r, t1r], axis=1), jnp.concatenate([t0i, t1i], axis=1)


def _spectrum_kernel(k_ref, inv_ref, f1h_ref, f1l_ref, f2h_ref, f2l_ref, step_ref, hf_ref, g_scr, tw_scr,
                     *, dh, dhp):
    _stage1(k_ref, f1h_ref[...], f1l_ref[...], g_scr, 3)
    _twiddle_init(tw_scr)
    inv = inv_ref[...]

    def body(h, carry):
        d = 2 * h
        gr = jnp.concatenate([g_scr[d], g_scr[d + 1]], axis=1)
        gi = jnp.concatenate([g_scr[dhp + d], g_scr[dhp + d + 1]], axis=1)
        twr, twi = _twiddle_pair(tw_scr, step_ref)
        t = jnp.concatenate([gr * twr - gi * twi, gr * twi + gi * twr], axis=0)
        y = _cdot(f2h_ref[...], f2l_ref[...], t, passes=3)
        hf_ref[0, d] = y[:LANE, :LANE] * inv
        hf_ref[0, d + 1] = y[:LANE, LANE:] * inv
        hf_ref[1, d] = y[LANE:, :LANE] * inv
        hf_ref[1, d + 1] = y[LANE:, LANE:] * inv
        return carry
    lax.fori_loop(0, dhp // 2, body, 0, unroll=HY_UNROLL2)


def _hyena_spectrum(kern, inv_norm):
    n, c = kern.shape
    nb = n // LANE
    cst = _dft_consts(nb, nb)
    dh, dhp = cst["dh"], cst["dhp"]
    const = lambda a: pl.BlockSpec(a.shape, lambda j: (0,) * a.ndim, pipeline_mode=pl.Buffered(1))
    consts = [*cst["f1"], *cst["f2"], cst["tw_step"]]
    return pl.pallas_call(
        functools.partial(_spectrum_kernel, dh=dh, dhp=dhp),
        grid=(c // LANE,),
        in_specs=[pl.BlockSpec((nb, LANE, LANE), lambda j: (0, 0, j), pipeline_mode=pl.Buffered(1)),
                  pl.BlockSpec((1, LANE), lambda j: (0, j))]
        + [const(a) for a in consts],
        out_specs=pl.BlockSpec((None, 2, dhp, LANE, LANE), lambda j: (j, 0, 0, 0, 0)),
        out_shape=jax.ShapeDtypeStruct((c // LANE, 2, dhp, LANE, LANE), F32),
        scratch_shapes=[pltpu.VMEM((2 * dhp, LANE, LANE), F32), pltpu.VMEM((2, LANE, LANE), F32)],
        compiler_params=_cparams(("arbitrary",)),
        name="hyena_spectrum",
    )(kern.reshape(nb, LANE, c), inv_norm, *consts)


def _conv_kernel(z_ref, gate_ref, skip_ref, hf_ref, f1h_ref, f1l_ref, f2h_ref, f2l_ref, f2ih_ref, f2il_ref,
                 f3h_ref, f3l_ref, step_ref, o_ref, g_scr, tw_scr, *, dh, dhp):
    nbh = z_ref.shape[1]
    _stage1(z_ref.at[0], f1h_ref[...], f1l_ref[...], g_scr, HY_PASSES)
    _twiddle_init(tw_scr)

    def body(h, carry):
        d = 2 * h
        gr = jnp.concatenate([g_scr[d], g_scr[d + 1]], axis=1)
        gi = jnp.concatenate([g_scr[dhp + d], g_scr[dhp + d + 1]], axis=1)
        twr, twi = _twiddle_pair(tw_scr, step_ref)
        t = jnp.concatenate([gr * twr - gi * twi, gr * twi + gi * twr], axis=0)
        y = _cdot(f2h_ref[...], f2l_ref[...], t)
        yr, yi = y[:LANE], y[LANE:]
        hr = jnp.concatenate([hf_ref[0, d], hf_ref[0, d + 1]], axis=1)
        hi = jnp.concatenate([hf_ref[1, d], hf_ref[1, d + 1]], axis=1)
        zz = jnp.concatenate([yr * hr - yi * hi, yr * hi + yi * hr], axis=0)
        u = _cdot(f2ih_ref[...], f2il_ref[...], zz)
        ur, ui = u[:LANE], u[LANE:]
        vr = ur * twr + ui * twi
        vi = ui * twr - ur * twi
        g_scr[d] = vr[:, :LANE]
        g_scr[d + 1] = vr[:, LANE:]
        g_scr[dhp + d] = vi[:, :LANE]
        g_scr[dhp + d + 1] = vi[:, LANE:]
        return carry
    lax.fori_loop(0, dhp // 2, body, 0, unroll=HY_UNROLL2)

    skip = skip_ref[...]
    f3h, f3l = f3h_ref[...], f3l_ref[...]

    def out_body(t, carry):
        r0 = pl.ds(pl.multiple_of(2 * t * SUBLANES, SUBLANES), SUBLANES)
        r1 = pl.ds(pl.multiple_of((2 * t + 1) * SUBLANES, SUBLANES), SUBLANES)
        ga = jnp.concatenate([g_scr[:, r0, :].reshape(2 * dhp * SUBLANES, LANE),
                              g_scr[:, r1, :].reshape(2 * dhp * SUBLANES, LANE)], axis=1)
        conv = _cdot(f3h, f3l, ga)
        for rows, c in ((r0, conv[:, :LANE]), (r1, conv[:, LANE:])):
            o_ref[0, :, rows, :] = gate_ref[0, :, rows, :] * (
                c.reshape(nbh, SUBLANES, LANE) + z_ref[0, :, rows, :] * skip)
        return carry
    lax.fori_loop(0, LANE // (2 * SUBLANES), out_body, 0, unroll=HY_UNROLL)


def _hyena_conv(z, gate, skip, hf, tile0, cst):
    b, l, w = z.shape
    nbh = l // LANE
    dh, dhp = cst["dh"], cst["dhp"]
    consts = [*cst["f1"], *cst["f2"], *cst["f2inv"], *cst["f3"], cst["tw_step"]]
    const = lambda a: pl.BlockSpec(a.shape, lambda j, bi: (0,) * a.ndim, pipeline_mode=pl.Buffered(1))
    tile = pl.BlockSpec((1, nbh, LANE, LANE), lambda j, bi: (bi, 0, 0, j))
    z, gate = z.reshape(b, nbh, LANE, w), gate.reshape(b, nbh, LANE, w)
    return pl.pallas_call(
        functools.partial(_conv_kernel, dh=dh, dhp=dhp),
        grid=(w // LANE, b),
        in_specs=[tile, tile, pl.BlockSpec((1, LANE), lambda j, bi: (0, j)),
                  pl.BlockSpec((None, 2, dhp, LANE, LANE), lambda j, bi: (j + tile0, 0, 0, 0, 0),
                               pipeline_mode=pl.Buffered(1))]
        + [const(a) for a in consts],
        out_specs=tile,
        out_shape=jax.ShapeDtypeStruct((b, nbh, LANE, w), F32),
        scratch_shapes=[pltpu.VMEM((2 * dhp, LANE, LANE), F32), pltpu.VMEM((2, LANE, LANE), F32)],
        compiler_params=_cparams(("arbitrary", "arbitrary")),
        name="hyena_conv",
    )(z, gate, skip, hf, *consts).reshape(b, l, w)


def _filter_mlp_kernel(ft_ref, w1_ref, b1_ref, w2_ref, b2_ref, fr_ref, w3_ref, dl_ref, k_ref, s_ref):
    i = pl.program_id(0)
    ft = ft_ref[...]

    def mm(x, w):
        x_hi, x_lo = _split(x)
        w_hi, w_lo = _split(w)
        return (jnp.dot(x_hi, w_hi, preferred_element_type=F32) + jnp.dot(x_lo, w_hi, preferred_element_type=F32)
                + jnp.dot(x_hi, w_lo, preferred_element_type=F32))

    hid = jnp.sin(fr_ref[0:1, :] * (mm(ft, w1_ref[...]) + b1_ref[...]))
    hid = jnp.sin(fr_ref[1:2, :] * (mm(hid, w2_ref[...]) + b2_ref[...]))
    h = mm(hid, w3_ref[0])
    k = h * jnp.exp(-ft[:, 0:1] * dl_ref[...]) * ft[:, LANE - 1:LANE]
    k_ref[...] = k

    @pl.when(i == 0)
    def _():
        s_ref[...] = jnp.zeros(s_ref.shape, F32)
    s_ref[...] += jnp.sum(jnp.abs(k), axis=0, keepdims=True)


def _hyena_filter_taps(n_tokens, w1, b1, w2, b2, freq, w3):
    L = n_tokens
    n = 2 * L
    idx = jnp.arange(n)
    m = jnp.where(idx < L, idx, n - idx).astype(F32)
    t = m / max(L - 1, 1)
    bands = jnp.linspace(1e-4, HY_BANDS - 1, HY_BANDS, dtype=F32)
    ang = (2.0 * math.pi / L) * m[:, None] * bands
    n_feat = 2 * HY_BANDS + 1
    feats = jnp.concatenate([t[:, None], jnp.cos(ang), -jnp.sin(ang),
                             jnp.zeros((n, LANE - n_feat - 1), F32),
                             (idx != L).astype(F32)[:, None]], axis=-1)
    w1p = jnp.pad(w1, ((0, LANE - n_feat), (0, 0)))
    cw = HY_ORDER * HY_WIDTH
    w3d = jnp.moveaxis(w3.reshape(HY_HIDDEN, 2, cw), 1, 0)
    deltas = jnp.abs(jnp.linspace(math.log(HY_TARGET) / HY_SLOW_PCT, math.log(HY_TARGET) / HY_FAST_PCT,
                                  HY_WIDTH, dtype=F32))
    dl = jnp.tile(deltas, HY_ORDER)[None, :]
    tr = min(1024, L)
    nt = n // tr
    const = lambda a: pl.BlockSpec(a.shape, lambda i: (0,) * a.ndim)
    b1r, b2r = b1[None, :], b2[None, :]
    return pl.pallas_call(
        _filter_mlp_kernel,
        grid=(nt,),
        in_specs=[pl.BlockSpec((tr, LANE), lambda i: (i, 0)), const(w1p), const(b1r), const(w2), const(b2r),
                  const(freq), pl.BlockSpec((1, HY_HIDDEN, cw), lambda i: (i // (nt // 2), 0, 0)), const(dl)],
        out_specs=[pl.BlockSpec((tr, cw), lambda i: (i, 0)), pl.BlockSpec((1, cw), lambda i: (0, 0))],
        out_shape=[jax.ShapeDtypeStruct((n, cw), F32), jax.ShapeDtypeStruct((1, cw), F32)],
        compiler_params=_cparams(("arbitrary",)),
        name="hyena_filter_mlp",
    )(feats, w1p, b1r, w2, b2r, freq, w3d, dl)


def _hyena_long(z, x1, x2, taps, norms, skip):
    b, l, w = z.shape
    nb = 2 * l // LANE
    hf = _hyena_spectrum(taps, 1.0 / norms)
    cst = _dft_consts(nb, nb // 2)
    for o, gate in enumerate((x1, x2)):
        z = _hyena_conv(z, gate, skip[o][None, :], hf, o * (w // LANE), cst)
    return z


def _dense_dft_consts(length):
    n = 2 * length
    h = length + 1
    hp = -(-h // 8) * 8
    k = np.arange(h)[:, None]
    pos = np.arange(n)[None, :]
    ang = 2 * np.pi * ((k * pos) % n) / n
    fwd = np.zeros((2 * hp, n))
    fwd[:h] = np.cos(ang)
    fwd[hp:hp + h] = -np.sin(ang)
    wk = np.where((k == 0) | (k == length), 1.0, 2.0) / n
    inv = np.zeros((length, 2 * hp))
    inv[:, :h] = (wk * np.cos(ang[:, :length])).T
    inv[:, hp:hp + h] = (-wk * np.sin(ang[:, :length])).T
    return dict(hp=hp, full=_np_split(fwd), fwd=_np_split(fwd[:, :length]), inv=_np_split(inv))


def _hyena_short_kernel(z_ref, x1_ref, x2_ref, k_ref, inv_ref, skip_ref, ffh_ref, ffl_ref, fh_ref, fl_ref,
                        fih_ref, fil_ref, o_ref, *, hp):
    z = z_ref[0]
    w = z.shape[1]
    for o, gate_ref in enumerate((x1_ref, x2_ref)):
        cols = slice(o * w, (o + 1) * w)
        hf = _cdot(ffh_ref[...], ffl_ref[...], k_ref[:, cols], passes=3) * inv_ref[:, cols]
        x = _cdot(fh_ref[...], fl_ref[...], z, passes=3)
        xr, xi, hr, hi = x[:hp], x[hp:], hf[:hp], hf[hp:]
        zz = jnp.concatenate([xr * hr - xi * hi, xr * hi + xi * hr], axis=0)
        conv = _cdot(fih_ref[...], fil_ref[...], zz, passes=3)
        z = gate_ref[0] * (conv + z * skip_ref[o:o + 1, :])
    o_ref[0] = z


def _hyena_short(z, x1, x2, taps, norms, skip):
    b, l, w = z.shape
    cst = _dense_dft_consts(l)
    consts = [*cst["full"], *cst["fwd"], *cst["inv"]]
    inv_norm = 1.0 / norms
    const = lambda a: pl.BlockSpec(a.shape, lambda bi: (0,) * a.ndim)
    tile = pl.BlockSpec((1, l, w), lambda bi: (bi, 0, 0))
    return pl.pallas_call(
        functools.partial(_hyena_short_kernel, hp=cst["hp"]),
        grid=(b,),
        in_specs=[tile, tile, tile, const(taps), const(inv_norm), const(skip)] + [const(a) for a in consts],
        out_specs=tile,
        out_shape=jax.ShapeDtypeStruct((b, l, w), F32),
        compiler_params=_cparams(("arbitrary",)),
        name="hyena_short",
    )(z, x1, x2, taps, inv_norm, skip, *consts)


def _rope_tables(rows, rot_dim, head_lanes, rope_off, identity_rows):
    half = rot_dim // 2
    n_freq = rot_dim // 4
    pos = np.arange(rows * GRID_W)
    inv_freq = ROPE_THETA ** (-np.arange(n_freq, dtype=np.float64) / n_freq)
    ang = np.concatenate([(pos // GRID_W)[:, None] * inv_freq, (pos % GRID_W)[:, None] * inv_freq], axis=-1)
    cos, sin = np.cos(ang), np.sin(ang)
    n = pos.shape[0]
    ct = np.ones((n, LANE))
    sa = np.zeros((n, LANE))
    sb = np.zeros((n, LANE))
    for h0 in range(0, LANE, head_lanes):
        lo = h0 + rope_off
        ct[:, lo:lo + half] = cos
        ct[:, lo + half:lo + rot_dim] = cos
        sa[:, lo:lo + half] = -sin
        sb[:, lo + half:lo + rot_dim] = sin
    ident = (np.ones((identity_rows, LANE)), np.zeros((identity_rows, LANE)), np.zeros((identity_rows, LANE)))
    lat = tuple(jnp.asarray(t, F32) for t in (ct, sa, sb))
    ctx = tuple(jnp.asarray(t, F32) for t in ident)
    return lat, ctx


def _pack_w_in(w):
    d = w.shape[0]
    o = 0
    mq = w[:, o:o + MLA_HEADS * (MLA_NOPE + MLA_ROPE)]; o += MLA_HEADS * (MLA_NOPE + MLA_ROPE)
    mckv = w[:, o:o + KV_RANK]; o += KV_RANK
    mkr = w[:, o:o + MLA_ROPE]; o += MLA_ROPE
    sq = w[:, o:o + SWA_HEADS * SWA_HEAD_DIM]; o += SWA_HEADS * SWA_HEAD_DIM
    sk = w[:, o:o + SWA_KV_HEADS * SWA_HEAD_DIM]; o += SWA_KV_HEADS * SWA_HEAD_DIM
    sv = w[:, o:o + SWA_KV_HEADS * SWA_HEAD_DIM]; o += SWA_KV_HEADS * SWA_HEAD_DIM
    hy = w[:, o:o + (HY_ORDER + 1) * HY_WIDTH]; o += (HY_ORDER + 1) * HY_WIDTH
    gt = w[:, o:]
    pad_q = HEAD_PAD - MLA_NOPE - MLA_ROPE
    mq = jnp.pad(mq.reshape(d, MLA_HEADS, MLA_NOPE + MLA_ROPE), ((0, 0), (0, 0), (0, pad_q))).reshape(d, -1)
    mkr = jnp.pad(mkr, ((0, 0), (MLA_NOPE, pad_q)))
    dup = lambda t: jnp.repeat(t.reshape(d, SWA_KV_HEADS, 1, SWA_HEAD_DIM), 2, axis=2).reshape(d, -1)
    return jnp.concatenate([mq, mckv, mkr, sq, dup(sk), dup(sv), hy, gt], axis=1).astype(BF16)


def _pack_w_kv(w):
    r = w.shape[0]
    w = w.reshape(r, MLA_HEADS, MLA_NOPE + MLA_V)
    k = jnp.pad(w[..., :MLA_NOPE], ((0, 0), (0, 0), (0, HEAD_PAD - MLA_NOPE))).reshape(r, -1)
    v = w[..., MLA_NOPE:].reshape(r, -1)
    return jnp.concatenate([k, v], axis=1).astype(BF16)


def kernel(x, c, ctx, c_ctx, w_mod, b_mod, norm_g, w_in, kv_norm_g, w_kv_up, swa_sink, hy_conv_w, hy_conv_b,
           hy_w1, hy_b1, hy_w2, hy_b2, hy_freq, hy_w3, hy_skip, w_branch, w_out, w_up, ffn_conv_w, ffn_conv_b,
           w_down):
    b, s, d = x.shape
    n_ctx = ctx.shape[1]
    depth = w_mod.shape[0]
    rows = s // GRID_W
    rope_m, rope_m_ctx = _rope_tables(rows, MLA_ROPE, HEAD_PAD, MLA_NOPE, n_ctx)
    rope_s, rope_s_ctx = _rope_tables(rows, SWA_HEAD_DIM, SWA_HEAD_DIM, 0, n_ctx)

    cvec = jnp.concatenate([c, c_ctx[None, :], jnp.zeros((8 - b - 1, d), F32)], axis=0)
    mod_all = _modulation(cvec, w_mod, b_mod)

    x_lat, x_ctx = x, ctx
    for l in range(depth):
        with_ctx = l < depth - 1
        m = mod_all[l].reshape(8, 6, d)
        lat = [m[:b, k][:, None, :] for k in range(6)]
        cx = [jnp.broadcast_to(m[b, k][None, None, :], (b, 1, d)) for k in range(6)]
        ng = [norm_g[l, k][None, :] for k in range(4)]
        w_pack = _pack_w_in(w_in[l])
        wkv_pack = _pack_w_kv(w_kv_up[l])
        kvg = kv_norm_g[l][None, :]
        cw, cb = hy_conv_w[l], hy_conv_b[l][None, :]

        q, kk, vv, sq, sk, sv, z, x1, x2, gt = _in_proj(
            x_lat, lat[0], lat[1], ng[0], w_pack, wkv_pack, kvg, rope_m, rope_s, cw, cb)
        qc, kkc, vvc, sqc, skc, svc, zc, x1c, x2c, gtc = _in_proj(
            x_ctx, cx[0], cx[1], ng[0], w_pack, wkv_pack, kvg, rope_m_ctx, rope_s_ctx, cw, cb)

        y_a = _mla(q, kkc, vvc, kk, vv)
        y_b = _swa(swa_sink[l], sq, skc, svc, sk, sv)
        hy_mlp = (hy_w1[l], hy_b1[l], hy_w2[l], hy_b2[l], hy_freq[l], hy_w3[l])
        y_c = _hyena_long(z, x1, x2, *_hyena_filter_taps(s, *hy_mlp), hy_skip[l])

        wb = w_branch[l].astype(BF16)
        wo = w_out[l].astype(BF16)
        wu = w_up[l].astype(BF16)
        wd = w_down[l].astype(BF16)
        fcw, fcb = ffn_conv_w[l], ffn_conv_b[l][None, :]

        x_lat = _merge(x_lat, y_a, y_b, y_c, gt, wb, wo, ng[1], lat[2])
        x_lat = _ffn(x_lat, lat[3], lat[4], lat[5], ng[2], ng[3], wu, fcw, fcb, wd)

        if with_ctx:
            yc_a = _mla(qc, kkc, vvc)
            yc_b = _swa(swa_sink[l], sqc, skc, svc)
            yc_c = _hyena_short(zc, x1c, x2c, *_hyena_filter_taps(n_ctx, *hy_mlp), hy_skip[l])
            x_ctx = _merge(x_ctx, yc_a, yc_b, yc_c, gtc, wb, wo, ng[1], cx[2])
            x_ctx = _ffn(x_ctx, cx[3], cx[4], cx[5], ng[2], ng[3], wu, fcw, fcb, wd)
    return x_lat
```

```python
import functools
import math

import numpy as np
import jax
import jax.numpy as jnp
from jax import lax
from jax.experimental import pallas as pl
from jax.experimental.pallas import tpu as pltpu

F32 = jnp.float32
BF16 = jnp.bfloat16

GRID_W = 64
EPS = 1e-6
ROPE_THETA = 10000.0
NEG_INF = -1e30
MLA_HEADS = 8
MLA_NOPE = 64
MLA_ROPE = 32
MLA_V = 64
KV_RANK = 256
MLA_SCALE = (MLA_NOPE + MLA_ROPE) ** -0.5
LOG2E = math.log2(math.e)
SWA_HEADS = 8
SWA_KV_HEADS = 2
SWA_HEAD_DIM = 64
SWA_BLOCK = 128
SWA_SCALE = SWA_HEAD_DIM ** -0.5
HY_WIDTH = 512
HY_ORDER = 2
HY_BANDS = 16
HY_HIDDEN = 64
HY_TARGET = 1e-2
HY_FAST_PCT = 0.3
HY_SLOW_PCT = 1.5
N_BRANCH = 3
BRANCH_WIDTH = 512
LANE = 128
SUBLANES = 8
HEAD_PAD = 128

VMEM_LIMIT = 56 * 1024 * 1024
IN_PROJ_ROWS = 512
FFN_ROWS = 512
FFN_CHUNKS = 1

C_Q = 0
C_CKV = C_Q + MLA_HEADS * HEAD_PAD
C_KR = C_CKV + KV_RANK
C_SQ = C_KR + HEAD_PAD
C_SK = C_SQ + SWA_HEADS * SWA_HEAD_DIM
C_SV = C_SK + 2 * SWA_KV_HEADS * SWA_HEAD_DIM
C_HY = C_SV + 2 * SWA_KV_HEADS * SWA_HEAD_DIM
C_GT = C_HY + (HY_ORDER + 1) * HY_WIDTH


def _cparams(sem):
    return pltpu.CompilerParams(dimension_semantics=sem, vmem_limit_bytes=VMEM_LIMIT)


def _rms(xf, g):
    return xf * lax.rsqrt(jnp.mean(xf * xf, axis=-1, keepdims=True) + EPS) * g


def _rope(x, cos, sa, sb, half):
    return x * cos + pltpu.roll(x, LANE - half, 1) * sa + pltpu.roll(x, half, 1) * sb


def _dwconv3(p, prev_row, next_row, cw_ref, cb_ref):
    tm = p.shape[0]
    rows = lax.broadcasted_iota(jnp.int32, (tm, 1), 0)
    up = jnp.where(rows == 0, prev_row, pltpu.roll(p, 1, 0))
    dn = jnp.where(rows == tm - 1, next_row, pltpu.roll(p, tm - 1, 0))
    return up * cw_ref[0:1, :] + p * cw_ref[1:2, :] + dn * cw_ref[2:3, :] + cb_ref[...]


def _mod_kernel(c_ref, w_ref, b_ref, o_ref):
    c = c_ref[...]
    a = c * jax.nn.sigmoid(c)
    a_hi = a.astype(BF16)
    a_lo = (a - a_hi.astype(F32)).astype(BF16)
    w = w_ref[0]
    w_hi = w.astype(BF16)
    w_lo = (w - w_hi.astype(F32)).astype(BF16)
    acc = jnp.dot(a_hi, w_hi, preferred_element_type=F32)
    acc += jnp.dot(a_lo, w_hi, preferred_element_type=F32)
    acc += jnp.dot(a_hi, w_lo, preferred_element_type=F32)
    o_ref[0] = acc + b_ref[0]


def _modulation(cvec, w_mod, b_mod):
    depth, d, n = w_mod.shape
    tn = 1536
    return pl.pallas_call(
        _mod_kernel,
        grid=(depth, n // tn),
        in_specs=[pl.BlockSpec((8, d), lambda l, j: (0, 0)),
                  pl.BlockSpec((1, d, tn), lambda l, j: (l, 0, j)),
                  pl.BlockSpec((1, 1, tn), lambda l, j: (l, 0, j))],
        out_specs=pl.BlockSpec((1, 8, tn), lambda l, j: (l, 0, j)),
        out_shape=jax.ShapeDtypeStruct((depth, 8, n), F32),
        compiler_params=_cparams(("arbitrary", "arbitrary")),
        name="adaln_mod",
    )(cvec, w_mod, b_mod.reshape(depth, 1, n))


def _in_proj_kernel(x_ref, xp_ref, xn_ref, sh_ref, sc_ref, g_ref, w_ref, wkv_ref, kvg_ref,
                    cm_ref, sma_ref, smb_ref, cs_ref, ssa_ref, ssb_ref, cw_ref, cb_ref,
                    q_ref, kk_ref, vv_ref, sq_ref, sk_ref, sv_ref, z_ref, x1_ref, x2_ref, gt_ref):
    i = pl.program_id(1)
    nt = pl.num_programs(1)
    g = g_ref[...]
    sh = sh_ref[0]
    sc = sc_ref[0]

    def mod(xf):
        return (_rms(xf, g) * (1.0 + sc) + sh).astype(BF16)

    tm = x_ref.shape[1]
    hb_ext = mod(jnp.concatenate([x_ref[0], xp_ref[0], xn_ref[0]], axis=0))
    hb = hb_ext[:tm]

    def proj(lo, hi, lhs=hb):
        return jnp.dot(lhs, w_ref[:, lo:hi], preferred_element_type=F32)

    cm, sma, smb = cm_ref[...], sma_ref[...], smb_ref[...]
    cs, ssa, ssb = cs_ref[...], ssa_ref[...], ssb_ref[...]

    pq = proj(C_Q, C_CKV)
    for h in range(MLA_HEADS):
        xh = pq[:, HEAD_PAD * h:HEAD_PAD * (h + 1)]
        q_ref[0, :, HEAD_PAD * h:HEAD_PAD * (h + 1)] = (
            _rope(xh, cm, sma, smb, MLA_ROPE // 2) * (MLA_SCALE * LOG2E)).astype(BF16)

    ckv = proj(C_CKV, C_KR)
    cn = _rms(ckv, kvg_ref[...]).astype(BF16)
    kv = jnp.dot(cn, wkv_ref[...], preferred_element_type=F32)
    krr = _rope(proj(C_KR, C_SQ), cm, sma, smb, MLA_ROPE // 2)
    for h in range(MLA_HEADS):
        kk_ref[0, :, HEAD_PAD * h:HEAD_PAD * (h + 1)] = (
            kv[:, HEAD_PAD * h:HEAD_PAD * (h + 1)] + krr).astype(BF16)
    ones = jnp.ones((hb.shape[0], LANE), BF16)
    for j in range(MLA_HEADS // 2):
        v0 = MLA_HEADS * HEAD_PAD + LANE * j
        vv_ref[0, :, 2 * LANE * j:2 * LANE * j + LANE] = kv[:, v0:v0 + LANE].astype(BF16)
        vv_ref[0, :, 2 * LANE * j + LANE:2 * LANE * (j + 1)] = ones

    psq = proj(C_SQ, C_SK)
    for j in range(SWA_HEADS // 2):
        xh = psq[:, LANE * j:LANE * (j + 1)]
        sq_ref[0, :, LANE * j:LANE * (j + 1)] = (
            _rope(xh, cs, ssa, ssb, SWA_HEAD_DIM // 2) * (SWA_SCALE * LOG2E)).astype(BF16)
    psk = proj(C_SK, C_SV)
    lane = lax.broadcasted_iota(jnp.int32, (hb.shape[0], LANE), 1)
    for k in range(SWA_KV_HEADS):
        rk = _rope(psk[:, LANE * k:LANE * (k + 1)], cs, ssa, ssb, SWA_HEAD_DIM // 2)
        sk_ref[0, :, 2 * LANE * k:2 * LANE * k + LANE] = jnp.where(lane < 64, rk, 0.0).astype(BF16)
        sk_ref[0, :, 2 * LANE * k + LANE:2 * LANE * (k + 1)] = jnp.where(lane >= 64, rk, 0.0).astype(BF16)
    sv_ref[0] = proj(C_SV, C_HY).astype(BF16)

    ph = proj(C_HY, C_GT, hb_ext)
    pprev = jnp.where(i > 0, ph[tm + 7:tm + 8, :], 0.0)
    pnext = jnp.where(i < nt - 1, ph[tm + 8:tm + 9, :], 0.0)
    u = _dwconv3(ph[:tm], pprev, pnext, cw_ref, cb_ref)
    z_ref[0] = u[:, :HY_WIDTH]
    x1_ref[0] = u[:, HY_WIDTH:2 * HY_WIDTH]
    x2_ref[0] = u[:, 2 * HY_WIDTH:]

    gt_ref[0] = jax.nn.sigmoid(proj(C_GT, w_ref.shape[1])).astype(BF16)


def _in_proj(x, sh, sc, g, w, wkv, kvg, rope_m, rope_s, cw, cb):
    b, s, d = x.shape
    tm = min(IN_PROJ_ROWS, s)
    nt = s // tm
    r8 = tm // 8
    row = lambda width: pl.BlockSpec((1, tm, width), lambda bi, i: (bi, i, 0))
    const2 = lambda a: pl.BlockSpec(a.shape, lambda bi, i: (0, 0), pipeline_mode=pl.Buffered(1))
    tab = pl.BlockSpec((tm, LANE), lambda bi, i: (i, 0))
    in_specs = [
        row(d),
        pl.BlockSpec((1, 8, d), lambda bi, i: (bi, jnp.maximum(i * r8 - 1, 0), 0)),
        pl.BlockSpec((1, 8, d), lambda bi, i: (bi, jnp.minimum((i + 1) * r8, s // 8 - 1), 0)),
        pl.BlockSpec((1, 1, d), lambda bi, i: (bi, 0, 0)),
        pl.BlockSpec((1, 1, d), lambda bi, i: (bi, 0, 0)),
        const2(g), const2(w), const2(wkv), const2(kvg),
        tab, tab, tab, tab, tab, tab,
        const2(cw), const2(cb),
    ]
    widths = [(MLA_HEADS * HEAD_PAD, BF16), (MLA_HEADS * HEAD_PAD, BF16), (MLA_HEADS * LANE, BF16),
              (SWA_HEADS * SWA_HEAD_DIM, BF16), (4 * LANE, BF16), (2 * LANE, BF16),
              (HY_WIDTH, F32), (HY_WIDTH, F32), (HY_WIDTH, F32), (N_BRANCH * d, BF16)]
    return pl.pallas_call(
        _in_proj_kernel,
        grid=(b, nt),
        in_specs=in_specs,
        out_specs=[row(wd) for wd, _ in widths],
        out_shape=[jax.ShapeDtypeStruct((b, s, wd), dt) for wd, dt in widths],
        compiler_params=_cparams(("arbitrary", "arbitrary")),
        name="in_proj",
    )(x, x, x, sh, sc, g, w, wkv, kvg, *rope_m, *rope_s, cw, cb)


def _mla_kernel(*refs, tk, n_lat, rb):
    if n_lat:
        q_ref, kc_ref, vc_ref, kl_ref, vl_ref, o_ref, m_scr, acc_scr = refs
    else:
        q_ref, kc_ref, vc_ref, o_ref, m_scr, acc_scr = refs
    tq = q_ref.shape[1]
    nt = (((1,), (1,)), ((), ()))
    m_scr[...] = jnp.full(m_scr.shape, NEG_INF, F32)
    acc_scr[...] = jnp.zeros(acc_scr.shape, F32)

    def step(k_ref, v_ref, off, n):
        v = v_ref[0, pl.ds(off, n), :]
        for e in range(2):
            k = k_ref[0, pl.ds(off, n), HEAD_PAD * e:HEAD_PAD * (e + 1)]
            for r in range(tq // rb):
                rows = pl.ds(r * rb, rb)
                s = lax.dot_general(q_ref[0, rows, HEAD_PAD * e:HEAD_PAD * (e + 1)], k, nt,
                                    preferred_element_type=F32)
                m_old = m_scr[e, rows, :]
                m_new = jnp.maximum(m_old, jnp.max(s, axis=1, keepdims=True))
                alpha = jnp.exp2(m_old - m_new)
                p = jnp.exp2(s - jnp.tile(m_new, (1, n // LANE))).astype(BF16)
                acc_scr[e, rows, :] = (jnp.tile(alpha, (1, 2)) * acc_scr[e, rows, :]
                                       + jnp.dot(p, v, preferred_element_type=F32))
                m_scr[e, rows, :] = m_new

    step(kc_ref, vc_ref, 0, kc_ref.shape[1])
    if n_lat:
        def body(j, carry):
            step(kl_ref, vl_ref, pl.multiple_of(j * tk, tk), tk)
            return carry
        lax.fori_loop(0, n_lat, body, 0)
    lane = lax.broadcasted_iota(jnp.int32, (tq, LANE), 1)
    o0 = acc_scr[0, :, :LANE] / acc_scr[0, :, LANE:]
    o1 = acc_scr[1, :, :LANE] / acc_scr[1, :, LANE:]
    o_ref[0] = jnp.where(lane < MLA_V, o0, o1).astype(o_ref.dtype)


def _mla(q, kc, vc, kl=None, vl=None, tq=4096, tk=512, rb=128):
    b, sq, _ = q.shape
    c = kc.shape[1]
    tq = min(tq, sq)
    rb = min(rb, tq)
    hp = MLA_HEADS // 2
    in_specs = [pl.BlockSpec((1, tq, 2 * HEAD_PAD), lambda bi, h, i: (bi, i, h)),
                pl.BlockSpec((1, c, 2 * HEAD_PAD), lambda bi, h, i: (bi, 0, h)),
                pl.BlockSpec((1, c, 2 * LANE), lambda bi, h, i: (bi, 0, h))]
    args = [q, kc, vc]
    n_lat = 0
    if kl is not None:
        s = kl.shape[1]
        tk = min(tk, s)
        n_lat = s // tk
        in_specs += [pl.BlockSpec((1, s, 2 * HEAD_PAD), lambda bi, h, i: (bi, 0, h)),
                     pl.BlockSpec((1, s, 2 * LANE), lambda bi, h, i: (bi, 0, h))]
        args += [kl, vl]
    return pl.pallas_call(
        functools.partial(_mla_kernel, tk=tk, n_lat=n_lat, rb=rb),
        grid=(b, hp, sq // tq),
        in_specs=in_specs,
        out_specs=pl.BlockSpec((1, tq, LANE), lambda bi, h, i: (bi, i, h)),
        out_shape=jax.ShapeDtypeStruct((b, sq, MLA_HEADS * MLA_V), BF16),
        scratch_shapes=[pltpu.VMEM((2, tq, LANE), F32), pltpu.VMEM((2, tq, 2 * LANE), F32)],
        compiler_params=_cparams(("arbitrary", "arbitrary", "arbitrary")),
        name="mla_attn",
    )(*args)


def _swa_kernel(*refs, band):
    if band:
        sink_ref, q_ref, kc_ref, vc_ref, kp_ref, k0_ref, kn_ref, vp_ref, v0_ref, vn_ref, o_ref = refs
        kband = jnp.concatenate([kp_ref[0], k0_ref[0], kn_ref[0]], axis=0)
        vband = jnp.concatenate([vp_ref[0], v0_ref[0], vn_ref[0]], axis=0)
    else:
        sink_ref, q_ref, kc_ref, vc_ref, o_ref = refs
    i = pl.program_id(1)
    nb = pl.num_programs(1)
    tq = q_ref.shape[1]
    nsub = tq // SWA_BLOCK
    n_ctx = kc_ref.shape[1]
    n_keys = n_ctx + (3 * SWA_BLOCK if band else 0)
    rows2 = 2 * SWA_BLOCK
    nt = (((1,), (1,)), ((), ()))
    lane = lax.broadcasted_iota(jnp.int32, (SWA_BLOCK, LANE), 1)
    row = lax.broadcasted_iota(jnp.int32, (rows2, 1), 0)
    if band:
        r_loc = lax.broadcasted_iota(jnp.int32, (rows2, n_keys), 0) % SWA_BLOCK
        col = lax.broadcasted_iota(jnp.int32, (rows2, n_keys), 1) - n_ctx
        prev_bias = jnp.where((col >= 0) & (col < SWA_BLOCK), NEG_INF, 0.0)
        next_bias = jnp.where(col >= 2 * SWA_BLOCK, NEG_INF, 0.0)
        base_bias = (jnp.where(r_loc > col, prev_bias, 0.0)
                     + jnp.where(col - 2 * SWA_BLOCK > r_loc, next_bias, 0.0))
    for r in range(nsub):
        rs = slice(r * SWA_BLOCK, (r + 1) * SWA_BLOCK)
        if band:
            bias = base_bias
            if r == 0:
                bias = jnp.minimum(bias, jnp.where(i == 0, prev_bias, 0.0))
            if r == nsub - 1:
                bias = jnp.minimum(bias, jnp.where(i == nb - 1, next_bias, 0.0))
            bs = slice(r * SWA_BLOCK, (r + 3) * SWA_BLOCK)
        for kvh in range(SWA_KV_HEADS):
            p0, p1 = 2 * kvh, 2 * kvh + 1
            qq = jnp.concatenate([q_ref[0, rs, LANE * p0:LANE * (p0 + 1)],
                                  q_ref[0, rs, LANE * p1:LANE * (p1 + 1)]], axis=0)
            vsl = slice(LANE * kvh, LANE * (kvh + 1))
            v = vc_ref[0, :, vsl]
            if band:
                v = jnp.concatenate([v, vband[bs, vsl]], axis=0)
            v = jnp.concatenate([v, jnp.ones_like(v)], axis=1)
            outs = []
            for e in range(2):
                ksl = slice(LANE * (2 * kvh + e), LANE * (2 * kvh + e + 1))
                k = kc_ref[0, :, ksl]
                if band:
                    k = jnp.concatenate([k, kband[bs, ksl]], axis=0)
                s = lax.dot_general(qq, k, nt, preferred_element_type=F32)
                if band:
                    s = s + bias
                sink = jnp.where(row < SWA_BLOCK, sink_ref[2 * p0 + e], sink_ref[2 * p1 + e]) * LOG2E
                m = jnp.maximum(jnp.max(s, axis=1, keepdims=True), sink)
                p = jnp.exp2(s - m).astype(BF16)
                acc = jnp.dot(p, v, preferred_element_type=F32)
                outs.append(acc[:, :LANE] / (acc[:, LANE:] + jnp.exp2(sink - m)))
            for t, pair in enumerate((p0, p1)):
                ts = slice(t * SWA_BLOCK, (t + 1) * SWA_BLOCK)
                o_ref[0, rs, LANE * pair:LANE * (pair + 1)] = jnp.where(
                    lane < SWA_HEAD_DIM, outs[0][ts], outs[1][ts]).astype(o_ref.dtype)


def _swa(sink, q, kc, vc, k=None, v=None, tq=512):
    b, sq, _ = q.shape
    c = kc.shape[1]
    tq = min(tq, sq)
    nb = sq // tq
    band = k is not None
    blk = lambda rows, width, f: pl.BlockSpec((1, rows, width), f)
    in_specs = [pl.BlockSpec(memory_space=pltpu.SMEM),
                blk(tq, SWA_HEADS * SWA_HEAD_DIM, lambda bi, i: (bi, i, 0)),
                pl.BlockSpec((1, c, 4 * LANE), lambda bi, i: (bi, 0, 0)),
                pl.BlockSpec((1, c, 2 * LANE), lambda bi, i: (bi, 0, 0))]
    args = [sink, q, kc, vc]
    if band:
        per = tq // SWA_BLOCK
        prev = lambda bi, i: (bi, jnp.maximum(i * per - 1, 0), 0)
        cur = lambda bi, i: (bi, i, 0)
        nxt = lambda bi, i: (bi, jnp.minimum((i + 1) * per, sq // SWA_BLOCK - 1), 0)
        in_specs += [blk(SWA_BLOCK, 4 * LANE, prev), blk(tq, 4 * LANE, cur), blk(SWA_BLOCK, 4 * LANE, nxt),
                     blk(SWA_BLOCK, 2 * LANE, prev), blk(tq, 2 * LANE, cur), blk(SWA_BLOCK, 2 * LANE, nxt)]
        args += [k, k, k, v, v, v]
    return pl.pallas_call(
        functools.partial(_swa_kernel, band=band),
        grid=(b, nb),
        in_specs=in_specs,
        out_specs=blk(tq, SWA_HEADS * SWA_HEAD_DIM, lambda bi, i: (bi, i, 0)),
        out_shape=jax.ShapeDtypeStruct((b, sq, SWA_HEADS * SWA_HEAD_DIM), BF16),
        compiler_params=_cparams(("arbitrary", "arbitrary")),
        name="swa_attn",
    )(*args)


def _merge_kernel(x_ref, ya_ref, yb_ref, yc_ref, gt_ref, wb_ref, wo_ref, ng_ref, g1_ref, o_ref):
    d = x_ref.shape[2]
    merged = None
    for k, y_ref in enumerate((ya_ref, yb_ref, yc_ref)):
        t = jnp.dot(y_ref[0].astype(BF16), wb_ref[k], preferred_element_type=F32)
        t = gt_ref[0, :, d * k:d * (k + 1)].astype(F32) * t
        merged = t if merged is None else merged + t
    y = jnp.dot(merged.astype(BF16), wo_ref[...], preferred_element_type=F32)
    o_ref[0] = x_ref[0] + g1_ref[0] * _rms(y, ng_ref[...])


def _merge(x, ya, yb, yc, gt, wb, wo, ng, g1):
    b, s, d = x.shape
    tm = min(512, s)
    row = lambda width: pl.BlockSpec((1, tm, width), lambda bi, i: (bi, i, 0))
    return pl.pallas_call(
        _merge_kernel,
        grid=(b, s // tm),
        in_specs=[row(d), row(BRANCH_WIDTH), row(BRANCH_WIDTH), row(BRANCH_WIDTH), row(N_BRANCH * d),
                  pl.BlockSpec(wb.shape, lambda bi, i: (0, 0, 0)),
                  pl.BlockSpec(wo.shape, lambda bi, i: (0, 0)),
                  pl.BlockSpec(ng.shape, lambda bi, i: (0, 0)),
                  pl.BlockSpec((1, 1, d), lambda bi, i: (bi, 0, 0))],
        out_specs=row(d),
        out_shape=jax.ShapeDtypeStruct((b, s, d), F32),
        compiler_params=_cparams(("arbitrary", "arbitrary")),
        name="merge",
    )(x, ya, yb, yc, gt, wb, wo, ng, g1)


def _ffn_kernel(x_ref, xp_ref, xn_ref, sh_ref, sc_ref, g2_ref, ng_in_ref, ng_out_ref,
                wu_ref, cw_ref, cb_ref, wd_ref, o_ref, *, chunk):
    i = pl.program_id(1)
    nt = pl.num_programs(1)
    g = ng_in_ref[...]
    sh = sh_ref[0]
    sc = sc_ref[0]

    def mod(xf):
        return (_rms(xf, g) * (1.0 + sc) + sh).astype(BF16)

    x = x_ref[0]
    tm = x.shape[0]
    hb = mod(jnp.concatenate([x, xp_ref[0], xn_ref[0]], axis=0))
    dff = wd_ref.shape[0]
    f = None
    for c0 in range(0, dff, chunk):
        halves = []
        for off in (c0, dff + c0):
            p = jnp.dot(hb, wu_ref[:, off:off + chunk], preferred_element_type=F32)
            pprev = jnp.where(i > 0, p[tm + 7:tm + 8, :], 0.0)
            pnext = jnp.where(i < nt - 1, p[tm + 8:tm + 9, :], 0.0)
            halves.append(_dwconv3(p[:tm], pprev, pnext, cw_ref.at[:, off:off + chunk],
                                   cb_ref.at[:, off:off + chunk]))
        a, bb = halves
        act = (a * jax.nn.sigmoid(a) * bb).astype(BF16)
        t = jnp.dot(act, wd_ref[c0:c0 + chunk, :], preferred_element_type=F32)
        f = t if f is None else f + t
    o_ref[0] = x + g2_ref[0] * _rms(f, ng_out_ref[...])


def _ffn(x, sh, sc, g2, ng_in, ng_out, wu, cw, cb, wd):
    b, s, d = x.shape
    tm = min(FFN_ROWS, s)
    r8 = tm // 8
    dff = wd.shape[0]
    chunk = dff // FFN_CHUNKS if dff % (FFN_CHUNKS * LANE) == 0 else dff
    row = pl.BlockSpec((1, tm, d), lambda bi, i: (bi, i, 0))
    vec = pl.BlockSpec((1, 1, d), lambda bi, i: (bi, 0, 0))
    const2 = lambda a: pl.BlockSpec(a.shape, lambda bi, i: (0, 0), pipeline_mode=pl.Buffered(1))
    return pl.pallas_call(
        functools.partial(_ffn_kernel, chunk=chunk),
        grid=(b, s // tm),
        in_specs=[row,
                  pl.BlockSpec((1, 8, d), lambda bi, i: (bi, jnp.maximum(i * r8 - 1, 0), 0)),
                  pl.BlockSpec((1, 8, d), lambda bi, i: (bi, jnp.minimum((i + 1) * r8, s // 8 - 1), 0)),
                  vec, vec, vec, const2(ng_in), const2(ng_out),
                  const2(wu), const2(cw), const2(cb), const2(wd)],
        out_specs=row,
        out_shape=jax.ShapeDtypeStruct((b, s, d), F32),
        compiler_params=_cparams(("arbitrary", "arbitrary")),
        name="conv_ffn",
    )(x, x, x, sh, sc, g2, ng_in, ng_out, wu, cw, cb, wd)


HY_PASSES = 1
HY_UNROLL = 2
HY_UNROLL2 = 3


def _split(x):
    hi = x.astype(BF16)
    return hi, (x - hi.astype(F32)).astype(BF16)


def _cdot(c_hi, c_lo, x, passes=HY_PASSES):
    x_hi, x_lo = _split(x)
    acc = jnp.dot(c_hi, x_hi, preferred_element_type=F32)
    if passes > 1:
        acc = acc + jnp.dot(c_lo, x_hi, preferred_element_type=F32)
        acc = acc + jnp.dot(c_hi, x_lo, preferred_element_type=F32)
    return acc


def _np_split(m):
    m = jnp.asarray(m, F32)
    hi = m.astype(BF16)
    return hi, (m - hi.astype(F32)).astype(BF16)


def _dft_consts(n_blocks, nonzero_blocks):
    nb = n_blocks
    n = LANE * nb
    dh = nb // 2 + 1
    dhp = -(-dh // 8) * 8
    d = np.arange(dh)[:, None]
    b = np.arange(nonzero_blocks)[None, :]
    ang = 2 * np.pi * ((d * b) % nb) / nb
    f1 = np.zeros((2 * dhp, nonzero_blocks))
    f1[:dh] = np.cos(ang)
    f1[dhp:dhp + dh] = -np.sin(ang)
    a = np.arange(LANE)
    ang2 = 2 * np.pi * ((a[:, None] * a[None, :]) % LANE) / LANE
    ar, ai = np.cos(ang2), -np.sin(ang2)
    f2 = np.block([[ar, -ai], [ai, ar]])
    f2inv = np.block([[ar, ai], [-ai, ar]])
    bo = np.arange(nb // 2)[:, None]
    do = np.arange(dh)[None, :]
    wd = np.where((do == 0) | (do == nb // 2), 1.0, 2.0) / n
    ang3 = 2 * np.pi * ((bo * do) % nb) / nb
    f3 = np.zeros((nb // 2, 2 * dhp))
    f3[:, :dh] = wd * np.cos(ang3)
    f3[:, dhp:dhp + dh] = -wd * np.sin(ang3)
    step_ang = 2 * np.pi * a / n
    tw_step = np.stack([np.broadcast_to(np.cos(step_ang)[:, None], (LANE, LANE)),
                        np.broadcast_to(-np.sin(step_ang)[:, None], (LANE, LANE))])
    eye = np.eye(SUBLANES)
    return dict(dh=dh, dhp=dhp, f1=_np_split(np.kron(f1, eye)), f2=_np_split(f2), f2inv=_np_split(f2inv),
                f3=_np_split(np.kron(f3, eye)), tw_step=jnp.asarray(tw_step, F32))


def _stage1(x_ref, f1_hi, f1_lo, g_scr, passes):
    n_in, n_out = x_ref.shape[0], g_scr.shape[0]

    def body(t, carry):
        r0 = pl.ds(pl.multiple_of(2 * t * SUBLANES, SUBLANES), SUBLANES)
        r1 = pl.ds(pl.multiple_of((2 * t + 1) * SUBLANES, SUBLANES), SUBLANES)
        xa = jnp.concatenate([x_ref[:, r0, :].reshape(n_in * SUBLANES, LANE),
                              x_ref[:, r1, :].reshape(n_in * SUBLANES, LANE)], axis=1)
        g = _cdot(f1_hi, f1_lo, xa, passes)
        g_scr[:, r0, :] = g[:, :LANE].reshape(n_out, SUBLANES, LANE)
        g_scr[:, r1, :] = g[:, LANE:].reshape(n_out, SUBLANES, LANE)
        return carry
    lax.fori_loop(0, LANE // (2 * SUBLANES), body, 0, unroll=HY_UNROLL)


def _twiddle_init(tw_scr):
    tw_scr[0] = jnp.ones((LANE, LANE), F32)
    tw_scr[1] = jnp.zeros((LANE, LANE), F32)


def _twiddle_next(twr, twi, step_ref):
    sr, si = step_ref[0], step_ref[1]
    return twr * sr - twi * si, twr * si + twi * sr


def _twiddle_pair(tw_scr, step_ref):
    t0r, t0i = tw_scr[0], tw_scr[1]
    t1r, t1i = _twiddle_next(t0r, t0i, step_ref)
    t2r, t2i = _twiddle_next(t1r, t1i, step_ref)
    tw_scr[0] = t2r
    tw_scr[1] = t2i
    return jnp.concatenate([t0r, t1r], axis=1), jnp.concatenate([t0i, t1i], axis=1)


def _spectrum_kernel(k_ref, inv_ref, f1h_ref, f1l_ref, f2h_ref, f2l_ref, step_ref, hf_ref, g_scr, tw_scr,
                     *, dh, dhp):
    _stage1(k_ref, f1h_ref[...], f1l_ref[...], g_scr, 3)
    _twiddle_init(tw_scr)
    inv = inv_ref[...]

    def body(h, carry):
        d = 2 * h
        gr = jnp.concatenate([g_scr[d], g_scr[d + 1]], axis=1)
        gi = jnp.concatenate([g_scr[dhp + d], g_scr[dhp + d + 1]], axis=1)
        twr, twi = _twiddle_pair(tw_scr, step_ref)
        t = jnp.concatenate([gr * twr - gi * twi, gr * twi + gi * twr], axis=0)
        y = _cdot(f2h_ref[...], f2l_ref[...], t, passes=3)
        hf_ref[0, d] = y[:LANE, :LANE] * inv
        hf_ref[0, d + 1] = y[:LANE, LANE:] * inv
        hf_ref[1, d] = y[LANE:, :LANE] * inv
        hf_ref[1, d + 1] = y[LANE:, LANE:] * inv
        return carry
    lax.fori_loop(0, dhp // 2, body, 0, unroll=HY_UNROLL2)


def _hyena_spectrum(kern, inv_norm):
    tiles, n, _ = kern.shape
    c = tiles * LANE
    nb = n // LANE
    cst = _dft_consts(nb, nb)
    dh, dhp = cst["dh"], cst["dhp"]
    const = lambda a: pl.BlockSpec(a.shape, lambda j: (0,) * a.ndim, pipeline_mode=pl.Buffered(1))
    consts = [*cst["f1"], *cst["f2"], cst["tw_step"]]
    return pl.pallas_call(
        functools.partial(_spectrum_kernel, dh=dh, dhp=dhp),
        grid=(c // LANE,),
        in_specs=[pl.BlockSpec((None, nb, LANE, LANE), lambda j: (j, 0, 0, 0), pipeline_mode=pl.Buffered(1)),
                  pl.BlockSpec((1, LANE), lambda j: (0, j))]
        + [const(a) for a in consts],
        out_specs=pl.BlockSpec((None, 2, dhp, LANE, LANE), lambda j: (j, 0, 0, 0, 0)),
        out_shape=jax.ShapeDtypeStruct((c // LANE, 2, dhp, LANE, LANE), F32),
        scratch_shapes=[pltpu.VMEM((2 * dhp, LANE, LANE), F32), pltpu.VMEM((2, LANE, LANE), F32)],
        compiler_params=_cparams(("arbitrary",)),
        name="hyena_spectrum",
    )(kern.reshape(tiles, nb, LANE, LANE), inv_norm, *consts)


def _conv_kernel(z_ref, gate_ref, skip_ref, hf_ref, f1h_ref, f1l_ref, f2h_ref, f2l_ref, f2ih_ref, f2il_ref,
                 f3h_ref, f3l_ref, step_ref, o_ref, g_scr, tw_scr, *, dh, dhp):
    nbh = z_ref.shape[1]
    _stage1(z_ref.at[0], f1h_ref[...], f1l_ref[...], g_scr, HY_PASSES)
    _twiddle_init(tw_scr)

    def body(h, carry):
        d = 2 * h
        gr = jnp.concatenate([g_scr[d], g_scr[d + 1]], axis=1)
        gi = jnp.concatenate([g_scr[dhp + d], g_scr[dhp + d + 1]], axis=1)
        twr, twi = _twiddle_pair(tw_scr, step_ref)
        t = jnp.concatenate([gr * twr - gi * twi, gr * twi + gi * twr], axis=0)
        y = _cdot(f2h_ref[...], f2l_ref[...], t)
        yr, yi = y[:LANE], y[LANE:]
        hr = jnp.concatenate([hf_ref[0, d], hf_ref[0, d + 1]], axis=1)
        hi = jnp.concatenate([hf_ref[1, d], hf_ref[1, d + 1]], axis=1)
        zz = jnp.concatenate([yr * hr - yi * hi, yr * hi + yi * hr], axis=0)
        u = _cdot(f2ih_ref[...], f2il_ref[...], zz)
        ur, ui = u[:LANE], u[LANE:]
        vr = ur * twr + ui * twi
        vi = ui * twr - ur * twi
        g_scr[d] = vr[:, :LANE]
        g_scr[d + 1] = vr[:, LANE:]
        g_scr[dhp + d] = vi[:, :LANE]
        g_scr[dhp + d + 1] = vi[:, LANE:]
        return carry
    lax.fori_loop(0, dhp // 2, body, 0, unroll=HY_UNROLL2)

    skip = skip_ref[...]
    f3h, f3l = f3h_ref[...], f3l_ref[...]

    def out_body(t, carry):
        r0 = pl.ds(pl.multiple_of(2 * t * SUBLANES, SUBLANES), SUBLANES)
        r1 = pl.ds(pl.multiple_of((2 * t + 1) * SUBLANES, SUBLANES), SUBLANES)
        ga = jnp.concatenate([g_scr[:, r0, :].reshape(2 * dhp * SUBLANES, LANE),
                              g_scr[:, r1, :].reshape(2 * dhp * SUBLANES, LANE)], axis=1)
        conv = _cdot(f3h, f3l, ga)
        for rows, c in ((r0, conv[:, :LANE]), (r1, conv[:, LANE:])):
            o_ref[0, :, rows, :] = gate_ref[0, :, rows, :] * (
                c.reshape(nbh, SUBLANES, LANE) + z_ref[0, :, rows, :] * skip)
        return carry
    lax.fori_loop(0, LANE // (2 * SUBLANES), out_body, 0, unroll=HY_UNROLL)


def _hyena_conv(z, gate, skip, hf, tile0, cst):
    b, l, w = z.shape
    nbh = l // LANE
    dh, dhp = cst["dh"], cst["dhp"]
    consts = [*cst["f1"], *cst["f2"], *cst["f2inv"], *cst["f3"], cst["tw_step"]]
    const = lambda a: pl.BlockSpec(a.shape, lambda j, bi: (0,) * a.ndim, pipeline_mode=pl.Buffered(1))
    tile = pl.BlockSpec((1, nbh, LANE, LANE), lambda j, bi: (bi, 0, 0, j))
    z, gate = z.reshape(b, nbh, LANE, w), gate.reshape(b, nbh, LANE, w)
    return pl.pallas_call(
        functools.partial(_conv_kernel, dh=dh, dhp=dhp),
        grid=(w // LANE, b),
        in_specs=[tile, tile, pl.BlockSpec((1, LANE), lambda j, bi: (0, j)),
                  pl.BlockSpec((None, 2, dhp, LANE, LANE), lambda j, bi: (j + tile0, 0, 0, 0, 0),
                               pipeline_mode=pl.Buffered(1))]
        + [const(a) for a in consts],
        out_specs=tile,
        out_shape=jax.ShapeDtypeStruct((b, nbh, LANE, w), F32),
        scratch_shapes=[pltpu.VMEM((2 * dhp, LANE, LANE), F32), pltpu.VMEM((2, LANE, LANE), F32)],
        compiler_params=_cparams(("arbitrary", "arbitrary")),
        name="hyena_conv",
    )(z, gate, skip, hf, *consts).reshape(b, l, w)


def _filter_mlp_kernel(ft_ref, w1_ref, b1_ref, w2_ref, b2_ref, fr_ref, w3_ref, dl_ref, k_ref, s_ref):
    i = pl.program_id(0)
    ft = ft_ref[...]

    def mm(x, w):
        x_hi, x_lo = _split(x)
        w_hi, w_lo = _split(w)
        return (jnp.dot(x_hi, w_hi, preferred_element_type=F32) + jnp.dot(x_lo, w_hi, preferred_element_type=F32)
                + jnp.dot(x_hi, w_lo, preferred_element_type=F32))

    hid = jnp.sin(fr_ref[0:1, :] * (mm(ft, w1_ref[...]) + b1_ref[...]))
    hid = jnp.sin(fr_ref[1:2, :] * (mm(hid, w2_ref[...]) + b2_ref[...]))
    h = mm(hid, w3_ref[0])
    k = h * jnp.exp(-ft[:, 0:1] * dl_ref[...]) * ft[:, LANE - 1:LANE]
    for j in range(k_ref.shape[0]):
        k_ref[j] = k[:, LANE * j:LANE * (j + 1)]

    @pl.when(i == 0)
    def _():
        s_ref[...] = jnp.zeros(s_ref.shape, F32)
    s_ref[...] += jnp.sum(jnp.abs(k), axis=0, keepdims=True)


def _hyena_filter_taps(n_tokens, w1, b1, w2, b2, freq, w3):
    L = n_tokens
    n = 2 * L
    idx = jnp.arange(n)
    m = jnp.where(idx < L, idx, n - idx).astype(F32)
    t = m / max(L - 1, 1)
    bands = jnp.linspace(1e-4, HY_BANDS - 1, HY_BANDS, dtype=F32)
    ang = (2.0 * math.pi / L) * m[:, None] * bands
    n_feat = 2 * HY_BANDS + 1
    feats = jnp.concatenate([t[:, None], jnp.cos(ang), -jnp.sin(ang),
                             jnp.zeros((n, LANE - n_feat - 1), F32),
                             (idx != L).astype(F32)[:, None]], axis=-1)
    w1p = jnp.pad(w1, ((0, LANE - n_feat), (0, 0)))
    cw = HY_ORDER * HY_WIDTH
    w3d = jnp.moveaxis(w3.reshape(HY_HIDDEN, 2, cw), 1, 0)
    deltas = jnp.abs(jnp.linspace(math.log(HY_TARGET) / HY_SLOW_PCT, math.log(HY_TARGET) / HY_FAST_PCT,
                                  HY_WIDTH, dtype=F32))
    dl = jnp.tile(deltas, HY_ORDER)[None, :]
    tr = min(1024, L)
    nt = n // tr
    const = lambda a: pl.BlockSpec(a.shape, lambda i: (0,) * a.ndim)
    b1r, b2r = b1[None, :], b2[None, :]
    return pl.pallas_call(
        _filter_mlp_kernel,
        grid=(nt,),
        in_specs=[pl.BlockSpec((tr, LANE), lambda i: (i, 0)), const(w1p), const(b1r), const(w2), const(b2r),
                  const(freq), pl.BlockSpec((1, HY_HIDDEN, cw), lambda i: (i // (nt // 2), 0, 0)), const(dl)],
        out_specs=[pl.BlockSpec((cw // LANE, tr, LANE), lambda i: (0, i, 0)), pl.BlockSpec((1, cw), lambda i: (0, 0))],
        out_shape=[jax.ShapeDtypeStruct((cw // LANE, n, LANE), F32), jax.ShapeDtypeStruct((1, cw), F32)],
        compiler_params=_cparams(("arbitrary",)),
        name="hyena_filter_mlp",
    )(feats, w1p, b1r, w2, b2r, freq, w3d, dl)


def _hyena_long(z, x1, x2, taps, norms, skip):
    b, l, w = z.shape
    nb = 2 * l // LANE
    hf = _hyena_spectrum(taps, 1.0 / norms)
    cst = _dft_consts(nb, nb // 2)
    for o, gate in enumerate((x1, x2)):
        z = _hyena_conv(z, gate, skip[o][None, :], hf, o * (w // LANE), cst)
    return z


def _dense_dft_consts(length):
    n = 2 * length
    h = length + 1
    hp = -(-h // 8) * 8
    k = np.arange(h)[:, None]
    pos = np.arange(n)[None, :]
    ang = 2 * np.pi * ((k * pos) % n) / n
    fwd = np.zeros((2 * hp, n))
    fwd[:h] = np.cos(ang)
    fwd[hp:hp + h] = -np.sin(ang)
    wk = np.where((k == 0) | (k == length), 1.0, 2.0) / n
    inv = np.zeros((length, 2 * hp))
    inv[:, :h] = (wk * np.cos(ang[:, :length])).T
    inv[:, hp:hp + h] = (-wk * np.sin(ang[:, :length])).T
    return dict(hp=hp, full=_np_split(fwd), fwd=_np_split(fwd[:, :length]), inv=_np_split(inv))


def _hyena_short_kernel(z_ref, x1_ref, x2_ref, k_ref, inv_ref, skip_ref, ffh_ref, ffl_ref, fh_ref, fl_ref,
                        fih_ref, fil_ref, o_ref, *, hp):
    z = z_ref[0]
    w = z.shape[1]
    for o, gate_ref in enumerate((x1_ref, x2_ref)):
        cols = slice(o * w, (o + 1) * w)
        hf = _cdot(ffh_ref[...], ffl_ref[...], k_ref[:, cols], passes=3) * inv_ref[:, cols]
        x = _cdot(fh_ref[...], fl_ref[...], z, passes=3)
        xr, xi, hr, hi = x[:hp], x[hp:], hf[:hp], hf[hp:]
        zz = jnp.concatenate([xr * hr - xi * hi, xr * hi + xi * hr], axis=0)
        conv = _cdot(fih_ref[...], fil_ref[...], zz, passes=3)
        z = gate_ref[0] * (conv + z * skip_ref[o:o + 1, :])
    o_ref[0] = z


def _hyena_short(z, x1, x2, taps, norms, skip):
    b, l, w = z.shape
    taps = jnp.moveaxis(taps, 0, 1).reshape(2 * l, -1)
    cst = _dense_dft_consts(l)
    consts = [*cst["full"], *cst["fwd"], *cst["inv"]]
    inv_norm = 1.0 / norms
    const = lambda a: pl.BlockSpec(a.shape, lambda bi: (0,) * a.ndim)
    tile = pl.BlockSpec((1, l, w), lambda bi: (bi, 0, 0))
    return pl.pallas_call(
        functools.partial(_hyena_short_kernel, hp=cst["hp"]),
        grid=(b,),
        in_specs=[tile, tile, tile, const(taps), const(inv_norm), const(skip)] + [const(a) for a in consts],
        out_specs=tile,
        out_shape=jax.ShapeDtypeStruct((b, l, w), F32),
        compiler_params=_cparams(("arbitrary",)),
        name="hyena_short",
    )(z, x1, x2, taps, inv_norm, skip, *consts)


def _rope_tables(rows, rot_dim, head_lanes, rope_off, identity_rows):
    half = rot_dim // 2
    n_freq = rot_dim // 4
    pos = np.arange(rows * GRID_W)
    inv_freq = ROPE_THETA ** (-np.arange(n_freq, dtype=np.float64) / n_freq)
    ang = np.concatenate([(pos // GRID_W)[:, None] * inv_freq, (pos % GRID_W)[:, None] * inv_freq], axis=-1)
    cos, sin = np.cos(ang), np.sin(ang)
    n = pos.shape[0]
    ct = np.ones((n, LANE))
    sa = np.zeros((n, LANE))
    sb = np.zeros((n, LANE))
    for h0 in range(0, LANE, head_lanes):
        lo = h0 + rope_off
        ct[:, lo:lo + half] = cos
        ct[:, lo + half:lo + rot_dim] = cos
        sa[:, lo:lo + half] = -sin
        sb[:, lo + half:lo + rot_dim] = sin
    ident = (np.ones((identity_rows, LANE)), np.zeros((identity_rows, LANE)), np.zeros((identity_rows, LANE)))
    lat = tuple(jnp.asarray(t, F32) for t in (ct, sa, sb))
    ctx = tuple(jnp.asarray(t, F32) for t in ident)
    return lat, ctx


def _pack_w_in(w):
    d = w.shape[0]
    o = 0
    mq = w[:, o:o + MLA_HEADS * (MLA_NOPE + MLA_ROPE)]; o += MLA_HEADS * (MLA_NOPE + MLA_ROPE)
    mckv = w[:, o:o + KV_RANK]; o += KV_RANK
    mkr = w[:, o:o + MLA_ROPE]; o += MLA_ROPE
    sq = w[:, o:o + SWA_HEADS * SWA_HEAD_DIM]; o += SWA_HEADS * SWA_HEAD_DIM
    sk = w[:, o:o + SWA_KV_HEADS * SWA_HEAD_DIM]; o += SWA_KV_HEADS * SWA_HEAD_DIM
    sv = w[:, o:o + SWA_KV_HEADS * SWA_HEAD_DIM]; o += SWA_KV_HEADS * SWA_HEAD_DIM
    hy = w[:, o:o + (HY_ORDER + 1) * HY_WIDTH]; o += (HY_ORDER + 1) * HY_WIDTH
    gt = w[:, o:]
    pad_q = HEAD_PAD - MLA_NOPE - MLA_ROPE
    mq = jnp.pad(mq.reshape(d, MLA_HEADS, MLA_NOPE + MLA_ROPE), ((0, 0), (0, 0), (0, pad_q))).reshape(d, -1)
    mkr = jnp.pad(mkr, ((0, 0), (MLA_NOPE, pad_q)))
    dup = lambda t: jnp.repeat(t.reshape(d, SWA_KV_HEADS, 1, SWA_HEAD_DIM), 2, axis=2).reshape(d, -1)
    return jnp.concatenate([mq, mckv, mkr, sq, dup(sk), dup(sv), hy, gt], axis=1).astype(BF16)


def _pack_w_kv(w):
    r = w.shape[0]
    w = w.reshape(r, MLA_HEADS, MLA_NOPE + MLA_V)
    k = jnp.pad(w[..., :MLA_NOPE], ((0, 0), (0, 0), (0, HEAD_PAD - MLA_NOPE))).reshape(r, -1)
    v = w[..., MLA_NOPE:].reshape(r, -1)
    return jnp.concatenate([k, v], axis=1).astype(BF16)


def kernel(x, c, ctx, c_ctx, w_mod, b_mod, norm_g, w_in, kv_norm_g, w_kv_up, swa_sink, hy_conv_w, hy_conv_b,
           hy_w1, hy_b1, hy_w2, hy_b2, hy_freq, hy_w3, hy_skip, w_branch, w_out, w_up, ffn_conv_w, ffn_conv_b,
           w_down):
    b, s, d = x.shape
    n_ctx = ctx.shape[1]
    depth = w_mod.shape[0]
    rows = s // GRID_W
    rope_m, rope_m_ctx = _rope_tables(rows, MLA_ROPE, HEAD_PAD, MLA_NOPE, n_ctx)
    rope_s, rope_s_ctx = _rope_tables(rows, SWA_HEAD_DIM, SWA_HEAD_DIM, 0, n_ctx)

    cvec = jnp.concatenate([c, c_ctx[None, :], jnp.zeros((8 - b - 1, d), F32)], axis=0)
    mod_all = _modulation(cvec, w_mod, b_mod)

    x_lat, x_ctx = x, ctx
    for l in range(depth):
        with_ctx = l < depth - 1
        m = mod_all[l].reshape(8, 6, d)
        lat = [m[:b, k][:, None, :] for k in range(6)]
        cx = [jnp.broadcast_to(m[b, k][None, None, :], (b, 1, d)) for k in range(6)]
        ng = [norm_g[l, k][None, :] for k in range(4)]
        w_pack = _pack_w_in(w_in[l])
        wkv_pack = _pack_w_kv(w_kv_up[l])
        kvg = kv_norm_g[l][None, :]
        cw, cb = hy_conv_w[l], hy_conv_b[l][None, :]

        q, kk, vv, sq, sk, sv, z, x1, x2, gt = _in_proj(
            x_lat, lat[0], lat[1], ng[0], w_pack, wkv_pack, kvg, rope_m, rope_s, cw, cb)
        qc, kkc, vvc, sqc, skc, svc, zc, x1c, x2c, gtc = _in_proj(
            x_ctx, cx[0], cx[1], ng[0], w_pack, wkv_pack, kvg, rope_m_ctx, rope_s_ctx, cw, cb)

        y_a = _mla(q, kkc, vvc, kk, vv)
        y_b = _swa(swa_sink[l], sq, skc, svc, sk, sv)
        hy_mlp = (hy_w1[l], hy_b1[l], hy_w2[l], hy_b2[l], hy_freq[l], hy_w3[l])
        y_c = _hyena_long(z, x1, x2, *_hyena_filter_taps(s, *hy_mlp), hy_skip[l])

        wb = w_branch[l].astype(BF16)
        wo = w_out[l].astype(BF16)
        wu = w_up[l].astype(BF16)
        wd = w_down[l].astype(BF16)
        fcw, fcb = ffn_conv_w[l], ffn_conv_b[l][None, :]

        x_lat = _merge(x_lat, y_a, y_b, y_c, gt, wb, wo, ng[1], lat[2])
        x_lat = _ffn(x_lat, lat[3], lat[4], lat[5], ng[2], ng[3], wu, fcw, fcb, wd)

        if with_ctx:
            yc_a = _mla(qc, kkc, vvc)
            yc_b = _swa(swa_sink[l], sqc, skc, svc)
            yc_c = _hyena_short(zc, x1c, x2c, *_hyena_filter_taps(n_ctx, *hy_mlp), hy_skip[l])
            x_ctx = _merge(x_ctx, yc_a, yc_b, yc_c, gtc, wb, wo, ng[1], cx[2])
            x_ctx = _ffn(x_ctx, cx[3], cx[4], cx[5], ng[2], ng[3], wu, fcw, fcb, wd)
    return x_lat
```

```python
import functools
import math

import numpy as np
import jax
import jax.numpy as jnp
from jax import lax
from jax.experimental import pallas as pl
from jax.experimental.pallas import tpu as pltpu

F32 = jnp.float32
BF16 = jnp.bfloat16

GRID_W = 64
EPS = 1e-6
ROPE_THETA = 10000.0
NEG_INF = -1e30
MLA_HEADS = 8
MLA_NOPE = 64
MLA_ROPE = 32
MLA_V = 64
KV_RANK = 256
MLA_SCALE = (MLA_NOPE + MLA_ROPE) ** -0.5
LOG2E = math.log2(math.e)
SWA_HEADS = 8
SWA_KV_HEADS = 2
SWA_HEAD_DIM = 64
SWA_BLOCK = 128
SWA_SCALE = SWA_HEAD_DIM ** -0.5
HY_WIDTH = 512
HY_ORDER = 2
HY_BANDS = 16
HY_HIDDEN = 64
HY_TARGET = 1e-2
HY_FAST_PCT = 0.3
HY_SLOW_PCT = 1.5
N_BRANCH = 3
BRANCH_WIDTH = 512
LANE = 128
SUBLANES = 8
HEAD_PAD = 128

VMEM_LIMIT = 56 * 1024 * 1024
IN_PROJ_ROWS = 512
FFN_ROWS = 512
FFN_CHUNKS = 1

C_Q = 0
C_CKV = C_Q + MLA_HEADS * HEAD_PAD
C_KR = C_CKV + KV_RANK
C_SQ = C_KR + HEAD_PAD
C_SK = C_SQ + SWA_HEADS * SWA_HEAD_DIM
C_SV = C_SK + 2 * SWA_KV_HEADS * SWA_HEAD_DIM
C_HY = C_SV + 2 * SWA_KV_HEADS * SWA_HEAD_DIM
C_GT = C_HY + (HY_ORDER + 1) * HY_WIDTH


def _cparams(sem):
    return pltpu.CompilerParams(dimension_semantics=sem, vmem_limit_bytes=VMEM_LIMIT)


def _rms(xf, g):
    return xf * lax.rsqrt(jnp.mean(xf * xf, axis=-1, keepdims=True) + EPS) * g


def _rope(x, cos, sa, sb, half):
    return x * cos + pltpu.roll(x, LANE - half, 1) * sa + pltpu.roll(x, half, 1) * sb


def _dwconv3(p, prev_row, next_row, cw_ref, cb_ref):
    tm = p.shape[0]
    rows = lax.broadcasted_iota(jnp.int32, (tm, 1), 0)
    up = jnp.where(rows == 0, prev_row, pltpu.roll(p, 1, 0))
    dn = jnp.where(rows == tm - 1, next_row, pltpu.roll(p, tm - 1, 0))
    return up * cw_ref[0:1, :] + p * cw_ref[1:2, :] + dn * cw_ref[2:3, :] + cb_ref[...]


def _mod_kernel(c_ref, w_ref, b_ref, o_ref):
    c = c_ref[...]
    a = c * jax.nn.sigmoid(c)
    a_hi = a.astype(BF16)
    a_lo = (a - a_hi.astype(F32)).astype(BF16)
    w = w_ref[0]
    w_hi = w.astype(BF16)
    w_lo = (w - w_hi.astype(F32)).astype(BF16)
    acc = jnp.dot(a_hi, w_hi, preferred_element_type=F32)
    acc += jnp.dot(a_lo, w_hi, preferred_element_type=F32)
    acc += jnp.dot(a_hi, w_lo, preferred_element_type=F32)
    o_ref[0] = acc + b_ref[0]


def _modulation(cvec, w_mod, b_mod):
    depth, d, n = w_mod.shape
    tn = 1536
    return pl.pallas_call(
        _mod_kernel,
        grid=(depth, n // tn),
        in_specs=[pl.BlockSpec((8, d), lambda l, j: (0, 0)),
                  pl.BlockSpec((1, d, tn), lambda l, j: (l, 0, j)),
                  pl.BlockSpec((1, 1, tn), lambda l, j: (l, 0, j))],
        out_specs=pl.BlockSpec((1, 8, tn), lambda l, j: (l, 0, j)),
        out_shape=jax.ShapeDtypeStruct((depth, 8, n), F32),
        compiler_params=_cparams(("arbitrary", "arbitrary")),
        name="adaln_mod",
    )(cvec, w_mod, b_mod.reshape(depth, 1, n))


def _in_proj_kernel(x_ref, xp_ref, xn_ref, sh_ref, sc_ref, g_ref, w_ref, wkv_ref, kvg_ref,
                    cm_ref, sma_ref, smb_ref, cs_ref, ssa_ref, ssb_ref, cw_ref, cb_ref,
                    q_ref, kk_ref, vv_ref, sq_ref, sk_ref, sv_ref, z_ref, x1_ref, x2_ref, gt_ref):
    i = pl.program_id(1)
    nt = pl.num_programs(1)
    g = g_ref[...]
    sh = sh_ref[0]
    sc = sc_ref[0]

    def mod(xf):
        return (_rms(xf, g) * (1.0 + sc) + sh).astype(BF16)

    tm = x_ref.shape[1]
    hb_ext = mod(jnp.concatenate([x_ref[0], xp_ref[0], xn_ref[0]], axis=0))
    hb = hb_ext[:tm]

    def proj(lo, hi, lhs=hb):
        return jnp.dot(lhs, w_ref[:, lo:hi], preferred_element_type=F32)

    cm, sma, smb = cm_ref[...], sma_ref[...], smb_ref[...]
    cs, ssa, ssb = cs_ref[...], ssa_ref[...], ssb_ref[...]

    pq = proj(C_Q, C_CKV)
    for h in range(MLA_HEADS):
        xh = pq[:, HEAD_PAD * h:HEAD_PAD * (h + 1)]
        q_ref[0, :, HEAD_PAD * h:HEAD_PAD * (h + 1)] = (
            _rope(xh, cm, sma, smb, MLA_ROPE // 2) * (MLA_SCALE * LOG2E)).astype(BF16)

    ckv = proj(C_CKV, C_KR)
    cn = _rms(ckv, kvg_ref[...]).astype(BF16)
    kv = jnp.dot(cn, wkv_ref[...], preferred_element_type=F32)
    krr = _rope(proj(C_KR, C_SQ), cm, sma, smb, MLA_ROPE // 2)
    for h in range(MLA_HEADS):
        kk_ref[0, :, HEAD_PAD * h:HEAD_PAD * (h + 1)] = (
            kv[:, HEAD_PAD * h:HEAD_PAD * (h + 1)] + krr).astype(BF16)
    ones = jnp.ones((hb.shape[0], LANE), BF16)
    for j in range(MLA_HEADS // 2):
        v0 = MLA_HEADS * HEAD_PAD + LANE * j
        vv_ref[0, :, 2 * LANE * j:2 * LANE * j + LANE] = kv[:, v0:v0 + LANE].astype(BF16)
        vv_ref[0, :, 2 * LANE * j + LANE:2 * LANE * (j + 1)] = ones

    psq = proj(C_SQ, C_SK)
    for j in range(SWA_HEADS // 2):
        xh = psq[:, LANE * j:LANE * (j + 1)]
        sq_ref[0, :, LANE * j:LANE * (j + 1)] = (
            _rope(xh, cs, ssa, ssb, SWA_HEAD_DIM // 2) * (SWA_SCALE * LOG2E)).astype(BF16)
    psk = proj(C_SK, C_SV)
    for k in range(SWA_KV_HEADS):
        sk_ref[0, :, LANE * k:LANE * (k + 1)] = _rope(
            psk[:, LANE * k:LANE * (k + 1)], cs, ssa, ssb, SWA_HEAD_DIM // 2).astype(BF16)
    sv_ref[0] = proj(C_SV, C_HY).astype(BF16)

    ph = proj(C_HY, C_GT, hb_ext)
    pprev = jnp.where(i > 0, ph[tm + 7:tm + 8, :], 0.0)
    pnext = jnp.where(i < nt - 1, ph[tm + 8:tm + 9, :], 0.0)
    u = _dwconv3(ph[:tm], pprev, pnext, cw_ref, cb_ref)
    z_ref[0] = u[:, :HY_WIDTH]
    x1_ref[0] = u[:, HY_WIDTH:2 * HY_WIDTH]
    x2_ref[0] = u[:, 2 * HY_WIDTH:]

    gt_ref[0] = jax.nn.sigmoid(proj(C_GT, w_ref.shape[1])).astype(BF16)


def _in_proj(x, sh, sc, g, w, wkv, kvg, rope_m, rope_s, cw, cb):
    b, s, d = x.shape
    tm = min(IN_PROJ_ROWS, s)
    nt = s // tm
    r8 = tm // 8
    row = lambda width: pl.BlockSpec((1, tm, width), lambda bi, i: (bi, i, 0))
    const2 = lambda a: pl.BlockSpec(a.shape, lambda bi, i: (0, 0), pipeline_mode=pl.Buffered(1))
    tab = pl.BlockSpec((tm, LANE), lambda bi, i: (i, 0))
    in_specs = [
        row(d),
        pl.BlockSpec((1, 8, d), lambda bi, i: (bi, jnp.maximum(i * r8 - 1, 0), 0)),
        pl.BlockSpec((1, 8, d), lambda bi, i: (bi, jnp.minimum((i + 1) * r8, s // 8 - 1), 0)),
        pl.BlockSpec((1, 1, d), lambda bi, i: (bi, 0, 0)),
        pl.BlockSpec((1, 1, d), lambda bi, i: (bi, 0, 0)),
        const2(g), const2(w), const2(wkv), const2(kvg),
        tab, tab, tab, tab, tab, tab,
        const2(cw), const2(cb),
    ]
    widths = [(MLA_HEADS * HEAD_PAD, BF16), (MLA_HEADS * HEAD_PAD, BF16), (MLA_HEADS * LANE, BF16),
              (SWA_HEADS * SWA_HEAD_DIM, BF16), (2 * LANE, BF16), (2 * LANE, BF16),
              (HY_WIDTH, F32), (HY_WIDTH, F32), (HY_WIDTH, F32), (N_BRANCH * d, BF16)]
    return pl.pallas_call(
        _in_proj_kernel,
        grid=(b, nt),
        in_specs=in_specs,
        out_specs=[row(wd) for wd, _ in widths],
        out_shape=[jax.ShapeDtypeStruct((b, s, wd), dt) for wd, dt in widths],
        compiler_params=_cparams(("arbitrary", "arbitrary")),
        name="in_proj",
    )(x, x, x, sh, sc, g, w, wkv, kvg, *rope_m, *rope_s, cw, cb)


def _mla_kernel(*refs, tk, n_lat, rb):
    if n_lat:
        q_ref, kc_ref, vc_ref, kl_ref, vl_ref, o_ref, m_scr, acc_scr = refs
    else:
        q_ref, kc_ref, vc_ref, o_ref, m_scr, acc_scr = refs
    tq = q_ref.shape[1]
    nt = (((1,), (1,)), ((), ()))
    m_scr[...] = jnp.full(m_scr.shape, NEG_INF, F32)
    acc_scr[...] = jnp.zeros(acc_scr.shape, F32)

    def step(k_ref, v_ref, off, n):
        v = v_ref[0, pl.ds(off, n), :]
        for e in range(2):
            k = k_ref[0, pl.ds(off, n), HEAD_PAD * e:HEAD_PAD * (e + 1)]
            for r in range(tq // rb):
                rows = pl.ds(r * rb, rb)
                s = lax.dot_general(q_ref[0, rows, HEAD_PAD * e:HEAD_PAD * (e + 1)], k, nt,
                                    preferred_element_type=F32)
                m_old = m_scr[e, rows, :]
                m_new = jnp.maximum(m_old, jnp.max(s, axis=1, keepdims=True))
                alpha = jnp.exp2(m_old - m_new)
                p = jnp.exp2(s - jnp.tile(m_new, (1, n // LANE))).astype(BF16)
                acc_scr[e, rows, :] = (jnp.tile(alpha, (1, 2)) * acc_scr[e, rows, :]
                                       + jnp.dot(p, v, preferred_element_type=F32))
                m_scr[e, rows, :] = m_new

    step(kc_ref, vc_ref, 0, kc_ref.shape[1])
    if n_lat:
        def body(j, carry):
            step(kl_ref, vl_ref, pl.multiple_of(j * tk, tk), tk)
            return carry
        lax.fori_loop(0, n_lat, body, 0)
    lane = lax.broadcasted_iota(jnp.int32, (tq, LANE), 1)
    o0 = acc_scr[0, :, :LANE] / acc_scr[0, :, LANE:]
    o1 = acc_scr[1, :, :LANE] / acc_scr[1, :, LANE:]
    o_ref[0] = jnp.where(lane < MLA_V, o0, o1).astype(o_ref.dtype)


def _mla(q, kc, vc, kl=None, vl=None, tq=4096, tk=512, rb=128):
    b, sq, _ = q.shape
    c = kc.shape[1]
    tq = min(tq, sq)
    rb = min(rb, tq)
    hp = MLA_HEADS // 2
    in_specs = [pl.BlockSpec((1, tq, 2 * HEAD_PAD), lambda bi, h, i: (bi, i, h)),
                pl.BlockSpec((1, c, 2 * HEAD_PAD), lambda bi, h, i: (bi, 0, h)),
                pl.BlockSpec((1, c, 2 * LANE), lambda bi, h, i: (bi, 0, h))]
    args = [q, kc, vc]
    n_lat = 0
    if kl is not None:
        s = kl.shape[1]
        tk = min(tk, s)
        n_lat = s // tk
        in_specs += [pl.BlockSpec((1, s, 2 * HEAD_PAD), lambda bi, h, i: (bi, 0, h)),
                     pl.BlockSpec((1, s, 2 * LANE), lambda bi, h, i: (bi, 0, h))]
        args += [kl, vl]
    return pl.pallas_call(
        functools.partial(_mla_kernel, tk=tk, n_lat=n_lat, rb=rb),
        grid=(b, hp, sq // tq),
        in_specs=in_specs,
        out_specs=pl.BlockSpec((1, tq, LANE), lambda bi, h, i: (bi, i, h)),
        out_shape=jax.ShapeDtypeStruct((b, sq, MLA_HEADS * MLA_V), BF16),
        scratch_shapes=[pltpu.VMEM((2, tq, LANE), F32), pltpu.VMEM((2, tq, 2 * LANE), F32)],
        compiler_params=_cparams(("arbitrary", "arbitrary", "arbitrary")),
        name="mla_attn",
    )(*args)


def _swa_kernel(*refs, band):
    if band:
        sink_ref, q_ref, kc_ref, vc_ref, kp_ref, k0_ref, kn_ref, vp_ref, v0_ref, vn_ref, o_ref = refs
        kband = jnp.concatenate([kp_ref[0], k0_ref[0], kn_ref[0]], axis=0)
        vband = jnp.concatenate([vp_ref[0], v0_ref[0], vn_ref[0]], axis=0)
    else:
        sink_ref, q_ref, kc_ref, vc_ref, o_ref = refs
    i = pl.program_id(1)
    nb = pl.num_programs(1)
    tq = q_ref.shape[1]
    nsub = tq // SWA_BLOCK
    n_ctx = kc_ref.shape[1]
    n_keys = n_ctx + (3 * SWA_BLOCK if band else 0)
    grp = SWA_HEADS // SWA_KV_HEADS
    rows_all = grp * SWA_BLOCK
    nt = (((1,), (1,)), ((), ()))
    lane = lax.broadcasted_iota(jnp.int32, (SWA_BLOCK, LANE), 1)
    low = lane < SWA_HEAD_DIM
    row = lax.broadcasted_iota(jnp.int32, (rows_all, 1), 0)
    if band:
        r_loc = lax.broadcasted_iota(jnp.int32, (rows_all, n_keys), 0) % SWA_BLOCK
        col = lax.broadcasted_iota(jnp.int32, (rows_all, n_keys), 1) - n_ctx
        prev_bias = jnp.where((col >= 0) & (col < SWA_BLOCK), NEG_INF, 0.0)
        next_bias = jnp.where(col >= 2 * SWA_BLOCK, NEG_INF, 0.0)
        base_bias = (jnp.where(r_loc > col, prev_bias, 0.0)
                     + jnp.where(col - 2 * SWA_BLOCK > r_loc, next_bias, 0.0))
    for r in range(nsub):
        rs = slice(r * SWA_BLOCK, (r + 1) * SWA_BLOCK)
        if band:
            bias = base_bias
            if r == 0:
                bias = jnp.minimum(bias, jnp.where(i == 0, prev_bias, 0.0))
            if r == nsub - 1:
                bias = jnp.minimum(bias, jnp.where(i == nb - 1, next_bias, 0.0))
            bs = slice(r * SWA_BLOCK, (r + 3) * SWA_BLOCK)
        for kvh in range(SWA_KV_HEADS):
            sl = slice(LANE * kvh, LANE * (kvh + 1))
            k, v = kc_ref[0, :, sl], vc_ref[0, :, sl]
            if band:
                k = jnp.concatenate([k, kband[bs, sl]], axis=0)
                v = jnp.concatenate([v, vband[bs, sl]], axis=0)
            v = jnp.concatenate([v, jnp.ones_like(v)], axis=1)
            parts, sink = [], None
            for t in range(grp // 2):
                pair = kvh * (grp // 2) + t
                qp = q_ref[0, rs, LANE * pair:LANE * (pair + 1)]
                parts += [jnp.where(low, qp, jnp.zeros_like(qp)), jnp.where(low, jnp.zeros_like(qp), qp)]
            for h in range(grp):
                sh = sink_ref[kvh * grp + h] * LOG2E
                sink = sh if sink is None else jnp.where(row < h * SWA_BLOCK, sink, sh)
            s = lax.dot_general(jnp.concatenate(parts, axis=0), k, nt, preferred_element_type=F32)
            if band:
                s = s + bias
            m = jnp.maximum(jnp.max(s, axis=1, keepdims=True), sink)
            p = jnp.exp2(s - m).astype(BF16)
            acc = jnp.dot(p, v, preferred_element_type=F32)
            o = acc[:, :LANE] / (acc[:, LANE:] + jnp.exp2(sink - m))
            for t in range(grp // 2):
                pair = kvh * (grp // 2) + t
                lo = o[(2 * t) * SWA_BLOCK:(2 * t + 1) * SWA_BLOCK]
                hi = o[(2 * t + 1) * SWA_BLOCK:(2 * t + 2) * SWA_BLOCK]
                o_ref[0, rs, LANE * pair:LANE * (pair + 1)] = jnp.where(low, lo, hi).astype(o_ref.dtype)


def _swa(sink, q, kc, vc, k=None, v=None, tq=1024):
    b, sq, _ = q.shape
    c = kc.shape[1]
    tq = min(tq, sq)
    nb = sq // tq
    band = k is not None
    blk = lambda rows, width, f: pl.BlockSpec((1, rows, width), f)
    in_specs = [pl.BlockSpec(memory_space=pltpu.SMEM),
                blk(tq, SWA_HEADS * SWA_HEAD_DIM, lambda bi, i: (bi, i, 0)),
                pl.BlockSpec((1, c, 2 * LANE), lambda bi, i: (bi, 0, 0)),
                pl.BlockSpec((1, c, 2 * LANE), lambda bi, i: (bi, 0, 0))]
    args = [sink, q, kc, vc]
    if band:
        per = tq // SWA_BLOCK
        prev = lambda bi, i: (bi, jnp.maximum(i * per - 1, 0), 0)
        cur = lambda bi, i: (bi, i, 0)
        nxt = lambda bi, i: (bi, jnp.minimum((i + 1) * per, sq // SWA_BLOCK - 1), 0)
        in_specs += [blk(SWA_BLOCK, 2 * LANE, prev), blk(tq, 2 * LANE, cur), blk(SWA_BLOCK, 2 * LANE, nxt),
                     blk(SWA_BLOCK, 2 * LANE, prev), blk(tq, 2 * LANE, cur), blk(SWA_BLOCK, 2 * LANE, nxt)]
        args += [k, k, k, v, v, v]
    return pl.pallas_call(
        functools.partial(_swa_kernel, band=band),
        grid=(b, nb),
        in_specs=in_specs,
        out_specs=blk(tq, SWA_HEADS * SWA_HEAD_DIM, lambda bi, i: (bi, i, 0)),
        out_shape=jax.ShapeDtypeStruct((b, sq, SWA_HEADS * SWA_HEAD_DIM), BF16),
        compiler_params=_cparams(("arbitrary", "arbitrary")),
        name="swa_attn",
    )(*args)


def _merge_kernel(x_ref, ya_ref, yb_ref, yc_ref, gt_ref, wb_ref, wo_ref, ng_ref, g1_ref, o_ref):
    d = x_ref.shape[2]
    merged = None
    for k, y_ref in enumerate((ya_ref, yb_ref, yc_ref)):
        t = jnp.dot(y_ref[0].astype(BF16), wb_ref[k], preferred_element_type=F32)
        t = gt_ref[0, :, d * k:d * (k + 1)].astype(F32) * t
        merged = t if merged is None else merged + t
    y = jnp.dot(merged.astype(BF16), wo_ref[...], preferred_element_type=F32)
    o_ref[0] = x_ref[0] + g1_ref[0] * _rms(y, ng_ref[...])


def _merge(x, ya, yb, yc, gt, wb, wo, ng, g1):
    b, s, d = x.shape
    tm = min(512, s)
    row = lambda width: pl.BlockSpec((1, tm, width), lambda bi, i: (bi, i, 0))
    return pl.pallas_call(
        _merge_kernel,
        grid=(b, s // tm),
        in_specs=[row(d), row(BRANCH_WIDTH), row(BRANCH_WIDTH), row(BRANCH_WIDTH), row(N_BRANCH * d),
                  pl.BlockSpec(wb.shape, lambda bi, i: (0, 0, 0)),
                  pl.BlockSpec(wo.shape, lambda bi, i: (0, 0)),
                  pl.BlockSpec(ng.shape, lambda bi, i: (0, 0)),
                  pl.BlockSpec((1, 1, d), lambda bi, i: (bi, 0, 0))],
        out_specs=row(d),
        out_shape=jax.ShapeDtypeStruct((b, s, d), F32),
        compiler_params=_cparams(("arbitrary", "arbitrary")),
        name="merge",
    )(x, ya, yb, yc, gt, wb, wo, ng, g1)


def _ffn_kernel(x_ref, xp_ref, xn_ref, sh_ref, sc_ref, g2_ref, ng_in_ref, ng_out_ref,
                wu_ref, cw_ref, cb_ref, wd_ref, o_ref, *, chunk):
    i = pl.program_id(1)
    nt = pl.num_programs(1)
    g = ng_in_ref[...]
    sh = sh_ref[0]
    sc = sc_ref[0]

    def mod(xf):
        return (_rms(xf, g) * (1.0 + sc) + sh).astype(BF16)

    x = x_ref[0]
    tm = x.shape[0]
    hb = mod(jnp.concatenate([x, xp_ref[0], xn_ref[0]], axis=0))
    dff = wd_ref.shape[0]
    f = None
    for c0 in range(0, dff, chunk):
        halves = []
        for off in (c0, dff + c0):
            p = jnp.dot(hb, wu_ref[:, off:off + chunk], preferred_element_type=F32)
            pprev = jnp.where(i > 0, p[tm + 7:tm + 8, :], 0.0)
            pnext = jnp.where(i < nt - 1, p[tm + 8:tm + 9, :], 0.0)
            halves.append(_dwconv3(p[:tm], pprev, pnext, cw_ref.at[:, off:off + chunk],
                                   cb_ref.at[:, off:off + chunk]))
        a, bb = halves
        act = (a * jax.nn.sigmoid(a) * bb).astype(BF16)
        t = jnp.dot(act, wd_ref[c0:c0 + chunk, :], preferred_element_type=F32)
        f = t if f is None else f + t
    o_ref[0] = x + g2_ref[0] * _rms(f, ng_out_ref[...])


def _ffn(x, sh, sc, g2, ng_in, ng_out, wu, cw, cb, wd):
    b, s, d = x.shape
    tm = min(FFN_ROWS, s)
    r8 = tm // 8
    dff = wd.shape[0]
    chunk = dff // FFN_CHUNKS if dff % (FFN_CHUNKS * LANE) == 0 else dff
    row = pl.BlockSpec((1, tm, d), lambda bi, i: (bi, i, 0))
    vec = pl.BlockSpec((1, 1, d), lambda bi, i: (bi, 0, 0))
    const2 = lambda a: pl.BlockSpec(a.shape, lambda bi, i: (0, 0), pipeline_mode=pl.Buffered(1))
    return pl.pallas_call(
        functools.partial(_ffn_kernel, chunk=chunk),
        grid=(b, s // tm),
        in_specs=[row,
                  pl.BlockSpec((1, 8, d), lambda bi, i: (bi, jnp.maximum(i * r8 - 1, 0), 0)),
                  pl.BlockSpec((1, 8, d), lambda bi, i: (bi, jnp.minimum((i + 1) * r8, s // 8 - 1), 0)),
                  vec, vec, vec, const2(ng_in), const2(ng_out),
                  const2(wu), const2(cw), const2(cb), const2(wd)],
        out_specs=row,
        out_shape=jax.ShapeDtypeStruct((b, s, d), F32),
        compiler_params=_cparams(("arbitrary", "arbitrary")),
        name="conv_ffn",
    )(x, x, x, sh, sc, g2, ng_in, ng_out, wu, cw, cb, wd)


HY_PASSES = 1
HY_UNROLL = 2
HY_UNROLL2 = 6


def _split(x):
    hi = x.astype(BF16)
    return hi, (x - hi.astype(F32)).astype(BF16)


def _cdot(c_hi, c_lo, x, passes=HY_PASSES):
    x_hi, x_lo = _split(x)
    acc = jnp.dot(c_hi, x_hi, preferred_element_type=F32)
    if passes > 1:
        acc = acc + jnp.dot(c_lo, x_hi, preferred_element_type=F32)
        acc = acc + jnp.dot(c_hi, x_lo, preferred_element_type=F32)
    return acc


def _np_split(m):
    m = jnp.asarray(m, F32)
    hi = m.astype(BF16)
    return hi, (m - hi.astype(F32)).astype(BF16)


def _dft_consts(n_blocks, nonzero_blocks):
    nb = n_blocks
    n = LANE * nb
    dh = nb // 2 + 1
    dhp = -(-dh // 8) * 8
    d = np.arange(dh)[:, None]
    b = np.arange(nonzero_blocks)[None, :]
    ang = 2 * np.pi * ((d * b) % nb) / nb
    f1 = np.zeros((2 * dhp, nonzero_blocks))
    f1[:dh] = np.cos(ang)
    f1[dhp:dhp + dh] = -np.sin(ang)
    a = np.arange(LANE)
    ang2 = 2 * np.pi * ((a[:, None] * a[None, :]) % LANE) / LANE
    ar, ai = np.cos(ang2), -np.sin(ang2)
    f2 = np.block([[ar, -ai], [ai, ar]])
    f2inv = np.block([[ar, ai], [-ai, ar]])
    bo = np.arange(nb // 2)[:, None]
    do = np.arange(dh)[None, :]
    wd = np.where((do == 0) | (do == nb // 2), 1.0, 2.0) / n
    ang3 = 2 * np.pi * ((bo * do) % nb) / nb
    f3 = np.zeros((nb // 2, 2 * dhp))
    f3[:, :dh] = wd * np.cos(ang3)
    f3[:, dhp:dhp + dh] = -wd * np.sin(ang3)
    step_ang = 2 * np.pi * a / n
    tw_step = np.stack([np.broadcast_to(np.cos(step_ang)[:, None], (LANE, LANE)),
                        np.broadcast_to(-np.sin(step_ang)[:, None], (LANE, LANE))])
    eye = np.eye(SUBLANES)
    return dict(dh=dh, dhp=dhp, f1=_np_split(np.kron(f1, eye)), f2=_np_split(f2), f2inv=_np_split(f2inv),
                f3=_np_split(np.kron(f3, eye)), tw_step=jnp.asarray(tw_step, F32))


def _stage1(x_ref, f1_hi, f1_lo, g_scr, passes):
    n_in, n_out = x_ref.shape[0], g_scr.shape[0]

    def body(t, carry):
        r0 = pl.ds(pl.multiple_of(2 * t * SUBLANES, SUBLANES), SUBLANES)
        r1 = pl.ds(pl.multiple_of((2 * t + 1) * SUBLANES, SUBLANES), SUBLANES)
        xa = jnp.concatenate([x_ref[:, r0, :].reshape(n_in * SUBLANES, LANE),
                              x_ref[:, r1, :].reshape(n_in * SUBLANES, LANE)], axis=1)
        g = _cdot(f1_hi, f1_lo, xa, passes)
        g_scr[:, r0, :] = g[:, :LANE].reshape(n_out, SUBLANES, LANE)
        g_scr[:, r1, :] = g[:, LANE:].reshape(n_out, SUBLANES, LANE)
        return carry
    lax.fori_loop(0, LANE // (2 * SUBLANES), body, 0, unroll=HY_UNROLL)


def _twiddle_init(tw_scr):
    tw_scr[0] = jnp.ones((LANE, LANE), F32)
    tw_scr[1] = jnp.zeros((LANE, LANE), F32)


def _twiddle_next(twr, twi, step_ref):
    sr, si = step_ref[0], step_ref[1]
    return twr * sr - twi * si, twr * si + twi * sr


def _twiddle_pair(tw_scr, step_ref):
    t0r, t0i = tw_scr[0], tw_scr[1]
    t1r, t1i = _twiddle_next(t0r, t0i, step_ref)
    t2r, t2i = _twiddle_next(t1r, t1i, step_ref)
    tw_scr[0] = t2r
    tw_scr[1] = t2i
    return jnp.concatenate([t0r, t1r], axis=1), jnp.concatenate([t0i, t1i], axis=1)


def _spectrum_kernel(k_ref, inv_ref, f1h_ref, f1l_ref, f2h_ref, f2l_ref, step_ref, hf_ref, g_scr, tw_scr,
                     *, dh, dhp):
    _stage1(k_ref, f1h_ref[...], f1l_ref[...], g_scr, 3)
    _twiddle_init(tw_scr)
    inv = inv_ref[...]

    def body(h, carry):
        d = 2 * h
        gr = jnp.concatenate([g_scr[d], g_scr[d + 1]], axis=1)
        gi = jnp.concatenate([g_scr[dhp + d], g_scr[dhp + d + 1]], axis=1)
        twr, twi = _twiddle_pair(tw_scr, step_ref)
        t = jnp.concatenate([gr * twr - gi * twi, gr * twi + gi * twr], axis=0)
        y = _cdot(f2h_ref[...], f2l_ref[...], t, passes=3)
        hf_ref[0, d] = y[:LANE, :LANE] * inv
        hf_ref[0, d + 1] = y[:LANE, LANE:] * inv
        hf_ref[1, d] = y[LANE:, :LANE] * inv
        hf_ref[1, d + 1] = y[LANE:, LANE:] * inv
        return carry
    lax.fori_loop(0, dhp // 2, body, 0, unroll=HY_UNROLL2)


def _hyena_spectrum(kern, inv_norm):
    tiles, n, _ = kern.shape
    c = tiles * LANE
    nb = n // LANE
    cst = _dft_consts(nb, nb)
    dh, dhp = cst["dh"], cst["dhp"]
    const = lambda a: pl.BlockSpec(a.shape, lambda j: (0,) * a.ndim, pipeline_mode=pl.Buffered(1))
    consts = [*cst["f1"], *cst["f2"], cst["tw_step"]]
    return pl.pallas_call(
        functools.partial(_spectrum_kernel, dh=dh, dhp=dhp),
        grid=(c // LANE,),
        in_specs=[pl.BlockSpec((None, nb, LANE, LANE), lambda j: (j, 0, 0, 0), pipeline_mode=pl.Buffered(1)),
                  pl.BlockSpec((1, LANE), lambda j: (0, j))]
        + [const(a) for a in consts],
        out_specs=pl.BlockSpec((None, 2, dhp, LANE, LANE), lambda j: (j, 0, 0, 0, 0)),
        out_shape=jax.ShapeDtypeStruct((c // LANE, 2, dhp, LANE, LANE), F32),
        scratch_shapes=[pltpu.VMEM((2 * dhp, LANE, LANE), F32), pltpu.VMEM((2, LANE, LANE), F32)],
        compiler_params=_cparams(("arbitrary",)),
        name="hyena_spectrum",
    )(kern.reshape(tiles, nb, LANE, LANE), inv_norm, *consts)


def _conv_kernel(z_ref, gate_ref, skip_ref, hf_ref, f1h_ref, f1l_ref, f2h_ref, f2l_ref, f2ih_ref, f2il_ref,
                 f3h_ref, f3l_ref, step_ref, o_ref, g_scr, tw_scr, *, dh, dhp):
    nbh = z_ref.shape[1]
    _stage1(z_ref.at[0], f1h_ref[...], f1l_ref[...], g_scr, HY_PASSES)
    _twiddle_init(tw_scr)

    def body(h, carry):
        d = 2 * h
        gr = jnp.concatenate([g_scr[d], g_scr[d + 1]], axis=1)
        gi = jnp.concatenate([g_scr[dhp + d], g_scr[dhp + d + 1]], axis=1)
        twr, twi = _twiddle_pair(tw_scr, step_ref)
        t = jnp.concatenate([gr * twr - gi * twi, gr * twi + gi * twr], axis=0)
        y = _cdot(f2h_ref[...], f2l_ref[...], t)
        yr, yi = y[:LANE], y[LANE:]
        hr = jnp.concatenate([hf_ref[0, d], hf_ref[0, d + 1]], axis=1)
        hi = jnp.concatenate([hf_ref[1, d], hf_ref[1, d + 1]], axis=1)
        zz = jnp.concatenate([yr * hr - yi * hi, yr * hi + yi * hr], axis=0)
        u = _cdot(f2ih_ref[...], f2il_ref[...], zz)
        ur, ui = u[:LANE], u[LANE:]
        vr = ur * twr + ui * twi
        vi = ui * twr - ur * twi
        g_scr[d] = vr[:, :LANE]
        g_scr[d + 1] = vr[:, LANE:]
        g_scr[dhp + d] = vi[:, :LANE]
        g_scr[dhp + d + 1] = vi[:, LANE:]
        return carry
    lax.fori_loop(0, dhp // 2, body, 0, unroll=HY_UNROLL2)

    skip = skip_ref[...]
    f3h, f3l = f3h_ref[...], f3l_ref[...]

    def out_body(t, carry):
        r0 = pl.ds(pl.multiple_of(2 * t * SUBLANES, SUBLANES), SUBLANES)
        r1 = pl.ds(pl.multiple_of((2 * t + 1) * SUBLANES, SUBLANES), SUBLANES)
        ga = jnp.concatenate([g_scr[:, r0, :].reshape(2 * dhp * SUBLANES, LANE),
                              g_scr[:, r1, :].reshape(2 * dhp * SUBLANES, LANE)], axis=1)
        conv = _cdot(f3h, f3l, ga)
        for rows, c in ((r0, conv[:, :LANE]), (r1, conv[:, LANE:])):
            o_ref[0, :, rows, :] = gate_ref[0, :, rows, :] * (
                c.reshape(nbh, SUBLANES, LANE) + z_ref[0, :, rows, :] * skip)
        return carry
    lax.fori_loop(0, LANE // (2 * SUBLANES), out_body, 0, unroll=HY_UNROLL)


def _hyena_conv(z, gate, skip, hf, tile0, cst):
    b, l, w = z.shape
    nbh = l // LANE
    dh, dhp = cst["dh"], cst["dhp"]
    consts = [*cst["f1"], *cst["f2"], *cst["f2inv"], *cst["f3"], cst["tw_step"]]
    const = lambda a: pl.BlockSpec(a.shape, lambda j, bi: (0,) * a.ndim, pipeline_mode=pl.Buffered(1))
    tile = pl.BlockSpec((1, nbh, LANE, LANE), lambda j, bi: (bi, 0, 0, j))
    z, gate = z.reshape(b, nbh, LANE, w), gate.reshape(b, nbh, LANE, w)
    return pl.pallas_call(
        functools.partial(_conv_kernel, dh=dh, dhp=dhp),
        grid=(w // LANE, b),
        in_specs=[tile, tile, pl.BlockSpec((1, LANE), lambda j, bi: (0, j)),
                  pl.BlockSpec((None, 2, dhp, LANE, LANE), lambda j, bi: (j + tile0, 0, 0, 0, 0),
                               pipeline_mode=pl.Buffered(1))]
        + [const(a) for a in consts],
        out_specs=tile,
        out_shape=jax.ShapeDtypeStruct((b, nbh, LANE, w), F32),
        scratch_shapes=[pltpu.VMEM((2 * dhp, LANE, LANE), F32), pltpu.VMEM((2, LANE, LANE), F32)],
        compiler_params=_cparams(("arbitrary", "arbitrary")),
        name="hyena_conv",
    )(z, gate, skip, hf, *consts).reshape(b, l, w)


def _filter_mlp_kernel(ft_ref, w1_ref, b1_ref, w2_ref, b2_ref, fr_ref, w3_ref, dl_ref, k_ref, s_ref):
    i = pl.program_id(0)
    ft = ft_ref[...]

    def mm(x, w):
        x_hi, x_lo = _split(x)
        w_hi, w_lo = _split(w)
        return (jnp.dot(x_hi, w_hi, preferred_element_type=F32) + jnp.dot(x_lo, w_hi, preferred_element_type=F32)
                + jnp.dot(x_hi, w_lo, preferred_element_type=F32))

    hid = jnp.sin(fr_ref[0:1, :] * (mm(ft, w1_ref[...]) + b1_ref[...]))
    hid = jnp.sin(fr_ref[1:2, :] * (mm(hid, w2_ref[...]) + b2_ref[...]))
    h = mm(hid, w3_ref[0])
    k = h * jnp.exp(-ft[:, 0:1] * dl_ref[...]) * ft[:, LANE - 1:LANE]
    for j in range(k_ref.shape[0]):
        k_ref[j] = k[:, LANE * j:LANE * (j + 1)]

    @pl.when(i == 0)
    def _():
        s_ref[...] = jnp.zeros(s_ref.shape, F32)
    s_ref[...] += jnp.sum(jnp.abs(k), axis=0, keepdims=True)


def _hyena_filter_taps(n_tokens, w1, b1, w2, b2, freq, w3):
    L = n_tokens
    n = 2 * L
    idx = jnp.arange(n)
    m = jnp.where(idx < L, idx, n - idx).astype(F32)
    t = m / max(L - 1, 1)
    bands = jnp.linspace(1e-4, HY_BANDS - 1, HY_BANDS, dtype=F32)
    ang = (2.0 * math.pi / L) * m[:, None] * bands
    n_feat = 2 * HY_BANDS + 1
    feats = jnp.concatenate([t[:, None], jnp.cos(ang), -jnp.sin(ang),
                             jnp.zeros((n, LANE - n_feat - 1), F32),
                             (idx != L).astype(F32)[:, None]], axis=-1)
    w1p = jnp.pad(w1, ((0, LANE - n_feat), (0, 0)))
    cw = HY_ORDER * HY_WIDTH
    w3d = jnp.moveaxis(w3.reshape(HY_HIDDEN, 2, cw), 1, 0)
    deltas = jnp.abs(jnp.linspace(math.log(HY_TARGET) / HY_SLOW_PCT, math.log(HY_TARGET) / HY_FAST_PCT,
                                  HY_WIDTH, dtype=F32))
    dl = jnp.tile(deltas, HY_ORDER)[None, :]
    tr = min(1024, L)
    nt = n // tr
    const = lambda a: pl.BlockSpec(a.shape, lambda i: (0,) * a.ndim)
    b1r, b2r = b1[None, :], b2[None, :]
    return pl.pallas_call(
        _filter_mlp_kernel,
        grid=(nt,),
        in_specs=[pl.BlockSpec((tr, LANE), lambda i: (i, 0)), const(w1p), const(b1r), const(w2), const(b2r),
                  const(freq), pl.BlockSpec((1, HY_HIDDEN, cw), lambda i: (i // (nt // 2), 0, 0)), const(dl)],
        out_specs=[pl.BlockSpec((cw // LANE, tr, LANE), lambda i: (0, i, 0)), pl.BlockSpec((1, cw), lambda i: (0, 0))],
        out_shape=[jax.ShapeDtypeStruct((cw // LANE, n, LANE), F32), jax.ShapeDtypeStruct((1, cw), F32)],
        compiler_params=_cparams(("arbitrary",)),
        name="hyena_filter_mlp",
    )(feats, w1p, b1r, w2, b2r, freq, w3d, dl)


def _hyena_long(z, x1, x2, taps, norms, skip):
    b, l, w = z.shape
    nb = 2 * l // LANE
    hf = _hyena_spectrum(taps, 1.0 / norms)
    cst = _dft_consts(nb, nb // 2)
    for o, gate in enumerate((x1, x2)):
        z = _hyena_conv(z, gate, skip[o][None, :], hf, o * (w // LANE), cst)
    return z


def _dense_dft_consts(length):
    n = 2 * length
    h = length + 1
    hp = -(-h // 8) * 8
    k = np.arange(h)[:, None]
    pos = np.arange(n)[None, :]
    ang = 2 * np.pi * ((k * pos) % n) / n
    fwd = np.zeros((2 * hp, n))
    fwd[:h] = np.cos(ang)
    fwd[hp:hp + h] = -np.sin(ang)
    wk = np.where((k == 0) | (k == length), 1.0, 2.0) / n
    inv = np.zeros((length, 2 * hp))
    inv[:, :h] = (wk * np.cos(ang[:, :length])).T
    inv[:, hp:hp + h] = (-wk * np.sin(ang[:, :length])).T
    return dict(hp=hp, full=_np_split(fwd), fwd=_np_split(fwd[:, :length]), inv=_np_split(inv))


def _hyena_short_kernel(z_ref, x1_ref, x2_ref, k_ref, inv_ref, skip_ref, ffh_ref, ffl_ref, fh_ref, fl_ref,
                        fih_ref, fil_ref, o_ref, *, hp):
    z = z_ref[0]
    w = z.shape[1]
    for o, gate_ref in enumerate((x1_ref, x2_ref)):
        cols = slice(o * w, (o + 1) * w)
        hf = _cdot(ffh_ref[...], ffl_ref[...], k_ref[:, cols], passes=3) * inv_ref[:, cols]
        x = _cdot(fh_ref[...], fl_ref[...], z, passes=3)
        xr, xi, hr, hi = x[:hp], x[hp:], hf[:hp], hf[hp:]
        zz = jnp.concatenate([xr * hr - xi * hi, xr * hi + xi * hr], axis=0)
        conv = _cdot(fih_ref[...], fil_ref[...], zz, passes=3)
        z = gate_ref[0] * (conv + z * skip_ref[o:o + 1, :])
    o_ref[0] = z


def _hyena_short(z, x1, x2, taps, norms, skip):
    b, l, w = z.shape
    taps = jnp.moveaxis(taps, 0, 1).reshape(2 * l, -1)
    cst = _dense_dft_consts(l)
    consts = [*cst["full"], *cst["fwd"], *cst["inv"]]
    inv_norm = 1.0 / norms
    const = lambda a: pl.BlockSpec(a.shape, lambda bi: (0,) * a.ndim)
    tile = pl.BlockSpec((1, l, w), lambda bi: (bi, 0, 0))
    return pl.pallas_call(
        functools.partial(_hyena_short_kernel, hp=cst["hp"]),
        grid=(b,),
        in_specs=[tile, tile, tile, const(taps), const(inv_norm), const(skip)] + [const(a) for a in consts],
        out_specs=tile,
        out_shape=jax.ShapeDtypeStruct((b, l, w), F32),
        compiler_params=_cparams(("arbitrary",)),
        name="hyena_short",
    )(z, x1, x2, taps, inv_norm, skip, *consts)


def _rope_tables(rows, rot_dim, head_lanes, rope_off, identity_rows):
    half = rot_dim // 2
    n_freq = rot_dim // 4
    pos = np.arange(rows * GRID_W)
    inv_freq = ROPE_THETA ** (-np.arange(n_freq, dtype=np.float64) / n_freq)
    ang = np.concatenate([(pos // GRID_W)[:, None] * inv_freq, (pos % GRID_W)[:, None] * inv_freq], axis=-1)
    cos, sin = np.cos(ang), np.sin(ang)
    n = pos.shape[0]
    ct = np.ones((n, LANE))
    sa = np.zeros((n, LANE))
    sb = np.zeros((n, LANE))
    for h0 in range(0, LANE, head_lanes):
        lo = h0 + rope_off
        ct[:, lo:lo + half] = cos
        ct[:, lo + half:lo + rot_dim] = cos
        sa[:, lo:lo + half] = -sin
        sb[:, lo + half:lo + rot_dim] = sin
    ident = (np.ones((identity_rows, LANE)), np.zeros((identity_rows, LANE)), np.zeros((identity_rows, LANE)))
    lat = tuple(jnp.asarray(t, F32) for t in (ct, sa, sb))
    ctx = tuple(jnp.asarray(t, F32) for t in ident)
    return lat, ctx


def _pack_w_in(w):
    d = w.shape[0]
    o = 0
    mq = w[:, o:o + MLA_HEADS * (MLA_NOPE + MLA_ROPE)]; o += MLA_HEADS * (MLA_NOPE + MLA_ROPE)
    mckv = w[:, o:o + KV_RANK]; o += KV_RANK
    mkr = w[:, o:o + MLA_ROPE]; o += MLA_ROPE
    sq = w[:, o:o + SWA_HEADS * SWA_HEAD_DIM]; o += SWA_HEADS * SWA_HEAD_DIM
    sk = w[:, o:o + SWA_KV_HEADS * SWA_HEAD_DIM]; o += SWA_KV_HEADS * SWA_HEAD_DIM
    sv = w[:, o:o + SWA_KV_HEADS * SWA_HEAD_DIM]; o += SWA_KV_HEADS * SWA_HEAD_DIM
    hy = w[:, o:o + (HY_ORDER + 1) * HY_WIDTH]; o += (HY_ORDER + 1) * HY_WIDTH
    gt = w[:, o:]
    pad_q = HEAD_PAD - MLA_NOPE - MLA_ROPE
    mq = jnp.pad(mq.reshape(d, MLA_HEADS, MLA_NOPE + MLA_ROPE), ((0, 0), (0, 0), (0, pad_q))).reshape(d, -1)
    mkr = jnp.pad(mkr, ((0, 0), (MLA_NOPE, pad_q)))
    dup = lambda t: jnp.repeat(t.reshape(d, SWA_KV_HEADS, 1, SWA_HEAD_DIM), 2, axis=2).reshape(d, -1)
    return jnp.concatenate([mq, mckv, mkr, sq, dup(sk), dup(sv), hy, gt], axis=1).astype(BF16)


def _pack_w_kv(w):
    r = w.shape[0]
    w = w.reshape(r, MLA_HEADS, MLA_NOPE + MLA_V)
    k = jnp.pad(w[..., :MLA_NOPE], ((0, 0), (0, 0), (0, HEAD_PAD - MLA_NOPE))).reshape(r, -1)
    v = w[..., MLA_NOPE:].reshape(r, -1)
    return jnp.concatenate([k, v], axis=1).astype(BF16)


def kernel(x, c, ctx, c_ctx, w_mod, b_mod, norm_g, w_in, kv_norm_g, w_kv_up, swa_sink, hy_conv_w, hy_conv_b,
           hy_w1, hy_b1, hy_w2, hy_b2, hy_freq, hy_w3, hy_skip, w_branch, w_out, w_up, ffn_conv_w, ffn_conv_b,
           w_down):
    b, s, d = x.shape
    n_ctx = ctx.shape[1]
    depth = w_mod.shape[0]
    rows = s // GRID_W
    rope_m, rope_m_ctx = _rope_tables(rows, MLA_ROPE, HEAD_PAD, MLA_NOPE, n_ctx)
    rope_s, rope_s_ctx = _rope_tables(rows, SWA_HEAD_DIM, SWA_HEAD_DIM, 0, n_ctx)

    cvec = jnp.concatenate([c, c_ctx[None, :], jnp.zeros((8 - b - 1, d), F32)], axis=0)
    mod_all = _modulation(cvec, w_mod, b_mod)

    x_lat, x_ctx = x, ctx
    for l in range(depth):
        with_ctx = l < depth - 1
        m = mod_all[l].reshape(8, 6, d)
        lat = [m[:b, k][:, None, :] for k in range(6)]
        cx = [jnp.broadcast_to(m[b, k][None, None, :], (b, 1, d)) for k in range(6)]
        ng = [norm_g[l, k][None, :] for k in range(4)]
        w_pack = _pack_w_in(w_in[l])
        wkv_pack = _pack_w_kv(w_kv_up[l])
        kvg = kv_norm_g[l][None, :]
        cw, cb = hy_conv_w[l], hy_conv_b[l][None, :]

        q, kk, vv, sq, sk, sv, z, x1, x2, gt = _in_proj(
            x_lat, lat[0], lat[1], ng[0], w_pack, wkv_pack, kvg, rope_m, rope_s, cw, cb)
        qc, kkc, vvc, sqc, skc, svc, zc, x1c, x2c, gtc = _in_proj(
            x_ctx, cx[0], cx[1], ng[0], w_pack, wkv_pack, kvg, rope_m_ctx, rope_s_ctx, cw, cb)

        y_a = _mla(q, kkc, vvc, kk, vv)
        y_b = _swa(swa_sink[l], sq, skc, svc, sk, sv)
        hy_mlp = (hy_w1[l], hy_b1[l], hy_w2[l], hy_b2[l], hy_freq[l], hy_w3[l])
        y_c = _hyena_long(z, x1, x2, *_hyena_filter_taps(s, *hy_mlp), hy_skip[l])

        wb = w_branch[l].astype(BF16)
        wo = w_out[l].astype(BF16)
        wu = w_up[l].astype(BF16)
        wd = w_down[l].astype(BF16)
        fcw, fcb = ffn_conv_w[l], ffn_conv_b[l][None, :]

        x_lat = _merge(x_lat, y_a, y_b, y_c, gt, wb, wo, ng[1], lat[2])
        x_lat = _ffn(x_lat, lat[3], lat[4], lat[5], ng[2], ng[3], wu, fcw, fcb, wd)

        if with_ctx:
            yc_a = _mla(qc, kkc, vvc)
            yc_b = _swa(swa_sink[l], sqc, skc, svc)
            yc_c = _hyena_short(zc, x1c, x2c, *_hyena_filter_taps(n_ctx, *hy_mlp), hy_skip[l])
            x_ctx = _merge(x_ctx, yc_a, yc_b, yc_c, gtc, wb, wo, ng[1], cx[2])
            x_ctx = _ffn(x_ctx, cx[3], cx[4], cx[5], ng[2], ng[3], wu, fcw, fcb, wd)
    return x_lat
```

```python
import functools
import math

import numpy as np
import jax
import jax.numpy as jnp
from jax import lax
from jax.experimental import pallas as pl
from jax.experimental.pallas import tpu as pltpu

F32 = jnp.float32
BF16 = jnp.bfloat16

GRID_W = 64
EPS = 1e-6
ROPE_THETA = 10000.0
NEG_INF = -1e30
MLA_HEADS = 8
MLA_NOPE = 64
MLA_ROPE = 32
MLA_V = 64
KV_RANK = 256
MLA_SCALE = (MLA_NOPE + MLA_ROPE) ** -0.5
LOG2E = math.log2(math.e)
SWA_HEADS = 8
SWA_KV_HEADS = 2
SWA_HEAD_DIM = 64
SWA_BLOCK = 128
SWA_SCALE = SWA_HEAD_DIM ** -0.5
HY_WIDTH = 512
HY_ORDER = 2
HY_BANDS = 16
HY_HIDDEN = 64
HY_TARGET = 1e-2
HY_FAST_PCT = 0.3
HY_SLOW_PCT = 1.5
N_BRANCH = 3
BRANCH_WIDTH = 512
LANE = 128
SUBLANES = 8
HEAD_PAD = 128

VMEM_LIMIT = 56 * 1024 * 1024
IN_PROJ_ROWS = 512
FFN_ROWS = 512
FFN_CHUNKS = 1
MERGE_ROWS = 1024

C_Q = 0
C_CKV = C_Q + MLA_HEADS * HEAD_PAD
C_KR = C_CKV + KV_RANK
C_SQ = C_KR + HEAD_PAD
C_SK = C_SQ + SWA_HEADS * SWA_HEAD_DIM
C_SV = C_SK + 2 * SWA_KV_HEADS * SWA_HEAD_DIM
C_HY = C_SV + 2 * SWA_KV_HEADS * SWA_HEAD_DIM
C_GT = C_HY + (HY_ORDER + 1) * HY_WIDTH


def _cparams(sem):
    return pltpu.CompilerParams(dimension_semantics=sem, vmem_limit_bytes=VMEM_LIMIT)


def _rms(xf, g):
    return xf * lax.rsqrt(jnp.mean(xf * xf, axis=-1, keepdims=True) + EPS) * g


def _rope(x, cos, sa, sb, half):
    return x * cos + pltpu.roll(x, LANE - half, 1) * sa + pltpu.roll(x, half, 1) * sb


def _dwconv3(p, prev_row, next_row, cw_ref, cb_ref):
    tm = p.shape[0]
    rows = lax.broadcasted_iota(jnp.int32, (tm, 1), 0)
    up = jnp.where(rows == 0, prev_row, pltpu.roll(p, 1, 0))
    dn = jnp.where(rows == tm - 1, next_row, pltpu.roll(p, tm - 1, 0))
    return up * cw_ref[0:1, :] + p * cw_ref[1:2, :] + dn * cw_ref[2:3, :] + cb_ref[...]


def _mod_kernel(c_ref, w_ref, b_ref, o_ref):
    c = c_ref[...]
    a = c * jax.nn.sigmoid(c)
    a_hi = a.astype(BF16)
    a_lo = (a - a_hi.astype(F32)).astype(BF16)
    w = w_ref[0]
    w_hi = w.astype(BF16)
    w_lo = (w - w_hi.astype(F32)).astype(BF16)
    acc = jnp.dot(a_hi, w_hi, preferred_element_type=F32)
    acc += jnp.dot(a_lo, w_hi, preferred_element_type=F32)
    acc += jnp.dot(a_hi, w_lo, preferred_element_type=F32)
    o_ref[0] = acc + b_ref[0]


def _modulation(cvec, w_mod, b_mod):
    depth, d, n = w_mod.shape
    tn = 1536
    return pl.pallas_call(
        _mod_kernel,
        grid=(depth, n // tn),
        in_specs=[pl.BlockSpec((8, d), lambda l, j: (0, 0)),
                  pl.BlockSpec((1, d, tn), lambda l, j: (l, 0, j)),
                  pl.BlockSpec((1, 1, tn), lambda l, j: (l, 0, j))],
        out_specs=pl.BlockSpec((1, 8, tn), lambda l, j: (l, 0, j)),
        out_shape=jax.ShapeDtypeStruct((depth, 8, n), F32),
        compiler_params=_cparams(("arbitrary", "arbitrary")),
        name="adaln_mod",
    )(cvec, w_mod, b_mod.reshape(depth, 1, n))


def _in_proj_kernel(x_ref, xp_ref, xn_ref, sh_ref, sc_ref, g_ref, w_ref, wkv_ref, kvg_ref,
                    cm_ref, sma_ref, smb_ref, cs_ref, ssa_ref, ssb_ref, cw_ref, cb_ref,
                    q_ref, kk_ref, vv_ref, sq_ref, sk_ref, sv_ref, z_ref, x1_ref, x2_ref, gt_ref):
    i = pl.program_id(1)
    nt = pl.num_programs(1)
    g = g_ref[...]
    sh = sh_ref[0]
    sc = sc_ref[0]

    def mod(xf):
        return (_rms(xf, g) * (1.0 + sc) + sh).astype(BF16)

    tm = x_ref.shape[1]
    hb_ext = mod(jnp.concatenate([x_ref[0], xp_ref[0], xn_ref[0]], axis=0))
    hb = hb_ext[:tm]

    def proj(lo, hi, lhs=hb):
        return jnp.dot(lhs, w_ref[:, lo:hi], preferred_element_type=F32)

    cm, sma, smb = cm_ref[...], sma_ref[...], smb_ref[...]
    cs, ssa, ssb = cs_ref[...], ssa_ref[...], ssb_ref[...]

    pq = proj(C_Q, C_CKV)
    for h in range(MLA_HEADS):
        xh = pq[:, HEAD_PAD * h:HEAD_PAD * (h + 1)]
        q_ref[0, :, HEAD_PAD * h:HEAD_PAD * (h + 1)] = (
            _rope(xh, cm, sma, smb, MLA_ROPE // 2) * (MLA_SCALE * LOG2E)).astype(BF16)

    ckv = proj(C_CKV, C_KR)
    cn = _rms(ckv, kvg_ref[...]).astype(BF16)
    kv = jnp.dot(cn, wkv_ref[...], preferred_element_type=F32)
    krr = _rope(proj(C_KR, C_SQ), cm, sma, smb, MLA_ROPE // 2)
    for h in range(MLA_HEADS):
        kk_ref[0, :, HEAD_PAD * h:HEAD_PAD * (h + 1)] = (
            kv[:, HEAD_PAD * h:HEAD_PAD * (h + 1)] + krr).astype(BF16)
    ones = jnp.ones((hb.shape[0], LANE), BF16)
    for j in range(MLA_HEADS // 2):
        v0 = MLA_HEADS * HEAD_PAD + LANE * j
        vv_ref[0, :, 2 * LANE * j:2 * LANE * j + LANE] = kv[:, v0:v0 + LANE].astype(BF16)
        vv_ref[0, :, 2 * LANE * j + LANE:2 * LANE * (j + 1)] = ones

    psq = proj(C_SQ, C_SK)
    for j in range(SWA_HEADS // 2):
        xh = psq[:, LANE * j:LANE * (j + 1)]
        sq_ref[0, :, LANE * j:LANE * (j + 1)] = (
            _rope(xh, cs, ssa, ssb, SWA_HEAD_DIM // 2) * (SWA_SCALE * LOG2E)).astype(BF16)
    psk = proj(C_SK, C_SV)
    for k in range(SWA_KV_HEADS):
        sk_ref[0, :, LANE * k:LANE * (k + 1)] = _rope(
            psk[:, LANE * k:LANE * (k + 1)], cs, ssa, ssb, SWA_HEAD_DIM // 2).astype(BF16)
    sv_ref[0] = proj(C_SV, C_HY).astype(BF16)

    ph = proj(C_HY, C_GT, hb_ext)
    pprev = jnp.where(i > 0, ph[tm + 7:tm + 8, :], 0.0)
    pnext = jnp.where(i < nt - 1, ph[tm + 8:tm + 9, :], 0.0)
    u = _dwconv3(ph[:tm], pprev, pnext, cw_ref, cb_ref)
    z_ref[0] = u[:, :HY_WIDTH]
    x1_ref[0] = u[:, HY_WIDTH:2 * HY_WIDTH]
    x2_ref[0] = u[:, 2 * HY_WIDTH:]

    gt_ref[0] = jax.nn.sigmoid(proj(C_GT, w_ref.shape[1])).astype(BF16)


def _in_proj(x, sh, sc, g, w, wkv, kvg, rope_m, rope_s, cw, cb):
    b, s, d = x.shape
    tm = min(IN_PROJ_ROWS, s)
    nt = s // tm
    r8 = tm // 8
    row = lambda width: pl.BlockSpec((1, tm, width), lambda bi, i: (bi, i, 0))
    const2 = lambda a: pl.BlockSpec(a.shape, lambda bi, i: (0, 0), pipeline_mode=pl.Buffered(1))
    tab = pl.BlockSpec((tm, LANE), lambda bi, i: (i, 0))
    in_specs = [
        row(d),
        pl.BlockSpec((1, 8, d), lambda bi, i: (bi, jnp.maximum(i * r8 - 1, 0), 0)),
        pl.BlockSpec((1, 8, d), lambda bi, i: (bi, jnp.minimum((i + 1) * r8, s // 8 - 1), 0)),
        pl.BlockSpec((1, 1, d), lambda bi, i: (bi, 0, 0)),
        pl.BlockSpec((1, 1, d), lambda bi, i: (bi, 0, 0)),
        const2(g), const2(w), const2(wkv), const2(kvg),
        tab, tab, tab, tab, tab, tab,
        const2(cw), const2(cb),
    ]
    widths = [(MLA_HEADS * HEAD_PAD, BF16), (MLA_HEADS * HEAD_PAD, BF16), (MLA_HEADS * LANE, BF16),
              (SWA_HEADS * SWA_HEAD_DIM, BF16), (2 * LANE, BF16), (2 * LANE, BF16),
              (HY_WIDTH, F32), (HY_WIDTH, F32), (HY_WIDTH, F32), (N_BRANCH * d, BF16)]
    return pl.pallas_call(
        _in_proj_kernel,
        grid=(b, nt),
        in_specs=in_specs,
        out_specs=[row(wd) for wd, _ in widths],
        out_shape=[jax.ShapeDtypeStruct((b, s, wd), dt) for wd, dt in widths],
        compiler_params=_cparams(("arbitrary", "arbitrary")),
        name="in_proj",
    )(x, x, x, sh, sc, g, w, wkv, kvg, *rope_m, *rope_s, cw, cb)


def _mla_kernel(*refs, tk, n_lat, rb):
    if n_lat:
        q_ref, kc_ref, vc_ref, kl_ref, vl_ref, o_ref, m_scr, acc_scr = refs
    else:
        q_ref, kc_ref, vc_ref, o_ref, m_scr, acc_scr = refs
    tq = q_ref.shape[1]
    nt = (((1,), (1,)), ((), ()))
    def step(k_ref, v_ref, off, n, first=False):
        v = v_ref[0, pl.ds(off, n), :]
        for e in range(2):
            k = k_ref[0, pl.ds(off, n), HEAD_PAD * e:HEAD_PAD * (e + 1)]
            for r in range(tq // rb):
                rows = pl.ds(r * rb, rb)
                s = lax.dot_general(q_ref[0, rows, HEAD_PAD * e:HEAD_PAD * (e + 1)], k, nt,
                                    preferred_element_type=F32)
                s_max = jnp.max(s, axis=1, keepdims=True)
                if first:
                    m_new = jnp.tile(s_max, (1, LANE))
                else:
                    m_old = m_scr[e, rows, :]
                    m_new = jnp.maximum(m_old, s_max)
                p = jnp.exp2(s - jnp.tile(m_new, (1, n // LANE))).astype(BF16)
                pv = jnp.dot(p, v, preferred_element_type=F32)
                if first:
                    acc_scr[e, rows, :] = pv
                else:
                    alpha = jnp.exp2(m_old - m_new)
                    acc_scr[e, rows, :] = jnp.tile(alpha, (1, 2)) * acc_scr[e, rows, :] + pv
                m_scr[e, rows, :] = m_new

    step(kc_ref, vc_ref, 0, kc_ref.shape[1], first=True)
    if n_lat:
        def body(j, carry):
            step(kl_ref, vl_ref, pl.multiple_of(j * tk, tk), tk)
            return carry
        lax.fori_loop(0, n_lat, body, 0)
    lane = lax.broadcasted_iota(jnp.int32, (tq, LANE), 1)
    o0 = acc_scr[0, :, :LANE] / acc_scr[0, :, LANE:]
    o1 = acc_scr[1, :, :LANE] / acc_scr[1, :, LANE:]
    o_ref[0] = jnp.where(lane < MLA_V, o0, o1).astype(o_ref.dtype)


def _mla(q, kc, vc, kl=None, vl=None, tq=4096, tk=512, rb=128):
    b, sq, _ = q.shape
    c = kc.shape[1]
    tq = min(tq, sq)
    rb = min(rb, tq)
    hp = MLA_HEADS // 2
    in_specs = [pl.BlockSpec((1, tq, 2 * HEAD_PAD), lambda bi, h, i: (bi, i, h)),
                pl.BlockSpec((1, c, 2 * HEAD_PAD), lambda bi, h, i: (bi, 0, h)),
                pl.BlockSpec((1, c, 2 * LANE), lambda bi, h, i: (bi, 0, h))]
    args = [q, kc, vc]
    n_lat = 0
    if kl is not None:
        s = kl.shape[1]
        tk = min(tk, s)
        n_lat = s // tk
        in_specs += [pl.BlockSpec((1, s, 2 * HEAD_PAD), lambda bi, h, i: (bi, 0, h)),
                     pl.BlockSpec((1, s, 2 * LANE), lambda bi, h, i: (bi, 0, h))]
        args += [kl, vl]
    return pl.pallas_call(
        functools.partial(_mla_kernel, tk=tk, n_lat=n_lat, rb=rb),
        grid=(b, hp, sq // tq),
        in_specs=in_specs,
        out_specs=pl.BlockSpec((1, tq, LANE), lambda bi, h, i: (bi, i, h)),
        out_shape=jax.ShapeDtypeStruct((b, sq, MLA_HEADS * MLA_V), BF16),
        scratch_shapes=[pltpu.VMEM((2, tq, LANE), F32), pltpu.VMEM((2, tq, 2 * LANE), F32)],
        compiler_params=_cparams(("arbitrary", "arbitrary", "arbitrary")),
        name="mla_attn",
    )(*args)


def _swa_kernel(*refs, band):
    if band:
        sink_ref, q_ref, kc_ref, vc_ref, kp_ref, k0_ref, kn_ref, vp_ref, v0_ref, vn_ref, o_ref = refs
        kband = jnp.concatenate([kp_ref[0], k0_ref[0], kn_ref[0]], axis=0)
        vband = jnp.concatenate([vp_ref[0], v0_ref[0], vn_ref[0]], axis=0)
    else:
        sink_ref, q_ref, kc_ref, vc_ref, o_ref = refs
    i = pl.program_id(1)
    nb = pl.num_programs(1)
    tq = q_ref.shape[1]
    nsub = tq // SWA_BLOCK
    n_ctx = kc_ref.shape[1]
    n_keys = n_ctx + (3 * SWA_BLOCK if band else 0)
    grp = SWA_HEADS // SWA_KV_HEADS
    rows_all = grp * SWA_BLOCK
    nt = (((1,), (1,)), ((), ()))
    lane = lax.broadcasted_iota(jnp.int32, (SWA_BLOCK, LANE), 1)
    low = lane < SWA_HEAD_DIM
    row = lax.broadcasted_iota(jnp.int32, (rows_all, 1), 0)
    if band:
        r_loc = lax.broadcasted_iota(jnp.int32, (rows_all, n_keys), 0) % SWA_BLOCK
        col = lax.broadcasted_iota(jnp.int32, (rows_all, n_keys), 1) - n_ctx
        prev_bias = jnp.where((col >= 0) & (col < SWA_BLOCK), NEG_INF, 0.0)
        next_bias = jnp.where(col >= 2 * SWA_BLOCK, NEG_INF, 0.0)
        base_bias = (jnp.where(r_loc > col, prev_bias, 0.0)
                     + jnp.where(col - 2 * SWA_BLOCK > r_loc, next_bias, 0.0))
    for r in range(nsub):
        rs = slice(r * SWA_BLOCK, (r + 1) * SWA_BLOCK)
        if band:
            bias = base_bias
            if r == 0:
                bias = jnp.minimum(bias, jnp.where(i == 0, prev_bias, 0.0))
            if r == nsub - 1:
                bias = jnp.minimum(bias, jnp.where(i == nb - 1, next_bias, 0.0))
            bs = slice(r * SWA_BLOCK, (r + 3) * SWA_BLOCK)
        for kvh in range(SWA_KV_HEADS):
            sl = slice(LANE * kvh, LANE * (kvh + 1))
            k, v = kc_ref[0, :, sl], vc_ref[0, :, sl]
            if band:
                k = jnp.concatenate([k, kband[bs, sl]], axis=0)
                v = jnp.concatenate([v, vband[bs, sl]], axis=0)
            v = jnp.concatenate([v, jnp.ones_like(v)], axis=1)
            parts, sink = [], None
            for t in range(grp // 2):
                pair = kvh * (grp // 2) + t
                qp = q_ref[0, rs, LANE * pair:LANE * (pair + 1)]
                parts += [jnp.where(low, qp, jnp.zeros_like(qp)), jnp.where(low, jnp.zeros_like(qp), qp)]
            for h in range(grp):
                sh = sink_ref[kvh * grp + h] * LOG2E
                sink = sh if sink is None else jnp.where(row < h * SWA_BLOCK, sink, sh)
            s = lax.dot_general(jnp.concatenate(parts, axis=0), k, nt, preferred_element_type=F32)
            if band:
                s = s + bias
            m = jnp.maximum(jnp.max(s, axis=1, keepdims=True), sink)
            p = jnp.exp2(s - m).astype(BF16)
            acc = jnp.dot(p, v, preferred_element_type=F32)
            o = acc[:, :LANE] / (acc[:, LANE:] + jnp.exp2(sink - m))
            for t in range(grp // 2):
                pair = kvh * (grp // 2) + t
                lo = o[(2 * t) * SWA_BLOCK:(2 * t + 1) * SWA_BLOCK]
                hi = o[(2 * t + 1) * SWA_BLOCK:(2 * t + 2) * SWA_BLOCK]
                o_ref[0, rs, LANE * pair:LANE * (pair + 1)] = jnp.where(low, lo, hi).astype(o_ref.dtype)


def _swa(sink, q, kc, vc, k=None, v=None, tq=1024):
    b, sq, _ = q.shape
    c = kc.shape[1]
    tq = min(tq, sq)
    nb = sq // tq
    band = k is not None
    blk = lambda rows, width, f: pl.BlockSpec((1, rows, width), f)
    in_specs = [pl.BlockSpec(memory_space=pltpu.SMEM),
                blk(tq, SWA_HEADS * SWA_HEAD_DIM, lambda bi, i: (bi, i, 0)),
                pl.BlockSpec((1, c, 2 * LANE), lambda bi, i: (bi, 0, 0)),
                pl.BlockSpec((1, c, 2 * LANE), lambda bi, i: (bi, 0, 0))]
    args = [sink, q, kc, vc]
    if band:
        per = tq // SWA_BLOCK
        prev = lambda bi, i: (bi, jnp.maximum(i * per - 1, 0), 0)
        cur = lambda bi, i: (bi, i, 0)
        nxt = lambda bi, i: (bi, jnp.minimum((i + 1) * per, sq // SWA_BLOCK - 1), 0)
        in_specs += [blk(SWA_BLOCK, 2 * LANE, prev), blk(tq, 2 * LANE, cur), blk(SWA_BLOCK, 2 * LANE, nxt),
                     blk(SWA_BLOCK, 2 * LANE, prev), blk(tq, 2 * LANE, cur), blk(SWA_BLOCK, 2 * LANE, nxt)]
        args += [k, k, k, v, v, v]
    return pl.pallas_call(
        functools.partial(_swa_kernel, band=band),
        grid=(b, nb),
        in_specs=in_specs,
        out_specs=blk(tq, SWA_HEADS * SWA_HEAD_DIM, lambda bi, i: (bi, i, 0)),
        out_shape=jax.ShapeDtypeStruct((b, sq, SWA_HEADS * SWA_HEAD_DIM), BF16),
        compiler_params=_cparams(("arbitrary", "arbitrary")),
        name="swa_attn",
    )(*args)


def _merge_kernel(x_ref, ya_ref, yb_ref, yc_ref, gt_ref, wb_ref, wo_ref, ng_ref, g1_ref, o_ref):
    d = x_ref.shape[2]
    merged = None
    for k, y_ref in enumerate((ya_ref, yb_ref, yc_ref)):
        t = jnp.dot(y_ref[0].astype(BF16), wb_ref[k], preferred_element_type=F32)
        t = gt_ref[0, :, d * k:d * (k + 1)].astype(F32) * t
        merged = t if merged is None else merged + t
    y = jnp.dot(merged.astype(BF16), wo_ref[...], preferred_element_type=F32)
    o_ref[0] = x_ref[0] + g1_ref[0] * _rms(y, ng_ref[...])


def _merge(x, ya, yb, yc, gt, wb, wo, ng, g1):
    b, s, d = x.shape
    tm = min(MERGE_ROWS, s)
    row = lambda width: pl.BlockSpec((1, tm, width), lambda bi, i: (bi, i, 0))
    return pl.pallas_call(
        _merge_kernel,
        grid=(b, s // tm),
        in_specs=[row(d), row(BRANCH_WIDTH), row(BRANCH_WIDTH), row(BRANCH_WIDTH), row(N_BRANCH * d),
                  pl.BlockSpec(wb.shape, lambda bi, i: (0, 0, 0)),
                  pl.BlockSpec(wo.shape, lambda bi, i: (0, 0)),
                  pl.BlockSpec(ng.shape, lambda bi, i: (0, 0)),
                  pl.BlockSpec((1, 1, d), lambda bi, i: (bi, 0, 0))],
        out_specs=row(d),
        out_shape=jax.ShapeDtypeStruct((b, s, d), F32),
        compiler_params=_cparams(("arbitrary", "arbitrary")),
        name="merge",
    )(x, ya, yb, yc, gt, wb, wo, ng, g1)


def _ffn_kernel(x_ref, xp_ref, xn_ref, sh_ref, sc_ref, g2_ref, ng_in_ref, ng_out_ref,
                wu_ref, cw_ref, cb_ref, wd_ref, o_ref, *, chunk):
    i = pl.program_id(1)
    nt = pl.num_programs(1)
    g = ng_in_ref[...]
    sh = sh_ref[0]
    sc = sc_ref[0]

    def mod(xf):
        return (_rms(xf, g) * (1.0 + sc) + sh).astype(BF16)

    x = x_ref[0]
    tm = x.shape[0]
    hb = mod(jnp.concatenate([x, xp_ref[0], xn_ref[0]], axis=0))
    dff = wd_ref.shape[0]
    f = None
    for c0 in range(0, dff, chunk):
        halves = []
        for off in (c0, dff + c0):
            p = jnp.dot(hb, wu_ref[:, off:off + chunk], preferred_element_type=F32)
            pprev = jnp.where(i > 0, p[tm + 7:tm + 8, :], 0.0)
            pnext = jnp.where(i < nt - 1, p[tm + 8:tm + 9, :], 0.0)
            halves.append(_dwconv3(p[:tm], pprev, pnext, cw_ref.at[:, off:off + chunk],
                                   cb_ref.at[:, off:off + chunk]))
        a, bb = halves
        act = (a * jax.nn.sigmoid(a) * bb).astype(BF16)
        t = jnp.dot(act, wd_ref[c0:c0 + chunk, :], preferred_element_type=F32)
        f = t if f is None else f + t
    o_ref[0] = x + g2_ref[0] * _rms(f, ng_out_ref[...])


def _ffn(x, sh, sc, g2, ng_in, ng_out, wu, cw, cb, wd):
    b, s, d = x.shape
    tm = min(FFN_ROWS, s)
    r8 = tm // 8
    dff = wd.shape[0]
    chunk = dff // FFN_CHUNKS if dff % (FFN_CHUNKS * LANE) == 0 else dff
    row = pl.BlockSpec((1, tm, d), lambda bi, i: (bi, i, 0))
    vec = pl.BlockSpec((1, 1, d), lambda bi, i: (bi, 0, 0))
    const2 = lambda a: pl.BlockSpec(a.shape, lambda bi, i: (0, 0), pipeline_mode=pl.Buffered(1))
    return pl.pallas_call(
        functools.partial(_ffn_kernel, chunk=chunk),
        grid=(b, s // tm),
        in_specs=[row,
                  pl.BlockSpec((1, 8, d), lambda bi, i: (bi, jnp.maximum(i * r8 - 1, 0), 0)),
                  pl.BlockSpec((1, 8, d), lambda bi, i: (bi, jnp.minimum((i + 1) * r8, s // 8 - 1), 0)),
                  vec, vec, vec, const2(ng_in), const2(ng_out),
                  const2(wu), const2(cw), const2(cb), const2(wd)],
        out_specs=row,
        out_shape=jax.ShapeDtypeStruct((b, s, d), F32),
        compiler_params=_cparams(("arbitrary", "arbitrary")),
        name="conv_ffn",
    )(x, x, x, sh, sc, g2, ng_in, ng_out, wu, cw, cb, wd)


HY_PASSES = 1
SPEC_PASSES = 1
HY_UNROLL = 2
HY_UNROLL2 = 6


def _split(x):
    hi = x.astype(BF16)
    return hi, (x - hi.astype(F32)).astype(BF16)


def _cdot(c_hi, c_lo, x, passes=HY_PASSES):
    x_hi, x_lo = _split(x)
    acc = jnp.dot(c_hi, x_hi, preferred_element_type=F32)
    if passes > 1:
        acc = acc + jnp.dot(c_lo, x_hi, preferred_element_type=F32)
        acc = acc + jnp.dot(c_hi, x_lo, preferred_element_type=F32)
    return acc


def _np_split(m):
    m = jnp.asarray(m, F32)
    hi = m.astype(BF16)
    return hi, (m - hi.astype(F32)).astype(BF16)


def _dft_consts(n_blocks, nonzero_blocks):
    nb = n_blocks
    n = LANE * nb
    dh = nb // 2 + 1
    dhp = -(-dh // 8) * 8
    d = np.arange(dh)[:, None]
    b = np.arange(nonzero_blocks)[None, :]
    ang = 2 * np.pi * ((d * b) % nb) / nb
    f1 = np.zeros((2 * dhp, nonzero_blocks))
    f1[:dh] = np.cos(ang)
    f1[dhp:dhp + dh] = -np.sin(ang)
    a = np.arange(LANE)
    ang2 = 2 * np.pi * ((a[:, None] * a[None, :]) % LANE) / LANE
    ar, ai = np.cos(ang2), -np.sin(ang2)
    f2 = np.block([[ar, -ai], [ai, ar]])
    f2inv = np.block([[ar, ai], [-ai, ar]])
    bo = np.arange(nb // 2)[:, None]
    do = np.arange(dh)[None, :]
    wd = np.where((do == 0) | (do == nb // 2), 1.0, 2.0) / n
    ang3 = 2 * np.pi * ((bo * do) % nb) / nb
    f3 = np.zeros((nb // 2, 2 * dhp))
    f3[:, :dh] = wd * np.cos(ang3)
    f3[:, dhp:dhp + dh] = -wd * np.sin(ang3)
    step_ang = 2 * np.pi * a / n
    tw_step = np.stack([np.broadcast_to(np.cos(step_ang)[:, None], (LANE, LANE)),
                        np.broadcast_to(-np.sin(step_ang)[:, None], (LANE, LANE))])
    eye = np.eye(SUBLANES)
    return dict(dh=dh, dhp=dhp, f1=_np_split(np.kron(f1, eye)), f2=_np_split(f2), f2inv=_np_split(f2inv),
                f3=_np_split(np.kron(f3, eye)), tw_step=jnp.asarray(tw_step, F32))


def _stage1(x_ref, f1_hi, f1_lo, g_scr, passes):
    n_in, n_out = x_ref.shape[0], g_scr.shape[0]

    def body(t, carry):
        r0 = pl.ds(pl.multiple_of(2 * t * SUBLANES, SUBLANES), SUBLANES)
        r1 = pl.ds(pl.multiple_of((2 * t + 1) * SUBLANES, SUBLANES), SUBLANES)
        xa = jnp.concatenate([x_ref[:, r0, :].reshape(n_in * SUBLANES, LANE),
                              x_ref[:, r1, :].reshape(n_in * SUBLANES, LANE)], axis=1)
        g = _cdot(f1_hi, f1_lo, xa, passes)
        g_scr[:, r0, :] = g[:, :LANE].reshape(n_out, SUBLANES, LANE)
        g_scr[:, r1, :] = g[:, LANE:].reshape(n_out, SUBLANES, LANE)
        return carry
    lax.fori_loop(0, LANE // (2 * SUBLANES), body, 0, unroll=HY_UNROLL)


def _twiddle_init(tw_scr):
    tw_scr[0] = jnp.ones((LANE, LANE), F32)
    tw_scr[1] = jnp.zeros((LANE, LANE), F32)


def _twiddle_next(twr, twi, step_ref):
    sr, si = step_ref[0], step_ref[1]
    return twr * sr - twi * si, twr * si + twi * sr


def _twiddle_pair(tw_scr, step_ref):
    t0r, t0i = tw_scr[0], tw_scr[1]
    t1r, t1i = _twiddle_next(t0r, t0i, step_ref)
    t2r, t2i = _twiddle_next(t1r, t1i, step_ref)
    tw_scr[0] = t2r
    tw_scr[1] = t2i
    return jnp.concatenate([t0r, t1r], axis=1), jnp.concatenate([t0i, t1i], axis=1)


def _spectrum_kernel(k_ref, inv_ref, f1h_ref, f1l_ref, f2h_ref, f2l_ref, step_ref, hf_ref, g_scr, tw_scr,
                     *, dh, dhp):
    _stage1(k_ref, f1h_ref[...], f1l_ref[...], g_scr, SPEC_PASSES)
    _twiddle_init(tw_scr)
    inv = inv_ref[...]

    def body(h, carry):
        d = 2 * h
        gr = jnp.concatenate([g_scr[d], g_scr[d + 1]], axis=1)
        gi = jnp.concatenate([g_scr[dhp + d], g_scr[dhp + d + 1]], axis=1)
        twr, twi = _twiddle_pair(tw_scr, step_ref)
        t = jnp.concatenate([gr * twr - gi * twi, gr * twi + gi * twr], axis=0)
        y = _cdot(f2h_ref[...], f2l_ref[...], t, passes=SPEC_PASSES)
        hf_ref[0, d] = y[:LANE, :LANE] * inv
        hf_ref[0, d + 1] = y[:LANE, LANE:] * inv
        hf_ref[1, d] = y[LANE:, :LANE] * inv
        hf_ref[1, d + 1] = y[LANE:, LANE:] * inv
        return carry
    lax.fori_loop(0, dhp // 2, body, 0, unroll=HY_UNROLL2)


def _hyena_spectrum(kern, inv_norm):
    tiles, n, _ = kern.shape
    c = tiles * LANE
    nb = n // LANE
    cst = _dft_consts(nb, nb)
    dh, dhp = cst["dh"], cst["dhp"]
    const = lambda a: pl.BlockSpec(a.shape, lambda j: (0,) * a.ndim, pipeline_mode=pl.Buffered(1))
    consts = [*cst["f1"], *cst["f2"], cst["tw_step"]]
    return pl.pallas_call(
        functools.partial(_spectrum_kernel, dh=dh, dhp=dhp),
        grid=(c // LANE,),
        in_specs=[pl.BlockSpec((None, nb, LANE, LANE), lambda j: (j, 0, 0, 0), pipeline_mode=pl.Buffered(1)),
                  pl.BlockSpec((1, LANE), lambda j: (0, j))]
        + [const(a) for a in consts],
        out_specs=pl.BlockSpec((None, 2, dhp, LANE, LANE), lambda j: (j, 0, 0, 0, 0)),
        out_shape=jax.ShapeDtypeStruct((c // LANE, 2, dhp, LANE, LANE), F32),
        scratch_shapes=[pltpu.VMEM((2 * dhp, LANE, LANE), F32), pltpu.VMEM((2, LANE, LANE), F32)],
        compiler_params=_cparams(("arbitrary",)),
        name="hyena_spectrum",
    )(kern.reshape(tiles, nb, LANE, LANE), inv_norm, *consts)


def _conv_kernel(z_ref, gate_ref, skip_ref, hf_ref, f1h_ref, f1l_ref, f2h_ref, f2l_ref, f2ih_ref, f2il_ref,
                 f3h_ref, f3l_ref, step_ref, o_ref, g_scr, tw_scr, *, dh, dhp):
    nbh = z_ref.shape[1]
    _stage1(z_ref.at[0], f1h_ref[...], f1l_ref[...], g_scr, HY_PASSES)
    _twiddle_init(tw_scr)

    def body(h, carry):
        d = 2 * h
        gr = jnp.concatenate([g_scr[d], g_scr[d + 1]], axis=1)
        gi = jnp.concatenate([g_scr[dhp + d], g_scr[dhp + d + 1]], axis=1)
        twr, twi = _twiddle_pair(tw_scr, step_ref)
        t = jnp.concatenate([gr * twr - gi * twi, gr * twi + gi * twr], axis=0)
        y = _cdot(f2h_ref[...], f2l_ref[...], t)
        yr, yi = y[:LANE], y[LANE:]
        hr = jnp.concatenate([hf_ref[0, d], hf_ref[0, d + 1]], axis=1)
        hi = jnp.concatenate([hf_ref[1, d], hf_ref[1, d + 1]], axis=1)
        zz = jnp.concatenate([yr * hr - yi * hi, yr * hi + yi * hr], axis=0)
        u = _cdot(f2ih_ref[...], f2il_ref[...], zz)
        ur, ui = u[:LANE], u[LANE:]
        vr = ur * twr + ui * twi
        vi = ui * twr - ur * twi
        g_scr[d] = vr[:, :LANE]
        g_scr[d + 1] = vr[:, LANE:]
        g_scr[dhp + d] = vi[:, :LANE]
        g_scr[dhp + d + 1] = vi[:, LANE:]
        return carry
    lax.fori_loop(0, dhp // 2, body, 0, unroll=HY_UNROLL2)

    skip = skip_ref[...]
    f3h, f3l = f3h_ref[...], f3l_ref[...]

    def out_body(t, carry):
        r0 = pl.ds(pl.multiple_of(2 * t * SUBLANES, SUBLANES), SUBLANES)
        r1 = pl.ds(pl.multiple_of((2 * t + 1) * SUBLANES, SUBLANES), SUBLANES)
        ga = jnp.concatenate([g_scr[:, r0, :].reshape(2 * dhp * SUBLANES, LANE),
                              g_scr[:, r1, :].reshape(2 * dhp * SUBLANES, LANE)], axis=1)
        conv = _cdot(f3h, f3l, ga)
        for rows, c in ((r0, conv[:, :LANE]), (r1, conv[:, LANE:])):
            o_ref[0, :, rows, :] = gate_ref[0, :, rows, :] * (
                c.reshape(nbh, SUBLANES, LANE) + z_ref[0, :, rows, :] * skip)
        return carry
    lax.fori_loop(0, LANE // (2 * SUBLANES), out_body, 0, unroll=HY_UNROLL)


def _hyena_conv(z, gate, skip, hf, tile0, cst):
    b, l, w = z.shape
    nbh = l // LANE
    dh, dhp = cst["dh"], cst["dhp"]
    consts = [*cst["f1"], *cst["f2"], *cst["f2inv"], *cst["f3"], cst["tw_step"]]
    const = lambda a: pl.BlockSpec(a.shape, lambda j, bi: (0,) * a.ndim, pipeline_mode=pl.Buffered(1))
    tile = pl.BlockSpec((1, nbh, LANE, LANE), lambda j, bi: (bi, 0, 0, j))
    z, gate = z.reshape(b, nbh, LANE, w), gate.reshape(b, nbh, LANE, w)
    return pl.pallas_call(
        functools.partial(_conv_kernel, dh=dh, dhp=dhp),
        grid=(w // LANE, b),
        in_specs=[tile, tile, pl.BlockSpec((1, LANE), lambda j, bi: (0, j)),
                  pl.BlockSpec((None, 2, dhp, LANE, LANE), lambda j, bi: (j + tile0, 0, 0, 0, 0),
                               pipeline_mode=pl.Buffered(1))]
        + [const(a) for a in consts],
        out_specs=tile,
        out_shape=jax.ShapeDtypeStruct((b, nbh, LANE, w), F32),
        scratch_shapes=[pltpu.VMEM((2 * dhp, LANE, LANE), F32), pltpu.VMEM((2, LANE, LANE), F32)],
        compiler_params=_cparams(("arbitrary", "arbitrary")),
        name="hyena_conv",
    )(z, gate, skip, hf, *consts).reshape(b, l, w)


def _filter_mlp_kernel(ft_ref, w1_ref, b1_ref, w2_ref, b2_ref, fr_ref, w3_ref, dl_ref, k_ref, s_ref):
    i = pl.program_id(0)
    ft = ft_ref[...]

    def mm(x, w):
        x_hi, x_lo = _split(x)
        w_hi, w_lo = _split(w)
        return (jnp.dot(x_hi, w_hi, preferred_element_type=F32) + jnp.dot(x_lo, w_hi, preferred_element_type=F32)
                + jnp.dot(x_hi, w_lo, preferred_element_type=F32))

    hid = jnp.sin(fr_ref[0:1, :] * (mm(ft, w1_ref[...]) + b1_ref[...]))
    hid = jnp.sin(fr_ref[1:2, :] * (mm(hid, w2_ref[...]) + b2_ref[...]))
    h = mm(hid, w3_ref[0])
    k = h * jnp.exp(-ft[:, 0:1] * dl_ref[...]) * ft[:, LANE - 1:LANE]
    for j in range(k_ref.shape[0]):
        k_ref[j] = k[:, LANE * j:LANE * (j + 1)]

    @pl.when(i == 0)
    def _():
        s_ref[...] = jnp.zeros(s_ref.shape, F32)
    s_ref[...] += jnp.sum(jnp.abs(k), axis=0, keepdims=True)


def _hyena_filter_taps(n_tokens, w1, b1, w2, b2, freq, w3):
    L = n_tokens
    n = 2 * L
    idx = jnp.arange(n)
    m = jnp.where(idx < L, idx, n - idx).astype(F32)
    t = m / max(L - 1, 1)
    bands = jnp.linspace(1e-4, HY_BANDS - 1, HY_BANDS, dtype=F32)
    ang = (2.0 * math.pi / L) * m[:, None] * bands
    n_feat = 2 * HY_BANDS + 1
    feats = jnp.concatenate([t[:, None], jnp.cos(ang), -jnp.sin(ang),
                             jnp.zeros((n, LANE - n_feat - 1), F32),
                             (idx != L).astype(F32)[:, None]], axis=-1)
    w1p = jnp.pad(w1, ((0, LANE - n_feat), (0, 0)))
    cw = HY_ORDER * HY_WIDTH
    w3d = jnp.moveaxis(w3.reshape(HY_HIDDEN, 2, cw), 1, 0)
    deltas = jnp.abs(jnp.linspace(math.log(HY_TARGET) / HY_SLOW_PCT, math.log(HY_TARGET) / HY_FAST_PCT,
                                  HY_WIDTH, dtype=F32))
    dl = jnp.tile(deltas, HY_ORDER)[None, :]
    tr = min(1024, L)
    nt = n // tr
    const = lambda a: pl.BlockSpec(a.shape, lambda i: (0,) * a.ndim)
    b1r, b2r = b1[None, :], b2[None, :]
    return pl.pallas_call(
        _filter_mlp_kernel,
        grid=(nt,),
        in_specs=[pl.BlockSpec((tr, LANE), lambda i: (i, 0)), const(w1p), const(b1r), const(w2), const(b2r),
                  const(freq), pl.BlockSpec((1, HY_HIDDEN, cw), lambda i: (i // (nt // 2), 0, 0)), const(dl)],
        out_specs=[pl.BlockSpec((cw // LANE, tr, LANE), lambda i: (0, i, 0)), pl.BlockSpec((1, cw), lambda i: (0, 0))],
        out_shape=[jax.ShapeDtypeStruct((cw // LANE, n, LANE), F32), jax.ShapeDtypeStruct((1, cw), F32)],
        compiler_params=_cparams(("arbitrary",)),
        name="hyena_filter_mlp",
    )(feats, w1p, b1r, w2, b2r, freq, w3d, dl)


def _hyena_long(z, x1, x2, taps, norms, skip):
    b, l, w = z.shape
    nb = 2 * l // LANE
    hf = _hyena_spectrum(taps, 1.0 / norms)
    cst = _dft_consts(nb, nb // 2)
    for o, gate in enumerate((x1, x2)):
        z = _hyena_conv(z, gate, skip[o][None, :], hf, o * (w // LANE), cst)
    return z


def _dense_dft_consts(length):
    n = 2 * length
    h = length + 1
    hp = -(-h // 8) * 8
    k = np.arange(h)[:, None]
    pos = np.arange(n)[None, :]
    ang = 2 * np.pi * ((k * pos) % n) / n
    fwd = np.zeros((2 * hp, n))
    fwd[:h] = np.cos(ang)
    fwd[hp:hp + h] = -np.sin(ang)
    wk = np.where((k == 0) | (k == length), 1.0, 2.0) / n
    inv = np.zeros((length, 2 * hp))
    inv[:, :h] = (wk * np.cos(ang[:, :length])).T
    inv[:, hp:hp + h] = (-wk * np.sin(ang[:, :length])).T
    return dict(hp=hp, full=_np_split(fwd), fwd=_np_split(fwd[:, :length]), inv=_np_split(inv))


def _hyena_short_kernel(z_ref, x1_ref, x2_ref, k_ref, inv_ref, skip_ref, ffh_ref, ffl_ref, fh_ref, fl_ref,
                        fih_ref, fil_ref, o_ref, *, hp):
    z = z_ref[0]
    w = z.shape[1]
    for o, gate_ref in enumerate((x1_ref, x2_ref)):
        cols = slice(o * w, (o + 1) * w)
        hf = _cdot(ffh_ref[...], ffl_ref[...], k_ref[:, cols], passes=3) * inv_ref[:, cols]
        x = _cdot(fh_ref[...], fl_ref[...], z, passes=3)
        xr, xi, hr, hi = x[:hp], x[hp:], hf[:hp], hf[hp:]
        zz = jnp.concatenate([xr * hr - xi * hi, xr * hi + xi * hr], axis=0)
        conv = _cdot(fih_ref[...], fil_ref[...], zz, passes=3)
        z = gate_ref[0] * (conv + z * skip_ref[o:o + 1, :])
    o_ref[0] = z


def _hyena_short(z, x1, x2, taps, norms, skip):
    b, l, w = z.shape
    taps = jnp.moveaxis(taps, 0, 1).reshape(2 * l, -1)
    cst = _dense_dft_consts(l)
    consts = [*cst["full"], *cst["fwd"], *cst["inv"]]
    inv_norm = 1.0 / norms
    const = lambda a: pl.BlockSpec(a.shape, lambda bi: (0,) * a.ndim)
    tile = pl.BlockSpec((1, l, w), lambda bi: (bi, 0, 0))
    return pl.pallas_call(
        functools.partial(_hyena_short_kernel, hp=cst["hp"]),
        grid=(b,),
        in_specs=[tile, tile, tile, const(taps), const(inv_norm), const(skip)] + [const(a) for a in consts],
        out_specs=tile,
        out_shape=jax.ShapeDtypeStruct((b, l, w), F32),
        compiler_params=_cparams(("arbitrary",)),
        name="hyena_short",
    )(z, x1, x2, taps, inv_norm, skip, *consts)


def _rope_tables(rows, rot_dim, head_lanes, rope_off, identity_rows):
    half = rot_dim // 2
    n_freq = rot_dim // 4
    pos = np.arange(rows * GRID_W)
    inv_freq = ROPE_THETA ** (-np.arange(n_freq, dtype=np.float64) / n_freq)
    ang = np.concatenate([(pos // GRID_W)[:, None] * inv_freq, (pos % GRID_W)[:, None] * inv_freq], axis=-1)
    cos, sin = np.cos(ang), np.sin(ang)
    n = pos.shape[0]
    ct = np.ones((n, LANE))
    sa = np.zeros((n, LANE))
    sb = np.zeros((n, LANE))
    for h0 in range(0, LANE, head_lanes):
        lo = h0 + rope_off
        ct[:, lo:lo + half] = cos
        ct[:, lo + half:lo + rot_dim] = cos
        sa[:, lo:lo + half] = -sin
        sb[:, lo + half:lo + rot_dim] = sin
    ident = (np.ones((identity_rows, LANE)), np.zeros((identity_rows, LANE)), np.zeros((identity_rows, LANE)))
    lat = tuple(jnp.asarray(t, F32) for t in (ct, sa, sb))
    ctx = tuple(jnp.asarray(t, F32) for t in ident)
    return lat, ctx


def _pack_w_in(w):
    d = w.shape[0]
    o = 0
    mq = w[:, o:o + MLA_HEADS * (MLA_NOPE + MLA_ROPE)]; o += MLA_HEADS * (MLA_NOPE + MLA_ROPE)
    mckv = w[:, o:o + KV_RANK]; o += KV_RANK
    mkr = w[:, o:o + MLA_ROPE]; o += MLA_ROPE
    sq = w[:, o:o + SWA_HEADS * SWA_HEAD_DIM]; o += SWA_HEADS * SWA_HEAD_DIM
    sk = w[:, o:o + SWA_KV_HEADS * SWA_HEAD_DIM]; o += SWA_KV_HEADS * SWA_HEAD_DIM
    sv = w[:, o:o + SWA_KV_HEADS * SWA_HEAD_DIM]; o += SWA_KV_HEADS * SWA_HEAD_DIM
    hy = w[:, o:o + (HY_ORDER + 1) * HY_WIDTH]; o += (HY_ORDER + 1) * HY_WIDTH
    gt = w[:, o:]
    pad_q = HEAD_PAD - MLA_NOPE - MLA_ROPE
    mq = jnp.pad(mq.reshape(d, MLA_HEADS, MLA_NOPE + MLA_ROPE), ((0, 0), (0, 0), (0, pad_q))).reshape(d, -1)
    mkr = jnp.pad(mkr, ((0, 0), (MLA_NOPE, pad_q)))
    dup = lambda t: jnp.repeat(t.reshape(d, SWA_KV_HEADS, 1, SWA_HEAD_DIM), 2, axis=2).reshape(d, -1)
    return jnp.concatenate([mq, mckv, mkr, sq, dup(sk), dup(sv), hy, gt], axis=1).astype(BF16)


def _pack_w_kv(w):
    r = w.shape[0]
    w = w.reshape(r, MLA_HEADS, MLA_NOPE + MLA_V)
    k = jnp.pad(w[..., :MLA_NOPE], ((0, 0), (0, 0), (0, HEAD_PAD - MLA_NOPE))).reshape(r, -1)
    v = w[..., MLA_NOPE:].reshape(r, -1)
    return jnp.concatenate([k, v], axis=1).astype(BF16)


def kernel(x, c, ctx, c_ctx, w_mod, b_mod, norm_g, w_in, kv_norm_g, w_kv_up, swa_sink, hy_conv_w, hy_conv_b,
           hy_w1, hy_b1, hy_w2, hy_b2, hy_freq, hy_w3, hy_skip, w_branch, w_out, w_up, ffn_conv_w, ffn_conv_b,
           w_down):
    b, s, d = x.shape
    n_ctx = ctx.shape[1]
    depth = w_mod.shape[0]
    rows = s // GRID_W
    rope_m, rope_m_ctx = _rope_tables(rows, MLA_ROPE, HEAD_PAD, MLA_NOPE, n_ctx)
    rope_s, rope_s_ctx = _rope_tables(rows, SWA_HEAD_DIM, SWA_HEAD_DIM, 0, n_ctx)

    cvec = jnp.concatenate([c, c_ctx[None, :], jnp.zeros((8 - b - 1, d), F32)], axis=0)
    mod_all = _modulation(cvec, w_mod, b_mod)

    x_lat, x_ctx = x, ctx
    for l in range(depth):
        with_ctx = l < depth - 1
        m = mod_all[l].reshape(8, 6, d)
        lat = [m[:b, k][:, None, :] for k in range(6)]
        cx = [jnp.broadcast_to(m[b, k][None, None, :], (b, 1, d)) for k in range(6)]
        ng = [norm_g[l, k][None, :] for k in range(4)]
        w_pack = _pack_w_in(w_in[l])
        wkv_pack = _pack_w_kv(w_kv_up[l])
        kvg = kv_norm_g[l][None, :]
        cw, cb = hy_conv_w[l], hy_conv_b[l][None, :]

        q, kk, vv, sq, sk, sv, z, x1, x2, gt = _in_proj(
            x_lat, lat[0], lat[1], ng[0], w_pack, wkv_pack, kvg, rope_m, rope_s, cw, cb)
        qc, kkc, vvc, sqc, skc, svc, zc, x1c, x2c, gtc = _in_proj(
            x_ctx, cx[0], cx[1], ng[0], w_pack, wkv_pack, kvg, rope_m_ctx, rope_s_ctx, cw, cb)

        y_a = _mla(q, kkc, vvc, kk, vv)
        y_b = _swa(swa_sink[l], sq, skc, svc, sk, sv)
        hy_mlp = (hy_w1[l], hy_b1[l], hy_w2[l], hy_b2[l], hy_freq[l], hy_w3[l])
        y_c = _hyena_long(z, x1, x2, *_hyena_filter_taps(s, *hy_mlp), hy_skip[l])

        wb = w_branch[l].astype(BF16)
        wo = w_out[l].astype(BF16)
        wu = w_up[l].astype(BF16)
        wd = w_down[l].astype(BF16)
        fcw, fcb = ffn_conv_w[l], ffn_conv_b[l][None, :]

        x_lat = _merge(x_lat, y_a, y_b, y_c, gt, wb, wo, ng[1], lat[2])
        x_lat = _ffn(x_lat, lat[3], lat[4], lat[5], ng[2], ng[3], wu, fcw, fcb, wd)

        if with_ctx:
            yc_a = _mla(qc, kkc, vvc)
            yc_b = _swa(swa_sink[l], sqc, skc, svc)
            yc_c = _hyena_short(zc, x1c, x2c, *_hyena_filter_taps(n_ctx, *hy_mlp), hy_skip[l])
            x_ctx = _merge(x_ctx, yc_a, yc_b, yc_c, gtc, wb, wo, ng[1], cx[2])
            x_ctx = _ffn(x_ctx, cx[3], cx[4], cx[5], ng[2], ng[3], wu, fcw, fcb, wd)
    return x_lat
```

```python
import functools
import math

import numpy as np
import jax
import jax.numpy as jnp
from jax import lax
from jax.experimental import pallas as pl
from jax.experimental.pallas import tpu as pltpu

F32 = jnp.float32
BF16 = jnp.bfloat16

GRID_W = 64
EPS = 1e-6
ROPE_THETA = 10000.0
NEG_INF = -1e30
MLA_HEADS = 8
MLA_NOPE = 64
MLA_ROPE = 32
MLA_V = 64
KV_RANK = 256
MLA_SCALE = (MLA_NOPE + MLA_ROPE) ** -0.5
LOG2E = math.log2(math.e)
SWA_HEADS = 8
SWA_KV_HEADS = 2
SWA_HEAD_DIM = 64
SWA_BLOCK = 128
SWA_SCALE = SWA_HEAD_DIM ** -0.5
HY_WIDTH = 512
HY_ORDER = 2
HY_BANDS = 16
HY_HIDDEN = 64
HY_TARGET = 1e-2
HY_FAST_PCT = 0.3
HY_SLOW_PCT = 1.5
N_BRANCH = 3
BRANCH_WIDTH = 512
LANE = 128
SUBLANES = 8
HEAD_PAD = 128

VMEM_LIMIT = 56 * 1024 * 1024
IN_PROJ_ROWS = 512
FFN_ROWS = 512
FFN_CHUNKS = 1
MERGE_ROWS = 1024

C_Q = 0
C_CKV = C_Q + MLA_HEADS * HEAD_PAD
C_KR = C_CKV + KV_RANK
C_SQ = C_KR + HEAD_PAD
C_SK = C_SQ + SWA_HEADS * SWA_HEAD_DIM
C_SV = C_SK + 2 * SWA_KV_HEADS * SWA_HEAD_DIM
C_HY = C_SV + 2 * SWA_KV_HEADS * SWA_HEAD_DIM
C_GT = C_HY + (HY_ORDER + 1) * HY_WIDTH


def _cparams(sem):
    return pltpu.CompilerParams(dimension_semantics=sem, vmem_limit_bytes=VMEM_LIMIT)


def _rms(xf, g):
    return xf * lax.rsqrt(jnp.mean(xf * xf, axis=-1, keepdims=True) + EPS) * g


def _rope(x, cos, sa, sb, half):
    return x * cos + pltpu.roll(x, LANE - half, 1) * sa + pltpu.roll(x, half, 1) * sb


def _halo_specs(tm, s, d):
    per = tm // SUBLANES
    last = s // SUBLANES - 1
    return (pl.BlockSpec((1, SUBLANES, d), lambda bi, i: (bi, jnp.maximum(i * per - 1, 0), 0)),
            pl.BlockSpec((1, SUBLANES, d), lambda bi, i: (bi, jnp.minimum((i + 1) * per, last), 0)))


def _halo_rows(p, tm, i, nt):
    prev_row = jnp.where(i > 0, p[tm + SUBLANES - 1:tm + SUBLANES, :], 0.0)
    next_row = jnp.where(i < nt - 1, p[tm + SUBLANES:tm + SUBLANES + 1, :], 0.0)
    return prev_row, next_row


def _dwconv3(p, prev_row, next_row, cw_ref, cb_ref):
    tm = p.shape[0]
    rows = lax.broadcasted_iota(jnp.int32, (tm, 1), 0)
    up = jnp.where(rows == 0, prev_row, pltpu.roll(p, 1, 0))
    dn = jnp.where(rows == tm - 1, next_row, pltpu.roll(p, tm - 1, 0))
    return up * cw_ref[0:1, :] + p * cw_ref[1:2, :] + dn * cw_ref[2:3, :] + cb_ref[...]


def _mod_kernel(c_ref, w_ref, b_ref, o_ref):
    c = c_ref[...]
    a = c * jax.nn.sigmoid(c)
    a_hi = a.astype(BF16)
    a_lo = (a - a_hi.astype(F32)).astype(BF16)
    w = w_ref[0]
    w_hi = w.astype(BF16)
    w_lo = (w - w_hi.astype(F32)).astype(BF16)
    acc = jnp.dot(a_hi, w_hi, preferred_element_type=F32)
    acc += jnp.dot(a_lo, w_hi, preferred_element_type=F32)
    acc += jnp.dot(a_hi, w_lo, preferred_element_type=F32)
    o_ref[0] = acc + b_ref[0]


def _modulation(cvec, w_mod, b_mod):
    depth, d, n = w_mod.shape
    rows = cvec.shape[0]
    tn = n // 4 if n % (4 * LANE) == 0 else n
    return pl.pallas_call(
        _mod_kernel,
        grid=(depth, n // tn),
        in_specs=[pl.BlockSpec((rows, d), lambda l, j: (0, 0)),
                  pl.BlockSpec((1, d, tn), lambda l, j: (l, 0, j)),
                  pl.BlockSpec((1, 1, tn), lambda l, j: (l, 0, j))],
        out_specs=pl.BlockSpec((1, rows, tn), lambda l, j: (l, 0, j)),
        out_shape=jax.ShapeDtypeStruct((depth, rows, n), F32),
        compiler_params=_cparams(("arbitrary", "arbitrary")),
        name="adaln_mod",
    )(cvec, w_mod, b_mod.reshape(depth, 1, n))


def _in_proj_kernel(x_ref, xp_ref, xn_ref, sh_ref, sc_ref, g_ref, w_ref, wkv_ref, kvg_ref,
                    cm_ref, sma_ref, smb_ref, cs_ref, ssa_ref, ssb_ref, cw_ref, cb_ref,
                    q_ref, kk_ref, vv_ref, sq_ref, sk_ref, sv_ref, z_ref, x1_ref, x2_ref, gt_ref):
    i = pl.program_id(1)
    nt = pl.num_programs(1)
    g = g_ref[...]
    sh = sh_ref[0]
    sc = sc_ref[0]

    def mod(xf):
        return (_rms(xf, g) * (1.0 + sc) + sh).astype(BF16)

    tm = x_ref.shape[1]
    hb_ext = mod(jnp.concatenate([x_ref[0], xp_ref[0], xn_ref[0]], axis=0))
    hb = hb_ext[:tm]

    def proj(lo, hi, lhs=hb):
        return jnp.dot(lhs, w_ref[:, lo:hi], preferred_element_type=F32)

    cm, sma, smb = cm_ref[...], sma_ref[...], smb_ref[...]
    cs, ssa, ssb = cs_ref[...], ssa_ref[...], ssb_ref[...]

    pq = proj(C_Q, C_CKV)
    for h in range(MLA_HEADS):
        xh = pq[:, HEAD_PAD * h:HEAD_PAD * (h + 1)]
        q_ref[0, :, HEAD_PAD * h:HEAD_PAD * (h + 1)] = (
            _rope(xh, cm, sma, smb, MLA_ROPE // 2) * (MLA_SCALE * LOG2E)).astype(BF16)

    ckv = proj(C_CKV, C_KR)
    cn = _rms(ckv, kvg_ref[...]).astype(BF16)
    kv = jnp.dot(cn, wkv_ref[...], preferred_element_type=F32)
    krr = _rope(proj(C_KR, C_SQ), cm, sma, smb, MLA_ROPE // 2)
    for h in range(MLA_HEADS):
        kk_ref[0, :, HEAD_PAD * h:HEAD_PAD * (h + 1)] = (
            kv[:, HEAD_PAD * h:HEAD_PAD * (h + 1)] + krr).astype(BF16)
    ones = jnp.ones((hb.shape[0], LANE), BF16)
    for j in range(MLA_HEADS // 2):
        v0 = MLA_HEADS * HEAD_PAD + LANE * j
        vv_ref[0, :, 2 * LANE * j:2 * LANE * j + LANE] = kv[:, v0:v0 + LANE].astype(BF16)
        vv_ref[0, :, 2 * LANE * j + LANE:2 * LANE * (j + 1)] = ones

    psq = proj(C_SQ, C_SK)
    for j in range(SWA_HEADS // 2):
        xh = psq[:, LANE * j:LANE * (j + 1)]
        sq_ref[0, :, LANE * j:LANE * (j + 1)] = (
            _rope(xh, cs, ssa, ssb, SWA_HEAD_DIM // 2) * (SWA_SCALE * LOG2E)).astype(BF16)
    psk = proj(C_SK, C_SV)
    for k in range(SWA_KV_HEADS):
        sk_ref[0, :, LANE * k:LANE * (k + 1)] = _rope(
            psk[:, LANE * k:LANE * (k + 1)], cs, ssa, ssb, SWA_HEAD_DIM // 2).astype(BF16)
    sv_ref[0] = proj(C_SV, C_HY).astype(BF16)

    ph = proj(C_HY, C_GT, hb_ext)
    pprev, pnext = _halo_rows(ph, tm, i, nt)
    u = _dwconv3(ph[:tm], pprev, pnext, cw_ref, cb_ref)
    z_ref[0] = u[:, :HY_WIDTH]
    x1_ref[0] = u[:, HY_WIDTH:2 * HY_WIDTH]
    x2_ref[0] = u[:, 2 * HY_WIDTH:]

    gt_ref[0] = jax.nn.sigmoid(proj(C_GT, w_ref.shape[1])).astype(BF16)


def _in_proj(x, sh, sc, g, w, wkv, kvg, rope_m, rope_s, cw, cb):
    b, s, d = x.shape
    tm = min(IN_PROJ_ROWS, s)
    nt = s // tm
    row = lambda width: pl.BlockSpec((1, tm, width), lambda bi, i: (bi, i, 0))
    const2 = lambda a: pl.BlockSpec(a.shape, lambda bi, i: (0, 0), pipeline_mode=pl.Buffered(1))
    tab = pl.BlockSpec((tm, LANE), lambda bi, i: (i, 0))
    in_specs = [
        row(d),
        *_halo_specs(tm, s, d),
        pl.BlockSpec((1, 1, d), lambda bi, i: (bi, 0, 0)),
        pl.BlockSpec((1, 1, d), lambda bi, i: (bi, 0, 0)),
        const2(g), const2(w), const2(wkv), const2(kvg),
        tab, tab, tab, tab, tab, tab,
        const2(cw), const2(cb),
    ]
    widths = [(MLA_HEADS * HEAD_PAD, BF16), (MLA_HEADS * HEAD_PAD, BF16), (MLA_HEADS * LANE, BF16),
              (SWA_HEADS * SWA_HEAD_DIM, BF16), (2 * LANE, BF16), (2 * LANE, BF16),
              (HY_WIDTH, F32), (HY_WIDTH, F32), (HY_WIDTH, F32), (N_BRANCH * d, BF16)]
    return pl.pallas_call(
        _in_proj_kernel,
        grid=(b, nt),
        in_specs=in_specs,
        out_specs=[row(wd) for wd, _ in widths],
        out_shape=[jax.ShapeDtypeStruct((b, s, wd), dt) for wd, dt in widths],
        compiler_params=_cparams(("arbitrary", "arbitrary")),
        name="in_proj",
    )(x, x, x, sh, sc, g, w, wkv, kvg, *rope_m, *rope_s, cw, cb)


def _mla_kernel(*refs, tk, n_lat, rb):
    if n_lat:
        q_ref, kc_ref, vc_ref, kl_ref, vl_ref, o_ref, m_scr, acc_scr = refs
    else:
        q_ref, kc_ref, vc_ref, o_ref, m_scr, acc_scr = refs
    tq = q_ref.shape[1]
    nt = (((1,), (1,)), ((), ()))
    def step(k_ref, v_ref, off, n, first=False):
        v = v_ref[0, pl.ds(off, n), :]
        for e in range(2):
            k = k_ref[0, pl.ds(off, n), HEAD_PAD * e:HEAD_PAD * (e + 1)]
            for r in range(tq // rb):
                rows = pl.ds(r * rb, rb)
                s = lax.dot_general(q_ref[0, rows, HEAD_PAD * e:HEAD_PAD * (e + 1)], k, nt,
                                    preferred_element_type=F32)
                s_max = jnp.max(s, axis=1, keepdims=True)
                if first:
                    m_new = jnp.tile(s_max, (1, LANE))
                else:
                    m_old = m_scr[e, rows, :]
                    m_new = jnp.maximum(m_old, s_max)
                p = jnp.exp2(s - jnp.tile(m_new, (1, n // LANE))).astype(BF16)
                pv = jnp.dot(p, v, preferred_element_type=F32)
                if first:
                    acc_scr[e, rows, :] = pv
                else:
                    alpha = jnp.exp2(m_old - m_new)
                    acc_scr[e, rows, :] = jnp.tile(alpha, (1, 2)) * acc_scr[e, rows, :] + pv
                m_scr[e, rows, :] = m_new

    step(kc_ref, vc_ref, 0, kc_ref.shape[1], first=True)
    if n_lat:
        def body(j, carry):
            step(kl_ref, vl_ref, pl.multiple_of(j * tk, tk), tk)
            return carry
        lax.fori_loop(0, n_lat, body, 0)
    lane = lax.broadcasted_iota(jnp.int32, (tq, LANE), 1)
    o0 = acc_scr[0, :, :LANE] / acc_scr[0, :, LANE:]
    o1 = acc_scr[1, :, :LANE] / acc_scr[1, :, LANE:]
    o_ref[0] = jnp.where(lane < MLA_V, o0, o1).astype(o_ref.dtype)


def _mla(q, kc, vc, kl=None, vl=None, tq=4096, tk=512, rb=128):
    b, sq, _ = q.shape
    c = kc.shape[1]
    tq = min(tq, sq)
    rb = min(rb, tq)
    hp = MLA_HEADS // 2
    in_specs = [pl.BlockSpec((1, tq, 2 * HEAD_PAD), lambda bi, h, i: (bi, i, h)),
                pl.BlockSpec((1, c, 2 * HEAD_PAD), lambda bi, h, i: (bi, 0, h)),
                pl.BlockSpec((1, c, 2 * LANE), lambda bi, h, i: (bi, 0, h))]
    args = [q, kc, vc]
    n_lat = 0
    if kl is not None:
        s = kl.shape[1]
        tk = min(tk, s)
        n_lat = s // tk
        in_specs += [pl.BlockSpec((1, s, 2 * HEAD_PAD), lambda bi, h, i: (bi, 0, h)),
                     pl.BlockSpec((1, s, 2 * LANE), lambda bi, h, i: (bi, 0, h))]
        args += [kl, vl]
    return pl.pallas_call(
        functools.partial(_mla_kernel, tk=tk, n_lat=n_lat, rb=rb),
        grid=(b, hp, sq // tq),
        in_specs=in_specs,
        out_specs=pl.BlockSpec((1, tq, LANE), lambda bi, h, i: (bi, i, h)),
        out_shape=jax.ShapeDtypeStruct((b, sq, MLA_HEADS * MLA_V), BF16),
        scratch_shapes=[pltpu.VMEM((2, tq, LANE), F32), pltpu.VMEM((2, tq, 2 * LANE), F32)],
        compiler_params=_cparams(("arbitrary", "arbitrary", "arbitrary")),
        name="mla_attn",
    )(*args)


def _swa_kernel(*refs, band):
    if band:
        sink_ref, q_ref, kc_ref, vc_ref, kp_ref, k0_ref, kn_ref, vp_ref, v0_ref, vn_ref, o_ref = refs
        kband = jnp.concatenate([kp_ref[0], k0_ref[0], kn_ref[0]], axis=0)
        vband = jnp.concatenate([vp_ref[0], v0_ref[0], vn_ref[0]], axis=0)
    else:
        sink_ref, q_ref, kc_ref, vc_ref, o_ref = refs
    i = pl.program_id(1)
    nb = pl.num_programs(1)
    tq = q_ref.shape[1]
    nsub = tq // SWA_BLOCK
    n_ctx = kc_ref.shape[1]
    n_keys = n_ctx + (3 * SWA_BLOCK if band else 0)
    grp = SWA_HEADS // SWA_KV_HEADS
    rows_all = grp * SWA_BLOCK
    nt = (((1,), (1,)), ((), ()))
    lane = lax.broadcasted_iota(jnp.int32, (SWA_BLOCK, LANE), 1)
    low = lane < SWA_HEAD_DIM
    row = lax.broadcasted_iota(jnp.int32, (rows_all, 1), 0)
    if band:
        r_loc = lax.broadcasted_iota(jnp.int32, (rows_all, n_keys), 0) % SWA_BLOCK
        col = lax.broadcasted_iota(jnp.int32, (rows_all, n_keys), 1) - n_ctx
        prev_bias = jnp.where((col >= 0) & (col < SWA_BLOCK), NEG_INF, 0.0)
        next_bias = jnp.where(col >= 2 * SWA_BLOCK, NEG_INF, 0.0)
        base_bias = (jnp.where(r_loc > col, prev_bias, 0.0)
                     + jnp.where(col - 2 * SWA_BLOCK > r_loc, next_bias, 0.0))
    for r in range(nsub):
        rs = slice(r * SWA_BLOCK, (r + 1) * SWA_BLOCK)
        if band:
            bias = base_bias
            if r == 0:
                bias = jnp.minimum(bias, jnp.where(i == 0, prev_bias, 0.0))
            if r == nsub - 1:
                bias = jnp.minimum(bias, jnp.where(i == nb - 1, next_bias, 0.0))
            bs = slice(r * SWA_BLOCK, (r + 3) * SWA_BLOCK)
        for kvh in range(SWA_KV_HEADS):
            sl = slice(LANE * kvh, LANE * (kvh + 1))
            k, v = kc_ref[0, :, sl], vc_ref[0, :, sl]
            if band:
                k = jnp.concatenate([k, kband[bs, sl]], axis=0)
                v = jnp.concatenate([v, vband[bs, sl]], axis=0)
            v = jnp.concatenate([v, jnp.ones_like(v)], axis=1)
            parts, sink = [], None
            for t in range(grp // 2):
                pair = kvh * (grp // 2) + t
                qp = q_ref[0, rs, LANE * pair:LANE * (pair + 1)]
                parts += [jnp.where(low, qp, jnp.zeros_like(qp)), jnp.where(low, jnp.zeros_like(qp), qp)]
            for h in range(grp):
                sh = sink_ref[kvh * grp + h] * LOG2E
                sink = sh if sink is None else jnp.where(row < h * SWA_BLOCK, sink, sh)
            s = lax.dot_general(jnp.concatenate(parts, axis=0), k, nt, preferred_element_type=F32)
            if band:
                s = s + bias
            m = jnp.maximum(jnp.max(s, axis=1, keepdims=True), sink)
            p = jnp.exp2(s - m).astype(BF16)
            acc = jnp.dot(p, v, preferred_element_type=F32)
            o = acc[:, :LANE] / (acc[:, LANE:] + jnp.exp2(sink - m))
            for t in range(grp // 2):
                pair = kvh * (grp // 2) + t
                lo = o[(2 * t) * SWA_BLOCK:(2 * t + 1) * SWA_BLOCK]
                hi = o[(2 * t + 1) * SWA_BLOCK:(2 * t + 2) * SWA_BLOCK]
                o_ref[0, rs, LANE * pair:LANE * (pair + 1)] = jnp.where(low, lo, hi).astype(o_ref.dtype)


def _swa(sink, q, kc, vc, k=None, v=None, tq=1024):
    b, sq, _ = q.shape
    c = kc.shape[1]
    tq = min(tq, sq)
    nb = sq // tq
    band = k is not None
    blk = lambda rows, width, f: pl.BlockSpec((1, rows, width), f)
    in_specs = [pl.BlockSpec(memory_space=pltpu.SMEM),
                blk(tq, SWA_HEADS * SWA_HEAD_DIM, lambda bi, i: (bi, i, 0)),
                pl.BlockSpec((1, c, 2 * LANE), lambda bi, i: (bi, 0, 0)),
                pl.BlockSpec((1, c, 2 * LANE), lambda bi, i: (bi, 0, 0))]
    args = [sink, q, kc, vc]
    if band:
        per = tq // SWA_BLOCK
        prev = lambda bi, i: (bi, jnp.maximum(i * per - 1, 0), 0)
        cur = lambda bi, i: (bi, i, 0)
        nxt = lambda bi, i: (bi, jnp.minimum((i + 1) * per, sq // SWA_BLOCK - 1), 0)
        in_specs += [blk(SWA_BLOCK, 2 * LANE, prev), blk(tq, 2 * LANE, cur), blk(SWA_BLOCK, 2 * LANE, nxt),
                     blk(SWA_BLOCK, 2 * LANE, prev), blk(tq, 2 * LANE, cur), blk(SWA_BLOCK, 2 * LANE, nxt)]
        args += [k, k, k, v, v, v]
    return pl.pallas_call(
        functools.partial(_swa_kernel, band=band),
        grid=(b, nb),
        in_specs=in_specs,
        out_specs=blk(tq, SWA_HEADS * SWA_HEAD_DIM, lambda bi, i: (bi, i, 0)),
        out_shape=jax.ShapeDtypeStruct((b, sq, SWA_HEADS * SWA_HEAD_DIM), BF16),
        compiler_params=_cparams(("arbitrary", "arbitrary")),
        name="swa_attn",
    )(*args)


def _merge_kernel(x_ref, ya_ref, yb_ref, yc_ref, gt_ref, wb_ref, wo_ref, ng_ref, g1_ref, o_ref):
    d = x_ref.shape[2]
    merged = None
    for k, y_ref in enumerate((ya_ref, yb_ref, yc_ref)):
        t = jnp.dot(y_ref[0].astype(BF16), wb_ref[k], preferred_element_type=F32)
        t = gt_ref[0, :, d * k:d * (k + 1)].astype(F32) * t
        merged = t if merged is None else merged + t
    y = jnp.dot(merged.astype(BF16), wo_ref[...], preferred_element_type=F32)
    o_ref[0] = x_ref[0] + g1_ref[0] * _rms(y, ng_ref[...])


def _merge(x, ya, yb, yc, gt, wb, wo, ng, g1):
    b, s, d = x.shape
    tm = min(MERGE_ROWS, s)
    row = lambda width: pl.BlockSpec((1, tm, width), lambda bi, i: (bi, i, 0))
    return pl.pallas_call(
        _merge_kernel,
        grid=(b, s // tm),
        in_specs=[row(d), row(BRANCH_WIDTH), row(BRANCH_WIDTH), row(BRANCH_WIDTH), row(N_BRANCH * d),
                  pl.BlockSpec(wb.shape, lambda bi, i: (0, 0, 0)),
                  pl.BlockSpec(wo.shape, lambda bi, i: (0, 0)),
                  pl.BlockSpec(ng.shape, lambda bi, i: (0, 0)),
                  pl.BlockSpec((1, 1, d), lambda bi, i: (bi, 0, 0))],
        out_specs=row(d),
        out_shape=jax.ShapeDtypeStruct((b, s, d), F32),
        compiler_params=_cparams(("arbitrary", "arbitrary")),
        name="merge",
    )(x, ya, yb, yc, gt, wb, wo, ng, g1)


def _ffn_kernel(x_ref, xp_ref, xn_ref, sh_ref, sc_ref, g2_ref, ng_in_ref, ng_out_ref,
                wu_ref, cw_ref, cb_ref, wd_ref, o_ref, *, chunk):
    i = pl.program_id(1)
    nt = pl.num_programs(1)
    g = ng_in_ref[...]
    sh = sh_ref[0]
    sc = sc_ref[0]

    def mod(xf):
        return (_rms(xf, g) * (1.0 + sc) + sh).astype(BF16)

    x = x_ref[0]
    tm = x.shape[0]
    hb = mod(jnp.concatenate([x, xp_ref[0], xn_ref[0]], axis=0))
    dff = wd_ref.shape[0]
    f = None
    for c0 in range(0, dff, chunk):
        halves = []
        for off in (c0, dff + c0):
            p = jnp.dot(hb, wu_ref[:, off:off + chunk], preferred_element_type=F32)
            pprev, pnext = _halo_rows(p, tm, i, nt)
            halves.append(_dwconv3(p[:tm], pprev, pnext, cw_ref.at[:, off:off + chunk],
                                   cb_ref.at[:, off:off + chunk]))
        a, bb = halves
        act = (a * jax.nn.sigmoid(a) * bb).astype(BF16)
        t = jnp.dot(act, wd_ref[c0:c0 + chunk, :], preferred_element_type=F32)
        f = t if f is None else f + t
    o_ref[0] = x + g2_ref[0] * _rms(f, ng_out_ref[...])


def _ffn(x, sh, sc, g2, ng_in, ng_out, wu, cw, cb, wd):
    b, s, d = x.shape
    tm = min(FFN_ROWS, s)
    dff = wd.shape[0]
    chunk = dff // FFN_CHUNKS if dff % (FFN_CHUNKS * LANE) == 0 else dff
    row = pl.BlockSpec((1, tm, d), lambda bi, i: (bi, i, 0))
    vec = pl.BlockSpec((1, 1, d), lambda bi, i: (bi, 0, 0))
    const2 = lambda a: pl.BlockSpec(a.shape, lambda bi, i: (0, 0), pipeline_mode=pl.Buffered(1))
    return pl.pallas_call(
        functools.partial(_ffn_kernel, chunk=chunk),
        grid=(b, s // tm),
        in_specs=[row, *_halo_specs(tm, s, d),
                  vec, vec, vec, const2(ng_in), const2(ng_out),
                  const2(wu), const2(cw), const2(cb), const2(wd)],
        out_specs=row,
        out_shape=jax.ShapeDtypeStruct((b, s, d), F32),
        compiler_params=_cparams(("arbitrary", "arbitrary")),
        name="conv_ffn",
    )(x, x, x, sh, sc, g2, ng_in, ng_out, wu, cw, cb, wd)


HY_PASSES = 1
SPEC_PASSES = 1
HY_UNROLL = 2
HY_UNROLL2 = 6


def _split(x):
    hi = x.astype(BF16)
    return hi, (x - hi.astype(F32)).astype(BF16)


def _cdot(c_hi, c_lo, x, passes=HY_PASSES):
    x_hi, x_lo = _split(x)
    acc = jnp.dot(c_hi, x_hi, preferred_element_type=F32)
    if passes > 1:
        acc = acc + jnp.dot(c_lo, x_hi, preferred_element_type=F32)
        acc = acc + jnp.dot(c_hi, x_lo, preferred_element_type=F32)
    return acc


def _np_split(m):
    m = jnp.asarray(m, F32)
    hi = m.astype(BF16)
    return hi, (m - hi.astype(F32)).astype(BF16)


def _dft_consts(n_blocks, nonzero_blocks):
    nb = n_blocks
    n = LANE * nb
    dh = nb // 2 + 1
    dhp = -(-dh // 8) * 8
    d = np.arange(dh)[:, None]
    b = np.arange(nonzero_blocks)[None, :]
    ang = 2 * np.pi * ((d * b) % nb) / nb
    f1 = np.zeros((2 * dhp, nonzero_blocks))
    f1[:dh] = np.cos(ang)
    f1[dhp:dhp + dh] = -np.sin(ang)
    a = np.arange(LANE)
    ang2 = 2 * np.pi * ((a[:, None] * a[None, :]) % LANE) / LANE
    ar, ai = np.cos(ang2), -np.sin(ang2)
    f2 = np.block([[ar, -ai], [ai, ar]])
    f2inv = np.block([[ar, ai], [-ai, ar]])
    bo = np.arange(nb // 2)[:, None]
    do = np.arange(dh)[None, :]
    wd = np.where((do == 0) | (do == nb // 2), 1.0, 2.0) / n
    ang3 = 2 * np.pi * ((bo * do) % nb) / nb
    f3 = np.zeros((nb // 2, 2 * dhp))
    f3[:, :dh] = wd * np.cos(ang3)
    f3[:, dhp:dhp + dh] = -wd * np.sin(ang3)
    step_ang = 2 * np.pi * a / n
    tw_step = np.stack([np.broadcast_to(np.cos(step_ang)[:, None], (LANE, LANE)),
                        np.broadcast_to(-np.sin(step_ang)[:, None], (LANE, LANE))])
    eye = np.eye(SUBLANES)
    return dict(dh=dh, dhp=dhp, f1=_np_split(np.kron(f1, eye)), f2=_np_split(f2), f2inv=_np_split(f2inv),
                f3=_np_split(np.kron(f3, eye)), tw_step=jnp.asarray(tw_step, F32))


def _stage1(x_ref, f1_hi, f1_lo, g_scr, passes):
    n_in, n_out = x_ref.shape[0], g_scr.shape[0]

    def body(t, carry):
        r0 = pl.ds(pl.multiple_of(2 * t * SUBLANES, SUBLANES), SUBLANES)
        r1 = pl.ds(pl.multiple_of((2 * t + 1) * SUBLANES, SUBLANES), SUBLANES)
        xa = jnp.concatenate([x_ref[:, r0, :].reshape(n_in * SUBLANES, LANE),
                              x_ref[:, r1, :].reshape(n_in * SUBLANES, LANE)], axis=1)
        g = _cdot(f1_hi, f1_lo, xa, passes)
        g_scr[:, r0, :] = g[:, :LANE].reshape(n_out, SUBLANES, LANE)
        g_scr[:, r1, :] = g[:, LANE:].reshape(n_out, SUBLANES, LANE)
        return carry
    lax.fori_loop(0, LANE // (2 * SUBLANES), body, 0, unroll=HY_UNROLL)


def _twiddle_init(tw_scr):
    tw_scr[0] = jnp.ones((LANE, LANE), F32)
    tw_scr[1] = jnp.zeros((LANE, LANE), F32)


def _twiddle_next(twr, twi, step_ref):
    sr, si = step_ref[0], step_ref[1]
    return twr * sr - twi * si, twr * si + twi * sr


def _twiddle_pair(tw_scr, step_ref):
    t0r, t0i = tw_scr[0], tw_scr[1]
    t1r, t1i = _twiddle_next(t0r, t0i, step_ref)
    t2r, t2i = _twiddle_next(t1r, t1i, step_ref)
    tw_scr[0] = t2r
    tw_scr[1] = t2i
    return jnp.concatenate([t0r, t1r], axis=1), jnp.concatenate([t0i, t1i], axis=1)


def _spectrum_kernel(k_ref, inv_ref, f1h_ref, f1l_ref, f2h_ref, f2l_ref, step_ref, hf_ref, g_scr, tw_scr,
                     *, dh, dhp):
    _stage1(k_ref, f1h_ref[...], f1l_ref[...], g_scr, SPEC_PASSES)
    _twiddle_init(tw_scr)
    inv = inv_ref[...]

    def body(h, carry):
        d = 2 * h
        gr = jnp.concatenate([g_scr[d], g_scr[d + 1]], axis=1)
        gi = jnp.concatenate([g_scr[dhp + d], g_scr[dhp + d + 1]], axis=1)
        twr, twi = _twiddle_pair(tw_scr, step_ref)
        t = jnp.concatenate([gr * twr - gi * twi, gr * twi + gi * twr], axis=0)
        y = _cdot(f2h_ref[...], f2l_ref[...], t, passes=SPEC_PASSES)
        hf_ref[0, d] = y[:LANE, :LANE] * inv
        hf_ref[0, d + 1] = y[:LANE, LANE:] * inv
        hf_ref[1, d] = y[LANE:, :LANE] * inv
        hf_ref[1, d + 1] = y[LANE:, LANE:] * inv
        return carry
    lax.fori_loop(0, dhp // 2, body, 0, unroll=HY_UNROLL2)


def _hyena_spectrum(kern, inv_norm):
    tiles, n, _ = kern.shape
    c = tiles * LANE
    nb = n // LANE
    cst = _dft_consts(nb, nb)
    dh, dhp = cst["dh"], cst["dhp"]
    const = lambda a: pl.BlockSpec(a.shape, lambda j: (0,) * a.ndim, pipeline_mode=pl.Buffered(1))
    consts = [*cst["f1"], *cst["f2"], cst["tw_step"]]
    return pl.pallas_call(
        functools.partial(_spectrum_kernel, dh=dh, dhp=dhp),
        grid=(c // LANE,),
        in_specs=[pl.BlockSpec((None, nb, LANE, LANE), lambda j: (j, 0, 0, 0)),
                  pl.BlockSpec((1, LANE), lambda j: (0, j))]
        + [const(a) for a in consts],
        out_specs=pl.BlockSpec((None, 2, dhp, LANE, LANE), lambda j: (j, 0, 0, 0, 0)),
        out_shape=jax.ShapeDtypeStruct((c // LANE, 2, dhp, LANE, LANE), F32),
        scratch_shapes=[pltpu.VMEM((2 * dhp, LANE, LANE), F32), pltpu.VMEM((2, LANE, LANE), F32)],
        compiler_params=_cparams(("arbitrary",)),
        name="hyena_spectrum",
    )(kern.reshape(tiles, nb, LANE, LANE), inv_norm, *consts)


def _conv_kernel(z_ref, gate_ref, skip_ref, hf_ref, f1h_ref, f1l_ref, f2h_ref, f2l_ref, f2ih_ref, f2il_ref,
                 f3h_ref, f3l_ref, step_ref, o_ref, g_scr, tw_scr, *, dh, dhp):
    nbh = z_ref.shape[1]
    _stage1(z_ref.at[0], f1h_ref[...], f1l_ref[...], g_scr, HY_PASSES)
    _twiddle_init(tw_scr)

    def body(h, carry):
        d = 2 * h
        gr = jnp.concatenate([g_scr[d], g_scr[d + 1]], axis=1)
        gi = jnp.concatenate([g_scr[dhp + d], g_scr[dhp + d + 1]], axis=1)
        twr, twi = _twiddle_pair(tw_scr, step_ref)
        t = jnp.concatenate([gr * twr - gi * twi, gr * twi + gi * twr], axis=0)
        y = _cdot(f2h_ref[...], f2l_ref[...], t)
        yr, yi = y[:LANE], y[LANE:]
        hr = jnp.concatenate([hf_ref[0, d], hf_ref[0, d + 1]], axis=1)
        hi = jnp.concatenate([hf_ref[1, d], hf_ref[1, d + 1]], axis=1)
        zz = jnp.concatenate([yr * hr - yi * hi, yr * hi + yi * hr], axis=0)
        u = _cdot(f2ih_ref[...], f2il_ref[...], zz)
        ur, ui = u[:LANE], u[LANE:]
        vr = ur * twr + ui * twi
        vi = ui * twr - ur * twi
        g_scr[d] = vr[:, :LANE]
        g_scr[d + 1] = vr[:, LANE:]
        g_scr[dhp + d] = vi[:, :LANE]
        g_scr[dhp + d + 1] = vi[:, LANE:]
        return carry
    lax.fori_loop(0, dhp // 2, body, 0, unroll=HY_UNROLL2)

    skip = skip_ref[...]
    f3h, f3l = f3h_ref[...], f3l_ref[...]

    def out_body(t, carry):
        r0 = pl.ds(pl.multiple_of(2 * t * SUBLANES, SUBLANES), SUBLANES)
        r1 = pl.ds(pl.multiple_of((2 * t + 1) * SUBLANES, SUBLANES), SUBLANES)
        ga = jnp.concatenate([g_scr[:, r0, :].reshape(2 * dhp * SUBLANES, LANE),
                              g_scr[:, r1, :].reshape(2 * dhp * SUBLANES, LANE)], axis=1)
        conv = _cdot(f3h, f3l, ga)
        for rows, c in ((r0, conv[:, :LANE]), (r1, conv[:, LANE:])):
            o_ref[0, :, rows, :] = gate_ref[0, :, rows, :] * (
                c.reshape(nbh, SUBLANES, LANE) + z_ref[0, :, rows, :] * skip)
        return carry
    lax.fori_loop(0, LANE // (2 * SUBLANES), out_body, 0, unroll=HY_UNROLL)


def _hyena_conv(z, gate, skip, hf, tile0, cst):
    b, l, w = z.shape
    nbh = l // LANE
    dh, dhp = cst["dh"], cst["dhp"]
    consts = [*cst["f1"], *cst["f2"], *cst["f2inv"], *cst["f3"], cst["tw_step"]]
    const = lambda a: pl.BlockSpec(a.shape, lambda j, bi: (0,) * a.ndim, pipeline_mode=pl.Buffered(1))
    tile = pl.BlockSpec((1, nbh, LANE, LANE), lambda j, bi: (bi, 0, 0, j))
    z, gate = z.reshape(b, nbh, LANE, w), gate.reshape(b, nbh, LANE, w)
    return pl.pallas_call(
        functools.partial(_conv_kernel, dh=dh, dhp=dhp),
        grid=(w // LANE, b),
        in_specs=[tile, tile, pl.BlockSpec((1, LANE), lambda j, bi: (0, j)),
                  pl.BlockSpec((None, 2, dhp, LANE, LANE), lambda j, bi: (j + tile0, 0, 0, 0, 0),
                               pipeline_mode=pl.Buffered(1))]
        + [const(a) for a in consts],
        out_specs=tile,
        out_shape=jax.ShapeDtypeStruct((b, nbh, LANE, w), F32),
        scratch_shapes=[pltpu.VMEM((2 * dhp, LANE, LANE), F32), pltpu.VMEM((2, LANE, LANE), F32)],
        compiler_params=_cparams(("arbitrary", "arbitrary")),
        name="hyena_conv",
    )(z, gate, skip, hf, *consts).reshape(b, l, w)


def _filter_mlp_kernel(ft_ref, w1_ref, b1_ref, w2_ref, b2_ref, fr_ref, w3_ref, dl_ref, k_ref, s_ref):
    i = pl.program_id(0)
    ft = ft_ref[...]

    def mm(x, w):
        x_hi, x_lo = _split(x)
        w_hi, w_lo = _split(w)
        return (jnp.dot(x_hi, w_hi, preferred_element_type=F32) + jnp.dot(x_lo, w_hi, preferred_element_type=F32)
                + jnp.dot(x_hi, w_lo, preferred_element_type=F32))

    half = ft.shape[0] // 2
    cw = dl_ref.shape[1]
    hid = jnp.sin(fr_ref[0:1, :] * (mm(jnp.concatenate([ft[:half], ft[half:]], axis=1), w1_ref[...]) + b1_ref[...]))
    hid = jnp.sin(fr_ref[1:2, :] * (mm(hid, w2_ref[...]) + b2_ref[...]))
    h2 = mm(hid, w3_ref[0])
    k = jnp.concatenate([h2[:, :cw], h2[:, cw:]], axis=0)
    k = k * jnp.exp(-ft[:, 0:1] * dl_ref[...]) * ft[:, LANE - 1:LANE]
    for j in range(k_ref.shape[0]):
        k_ref[j] = k[:, LANE * j:LANE * (j + 1)]

    @pl.when(i == 0)
    def _():
        s_ref[...] = jnp.zeros(s_ref.shape, F32)
    s_ref[...] += jnp.sum(jnp.abs(k), axis=0, keepdims=True)


def _hyena_filter_taps(n_tokens, w1, b1, w2, b2, freq, w3):
    L = n_tokens
    n = 2 * L
    idx = jnp.arange(n)
    m = jnp.where(idx < L, idx, n - idx).astype(F32)
    t = m / max(L - 1, 1)
    bands = jnp.linspace(1e-4, HY_BANDS - 1, HY_BANDS, dtype=F32)
    ang = (2.0 * math.pi / L) * m[:, None] * bands
    n_feat = 2 * HY_BANDS + 1
    feats = jnp.concatenate([t[:, None], jnp.cos(ang), -jnp.sin(ang),
                             jnp.zeros((n, LANE - n_feat - 1), F32),
                             (idx != L).astype(F32)[:, None]], axis=-1)
    two = lambda a: jnp.concatenate([a, a], axis=-1)
    bdiag = lambda a: jnp.concatenate([jnp.pad(a, ((0, 0), (0, a.shape[1]))), jnp.pad(a, ((0, 0), (a.shape[1], 0)))], axis=0)
    w1p = bdiag(jnp.pad(w1, ((0, LANE - n_feat), (0, 0))))
    cw = HY_ORDER * HY_WIDTH
    w3d = jnp.moveaxis(w3.reshape(HY_HIDDEN, 2, cw), 1, 0)
    w3d = jnp.stack([bdiag(w3d[0]), bdiag(w3d[1])])
    w2, freq = bdiag(w2), two(freq)
    deltas = jnp.abs(jnp.linspace(math.log(HY_TARGET) / HY_SLOW_PCT, math.log(HY_TARGET) / HY_FAST_PCT,
                                  HY_WIDTH, dtype=F32))
    dl = jnp.tile(deltas, HY_ORDER)[None, :]
    tr = min(1024, L)
    nt = n // tr
    const = lambda a: pl.BlockSpec(a.shape, lambda i: (0,) * a.ndim)
    b1r, b2r = two(b1)[None, :], two(b2)[None, :]
    return pl.pallas_call(
        _filter_mlp_kernel,
        grid=(nt,),
        in_specs=[pl.BlockSpec((tr, LANE), lambda i: (i, 0)), const(w1p), const(b1r), const(w2), const(b2r),
                  const(freq), pl.BlockSpec((1, 2 * HY_HIDDEN, 2 * cw), lambda i: (i // (nt // 2), 0, 0)), const(dl)],
        out_specs=[pl.BlockSpec((cw // LANE, tr, LANE), lambda i: (0, i, 0)), pl.BlockSpec((1, cw), lambda i: (0, 0))],
        out_shape=[jax.ShapeDtypeStruct((cw // LANE, n, LANE), F32), jax.ShapeDtypeStruct((1, cw), F32)],
        compiler_params=_cparams(("arbitrary",)),
        name="hyena_filter_mlp",
    )(feats, w1p, b1r, w2, b2r, freq, w3d, dl)


def _hyena_long(z, x1, x2, taps, norms, skip):
    b, l, w = z.shape
    nb = 2 * l // LANE
    hf = _hyena_spectrum(taps, 1.0 / norms)
    cst = _dft_consts(nb, nb // 2)
    for o, gate in enumerate((x1, x2)):
        z = _hyena_conv(z, gate, skip[o][None, :], hf, o * (w // LANE), cst)
    return z


def _dense_dft_consts(length):
    n = 2 * length
    h = length + 1
    hp = -(-h // 8) * 8
    k = np.arange(h)[:, None]
    pos = np.arange(n)[None, :]
    ang = 2 * np.pi * ((k * pos) % n) / n
    fwd = np.zeros((2 * hp, n))
    fwd[:h] = np.cos(ang)
    fwd[hp:hp + h] = -np.sin(ang)
    wk = np.where((k == 0) | (k == length), 1.0, 2.0) / n
    inv = np.zeros((length, 2 * hp))
    inv[:, :h] = (wk * np.cos(ang[:, :length])).T
    inv[:, hp:hp + h] = (-wk * np.sin(ang[:, :length])).T
    return dict(hp=hp, full=_np_split(fwd), fwd=_np_split(fwd[:, :length]), inv=_np_split(inv))


def _hyena_short_kernel(z_ref, x1_ref, x2_ref, k_ref, inv_ref, skip_ref, ffh_ref, ffl_ref, fh_ref, fl_ref,
                        fih_ref, fil_ref, o_ref, *, hp):
    z = z_ref[0]
    w = z.shape[1]
    for o, gate_ref in enumerate((x1_ref, x2_ref)):
        cols = slice(o * w, (o + 1) * w)
        hf = _cdot(ffh_ref[...], ffl_ref[...], k_ref[:, cols], passes=3) * inv_ref[:, cols]
        x = _cdot(fh_ref[...], fl_ref[...], z, passes=3)
        xr, xi, hr, hi = x[:hp], x[hp:], hf[:hp], hf[hp:]
        zz = jnp.concatenate([xr * hr - xi * hi, xr * hi + xi * hr], axis=0)
        conv = _cdot(fih_ref[...], fil_ref[...], zz, passes=3)
        z = gate_ref[0] * (conv + z * skip_ref[o:o + 1, :])
    o_ref[0] = z


def _hyena_short(z, x1, x2, taps, norms, skip):
    b, l, w = z.shape
    taps = jnp.moveaxis(taps, 0, 1).reshape(2 * l, -1)
    cst = _dense_dft_consts(l)
    consts = [*cst["full"], *cst["fwd"], *cst["inv"]]
    inv_norm = 1.0 / norms
    const = lambda a: pl.BlockSpec(a.shape, lambda bi: (0,) * a.ndim)
    tile = pl.BlockSpec((1, l, w), lambda bi: (bi, 0, 0))
    return pl.pallas_call(
        functools.partial(_hyena_short_kernel, hp=cst["hp"]),
        grid=(b,),
        in_specs=[tile, tile, tile, const(taps), const(inv_norm), const(skip)] + [const(a) for a in consts],
        out_specs=tile,
        out_shape=jax.ShapeDtypeStruct((b, l, w), F32),
        compiler_params=_cparams(("arbitrary",)),
        name="hyena_short",
    )(z, x1, x2, taps, inv_norm, skip, *consts)


def _rope_tables(rows, rot_dim, head_lanes, rope_off, identity_rows):
    half = rot_dim // 2
    n_freq = rot_dim // 4
    pos = np.arange(rows * GRID_W)
    inv_freq = ROPE_THETA ** (-np.arange(n_freq, dtype=np.float64) / n_freq)
    ang = np.concatenate([(pos // GRID_W)[:, None] * inv_freq, (pos % GRID_W)[:, None] * inv_freq], axis=-1)
    cos, sin = np.cos(ang), np.sin(ang)
    n = pos.shape[0]
    ct = np.ones((n, LANE))
    sa = np.zeros((n, LANE))
    sb = np.zeros((n, LANE))
    for h0 in range(0, LANE, head_lanes):
        lo = h0 + rope_off
        ct[:, lo:lo + half] = cos
        ct[:, lo + half:lo + rot_dim] = cos
        sa[:, lo:lo + half] = -sin
        sb[:, lo + half:lo + rot_dim] = sin
    ident = (np.ones((identity_rows, LANE)), np.zeros((identity_rows, LANE)), np.zeros((identity_rows, LANE)))
    lat = tuple(jnp.asarray(t, F32) for t in (ct, sa, sb))
    ctx = tuple(jnp.asarray(t, F32) for t in ident)
    return lat, ctx


def _pack_w_in(w):
    d = w.shape[0]
    o = 0
    mq = w[:, o:o + MLA_HEADS * (MLA_NOPE + MLA_ROPE)]; o += MLA_HEADS * (MLA_NOPE + MLA_ROPE)
    mckv = w[:, o:o + KV_RANK]; o += KV_RANK
    mkr = w[:, o:o + MLA_ROPE]; o += MLA_ROPE
    sq = w[:, o:o + SWA_HEADS * SWA_HEAD_DIM]; o += SWA_HEADS * SWA_HEAD_DIM
    sk = w[:, o:o + SWA_KV_HEADS * SWA_HEAD_DIM]; o += SWA_KV_HEADS * SWA_HEAD_DIM
    sv = w[:, o:o + SWA_KV_HEADS * SWA_HEAD_DIM]; o += SWA_KV_HEADS * SWA_HEAD_DIM
    hy = w[:, o:o + (HY_ORDER + 1) * HY_WIDTH]; o += (HY_ORDER + 1) * HY_WIDTH
    gt = w[:, o:]
    pad_q = HEAD_PAD - MLA_NOPE - MLA_ROPE
    mq = jnp.pad(mq.reshape(d, MLA_HEADS, MLA_NOPE + MLA_ROPE), ((0, 0), (0, 0), (0, pad_q))).reshape(d, -1)
    mkr = jnp.pad(mkr, ((0, 0), (MLA_NOPE, pad_q)))
    dup = lambda t: jnp.repeat(t.reshape(d, SWA_KV_HEADS, 1, SWA_HEAD_DIM), 2, axis=2).reshape(d, -1)
    return jnp.concatenate([mq, mckv, mkr, sq, dup(sk), dup(sv), hy, gt], axis=1).astype(BF16)


def _pack_w_kv(w):
    r = w.shape[0]
    w = w.reshape(r, MLA_HEADS, MLA_NOPE + MLA_V)
    k = jnp.pad(w[..., :MLA_NOPE], ((0, 0), (0, 0), (0, HEAD_PAD - MLA_NOPE))).reshape(r, -1)
    v = w[..., MLA_NOPE:].reshape(r, -1)
    return jnp.concatenate([k, v], axis=1).astype(BF16)


def kernel(x, c, ctx, c_ctx, w_mod, b_mod, norm_g, w_in, kv_norm_g, w_kv_up, swa_sink, hy_conv_w, hy_conv_b,
           hy_w1, hy_b1, hy_w2, hy_b2, hy_freq, hy_w3, hy_skip, w_branch, w_out, w_up, ffn_conv_w, ffn_conv_b,
           w_down):
    b, s, d = x.shape
    n_ctx = ctx.shape[1]
    depth = w_mod.shape[0]
    assert s % (2 * SWA_BLOCK) == 0 and s % GRID_W == 0, "latent length must tile into 128-token blocks / grid rows"
    assert n_ctx % SWA_BLOCK == 0, "context length must be a multiple of the 128-token block"
    assert d % LANE == 0 and w_in.shape[1] == d
    rows = s // GRID_W
    rope_m, rope_m_ctx = _rope_tables(rows, MLA_ROPE, HEAD_PAD, MLA_NOPE, n_ctx)
    rope_s, rope_s_ctx = _rope_tables(rows, SWA_HEAD_DIM, SWA_HEAD_DIM, 0, n_ctx)

    pad_rows = -(b + 1) % SUBLANES
    cvec = jnp.concatenate([c, c_ctx[None, :], jnp.zeros((pad_rows, d), F32)], axis=0)
    mod_all = _modulation(cvec, w_mod, b_mod)

    x_lat, x_ctx = x, ctx
    for l in range(depth):
        with_ctx = l < depth - 1
        m = mod_all[l].reshape(-1, 6, d)
        lat = [m[:b, k][:, None, :] for k in range(6)]
        cx = [jnp.broadcast_to(m[b, k][None, None, :], (b, 1, d)) for k in range(6)]
        ng = [norm_g[l, k][None, :] for k in range(4)]
        w_pack = _pack_w_in(w_in[l])
        wkv_pack = _pack_w_kv(w_kv_up[l])
        kvg = kv_norm_g[l][None, :]
        cw, cb = hy_conv_w[l], hy_conv_b[l][None, :]

        q, kk, vv, sq, sk, sv, z, x1, x2, gt = _in_proj(
            x_lat, lat[0], lat[1], ng[0], w_pack, wkv_pack, kvg, rope_m, rope_s, cw, cb)
        qc, kkc, vvc, sqc, skc, svc, zc, x1c, x2c, gtc = _in_proj(
            x_ctx, cx[0], cx[1], ng[0], w_pack, wkv_pack, kvg, rope_m_ctx, rope_s_ctx, cw, cb)

        y_a = _mla(q, kkc, vvc, kk, vv)
        y_b = _swa(swa_sink[l], sq, skc, svc, sk, sv)
        hy_mlp = (hy_w1[l], hy_b1[l], hy_w2[l], hy_b2[l], hy_freq[l], hy_w3[l])
        y_c = _hyena_long(z, x1, x2, *_hyena_filter_taps(s, *hy_mlp), hy_skip[l])

        wb = w_branch[l].astype(BF16)
        wo = w_out[l].astype(BF16)
        wu = w_up[l].astype(BF16)
        wd = w_down[l].astype(BF16)
        fcw, fcb = ffn_conv_w[l], ffn_conv_b[l][None, :]

        x_lat = _merge(x_lat, y_a, y_b, y_c, gt, wb, wo, ng[1], lat[2])
        x_lat = _ffn(x_lat, lat[3], lat[4], lat[5], ng[2], ng[3], wu, fcw, fcb, wd)

        if with_ctx:
            yc_a = _mla(qc, kkc, vvc)
            yc_b = _swa(swa_sink[l], sqc, skc, svc)
            yc_c = _hyena_short(zc, x1c, x2c, *_hyena_filter_taps(n_ctx, *hy_mlp), hy_skip[l])
            x_ctx = _merge(x_ctx, yc_a, yc_b, yc_c, gtc, wb, wo, ng[1], cx[2])
            x_ctx = _ffn(x_ctx, cx[3], cx[4], cx[5], ng[2], ng[3], wu, fcw, fcb, wd)
    return x_lat
```

```python
import functools
import math

import numpy as np
import jax
import jax.numpy as jnp
from jax import lax
from jax.experimental import pallas as pl
from jax.experimental.pallas import tpu as pltpu

F32 = jnp.float32
BF16 = jnp.bfloat16

GRID_W = 64
EPS = 1e-6
ROPE_THETA = 10000.0
NEG_INF = -1e30
MLA_HEADS = 8
MLA_NOPE = 64
MLA_ROPE = 32
MLA_V = 64
KV_RANK = 256
MLA_SCALE = (MLA_NOPE + MLA_ROPE) ** -0.5
LOG2E = math.log2(math.e)
SWA_HEADS = 8
SWA_KV_HEADS = 2
SWA_HEAD_DIM = 64
SWA_BLOCK = 128
SWA_SCALE = SWA_HEAD_DIM ** -0.5
HY_WIDTH = 512
HY_ORDER = 2
HY_BANDS = 16
HY_HIDDEN = 64
HY_TARGET = 1e-2
HY_FAST_PCT = 0.3
HY_SLOW_PCT = 1.5
N_BRANCH = 3
BRANCH_WIDTH = 512
LANE = 128
SUBLANES = 8
HEAD_PAD = 128

VMEM_LIMIT = 56 * 1024 * 1024
IN_PROJ_ROWS = 512
FFN_ROWS = 512
FFN_CHUNKS = 1
MERGE_ROWS = 1024

C_Q = 0
C_CKV = C_Q + MLA_HEADS * HEAD_PAD
C_KR = C_CKV + KV_RANK
C_SQ = C_KR + HEAD_PAD
C_SK = C_SQ + SWA_HEADS * SWA_HEAD_DIM
C_SV = C_SK + 2 * SWA_KV_HEADS * SWA_HEAD_DIM
C_HY = C_SV + 2 * SWA_KV_HEADS * SWA_HEAD_DIM
C_GT = C_HY + (HY_ORDER + 1) * HY_WIDTH


def _cparams(sem):
    return pltpu.CompilerParams(dimension_semantics=sem, vmem_limit_bytes=VMEM_LIMIT)


def _rms(xf, g):
    return xf * lax.rsqrt(jnp.mean(xf * xf, axis=-1, keepdims=True) + EPS) * g


def _rope(x, cos, sa, sb, half):
    return x * cos + pltpu.roll(x, LANE - half, 1) * sa + pltpu.roll(x, half, 1) * sb


def _halo_specs(tm, s, d):
    per = tm // SUBLANES
    last = s // SUBLANES - 1
    return (pl.BlockSpec((1, SUBLANES, d), lambda bi, i: (bi, jnp.maximum(i * per - 1, 0), 0)),
            pl.BlockSpec((1, SUBLANES, d), lambda bi, i: (bi, jnp.minimum((i + 1) * per, last), 0)))


def _halo_rows(p, tm, i, nt):
    prev_row = jnp.where(i > 0, p[tm + SUBLANES - 1:tm + SUBLANES, :], 0.0)
    next_row = jnp.where(i < nt - 1, p[tm + SUBLANES:tm + SUBLANES + 1, :], 0.0)
    return prev_row, next_row


def _dwconv3(p, prev_row, next_row, cw_ref, cb_ref):
    tm = p.shape[0]
    rows = lax.broadcasted_iota(jnp.int32, (tm, 1), 0)
    up = jnp.where(rows == 0, prev_row, pltpu.roll(p, 1, 0))
    dn = jnp.where(rows == tm - 1, next_row, pltpu.roll(p, tm - 1, 0))
    return up * cw_ref[0:1, :] + p * cw_ref[1:2, :] + dn * cw_ref[2:3, :] + cb_ref[...]


def _mod_kernel(c_ref, w_ref, b_ref, o_ref):
    c = c_ref[...]
    a = c * jax.nn.sigmoid(c)
    a_hi = a.astype(BF16)
    a_lo = (a - a_hi.astype(F32)).astype(BF16)
    w = w_ref[0]
    w_hi = w.astype(BF16)
    w_lo = (w - w_hi.astype(F32)).astype(BF16)
    acc = jnp.dot(a_hi, w_hi, preferred_element_type=F32)
    acc += jnp.dot(a_lo, w_hi, preferred_element_type=F32)
    acc += jnp.dot(a_hi, w_lo, preferred_element_type=F32)
    o_ref[0] = acc + b_ref[0]


def _modulation(cvec, w_mod, b_mod):
    depth, d, n = w_mod.shape
    rows = cvec.shape[0]
    tn = n // 4 if n % (4 * LANE) == 0 else n
    return pl.pallas_call(
        _mod_kernel,
        grid=(depth, n // tn),
        in_specs=[pl.BlockSpec((rows, d), lambda l, j: (0, 0)),
                  pl.BlockSpec((1, d, tn), lambda l, j: (l, 0, j)),
                  pl.BlockSpec((1, 1, tn), lambda l, j: (l, 0, j))],
        out_specs=pl.BlockSpec((1, rows, tn), lambda l, j: (l, 0, j)),
        out_shape=jax.ShapeDtypeStruct((depth, rows, n), F32),
        compiler_params=_cparams(("arbitrary", "arbitrary")),
        name="adaln_mod",
    )(cvec, w_mod, b_mod.reshape(depth, 1, n))


def _in_proj_kernel(x_ref, xp_ref, xn_ref, sh_ref, sc_ref, g_ref, w_ref, wkv_ref, kvg_ref,
                    cm_ref, sma_ref, smb_ref, cs_ref, ssa_ref, ssb_ref, cw_ref, cb_ref,
                    q_ref, kk_ref, vv_ref, sq_ref, sk_ref, sv_ref, z_ref, x1_ref, x2_ref, gt_ref):
    i = pl.program_id(1)
    nt = pl.num_programs(1)
    g = g_ref[...]
    sh = sh_ref[0]
    sc = sc_ref[0]

    def mod(xf):
        return (_rms(xf, g) * (1.0 + sc) + sh).astype(BF16)

    tm = x_ref.shape[1]
    hb_ext = mod(jnp.concatenate([x_ref[0], xp_ref[0], xn_ref[0]], axis=0))
    hb = hb_ext[:tm]

    def proj(lo, hi, lhs=hb):
        return jnp.dot(lhs, w_ref[:, lo:hi], preferred_element_type=F32)

    cm, sma, smb = cm_ref[...], sma_ref[...], smb_ref[...]
    cs, ssa, ssb = cs_ref[...], ssa_ref[...], ssb_ref[...]

    pq = proj(C_Q, C_CKV)
    for h in range(MLA_HEADS):
        xh = pq[:, HEAD_PAD * h:HEAD_PAD * (h + 1)]
        q_ref[0, :, HEAD_PAD * h:HEAD_PAD * (h + 1)] = (
            _rope(xh, cm, sma, smb, MLA_ROPE // 2) * (MLA_SCALE * LOG2E)).astype(BF16)

    ckv = proj(C_CKV, C_KR)
    cn = _rms(ckv, kvg_ref[...]).astype(BF16)
    kv = jnp.dot(cn, wkv_ref[...], preferred_element_type=F32)
    krr = _rope(proj(C_KR, C_SQ), cm, sma, smb, MLA_ROPE // 2)
    for h in range(MLA_HEADS):
        kk_ref[0, :, HEAD_PAD * h:HEAD_PAD * (h + 1)] = (
            kv[:, HEAD_PAD * h:HEAD_PAD * (h + 1)] + krr).astype(BF16)
    ones = jnp.ones((hb.shape[0], LANE), BF16)
    for j in range(MLA_HEADS // 2):
        v0 = MLA_HEADS * HEAD_PAD + LANE * j
        vv_ref[0, :, 2 * LANE * j:2 * LANE * j + LANE] = kv[:, v0:v0 + LANE].astype(BF16)
        vv_ref[0, :, 2 * LANE * j + LANE:2 * LANE * (j + 1)] = ones

    psq = proj(C_SQ, C_SK)
    for j in range(SWA_HEADS // 2):
        xh = psq[:, LANE * j:LANE * (j + 1)]
        sq_ref[0, :, LANE * j:LANE * (j + 1)] = (
            _rope(xh, cs, ssa, ssb, SWA_HEAD_DIM // 2) * (SWA_SCALE * LOG2E)).astype(BF16)
    psk = proj(C_SK, C_SV)
    for k in range(SWA_KV_HEADS):
        sk_ref[0, :, LANE * k:LANE * (k + 1)] = _rope(
            psk[:, LANE * k:LANE * (k + 1)], cs, ssa, ssb, SWA_HEAD_DIM // 2).astype(BF16)
    sv_ref[0] = proj(C_SV, C_HY).astype(BF16)

    ph = proj(C_HY, C_GT, hb_ext)
    pprev, pnext = _halo_rows(ph, tm, i, nt)
    u = _dwconv3(ph[:tm], pprev, pnext, cw_ref, cb_ref)
    z_ref[0] = u[:, :HY_WIDTH]
    x1_ref[0] = u[:, HY_WIDTH:2 * HY_WIDTH]
    x2_ref[0] = u[:, 2 * HY_WIDTH:]

    gt_ref[0] = jax.nn.sigmoid(proj(C_GT, w_ref.shape[1])).astype(BF16)


def _in_proj(x, sh, sc, g, w, wkv, kvg, rope_m, rope_s, cw, cb):
    b, s, d = x.shape
    tm = min(IN_PROJ_ROWS, s)
    nt = s // tm
    row = lambda width: pl.BlockSpec((1, tm, width), lambda bi, i: (bi, i, 0))
    const2 = lambda a: pl.BlockSpec(a.shape, lambda bi, i: (0, 0), pipeline_mode=pl.Buffered(1))
    tab = pl.BlockSpec((tm, LANE), lambda bi, i: (i, 0))
    in_specs = [
        row(d),
        *_halo_specs(tm, s, d),
        pl.BlockSpec((1, 1, d), lambda bi, i: (bi, 0, 0)),
        pl.BlockSpec((1, 1, d), lambda bi, i: (bi, 0, 0)),
        const2(g), const2(w), const2(wkv), const2(kvg),
        tab, tab, tab, tab, tab, tab,
        const2(cw), const2(cb),
    ]
    widths = [(MLA_HEADS * HEAD_PAD, BF16), (MLA_HEADS * HEAD_PAD, BF16), (MLA_HEADS * LANE, BF16),
              (SWA_HEADS * SWA_HEAD_DIM, BF16), (2 * LANE, BF16), (2 * LANE, BF16),
              (HY_WIDTH, F32), (HY_WIDTH, F32), (HY_WIDTH, F32), (N_BRANCH * d, BF16)]
    return pl.pallas_call(
        _in_proj_kernel,
        grid=(b, nt),
        in_specs=in_specs,
        out_specs=[row(wd) for wd, _ in widths],
        out_shape=[jax.ShapeDtypeStruct((b, s, wd), dt) for wd, dt in widths],
        compiler_params=_cparams(("arbitrary", "arbitrary")),
        name="in_proj",
    )(x, x, x, sh, sc, g, w, wkv, kvg, *rope_m, *rope_s, cw, cb)


def _mla_kernel(*refs, tk, n_lat, rb):
    if n_lat:
        q_ref, kc_ref, vc_ref, kl_ref, vl_ref, o_ref, m_scr, acc_scr, p_scr = refs
    else:
        q_ref, kc_ref, vc_ref, o_ref, m_scr, acc_scr = refs
    tq = q_ref.shape[1]
    nt = (((1,), (1,)), ((), ()))
    blocks = [(e, pl.ds(r * rb, rb)) for e in range(2) for r in range(tq // rb)]

    def scores(e, rows, k_ref, off, n):
        return lax.dot_general(q_ref[0, rows, HEAD_PAD * e:HEAD_PAD * (e + 1)],
                               k_ref[0, pl.ds(off, n), HEAD_PAD * e:HEAD_PAD * (e + 1)], nt,
                               preferred_element_type=F32)

    n_ctx = kc_ref.shape[1]
    for e, rows in blocks:
        s = scores(e, rows, kc_ref, 0, n_ctx)
        m_new = jnp.tile(jnp.max(s, axis=1, keepdims=True), (1, LANE))
        p = jnp.exp2(s - jnp.tile(m_new, (1, n_ctx // LANE))).astype(BF16)
        acc_scr[e, rows, :] = jnp.dot(p, vc_ref[0], preferred_element_type=F32)
        m_scr[e, rows, :] = m_new

    if n_lat:
        def softmax_chunk(off, pv_off):
            for e, rows in blocks:
                s = scores(e, rows, kl_ref, off, tk)
                m_old = m_scr[e, rows, :]
                m_new = jnp.maximum(m_old, jnp.max(s, axis=1, keepdims=True))
                alpha = jnp.tile(jnp.exp2(m_old - m_new), (1, 2))
                acc = acc_scr[e, rows, :]
                if pv_off is not None:
                    acc = acc + jnp.dot(p_scr[e, rows, :], vl_ref[0, pl.ds(pv_off, tk), :],
                                        preferred_element_type=F32)
                acc_scr[e, rows, :] = alpha * acc
                p_scr[e, rows, :] = jnp.exp2(s - jnp.tile(m_new, (1, tk // LANE))).astype(BF16)
                m_scr[e, rows, :] = m_new

        softmax_chunk(0, None)

        def body(j, carry):
            off = pl.multiple_of(j * tk, tk)
            softmax_chunk(off, pl.multiple_of(off - tk, tk))
            return carry
        lax.fori_loop(1, n_lat, body, 0)
        last = (n_lat - 1) * tk
        for e, rows in blocks:
            acc_scr[e, rows, :] += jnp.dot(p_scr[e, rows, :], vl_ref[0, pl.ds(last, tk), :],
                                           preferred_element_type=F32)
    lane = lax.broadcasted_iota(jnp.int32, (tq, LANE), 1)
    o0 = acc_scr[0, :, :LANE] / acc_scr[0, :, LANE:]
    o1 = acc_scr[1, :, :LANE] / acc_scr[1, :, LANE:]
    o_ref[0] = jnp.where(lane < MLA_V, o0, o1).astype(o_ref.dtype)


def _mla(q, kc, vc, kl=None, vl=None, tq=4096, tk=512, rb=128):
    b, sq, _ = q.shape
    c = kc.shape[1]
    tq = min(tq, sq)
    rb = min(rb, tq)
    hp = MLA_HEADS // 2
    in_specs = [pl.BlockSpec((1, tq, 2 * HEAD_PAD), lambda bi, h, i: (bi, i, h)),
                pl.BlockSpec((1, c, 2 * HEAD_PAD), lambda bi, h, i: (bi, 0, h)),
                pl.BlockSpec((1, c, 2 * LANE), lambda bi, h, i: (bi, 0, h))]
    args = [q, kc, vc]
    n_lat = 0
    if kl is not None:
        s = kl.shape[1]
        tk = min(tk, s)
        n_lat = s // tk
        in_specs += [pl.BlockSpec((1, s, 2 * HEAD_PAD), lambda bi, h, i: (bi, 0, h)),
                     pl.BlockSpec((1, s, 2 * LANE), lambda bi, h, i: (bi, 0, h))]
        args += [kl, vl]
    return pl.pallas_call(
        functools.partial(_mla_kernel, tk=tk, n_lat=n_lat, rb=rb),
        grid=(b, hp, sq // tq),
        in_specs=in_specs,
        out_specs=pl.BlockSpec((1, tq, LANE), lambda bi, h, i: (bi, i, h)),
        out_shape=jax.ShapeDtypeStruct((b, sq, MLA_HEADS * MLA_V), BF16),
        scratch_shapes=[pltpu.VMEM((2, tq, LANE), F32), pltpu.VMEM((2, tq, 2 * LANE), F32)]
        + ([pltpu.VMEM((2, tq, tk), BF16)] if n_lat else []),
        compiler_params=_cparams(("arbitrary", "arbitrary", "arbitrary")),
        name="mla_attn",
    )(*args)


def _swa_kernel(*refs, band):
    if band:
        sink_ref, q_ref, kc_ref, vc_ref, kp_ref, k0_ref, kn_ref, vp_ref, v0_ref, vn_ref, o_ref = refs
        kband = jnp.concatenate([kp_ref[0], k0_ref[0], kn_ref[0]], axis=0)
        vband = jnp.concatenate([vp_ref[0], v0_ref[0], vn_ref[0]], axis=0)
    else:
        sink_ref, q_ref, kc_ref, vc_ref, o_ref = refs
    i = pl.program_id(1)
    nb = pl.num_programs(1)
    tq = q_ref.shape[1]
    nsub = tq // SWA_BLOCK
    n_ctx = kc_ref.shape[1]
    n_keys = n_ctx + (3 * SWA_BLOCK if band else 0)
    grp = SWA_HEADS // SWA_KV_HEADS
    rows_all = grp * SWA_BLOCK
    nt = (((1,), (1,)), ((), ()))
    lane = lax.broadcasted_iota(jnp.int32, (SWA_BLOCK, LANE), 1)
    low = lane < SWA_HEAD_DIM
    row = lax.broadcasted_iota(jnp.int32, (rows_all, 1), 0)
    if band:
        r_loc = lax.broadcasted_iota(jnp.int32, (rows_all, n_keys), 0) % SWA_BLOCK
        col = lax.broadcasted_iota(jnp.int32, (rows_all, n_keys), 1) - n_ctx
        prev_bias = jnp.where((col >= 0) & (col < SWA_BLOCK), NEG_INF, 0.0)
        next_bias = jnp.where(col >= 2 * SWA_BLOCK, NEG_INF, 0.0)
        base_bias = (jnp.where(r_loc > col, prev_bias, 0.0)
                     + jnp.where(col - 2 * SWA_BLOCK > r_loc, next_bias, 0.0))
    for r in range(nsub):
        rs = slice(r * SWA_BLOCK, (r + 1) * SWA_BLOCK)
        if band:
            bias = base_bias
            if r == 0:
                bias = jnp.minimum(bias, jnp.where(i == 0, prev_bias, 0.0))
            if r == nsub - 1:
                bias = jnp.minimum(bias, jnp.where(i == nb - 1, next_bias, 0.0))
            bs = slice(r * SWA_BLOCK, (r + 3) * SWA_BLOCK)
        for kvh in range(SWA_KV_HEADS):
            sl = slice(LANE * kvh, LANE * (kvh + 1))
            k, v = kc_ref[0, :, sl], vc_ref[0, :, sl]
            if band:
                k = jnp.concatenate([k, kband[bs, sl]], axis=0)
                v = jnp.concatenate([v, vband[bs, sl]], axis=0)
            v = jnp.concatenate([v, jnp.ones_like(v)], axis=1)
            parts, sink = [], None
            for t in range(grp // 2):
                pair = kvh * (grp // 2) + t
                qp = q_ref[0, rs, LANE * pair:LANE * (pair + 1)]
                parts += [jnp.where(low, qp, jnp.zeros_like(qp)), jnp.where(low, jnp.zeros_like(qp), qp)]
            for h in range(grp):
                sh = sink_ref[kvh * grp + h] * LOG2E
                sink = sh if sink is None else jnp.where(row < h * SWA_BLOCK, sink, sh)
            s = lax.dot_general(jnp.concatenate(parts, axis=0), k, nt, preferred_element_type=F32)
            if band:
                s = s + bias
            m = jnp.maximum(jnp.max(s, axis=1, keepdims=True), sink)
            p = jnp.exp2(s - m).astype(BF16)
            acc = jnp.dot(p, v, preferred_element_type=F32)
            o = acc[:, :LANE] / (acc[:, LANE:] + jnp.exp2(sink - m))
            for t in range(grp // 2):
                pair = kvh * (grp // 2) + t
                lo = o[(2 * t) * SWA_BLOCK:(2 * t + 1) * SWA_BLOCK]
                hi = o[(2 * t + 1) * SWA_BLOCK:(2 * t + 2) * SWA_BLOCK]
                o_ref[0, rs, LANE * pair:LANE * (pair + 1)] = jnp.where(low, lo, hi).astype(o_ref.dtype)


def _swa(sink, q, kc, vc, k=None, v=None, tq=1024):
    b, sq, _ = q.shape
    c = kc.shape[1]
    tq = min(tq, sq)
    nb = sq // tq
    band = k is not None
    blk = lambda rows, width, f: pl.BlockSpec((1, rows, width), f)
    in_specs = [pl.BlockSpec(memory_space=pltpu.SMEM),
                blk(tq, SWA_HEADS * SWA_HEAD_DIM, lambda bi, i: (bi, i, 0)),
                pl.BlockSpec((1, c, 2 * LANE), lambda bi, i: (bi, 0, 0)),
                pl.BlockSpec((1, c, 2 * LANE), lambda bi, i: (bi, 0, 0))]
    args = [sink, q, kc, vc]
    if band:
        per = tq // SWA_BLOCK
        prev = lambda bi, i: (bi, jnp.maximum(i * per - 1, 0), 0)
        cur = lambda bi, i: (bi, i, 0)
        nxt = lambda bi, i: (bi, jnp.minimum((i + 1) * per, sq // SWA_BLOCK - 1), 0)
        in_specs += [blk(SWA_BLOCK, 2 * LANE, prev), blk(tq, 2 * LANE, cur), blk(SWA_BLOCK, 2 * LANE, nxt),
                     blk(SWA_BLOCK, 2 * LANE, prev), blk(tq, 2 * LANE, cur), blk(SWA_BLOCK, 2 * LANE, nxt)]
        args += [k, k, k, v, v, v]
    return pl.pallas_call(
        functools.partial(_swa_kernel, band=band),
        grid=(b, nb),
        in_specs=in_specs,
        out_specs=blk(tq, SWA_HEADS * SWA_HEAD_DIM, lambda bi, i: (bi, i, 0)),
        out_shape=jax.ShapeDtypeStruct((b, sq, SWA_HEADS * SWA_HEAD_DIM), BF16),
        compiler_params=_cparams(("arbitrary", "arbitrary")),
        name="swa_attn",
    )(*args)


def _merge_kernel(x_ref, ya_ref, yb_ref, yc_ref, gt_ref, wb_ref, wo_ref, ng_ref, g1_ref, o_ref):
    d = x_ref.shape[2]
    merged = None
    for k, y_ref in enumerate((ya_ref, yb_ref, yc_ref)):
        t = jnp.dot(y_ref[0].astype(BF16), wb_ref[k], preferred_element_type=F32)
        t = gt_ref[0, :, d * k:d * (k + 1)].astype(F32) * t
        merged = t if merged is None else merged + t
    y = jnp.dot(merged.astype(BF16), wo_ref[...], preferred_element_type=F32)
    o_ref[0] = x_ref[0] + g1_ref[0] * _rms(y, ng_ref[...])


def _merge(x, ya, yb, yc, gt, wb, wo, ng, g1):
    b, s, d = x.shape
    tm = min(MERGE_ROWS, s)
    row = lambda width: pl.BlockSpec((1, tm, width), lambda bi, i: (bi, i, 0))
    return pl.pallas_call(
        _merge_kernel,
        grid=(b, s // tm),
        in_specs=[row(d), row(BRANCH_WIDTH), row(BRANCH_WIDTH), row(BRANCH_WIDTH), row(N_BRANCH * d),
                  pl.BlockSpec(wb.shape, lambda bi, i: (0, 0, 0)),
                  pl.BlockSpec(wo.shape, lambda bi, i: (0, 0)),
                  pl.BlockSpec(ng.shape, lambda bi, i: (0, 0)),
                  pl.BlockSpec((1, 1, d), lambda bi, i: (bi, 0, 0))],
        out_specs=row(d),
        out_shape=jax.ShapeDtypeStruct((b, s, d), F32),
        compiler_params=_cparams(("arbitrary", "arbitrary")),
        name="merge",
    )(x, ya, yb, yc, gt, wb, wo, ng, g1)


def _ffn_kernel(x_ref, xp_ref, xn_ref, sh_ref, sc_ref, g2_ref, ng_in_ref, ng_out_ref,
                wu_ref, cw_ref, cb_ref, wd_ref, o_ref, *, chunk):
    i = pl.program_id(1)
    nt = pl.num_programs(1)
    g = ng_in_ref[...]
    sh = sh_ref[0]
    sc = sc_ref[0]

    def mod(xf):
        return (_rms(xf, g) * (1.0 + sc) + sh).astype(BF16)

    x = x_ref[0]
    tm = x.shape[0]
    hb = mod(jnp.concatenate([x, xp_ref[0], xn_ref[0]], axis=0))
    dff = wd_ref.shape[0]
    f = None
    for c0 in range(0, dff, chunk):
        halves = []
        for off in (c0, dff + c0):
            p = jnp.dot(hb, wu_ref[:, off:off + chunk], preferred_element_type=F32)
            pprev, pnext = _halo_rows(p, tm, i, nt)
            halves.append(_dwconv3(p[:tm], pprev, pnext, cw_ref.at[:, off:off + chunk],
                                   cb_ref.at[:, off:off + chunk]))
        a, bb = halves
        act = (a * jax.nn.sigmoid(a) * bb).astype(BF16)
        t = jnp.dot(act, wd_ref[c0:c0 + chunk, :], preferred_element_type=F32)
        f = t if f is None else f + t
    o_ref[0] = x + g2_ref[0] * _rms(f, ng_out_ref[...])


def _ffn(x, sh, sc, g2, ng_in, ng_out, wu, cw, cb, wd):
    b, s, d = x.shape
    tm = min(FFN_ROWS, s)
    dff = wd.shape[0]
    chunk = dff // FFN_CHUNKS if dff % (FFN_CHUNKS * LANE) == 0 else dff
    row = pl.BlockSpec((1, tm, d), lambda bi, i: (bi, i, 0))
    vec = pl.BlockSpec((1, 1, d), lambda bi, i: (bi, 0, 0))
    const2 = lambda a: pl.BlockSpec(a.shape, lambda bi, i: (0, 0), pipeline_mode=pl.Buffered(1))
    return pl.pallas_call(
        functools.partial(_ffn_kernel, chunk=chunk),
        grid=(b, s // tm),
        in_specs=[row, *_halo_specs(tm, s, d),
                  vec, vec, vec, const2(ng_in), const2(ng_out),
                  const2(wu), const2(cw), const2(cb), const2(wd)],
        out_specs=row,
        out_shape=jax.ShapeDtypeStruct((b, s, d), F32),
        compiler_params=_cparams(("arbitrary", "arbitrary")),
        name="conv_ffn",
    )(x, x, x, sh, sc, g2, ng_in, ng_out, wu, cw, cb, wd)


HY_PASSES = 1
SPEC_PASSES = 1
HY_UNROLL = 2
HY_UNROLL2 = 6


def _split(x):
    hi = x.astype(BF16)
    return hi, (x - hi.astype(F32)).astype(BF16)


def _cdot(c_hi, c_lo, x, passes=HY_PASSES):
    x_hi, x_lo = _split(x)
    acc = jnp.dot(c_hi, x_hi, preferred_element_type=F32)
    if passes > 1:
        acc = acc + jnp.dot(c_lo, x_hi, preferred_element_type=F32)
        acc = acc + jnp.dot(c_hi, x_lo, preferred_element_type=F32)
    return acc


def _np_split(m):
    m = jnp.asarray(m, F32)
    hi = m.astype(BF16)
    return hi, (m - hi.astype(F32)).astype(BF16)


def _dft_consts(n_blocks, nonzero_blocks):
    nb = n_blocks
    n = LANE * nb
    dh = nb // 2 + 1
    dhp = -(-dh // 8) * 8
    d = np.arange(dh)[:, None]
    b = np.arange(nonzero_blocks)[None, :]
    ang = 2 * np.pi * ((d * b) % nb) / nb
    f1 = np.zeros((2 * dhp, nonzero_blocks))
    f1[:dh] = np.cos(ang)
    f1[dhp:dhp + dh] = -np.sin(ang)
    a = np.arange(LANE)
    ang2 = 2 * np.pi * ((a[:, None] * a[None, :]) % LANE) / LANE
    ar, ai = np.cos(ang2), -np.sin(ang2)
    f2 = np.block([[ar, -ai], [ai, ar]])
    f2inv = np.block([[ar, ai], [-ai, ar]])
    bo = np.arange(nb // 2)[:, None]
    do = np.arange(dh)[None, :]
    wd = np.where((do == 0) | (do == nb // 2), 1.0, 2.0) / n
    ang3 = 2 * np.pi * ((bo * do) % nb) / nb
    f3 = np.zeros((nb // 2, 2 * dhp))
    f3[:, :dh] = wd * np.cos(ang3)
    f3[:, dhp:dhp + dh] = -wd * np.sin(ang3)
    step_ang = 2 * np.pi * a / n
    tw_step = np.stack([np.broadcast_to(np.cos(step_ang)[:, None], (LANE, LANE)),
                        np.broadcast_to(-np.sin(step_ang)[:, None], (LANE, LANE))])
    eye = np.eye(SUBLANES)
    return dict(dh=dh, dhp=dhp, f1=_np_split(np.kron(f1, eye)), f2=_np_split(f2), f2inv=_np_split(f2inv),
                f3=_np_split(np.kron(f3, eye)), tw_step=jnp.asarray(tw_step, F32))


def _stage1(x_ref, f1_hi, f1_lo, g_scr, passes):
    n_in, n_out = x_ref.shape[0], g_scr.shape[0]

    def body(t, carry):
        r0 = pl.ds(pl.multiple_of(2 * t * SUBLANES, SUBLANES), SUBLANES)
        r1 = pl.ds(pl.multiple_of((2 * t + 1) * SUBLANES, SUBLANES), SUBLANES)
        xa = jnp.concatenate([x_ref[:, r0, :].reshape(n_in * SUBLANES, LANE),
                              x_ref[:, r1, :].reshape(n_in * SUBLANES, LANE)], axis=1)
        g = _cdot(f1_hi, f1_lo, xa, passes)
        g_scr[:, r0, :] = g[:, :LANE].reshape(n_out, SUBLANES, LANE)
        g_scr[:, r1, :] = g[:, LANE:].reshape(n_out, SUBLANES, LANE)
        return carry
    lax.fori_loop(0, LANE // (2 * SUBLANES), body, 0, unroll=HY_UNROLL)


def _twiddle_init(tw_scr):
    tw_scr[0] = jnp.ones((LANE, LANE), F32)
    tw_scr[1] = jnp.zeros((LANE, LANE), F32)


def _twiddle_next(twr, twi, step_ref):
    sr, si = step_ref[0], step_ref[1]
    return twr * sr - twi * si, twr * si + twi * sr


def _twiddle_pair(tw_scr, step_ref):
    t0r, t0i = tw_scr[0], tw_scr[1]
    t1r, t1i = _twiddle_next(t0r, t0i, step_ref)
    t2r, t2i = _twiddle_next(t1r, t1i, step_ref)
    tw_scr[0] = t2r
    tw_scr[1] = t2i
    return jnp.concatenate([t0r, t1r], axis=1), jnp.concatenate([t0i, t1i], axis=1)


def _spectrum_kernel(k_ref, inv_ref, f1h_ref, f1l_ref, f2h_ref, f2l_ref, step_ref, hf_ref, g_scr, tw_scr,
                     *, dh, dhp):
    _stage1(k_ref, f1h_ref[...], f1l_ref[...], g_scr, SPEC_PASSES)
    _twiddle_init(tw_scr)
    inv = inv_ref[...]

    def body(h, carry):
        d = 2 * h
        gr = jnp.concatenate([g_scr[d], g_scr[d + 1]], axis=1)
        gi = jnp.concatenate([g_scr[dhp + d], g_scr[dhp + d + 1]], axis=1)
        twr, twi = _twiddle_pair(tw_scr, step_ref)
        t = jnp.concatenate([gr * twr - gi * twi, gr * twi + gi * twr], axis=0)
        y = _cdot(f2h_ref[...], f2l_ref[...], t, passes=SPEC_PASSES)
        hf_ref[0, d] = y[:LANE, :LANE] * inv
        hf_ref[0, d + 1] = y[:LANE, LANE:] * inv
        hf_ref[1, d] = y[LANE:, :LANE] * inv
        hf_ref[1, d + 1] = y[LANE:, LANE:] * inv
        return carry
    lax.fori_loop(0, dhp // 2, body, 0, unroll=HY_UNROLL2)


def _hyena_spectrum(kern, inv_norm):
    tiles, n, _ = kern.shape
    c = tiles * LANE
    nb = n // LANE
    cst = _dft_consts(nb, nb)
    dh, dhp = cst["dh"], cst["dhp"]
    const = lambda a: pl.BlockSpec(a.shape, lambda j: (0,) * a.ndim, pipeline_mode=pl.Buffered(1))
    consts = [*cst["f1"], *cst["f2"], cst["tw_step"]]
    return pl.pallas_call(
        functools.partial(_spectrum_kernel, dh=dh, dhp=dhp),
        grid=(c // LANE,),
        in_specs=[pl.BlockSpec((None, nb, LANE, LANE), lambda j: (j, 0, 0, 0)),
                  pl.BlockSpec((1, LANE), lambda j: (0, j))]
        + [const(a) for a in consts],
        out_specs=pl.BlockSpec((None, 2, dhp, LANE, LANE), lambda j: (j, 0, 0, 0, 0)),
        out_shape=jax.ShapeDtypeStruct((c // LANE, 2, dhp, LANE, LANE), F32),
        scratch_shapes=[pltpu.VMEM((2 * dhp, LANE, LANE), F32), pltpu.VMEM((2, LANE, LANE), F32)],
        compiler_params=_cparams(("arbitrary",)),
        name="hyena_spectrum",
    )(kern.reshape(tiles, nb, LANE, LANE), inv_norm, *consts)


def _conv_kernel(z_ref, gate_ref, skip_ref, hf_ref, f1h_ref, f1l_ref, f2h_ref, f2l_ref, f2ih_ref, f2il_ref,
                 f3h_ref, f3l_ref, step_ref, o_ref, g_scr, tw_scr, *, dh, dhp):
    nbh = z_ref.shape[1]
    _stage1(z_ref.at[0], f1h_ref[...], f1l_ref[...], g_scr, HY_PASSES)
    _twiddle_init(tw_scr)

    def body(h, carry):
        d = 2 * h
        gr = jnp.concatenate([g_scr[d], g_scr[d + 1]], axis=1)
        gi = jnp.concatenate([g_scr[dhp + d], g_scr[dhp + d + 1]], axis=1)
        twr, twi = _twiddle_pair(tw_scr, step_ref)
        t = jnp.concatenate([gr * twr - gi * twi, gr * twi + gi * twr], axis=0)
        y = _cdot(f2h_ref[...], f2l_ref[...], t)
        yr, yi = y[:LANE], y[LANE:]
        hr = jnp.concatenate([hf_ref[0, d], hf_ref[0, d + 1]], axis=1)
        hi = jnp.concatenate([hf_ref[1, d], hf_ref[1, d + 1]], axis=1)
        zz = jnp.concatenate([yr * hr - yi * hi, yr * hi + yi * hr], axis=0)
        u = _cdot(f2ih_ref[...], f2il_ref[...], zz)
        ur, ui = u[:LANE], u[LANE:]
        vr = ur * twr + ui * twi
        vi = ui * twr - ur * twi
        g_scr[d] = vr[:, :LANE]
        g_scr[d + 1] = vr[:, LANE:]
        g_scr[dhp + d] = vi[:, :LANE]
        g_scr[dhp + d + 1] = vi[:, LANE:]
        return carry
    lax.fori_loop(0, dhp // 2, body, 0, unroll=HY_UNROLL2)

    skip = skip_ref[...]
    f3h, f3l = f3h_ref[...], f3l_ref[...]

    def out_body(t, carry):
        r0 = pl.ds(pl.multiple_of(2 * t * SUBLANES, SUBLANES), SUBLANES)
        r1 = pl.ds(pl.multiple_of((2 * t + 1) * SUBLANES, SUBLANES), SUBLANES)
        ga = jnp.concatenate([g_scr[:, r0, :].reshape(2 * dhp * SUBLANES, LANE),
                              g_scr[:, r1, :].reshape(2 * dhp * SUBLANES, LANE)], axis=1)
        conv = _cdot(f3h, f3l, ga)
        for rows, c in ((r0, conv[:, :LANE]), (r1, conv[:, LANE:])):
            o_ref[0, :, rows, :] = gate_ref[0, :, rows, :] * (
                c.reshape(nbh, SUBLANES, LANE) + z_ref[0, :, rows, :] * skip)
        return carry
    lax.fori_loop(0, LANE // (2 * SUBLANES), out_body, 0, unroll=HY_UNROLL)


def _hyena_conv(z, gate, skip, hf, tile0, cst):
    b, l, w = z.shape
    nbh = l // LANE
    dh, dhp = cst["dh"], cst["dhp"]
    consts = [*cst["f1"], *cst["f2"], *cst["f2inv"], *cst["f3"], cst["tw_step"]]
    const = lambda a: pl.BlockSpec(a.shape, lambda j, bi: (0,) * a.ndim, pipeline_mode=pl.Buffered(1))
    tile = pl.BlockSpec((1, nbh, LANE, LANE), lambda j, bi: (bi, 0, 0, j))
    z, gate = z.reshape(b, nbh, LANE, w), gate.reshape(b, nbh, LANE, w)
    return pl.pallas_call(
        functools.partial(_conv_kernel, dh=dh, dhp=dhp),
        grid=(w // LANE, b),
        in_specs=[tile, tile, pl.BlockSpec((1, LANE), lambda j, bi: (0, j)),
                  pl.BlockSpec((None, 2, dhp, LANE, LANE), lambda j, bi: (j + tile0, 0, 0, 0, 0),
                               pipeline_mode=pl.Buffered(1))]
        + [const(a) for a in consts],
        out_specs=tile,
        out_shape=jax.ShapeDtypeStruct((b, nbh, LANE, w), F32),
        scratch_shapes=[pltpu.VMEM((2 * dhp, LANE, LANE), F32), pltpu.VMEM((2, LANE, LANE), F32)],
        compiler_params=_cparams(("arbitrary", "arbitrary")),
        name="hyena_conv",
    )(z, gate, skip, hf, *consts).reshape(b, l, w)


def _filter_mlp_kernel(ft_ref, w1_ref, b1_ref, w2_ref, b2_ref, fr_ref, w3_ref, dl_ref, k_ref, s_ref):
    i = pl.program_id(0)
    ft = ft_ref[...]

    def mm(x, w):
        x_hi, x_lo = _split(x)
        w_hi, w_lo = _split(w)
        return (jnp.dot(x_hi, w_hi, preferred_element_type=F32) + jnp.dot(x_lo, w_hi, preferred_element_type=F32)
                + jnp.dot(x_hi, w_lo, preferred_element_type=F32))

    half = ft.shape[0] // 2
    cw = dl_ref.shape[1]
    hid = jnp.sin(fr_ref[0:1, :] * (mm(jnp.concatenate([ft[:half], ft[half:]], axis=1), w1_ref[...]) + b1_ref[...]))
    hid = jnp.sin(fr_ref[1:2, :] * (mm(hid, w2_ref[...]) + b2_ref[...]))
    h2 = mm(hid, w3_ref[0])
    k = jnp.concatenate([h2[:, :cw], h2[:, cw:]], axis=0)
    k = k * jnp.exp(-ft[:, 0:1] * dl_ref[...]) * ft[:, LANE - 1:LANE]
    for j in range(k_ref.shape[0]):
        k_ref[j] = k[:, LANE * j:LANE * (j + 1)]

    @pl.when(i == 0)
    def _():
        s_ref[...] = jnp.zeros(s_ref.shape, F32)
    s_ref[...] += jnp.sum(jnp.abs(k), axis=0, keepdims=True)


def _hyena_filter_taps(n_tokens, w1, b1, w2, b2, freq, w3):
    L = n_tokens
    n = 2 * L
    idx = jnp.arange(n)
    m = jnp.where(idx < L, idx, n - idx).astype(F32)
    t = m / max(L - 1, 1)
    bands = jnp.linspace(1e-4, HY_BANDS - 1, HY_BANDS, dtype=F32)
    ang = (2.0 * math.pi / L) * m[:, None] * bands
    n_feat = 2 * HY_BANDS + 1
    feats = jnp.concatenate([t[:, None], jnp.cos(ang), -jnp.sin(ang),
                             jnp.zeros((n, LANE - n_feat - 1), F32),
                             (idx != L).astype(F32)[:, None]], axis=-1)
    two = lambda a: jnp.concatenate([a, a], axis=-1)
    bdiag = lambda a: jnp.concatenate([jnp.pad(a, ((0, 0), (0, a.shape[1]))), jnp.pad(a, ((0, 0), (a.shape[1], 0)))], axis=0)
    w1p = bdiag(jnp.pad(w1, ((0, LANE - n_feat), (0, 0))))
    cw = HY_ORDER * HY_WIDTH
    w3d = jnp.moveaxis(w3.reshape(HY_HIDDEN, 2, cw), 1, 0)
    w3d = jnp.stack([bdiag(w3d[0]), bdiag(w3d[1])])
    w2, freq = bdiag(w2), two(freq)
    deltas = jnp.abs(jnp.linspace(math.log(HY_TARGET) / HY_SLOW_PCT, math.log(HY_TARGET) / HY_FAST_PCT,
                                  HY_WIDTH, dtype=F32))
    dl = jnp.tile(deltas, HY_ORDER)[None, :]
    tr = min(1024, L)
    nt = n // tr
    const = lambda a: pl.BlockSpec(a.shape, lambda i: (0,) * a.ndim)
    b1r, b2r = two(b1)[None, :], two(b2)[None, :]
    return pl.pallas_call(
        _filter_mlp_kernel,
        grid=(nt,),
        in_specs=[pl.BlockSpec((tr, LANE), lambda i: (i, 0)), const(w1p), const(b1r), const(w2), const(b2r),
                  const(freq), pl.BlockSpec((1, 2 * HY_HIDDEN, 2 * cw), lambda i: (i // (nt // 2), 0, 0)), const(dl)],
        out_specs=[pl.BlockSpec((cw // LANE, tr, LANE), lambda i: (0, i, 0)), pl.BlockSpec((1, cw), lambda i: (0, 0))],
        out_shape=[jax.ShapeDtypeStruct((cw // LANE, n, LANE), F32), jax.ShapeDtypeStruct((1, cw), F32)],
        compiler_params=_cparams(("arbitrary",)),
        name="hyena_filter_mlp",
    )(feats, w1p, b1r, w2, b2r, freq, w3d, dl)


def _hyena_long(z, x1, x2, taps, norms, skip):
    b, l, w = z.shape
    nb = 2 * l // LANE
    hf = _hyena_spectrum(taps, 1.0 / norms)
    cst = _dft_consts(nb, nb // 2)
    for o, gate in enumerate((x1, x2)):
        z = _hyena_conv(z, gate, skip[o][None, :], hf, o * (w // LANE), cst)
    return z


def _dense_dft_consts(length):
    n = 2 * length
    h = length + 1
    hp = -(-h // 8) * 8
    k = np.arange(h)[:, None]
    pos = np.arange(n)[None, :]
    ang = 2 * np.pi * ((k * pos) % n) / n
    fwd = np.zeros((2 * hp, n))
    fwd[:h] = np.cos(ang)
    fwd[hp:hp + h] = -np.sin(ang)
    wk = np.where((k == 0) | (k == length), 1.0, 2.0) / n
    inv = np.zeros((length, 2 * hp))
    inv[:, :h] = (wk * np.cos(ang[:, :length])).T
    inv[:, hp:hp + h] = (-wk * np.sin(ang[:, :length])).T
    return dict(hp=hp, full=_np_split(fwd), fwd=_np_split(fwd[:, :length]), inv=_np_split(inv))


def _hyena_short_kernel(z_ref, x1_ref, x2_ref, k_ref, inv_ref, skip_ref, ffh_ref, ffl_ref, fh_ref, fl_ref,
                        fih_ref, fil_ref, o_ref, *, hp):
    z = z_ref[0]
    w = z.shape[1]
    for o, gate_ref in enumerate((x1_ref, x2_ref)):
        cols = slice(o * w, (o + 1) * w)
        hf = _cdot(ffh_ref[...], ffl_ref[...], k_ref[:, cols], passes=3) * inv_ref[:, cols]
        x = _cdot(fh_ref[...], fl_ref[...], z, passes=3)
        xr, xi, hr, hi = x[:hp], x[hp:], hf[:hp], hf[hp:]
        zz = jnp.concatenate([xr * hr - xi * hi, xr * hi + xi * hr], axis=0)
        conv = _cdot(fih_ref[...], fil_ref[...], zz, passes=3)
        z = gate_ref[0] * (conv + z * skip_ref[o:o + 1, :])
    o_ref[0] = z


def _hyena_short(z, x1, x2, taps, norms, skip):
    b, l, w = z.shape
    taps = jnp.moveaxis(taps, 0, 1).reshape(2 * l, -1)
    cst = _dense_dft_consts(l)
    consts = [*cst["full"], *cst["fwd"], *cst["inv"]]
    inv_norm = 1.0 / norms
    const = lambda a: pl.BlockSpec(a.shape, lambda bi: (0,) * a.ndim)
    tile = pl.BlockSpec((1, l, w), lambda bi: (bi, 0, 0))
    return pl.pallas_call(
        functools.partial(_hyena_short_kernel, hp=cst["hp"]),
        grid=(b,),
        in_specs=[tile, tile, tile, const(taps), const(inv_norm), const(skip)] + [const(a) for a in consts],
        out_specs=tile,
        out_shape=jax.ShapeDtypeStruct((b, l, w), F32),
        compiler_params=_cparams(("arbitrary",)),
        name="hyena_short",
    )(z, x1, x2, taps, inv_norm, skip, *consts)


def _rope_tables(rows, rot_dim, head_lanes, rope_off, identity_rows):
    half = rot_dim // 2
    n_freq = rot_dim // 4
    pos = np.arange(rows * GRID_W)
    inv_freq = ROPE_THETA ** (-np.arange(n_freq, dtype=np.float64) / n_freq)
    ang = np.concatenate([(pos // GRID_W)[:, None] * inv_freq, (pos % GRID_W)[:, None] * inv_freq], axis=-1)
    cos, sin = np.cos(ang), np.sin(ang)
    n = pos.shape[0]
    ct = np.ones((n, LANE))
    sa = np.zeros((n, LANE))
    sb = np.zeros((n, LANE))
    for h0 in range(0, LANE, head_lanes):
        lo = h0 + rope_off
        ct[:, lo:lo + half] = cos
        ct[:, lo + half:lo + rot_dim] = cos
        sa[:, lo:lo + half] = -sin
        sb[:, lo + half:lo + rot_dim] = sin
    ident = (np.ones((identity_rows, LANE)), np.zeros((identity_rows, LANE)), np.zeros((identity_rows, LANE)))
    lat = tuple(jnp.asarray(t, F32) for t in (ct, sa, sb))
    ctx = tuple(jnp.asarray(t, F32) for t in ident)
    return lat, ctx


def _pack_w_in(w):
    d = w.shape[0]
    o = 0
    mq = w[:, o:o + MLA_HEADS * (MLA_NOPE + MLA_ROPE)]; o += MLA_HEADS * (MLA_NOPE + MLA_ROPE)
    mckv = w[:, o:o + KV_RANK]; o += KV_RANK
    mkr = w[:, o:o + MLA_ROPE]; o += MLA_ROPE
    sq = w[:, o:o + SWA_HEADS * SWA_HEAD_DIM]; o += SWA_HEADS * SWA_HEAD_DIM
    sk = w[:, o:o + SWA_KV_HEADS * SWA_HEAD_DIM]; o += SWA_KV_HEADS * SWA_HEAD_DIM
    sv = w[:, o:o + SWA_KV_HEADS * SWA_HEAD_DIM]; o += SWA_KV_HEADS * SWA_HEAD_DIM
    hy = w[:, o:o + (HY_ORDER + 1) * HY_WIDTH]; o += (HY_ORDER + 1) * HY_WIDTH
    gt = w[:, o:]
    pad_q = HEAD_PAD - MLA_NOPE - MLA_ROPE
    mq = jnp.pad(mq.reshape(d, MLA_HEADS, MLA_NOPE + MLA_ROPE), ((0, 0), (0, 0), (0, pad_q))).reshape(d, -1)
    mkr = jnp.pad(mkr, ((0, 0), (MLA_NOPE, pad_q)))
    dup = lambda t: jnp.repeat(t.reshape(d, SWA_KV_HEADS, 1, SWA_HEAD_DIM), 2, axis=2).reshape(d, -1)
    return jnp.concatenate([mq, mckv, mkr, sq, dup(sk), dup(sv), hy, gt], axis=1).astype(BF16)


def _pack_w_kv(w):
    r = w.shape[0]
    w = w.reshape(r, MLA_HEADS, MLA_NOPE + MLA_V)
    k = jnp.pad(w[..., :MLA_NOPE], ((0, 0), (0, 0), (0, HEAD_PAD - MLA_NOPE))).reshape(r, -1)
    v = w[..., MLA_NOPE:].reshape(r, -1)
    return jnp.concatenate([k, v], axis=1).astype(BF16)


def kernel(x, c, ctx, c_ctx, w_mod, b_mod, norm_g, w_in, kv_norm_g, w_kv_up, swa_sink, hy_conv_w, hy_conv_b,
           hy_w1, hy_b1, hy_w2, hy_b2, hy_freq, hy_w3, hy_skip, w_branch, w_out, w_up, ffn_conv_w, ffn_conv_b,
           w_down):
    b, s, d = x.shape
    n_ctx = ctx.shape[1]
    depth = w_mod.shape[0]
    assert s % (2 * SWA_BLOCK) == 0 and s % GRID_W == 0, "latent length must tile into 128-token blocks / grid rows"
    assert n_ctx % SWA_BLOCK == 0, "context length must be a multiple of the 128-token block"
    assert d % LANE == 0 and w_in.shape[1] == d
    rows = s // GRID_W
    rope_m, rope_m_ctx = _rope_tables(rows, MLA_ROPE, HEAD_PAD, MLA_NOPE, n_ctx)
    rope_s, rope_s_ctx = _rope_tables(rows, SWA_HEAD_DIM, SWA_HEAD_DIM, 0, n_ctx)

    pad_rows = -(b + 1) % SUBLANES
    cvec = jnp.concatenate([c, c_ctx[None, :], jnp.zeros((pad_rows, d), F32)], axis=0)
    mod_all = _modulation(cvec, w_mod, b_mod)

    x_lat, x_ctx = x, ctx
    for l in range(depth):
        with_ctx = l < depth - 1
        m = mod_all[l].reshape(-1, 6, d)
        lat = [m[:b, k][:, None, :] for k in range(6)]
        cx = [jnp.broadcast_to(m[b, k][None, None, :], (b, 1, d)) for k in range(6)]
        ng = [norm_g[l, k][None, :] for k in range(4)]
        w_pack = _pack_w_in(w_in[l])
        wkv_pack = _pack_w_kv(w_kv_up[l])
        kvg = kv_norm_g[l][None, :]
        cw, cb = hy_conv_w[l], hy_conv_b[l][None, :]

        q, kk, vv, sq, sk, sv, z, x1, x2, gt = _in_proj(
            x_lat, lat[0], lat[1], ng[0], w_pack, wkv_pack, kvg, rope_m, rope_s, cw, cb)
        qc, kkc, vvc, sqc, skc, svc, zc, x1c, x2c, gtc = _in_proj(
            x_ctx, cx[0], cx[1], ng[0], w_pack, wkv_pack, kvg, rope_m_ctx, rope_s_ctx, cw, cb)

        y_a = _mla(q, kkc, vvc, kk, vv)
        y_b = _swa(swa_sink[l], sq, skc, svc, sk, sv)
        hy_mlp = (hy_w1[l], hy_b1[l], hy_w2[l], hy_b2[l], hy_freq[l], hy_w3[l])
        y_c = _hyena_long(z, x1, x2, *_hyena_filter_taps(s, *hy_mlp), hy_skip[l])

        wb = w_branch[l].astype(BF16)
        wo = w_out[l].astype(BF16)
        wu = w_up[l].astype(BF16)
        wd = w_down[l].astype(BF16)
        fcw, fcb = ffn_conv_w[l], ffn_conv_b[l][None, :]

        x_lat = _merge(x_lat, y_a, y_b, y_c, gt, wb, wo, ng[1], lat[2])
        x_lat = _ffn(x_lat, lat[3], lat[4], lat[5], ng[2], ng[3], wu, fcw, fcb, wd)

        if with_ctx:
            yc_a = _mla(qc, kkc, vvc)
            yc_b = _swa(swa_sink[l], sqc, skc, svc)
            yc_c = _hyena_short(zc, x1c, x2c, *_hyena_filter_taps(n_ctx, *hy_mlp), hy_skip[l])
            x_ctx = _merge(x_ctx, yc_a, yc_b, yc_c, gtc, wb, wo, ng[1], cx[2])
            x_ctx = _ffn(x_ctx, cx[3], cx[4], cx[5], ng[2], ng[3], wu, fcw, fcb, wd)
    return x_lat
```

```python
import functools
import math

import numpy as np
import jax
import jax.numpy as jnp
from jax import lax
from jax.experimental import pallas as pl
from jax.experimental.pallas import tpu as pltpu

F32 = jnp.float32
BF16 = jnp.bfloat16

GRID_W = 64
EPS = 1e-6
ROPE_THETA = 10000.0
NEG_INF = -1e30
MLA_HEADS = 8
MLA_NOPE = 64
MLA_ROPE = 32
MLA_V = 64
KV_RANK = 256
MLA_SCALE = (MLA_NOPE + MLA_ROPE) ** -0.5
LOG2E = math.log2(math.e)
SWA_HEADS = 8
SWA_KV_HEADS = 2
SWA_HEAD_DIM = 64
SWA_BLOCK = 128
SWA_SCALE = SWA_HEAD_DIM ** -0.5
HY_WIDTH = 512
HY_ORDER = 2
HY_BANDS = 16
HY_HIDDEN = 64
HY_TARGET = 1e-2
HY_FAST_PCT = 0.3
HY_SLOW_PCT = 1.5
N_BRANCH = 3
BRANCH_WIDTH = 512
LANE = 128
SUBLANES = 8
HEAD_PAD = 128

VMEM_LIMIT = 56 * 1024 * 1024
IN_PROJ_ROWS = 512
FFN_ROWS = 512
FFN_CHUNKS = 1
MERGE_ROWS = 1024

C_Q = 0
C_CKV = C_Q + MLA_HEADS * HEAD_PAD
C_KR = C_CKV + KV_RANK
C_SQ = C_KR + HEAD_PAD
C_SK = C_SQ + SWA_HEADS * SWA_HEAD_DIM
C_SV = C_SK + 2 * SWA_KV_HEADS * SWA_HEAD_DIM
C_HY = C_SV + 2 * SWA_KV_HEADS * SWA_HEAD_DIM
C_GT = C_HY + (HY_ORDER + 1) * HY_WIDTH


def _cparams(sem):
    return pltpu.CompilerParams(dimension_semantics=sem, vmem_limit_bytes=VMEM_LIMIT)


def _rms(xf, g):
    return xf * lax.rsqrt(jnp.mean(xf * xf, axis=-1, keepdims=True) + EPS) * g


def _rope(x, cos, sa, sb, half):
    return x * cos + pltpu.roll(x, LANE - half, 1) * sa + pltpu.roll(x, half, 1) * sb


def _halo_specs(tm, s, d):
    per = tm // SUBLANES
    last = s // SUBLANES - 1
    return (pl.BlockSpec((1, SUBLANES, d), lambda bi, i: (bi, jnp.maximum(i * per - 1, 0), 0)),
            pl.BlockSpec((1, SUBLANES, d), lambda bi, i: (bi, jnp.minimum((i + 1) * per, last), 0)))


def _halo_rows(p, tm, i, nt):
    prev_row = jnp.where(i > 0, p[tm + SUBLANES - 1:tm + SUBLANES, :], 0.0)
    next_row = jnp.where(i < nt - 1, p[tm + SUBLANES:tm + SUBLANES + 1, :], 0.0)
    return prev_row, next_row


def _dwconv3(p, prev_row, next_row, cw_ref, cb_ref):
    tm = p.shape[0]
    rows = lax.broadcasted_iota(jnp.int32, (tm, 1), 0)
    up = jnp.where(rows == 0, prev_row, pltpu.roll(p, 1, 0))
    dn = jnp.where(rows == tm - 1, next_row, pltpu.roll(p, tm - 1, 0))
    return up * cw_ref[0:1, :] + p * cw_ref[1:2, :] + dn * cw_ref[2:3, :] + cb_ref[...]


def _mod_kernel(c_ref, w_ref, b_ref, o_ref):
    c = c_ref[...]
    a = c * jax.nn.sigmoid(c)
    a_hi = a.astype(BF16)
    a_lo = (a - a_hi.astype(F32)).astype(BF16)
    w = w_ref[0]
    w_hi = w.astype(BF16)
    w_lo = (w - w_hi.astype(F32)).astype(BF16)
    acc = jnp.dot(a_hi, w_hi, preferred_element_type=F32)
    acc += jnp.dot(a_lo, w_hi, preferred_element_type=F32)
    acc += jnp.dot(a_hi, w_lo, preferred_element_type=F32)
    o_ref[0] = acc + b_ref[0]


def _modulation(cvec, w_mod, b_mod):
    depth, d, n = w_mod.shape
    rows = cvec.shape[0]
    tn = n // 4 if n % (4 * LANE) == 0 else n
    return pl.pallas_call(
        _mod_kernel,
        grid=(depth, n // tn),
        in_specs=[pl.BlockSpec((rows, d), lambda l, j: (0, 0)),
                  pl.BlockSpec((1, d, tn), lambda l, j: (l, 0, j)),
                  pl.BlockSpec((1, 1, tn), lambda l, j: (l, 0, j))],
        out_specs=pl.BlockSpec((1, rows, tn), lambda l, j: (l, 0, j)),
        out_shape=jax.ShapeDtypeStruct((depth, rows, n), F32),
        compiler_params=_cparams(("arbitrary", "arbitrary")),
        name="adaln_mod",
    )(cvec, w_mod, b_mod.reshape(depth, 1, n))


def _in_proj_kernel(x_ref, xp_ref, xn_ref, sh_ref, sc_ref, g_ref, w_ref, wkv_ref, kvg_ref,
                    cm_ref, sma_ref, smb_ref, cs_ref, ssa_ref, ssb_ref, cw_ref, cb_ref,
                    q_ref, kk_ref, vv_ref, sq_ref, sk_ref, sv_ref, z_ref, x1_ref, x2_ref, gt_ref):
    i = pl.program_id(1)
    nt = pl.num_programs(1)
    g = g_ref[...]
    sh = sh_ref[0]
    sc = sc_ref[0]

    def mod(xf):
        return (_rms(xf, g) * (1.0 + sc) + sh).astype(BF16)

    tm = x_ref.shape[1]
    hb_ext = mod(jnp.concatenate([x_ref[0], xp_ref[0], xn_ref[0]], axis=0))
    hb = hb_ext[:tm]

    def proj(lo, hi, lhs=hb):
        return jnp.dot(lhs, w_ref[:, lo:hi], preferred_element_type=F32)

    cm, sma, smb = cm_ref[...], sma_ref[...], smb_ref[...]
    cs, ssa, ssb = cs_ref[...], ssa_ref[...], ssb_ref[...]

    pq = proj(C_Q, C_CKV)
    for h in range(MLA_HEADS):
        xh = pq[:, HEAD_PAD * h:HEAD_PAD * (h + 1)]
        q_ref[0, :, HEAD_PAD * h:HEAD_PAD * (h + 1)] = (
            _rope(xh, cm, sma, smb, MLA_ROPE // 2) * (MLA_SCALE * LOG2E)).astype(BF16)

    ckv = proj(C_CKV, C_KR)
    cn = _rms(ckv, kvg_ref[...]).astype(BF16)
    kv = jnp.dot(cn, wkv_ref[...], preferred_element_type=F32)
    krr = _rope(proj(C_KR, C_SQ), cm, sma, smb, MLA_ROPE // 2)
    for h in range(MLA_HEADS):
        kk_ref[0, :, HEAD_PAD * h:HEAD_PAD * (h + 1)] = (
            kv[:, HEAD_PAD * h:HEAD_PAD * (h + 1)] + krr).astype(BF16)
    ones = jnp.ones((hb.shape[0], LANE), BF16)
    for j in range(MLA_HEADS // 2):
        v0 = MLA_HEADS * HEAD_PAD + LANE * j
        vv_ref[0, :, 2 * LANE * j:2 * LANE * j + LANE] = kv[:, v0:v0 + LANE].astype(BF16)
        vv_ref[0, :, 2 * LANE * j + LANE:2 * LANE * (j + 1)] = ones

    psq = proj(C_SQ, C_SK)
    for j in range(SWA_HEADS // 2):
        xh = psq[:, LANE * j:LANE * (j + 1)]
        sq_ref[0, :, LANE * j:LANE * (j + 1)] = (
            _rope(xh, cs, ssa, ssb, SWA_HEAD_DIM // 2) * (SWA_SCALE * LOG2E)).astype(BF16)
    psk = proj(C_SK, C_SV)
    for k in range(SWA_KV_HEADS):
        sk_ref[0, :, LANE * k:LANE * (k + 1)] = _rope(
            psk[:, LANE * k:LANE * (k + 1)], cs, ssa, ssb, SWA_HEAD_DIM // 2).astype(BF16)
    sv_ref[0] = proj(C_SV, C_HY).astype(BF16)

    ph = proj(C_HY, C_GT, hb_ext)
    pprev, pnext = _halo_rows(ph, tm, i, nt)
    u = _dwconv3(ph[:tm], pprev, pnext, cw_ref, cb_ref)
    z_ref[0] = u[:, :HY_WIDTH]
    x1_ref[0] = u[:, HY_WIDTH:2 * HY_WIDTH]
    x2_ref[0] = u[:, 2 * HY_WIDTH:]

    gt_ref[0] = jax.nn.sigmoid(proj(C_GT, w_ref.shape[1])).astype(BF16)


def _in_proj(x, sh, sc, g, w, wkv, kvg, rope_m, rope_s, cw, cb):
    b, s, d = x.shape
    tm = min(IN_PROJ_ROWS, s)
    nt = s // tm
    row = lambda width: pl.BlockSpec((1, tm, width), lambda bi, i: (bi, i, 0))
    const2 = lambda a: pl.BlockSpec(a.shape, lambda bi, i: (0, 0), pipeline_mode=pl.Buffered(1))
    tab = pl.BlockSpec((tm, LANE), lambda bi, i: (i, 0))
    in_specs = [
        row(d),
        *_halo_specs(tm, s, d),
        pl.BlockSpec((1, 1, d), lambda bi, i: (bi, 0, 0)),
        pl.BlockSpec((1, 1, d), lambda bi, i: (bi, 0, 0)),
        const2(g), const2(w), const2(wkv), const2(kvg),
        tab, tab, tab, tab, tab, tab,
        const2(cw), const2(cb),
    ]
    widths = [(MLA_HEADS * HEAD_PAD, BF16), (MLA_HEADS * HEAD_PAD, BF16), (MLA_HEADS * LANE, BF16),
              (SWA_HEADS * SWA_HEAD_DIM, BF16), (2 * LANE, BF16), (2 * LANE, BF16),
              (HY_WIDTH, F32), (HY_WIDTH, F32), (HY_WIDTH, F32), (N_BRANCH * d, BF16)]
    return pl.pallas_call(
        _in_proj_kernel,
        grid=(b, nt),
        in_specs=in_specs,
        out_specs=[row(wd) for wd, _ in widths],
        out_shape=[jax.ShapeDtypeStruct((b, s, wd), dt) for wd, dt in widths],
        compiler_params=_cparams(("arbitrary", "arbitrary")),
        name="in_proj",
    )(x, x, x, sh, sc, g, w, wkv, kvg, *rope_m, *rope_s, cw, cb)


def _mla_kernel(*refs, tk, n_lat, rb):
    if n_lat:
        q_ref, kc_ref, vc_ref, kl_ref, vl_ref, o_ref, m_scr, acc_scr, p_scr = refs
    else:
        q_ref, kc_ref, vc_ref, o_ref, m_scr, acc_scr = refs
    tq = q_ref.shape[1]
    nt = (((1,), (1,)), ((), ()))
    blocks = [(e, pl.ds(r * rb, rb)) for e in range(2) for r in range(tq // rb)]

    def scores(e, rows, k_ref, off, n):
        return lax.dot_general(q_ref[0, rows, HEAD_PAD * e:HEAD_PAD * (e + 1)],
                               k_ref[0, pl.ds(off, n), HEAD_PAD * e:HEAD_PAD * (e + 1)], nt,
                               preferred_element_type=F32)

    n_ctx = kc_ref.shape[1]
    for e, rows in blocks:
        s = scores(e, rows, kc_ref, 0, n_ctx)
        m_new = jnp.tile(jnp.max(s, axis=1, keepdims=True), (1, LANE))
        p = jnp.exp2(s - jnp.tile(m_new, (1, n_ctx // LANE))).astype(BF16)
        if n_lat:
            p_scr[e, rows, :n_ctx] = p
        else:
            acc_scr[e, rows, :] = jnp.dot(p, vc_ref[0], preferred_element_type=F32)
        m_scr[e, rows, :] = m_new

    if n_lat:
        def softmax_chunk(off, pv_off):
            for e, rows in blocks:
                s = scores(e, rows, kl_ref, off, tk)
                m_old = m_scr[e, rows, :]
                m_new = jnp.maximum(m_old, jnp.max(s, axis=1, keepdims=True))
                alpha = jnp.tile(jnp.exp2(m_old - m_new), (1, 2))
                if pv_off is None:
                    acc = jnp.dot(p_scr[e, rows, :n_ctx], vc_ref[0], preferred_element_type=F32)
                else:
                    acc = acc_scr[e, rows, :] + jnp.dot(p_scr[e, rows, :], vl_ref[0, pl.ds(pv_off, tk), :],
                                                        preferred_element_type=F32)
                acc_scr[e, rows, :] = alpha * acc
                p_scr[e, rows, :] = jnp.exp2(s - jnp.tile(m_new, (1, tk // LANE))).astype(BF16)
                m_scr[e, rows, :] = m_new

        softmax_chunk(0, None)

        def body(j, carry):
            off = pl.multiple_of(j * tk, tk)
            softmax_chunk(off, pl.multiple_of(off - tk, tk))
            return carry
        lax.fori_loop(1, n_lat, body, 0)
        last = (n_lat - 1) * tk
        for e, rows in blocks:
            acc_scr[e, rows, :] += jnp.dot(p_scr[e, rows, :], vl_ref[0, pl.ds(last, tk), :],
                                           preferred_element_type=F32)
    lane = lax.broadcasted_iota(jnp.int32, (tq, LANE), 1)
    o0 = acc_scr[0, :, :LANE] / acc_scr[0, :, LANE:]
    o1 = acc_scr[1, :, :LANE] / acc_scr[1, :, LANE:]
    o_ref[0] = jnp.where(lane < MLA_V, o0, o1).astype(o_ref.dtype)


def _mla(q, kc, vc, kl=None, vl=None, tq=4096, tk=512, rb=128):
    b, sq, _ = q.shape
    c = kc.shape[1]
    tq = min(tq, sq)
    rb = min(rb, tq)
    hp = MLA_HEADS // 2
    in_specs = [pl.BlockSpec((1, tq, 2 * HEAD_PAD), lambda bi, h, i: (bi, i, h)),
                pl.BlockSpec((1, c, 2 * HEAD_PAD), lambda bi, h, i: (bi, 0, h)),
                pl.BlockSpec((1, c, 2 * LANE), lambda bi, h, i: (bi, 0, h))]
    args = [q, kc, vc]
    n_lat = 0
    if kl is not None:
        s = kl.shape[1]
        tk = min(tk, s)
        n_lat = s // tk
        assert c <= tk, "the context chunk shares the probability buffer of a latent chunk"
        in_specs += [pl.BlockSpec((1, s, 2 * HEAD_PAD), lambda bi, h, i: (bi, 0, h)),
                     pl.BlockSpec((1, s, 2 * LANE), lambda bi, h, i: (bi, 0, h))]
        args += [kl, vl]
    return pl.pallas_call(
        functools.partial(_mla_kernel, tk=tk, n_lat=n_lat, rb=rb),
        grid=(b, hp, sq // tq),
        in_specs=in_specs,
        out_specs=pl.BlockSpec((1, tq, LANE), lambda bi, h, i: (bi, i, h)),
        out_shape=jax.ShapeDtypeStruct((b, sq, MLA_HEADS * MLA_V), BF16),
        scratch_shapes=[pltpu.VMEM((2, tq, LANE), F32), pltpu.VMEM((2, tq, 2 * LANE), F32)]
        + ([pltpu.VMEM((2, tq, tk), BF16)] if n_lat else []),
        compiler_params=_cparams(("arbitrary", "arbitrary", "arbitrary")),
        name="mla_attn",
    )(*args)


def _swa_kernel(*refs, band):
    if band:
        sink_ref, q_ref, kc_ref, vc_ref, kp_ref, k0_ref, kn_ref, vp_ref, v0_ref, vn_ref, o_ref = refs
        kband = jnp.concatenate([kp_ref[0], k0_ref[0], kn_ref[0]], axis=0)
        vband = jnp.concatenate([vp_ref[0], v0_ref[0], vn_ref[0]], axis=0)
    else:
        sink_ref, q_ref, kc_ref, vc_ref, o_ref = refs
    i = pl.program_id(1)
    nb = pl.num_programs(1)
    tq = q_ref.shape[1]
    nsub = tq // SWA_BLOCK
    n_ctx = kc_ref.shape[1]
    n_keys = n_ctx + (3 * SWA_BLOCK if band else 0)
    grp = SWA_HEADS // SWA_KV_HEADS
    rows_all = grp * SWA_BLOCK
    nt = (((1,), (1,)), ((), ()))
    lane = lax.broadcasted_iota(jnp.int32, (SWA_BLOCK, LANE), 1)
    low = lane < SWA_HEAD_DIM
    row = lax.broadcasted_iota(jnp.int32, (rows_all, 1), 0)
    if band:
        r_loc = lax.broadcasted_iota(jnp.int32, (rows_all, n_keys), 0) % SWA_BLOCK
        col = lax.broadcasted_iota(jnp.int32, (rows_all, n_keys), 1) - n_ctx
        prev_bias = jnp.where((col >= 0) & (col < SWA_BLOCK), NEG_INF, 0.0)
        next_bias = jnp.where(col >= 2 * SWA_BLOCK, NEG_INF, 0.0)
        base_bias = (jnp.where(r_loc > col, prev_bias, 0.0)
                     + jnp.where(col - 2 * SWA_BLOCK > r_loc, next_bias, 0.0))
    for r in range(nsub):
        rs = slice(r * SWA_BLOCK, (r + 1) * SWA_BLOCK)
        if band:
            bias = base_bias
            if r == 0:
                bias = jnp.minimum(bias, jnp.where(i == 0, prev_bias, 0.0))
            if r == nsub - 1:
                bias = jnp.minimum(bias, jnp.where(i == nb - 1, next_bias, 0.0))
            bs = slice(r * SWA_BLOCK, (r + 3) * SWA_BLOCK)
        for kvh in range(SWA_KV_HEADS):
            sl = slice(LANE * kvh, LANE * (kvh + 1))
            k, v = kc_ref[0, :, sl], vc_ref[0, :, sl]
            if band:
                k = jnp.concatenate([k, kband[bs, sl]], axis=0)
                v = jnp.concatenate([v, vband[bs, sl]], axis=0)
            v = jnp.concatenate([v, jnp.ones_like(v)], axis=1)
            parts, sink = [], None
            for t in range(grp // 2):
                pair = kvh * (grp // 2) + t
                qp = q_ref[0, rs, LANE * pair:LANE * (pair + 1)]
                parts += [jnp.where(low, qp, jnp.zeros_like(qp)), jnp.where(low, jnp.zeros_like(qp), qp)]
            for h in range(grp):
                sh = sink_ref[kvh * grp + h] * LOG2E
                sink = sh if sink is None else jnp.where(row < h * SWA_BLOCK, sink, sh)
            s = lax.dot_general(jnp.concatenate(parts, axis=0), k, nt, preferred_element_type=F32)
            if band:
                s = s + bias
            m = jnp.maximum(jnp.max(s, axis=1, keepdims=True), sink)
            p = jnp.exp2(s - m).astype(BF16)
            acc = jnp.dot(p, v, preferred_element_type=F32)
            o = acc[:, :LANE] / (acc[:, LANE:] + jnp.exp2(sink - m))
            for t in range(grp // 2):
                pair = kvh * (grp // 2) + t
                lo = o[(2 * t) * SWA_BLOCK:(2 * t + 1) * SWA_BLOCK]
                hi = o[(2 * t + 1) * SWA_BLOCK:(2 * t + 2) * SWA_BLOCK]
                o_ref[0, rs, LANE * pair:LANE * (pair + 1)] = jnp.where(low, lo, hi).astype(o_ref.dtype)


def _swa(sink, q, kc, vc, k=None, v=None, tq=1024):
    b, sq, _ = q.shape
    c = kc.shape[1]
    tq = min(tq, sq)
    nb = sq // tq
    band = k is not None
    blk = lambda rows, width, f: pl.BlockSpec((1, rows, width), f)
    in_specs = [pl.BlockSpec(memory_space=pltpu.SMEM),
                blk(tq, SWA_HEADS * SWA_HEAD_DIM, lambda bi, i: (bi, i, 0)),
                pl.BlockSpec((1, c, 2 * LANE), lambda bi, i: (bi, 0, 0)),
                pl.BlockSpec((1, c, 2 * LANE), lambda bi, i: (bi, 0, 0))]
    args = [sink, q, kc, vc]
    if band:
        per = tq // SWA_BLOCK
        prev = lambda bi, i: (bi, jnp.maximum(i * per - 1, 0), 0)
        cur = lambda bi, i: (bi, i, 0)
        nxt = lambda bi, i: (bi, jnp.minimum((i + 1) * per, sq // SWA_BLOCK - 1), 0)
        in_specs += [blk(SWA_BLOCK, 2 * LANE, prev), blk(tq, 2 * LANE, cur), blk(SWA_BLOCK, 2 * LANE, nxt),
                     blk(SWA_BLOCK, 2 * LANE, prev), blk(tq, 2 * LANE, cur), blk(SWA_BLOCK, 2 * LANE, nxt)]
        args += [k, k, k, v, v, v]
    return pl.pallas_call(
        functools.partial(_swa_kernel, band=band),
        grid=(b, nb),
        in_specs=in_specs,
        out_specs=blk(tq, SWA_HEADS * SWA_HEAD_DIM, lambda bi, i: (bi, i, 0)),
        out_shape=jax.ShapeDtypeStruct((b, sq, SWA_HEADS * SWA_HEAD_DIM), BF16),
        compiler_params=_cparams(("arbitrary", "arbitrary")),
        name="swa_attn",
    )(*args)


def _merge_kernel(x_ref, ya_ref, yb_ref, yc_ref, gt_ref, wb_ref, wo_ref, ng_ref, g1_ref, o_ref):
    d = x_ref.shape[2]
    merged = None
    for k, y_ref in enumerate((ya_ref, yb_ref, yc_ref)):
        t = jnp.dot(y_ref[0].astype(BF16), wb_ref[k], preferred_element_type=F32)
        t = gt_ref[0, :, d * k:d * (k + 1)].astype(F32) * t
        merged = t if merged is None else merged + t
    y = jnp.dot(merged.astype(BF16), wo_ref[...], preferred_element_type=F32)
    o_ref[0] = x_ref[0] + g1_ref[0] * _rms(y, ng_ref[...])


def _merge(x, ya, yb, yc, gt, wb, wo, ng, g1):
    b, s, d = x.shape
    tm = min(MERGE_ROWS, s)
    row = lambda width: pl.BlockSpec((1, tm, width), lambda bi, i: (bi, i, 0))
    return pl.pallas_call(
        _merge_kernel,
        grid=(b, s // tm),
        in_specs=[row(d), row(BRANCH_WIDTH), row(BRANCH_WIDTH), row(BRANCH_WIDTH), row(N_BRANCH * d),
                  pl.BlockSpec(wb.shape, lambda bi, i: (0, 0, 0)),
                  pl.BlockSpec(wo.shape, lambda bi, i: (0, 0)),
                  pl.BlockSpec(ng.shape, lambda bi, i: (0, 0)),
                  pl.BlockSpec((1, 1, d), lambda bi, i: (bi, 0, 0))],
        out_specs=row(d),
        out_shape=jax.ShapeDtypeStruct((b, s, d), F32),
        compiler_params=_cparams(("arbitrary", "arbitrary")),
        name="merge",
    )(x, ya, yb, yc, gt, wb, wo, ng, g1)


def _ffn_kernel(x_ref, xp_ref, xn_ref, sh_ref, sc_ref, g2_ref, ng_in_ref, ng_out_ref,
                wu_ref, cw_ref, cb_ref, wd_ref, o_ref, *, chunk):
    i = pl.program_id(1)
    nt = pl.num_programs(1)
    g = ng_in_ref[...]
    sh = sh_ref[0]
    sc = sc_ref[0]

    def mod(xf):
        return (_rms(xf, g) * (1.0 + sc) + sh).astype(BF16)

    x = x_ref[0]
    tm = x.shape[0]
    hb = mod(jnp.concatenate([x, xp_ref[0], xn_ref[0]], axis=0))
    dff = wd_ref.shape[0]
    f = None
    for c0 in range(0, dff, chunk):
        halves = []
        for off in (c0, dff + c0):
            p = jnp.dot(hb, wu_ref[:, off:off + chunk], preferred_element_type=F32)
            pprev, pnext = _halo_rows(p, tm, i, nt)
            halves.append(_dwconv3(p[:tm], pprev, pnext, cw_ref.at[:, off:off + chunk],
                                   cb_ref.at[:, off:off + chunk]))
        a, bb = halves
        act = (a * jax.nn.sigmoid(a) * bb).astype(BF16)
        t = jnp.dot(act, wd_ref[c0:c0 + chunk, :], preferred_element_type=F32)
        f = t if f is None else f + t
    o_ref[0] = x + g2_ref[0] * _rms(f, ng_out_ref[...])


def _ffn(x, sh, sc, g2, ng_in, ng_out, wu, cw, cb, wd):
    b, s, d = x.shape
    tm = min(FFN_ROWS, s)
    dff = wd.shape[0]
    chunk = dff // FFN_CHUNKS if dff % (FFN_CHUNKS * LANE) == 0 else dff
    row = pl.BlockSpec((1, tm, d), lambda bi, i: (bi, i, 0))
    vec = pl.BlockSpec((1, 1, d), lambda bi, i: (bi, 0, 0))
    const2 = lambda a: pl.BlockSpec(a.shape, lambda bi, i: (0, 0), pipeline_mode=pl.Buffered(1))
    return pl.pallas_call(
        functools.partial(_ffn_kernel, chunk=chunk),
        grid=(b, s // tm),
        in_specs=[row, *_halo_specs(tm, s, d),
                  vec, vec, vec, const2(ng_in), const2(ng_out),
                  const2(wu), const2(cw), const2(cb), const2(wd)],
        out_specs=row,
        out_shape=jax.ShapeDtypeStruct((b, s, d), F32),
        compiler_params=_cparams(("arbitrary", "arbitrary")),
        name="conv_ffn",
    )(x, x, x, sh, sc, g2, ng_in, ng_out, wu, cw, cb, wd)


HY_PASSES = 1
SPEC_PASSES = 1
HY_UNROLL = 2
HY_UNROLL2 = 11


def _split(x):
    hi = x.astype(BF16)
    return hi, (x - hi.astype(F32)).astype(BF16)


def _cdot(c_hi, c_lo, x, passes=HY_PASSES):
    x_hi, x_lo = _split(x)
    acc = jnp.dot(c_hi, x_hi, preferred_element_type=F32)
    if passes > 1:
        acc = acc + jnp.dot(c_lo, x_hi, preferred_element_type=F32)
        acc = acc + jnp.dot(c_hi, x_lo, preferred_element_type=F32)
    return acc


def _np_split(m):
    m = jnp.asarray(m, F32)
    hi = m.astype(BF16)
    return hi, (m - hi.astype(F32)).astype(BF16)


def _dft_consts(n_blocks, nonzero_blocks):
    nb = n_blocks
    n = LANE * nb
    dh = nb // 2 + 1
    dhp = dh + dh % 2
    d = np.arange(dh)[:, None]
    b = np.arange(nonzero_blocks)[None, :]
    ang = 2 * np.pi * ((d * b) % nb) / nb
    f1 = np.zeros((2 * dhp, nonzero_blocks))
    f1[:dh] = np.cos(ang)
    f1[dhp:dhp + dh] = -np.sin(ang)
    a = np.arange(LANE)
    ang2 = 2 * np.pi * ((a[:, None] * a[None, :]) % LANE) / LANE
    ar, ai = np.cos(ang2), -np.sin(ang2)
    f2 = np.block([[ar, -ai], [ai, ar]])
    f2inv = np.block([[ar, ai], [-ai, ar]])
    bo = np.arange(nb // 2)[:, None]
    do = np.arange(dh)[None, :]
    wd = np.where((do == 0) | (do == nb // 2), 1.0, 2.0) / n
    ang3 = 2 * np.pi * ((bo * do) % nb) / nb
    f3 = np.zeros((nb // 2, 2 * dhp))
    f3[:, :dh] = wd * np.cos(ang3)
    f3[:, dhp:dhp + dh] = -wd * np.sin(ang3)
    step_ang = 2 * np.pi * a / n
    tw_step = np.stack([np.broadcast_to(np.cos(step_ang)[:, None], (LANE, LANE)),
                        np.broadcast_to(-np.sin(step_ang)[:, None], (LANE, LANE))])
    eye = np.eye(SUBLANES)
    return dict(dh=dh, dhp=dhp, f1=_np_split(np.kron(f1, eye)), f2=_np_split(f2), f2inv=_np_split(f2inv),
                f3=_np_split(np.kron(f3, eye)), tw_step=jnp.asarray(tw_step, F32))


def _stage1(x_ref, f1_hi, f1_lo, g_scr, passes):
    n_in, n_out = x_ref.shape[0], g_scr.shape[0]

    def body(t, carry):
        r0 = pl.ds(pl.multiple_of(2 * t * SUBLANES, SUBLANES), SUBLANES)
        r1 = pl.ds(pl.multiple_of((2 * t + 1) * SUBLANES, SUBLANES), SUBLANES)
        xa = jnp.concatenate([x_ref[:, r0, :].reshape(n_in * SUBLANES, LANE),
                              x_ref[:, r1, :].reshape(n_in * SUBLANES, LANE)], axis=1)
        g = _cdot(f1_hi, f1_lo, xa, passes)
        g_scr[:, r0, :] = g[:, :LANE].reshape(n_out, SUBLANES, LANE)
        g_scr[:, r1, :] = g[:, LANE:].reshape(n_out, SUBLANES, LANE)
        return carry
    lax.fori_loop(0, LANE // (2 * SUBLANES), body, 0, unroll=HY_UNROLL)


def _twiddle_init(tw_scr):
    tw_scr[0] = jnp.ones((LANE, LANE), F32)
    tw_scr[1] = jnp.zeros((LANE, LANE), F32)


def _twiddle_next(twr, twi, step_ref):
    sr, si = step_ref[0], step_ref[1]
    return twr * sr - twi * si, twr * si + twi * sr


def _twiddle_pair(tw_scr, step_ref):
    t0r, t0i = tw_scr[0], tw_scr[1]
    t1r, t1i = _twiddle_next(t0r, t0i, step_ref)
    t2r, t2i = _twiddle_next(t1r, t1i, step_ref)
    tw_scr[0] = t2r
    tw_scr[1] = t2i
    return jnp.concatenate([t0r, t1r], axis=1), jnp.concatenate([t0i, t1i], axis=1)


def _spectrum_kernel(k_ref, inv_ref, f1h_ref, f1l_ref, f2h_ref, f2l_ref, step_ref, hf_ref, g_scr, tw_scr,
                     *, dh, dhp):
    _stage1(k_ref, f1h_ref[...], f1l_ref[...], g_scr, SPEC_PASSES)
    _twiddle_init(tw_scr)
    inv = inv_ref[...]

    def body(h, carry):
        d = 2 * h
        gr = jnp.concatenate([g_scr[d], g_scr[d + 1]], axis=1)
        gi = jnp.concatenate([g_scr[dhp + d], g_scr[dhp + d + 1]], axis=1)
        twr, twi = _twiddle_pair(tw_scr, step_ref)
        t = jnp.concatenate([gr * twr - gi * twi, gr * twi + gi * twr], axis=0)
        y = _cdot(f2h_ref[...], f2l_ref[...], t, passes=SPEC_PASSES)
        hf_ref[0, d] = y[:LANE, :LANE] * inv
        hf_ref[0, d + 1] = y[:LANE, LANE:] * inv
        hf_ref[1, d] = y[LANE:, :LANE] * inv
        hf_ref[1, d + 1] = y[LANE:, LANE:] * inv
        return carry
    lax.fori_loop(0, dhp // 2, body, 0, unroll=min(HY_UNROLL2, dhp // 2))


def _hyena_spectrum(kern, inv_norm):
    tiles, n, _ = kern.shape
    c = tiles * LANE
    nb = n // LANE
    cst = _dft_consts(nb, nb)
    dh, dhp = cst["dh"], cst["dhp"]
    const = lambda a: pl.BlockSpec(a.shape, lambda j: (0,) * a.ndim, pipeline_mode=pl.Buffered(1))
    consts = [*cst["f1"], *cst["f2"], cst["tw_step"]]
    return pl.pallas_call(
        functools.partial(_spectrum_kernel, dh=dh, dhp=dhp),
        grid=(c // LANE,),
        in_specs=[pl.BlockSpec((None, nb, LANE, LANE), lambda j: (j, 0, 0, 0)),
                  pl.BlockSpec((1, LANE), lambda j: (0, j))]
        + [const(a) for a in consts],
        out_specs=pl.BlockSpec((None, 2, dhp, LANE, LANE), lambda j: (j, 0, 0, 0, 0)),
        out_shape=jax.ShapeDtypeStruct((c // LANE, 2, dhp, LANE, LANE), F32),
        scratch_shapes=[pltpu.VMEM((2 * dhp, LANE, LANE), F32), pltpu.VMEM((2, LANE, LANE), F32)],
        compiler_params=_cparams(("arbitrary",)),
        name="hyena_spectrum",
    )(kern.reshape(tiles, nb, LANE, LANE), inv_norm, *consts)


def _conv_kernel(z_ref, gate_ref, skip_ref, hf_ref, f1h_ref, f1l_ref, f2h_ref, f2l_ref, f2ih_ref, f2il_ref,
                 f3h_ref, f3l_ref, step_ref, o_ref, g_scr, tw_scr, *, dh, dhp):
    nbh = z_ref.shape[1]
    _stage1(z_ref.at[0], f1h_ref[...], f1l_ref[...], g_scr, HY_PASSES)
    _twiddle_init(tw_scr)

    def body(h, carry):
        d = 2 * h
        gr = jnp.concatenate([g_scr[d], g_scr[d + 1]], axis=1)
        gi = jnp.concatenate([g_scr[dhp + d], g_scr[dhp + d + 1]], axis=1)
        twr, twi = _twiddle_pair(tw_scr, step_ref)
        t = jnp.concatenate([gr * twr - gi * twi, gr * twi + gi * twr], axis=0)
        y = _cdot(f2h_ref[...], f2l_ref[...], t)
        yr, yi = y[:LANE], y[LANE:]
        hr = jnp.concatenate([hf_ref[0, d], hf_ref[0, d + 1]], axis=1)
        hi = jnp.concatenate([hf_ref[1, d], hf_ref[1, d + 1]], axis=1)
        zz = jnp.concatenate([yr * hr - yi * hi, yr * hi + yi * hr], axis=0)
        u = _cdot(f2ih_ref[...], f2il_ref[...], zz)
        ur, ui = u[:LANE], u[LANE:]
        vr = ur * twr + ui * twi
        vi = ui * twr - ur * twi
        g_scr[d] = vr[:, :LANE]
        g_scr[d + 1] = vr[:, LANE:]
        g_scr[dhp + d] = vi[:, :LANE]
        g_scr[dhp + d + 1] = vi[:, LANE:]
        return carry
    lax.fori_loop(0, dhp // 2, body, 0, unroll=min(HY_UNROLL2, dhp // 2))

    skip = skip_ref[...]
    f3h, f3l = f3h_ref[...], f3l_ref[...]

    def out_body(t, carry):
        r0 = pl.ds(pl.multiple_of(2 * t * SUBLANES, SUBLANES), SUBLANES)
        r1 = pl.ds(pl.multiple_of((2 * t + 1) * SUBLANES, SUBLANES), SUBLANES)
        ga = jnp.concatenate([g_scr[:, r0, :].reshape(2 * dhp * SUBLANES, LANE),
                              g_scr[:, r1, :].reshape(2 * dhp * SUBLANES, LANE)], axis=1)
        conv = _cdot(f3h, f3l, ga)
        for rows, c in ((r0, conv[:, :LANE]), (r1, conv[:, LANE:])):
            o_ref[0, :, rows, :] = gate_ref[0, :, rows, :] * (
                c.reshape(nbh, SUBLANES, LANE) + z_ref[0, :, rows, :] * skip)
        return carry
    lax.fori_loop(0, LANE // (2 * SUBLANES), out_body, 0, unroll=HY_UNROLL)


def _hyena_conv(z, gate, skip, hf, tile0, cst):
    b, l, w = z.shape
    nbh = l // LANE
    dh, dhp = cst["dh"], cst["dhp"]
    consts = [*cst["f1"], *cst["f2"], *cst["f2inv"], *cst["f3"], cst["tw_step"]]
    const = lambda a: pl.BlockSpec(a.shape, lambda j, bi: (0,) * a.ndim, pipeline_mode=pl.Buffered(1))
    tile = pl.BlockSpec((1, nbh, LANE, LANE), lambda j, bi: (bi, 0, 0, j))
    z, gate = z.reshape(b, nbh, LANE, w), gate.reshape(b, nbh, LANE, w)
    return pl.pallas_call(
        functools.partial(_conv_kernel, dh=dh, dhp=dhp),
        grid=(w // LANE, b),
        in_specs=[tile, tile, pl.BlockSpec((1, LANE), lambda j, bi: (0, j)),
                  pl.BlockSpec((None, 2, dhp, LANE, LANE), lambda j, bi: (j + tile0, 0, 0, 0, 0),
                               pipeline_mode=pl.Buffered(1))]
        + [const(a) for a in consts],
        out_specs=tile,
        out_shape=jax.ShapeDtypeStruct((b, nbh, LANE, w), F32),
        scratch_shapes=[pltpu.VMEM((2 * dhp, LANE, LANE), F32), pltpu.VMEM((2, LANE, LANE), F32)],
        compiler_params=_cparams(("arbitrary", "arbitrary")),
        name="hyena_conv",
    )(z, gate, skip, hf, *consts).reshape(b, l, w)


def _filter_mlp_kernel(ft_ref, w1_ref, b1_ref, w2_ref, b2_ref, fr_ref, w3_ref, dl_ref, k_ref, s_ref):
    i = pl.program_id(0)
    ft = ft_ref[...]

    def mm(x, w):
        x_hi, x_lo = _split(x)
        w_hi, w_lo = _split(w)
        return (jnp.dot(x_hi, w_hi, preferred_element_type=F32) + jnp.dot(x_lo, w_hi, preferred_element_type=F32)
                + jnp.dot(x_hi, w_lo, preferred_element_type=F32))

    half = ft.shape[0] // 2
    cw = dl_ref.shape[1]
    hid = jnp.sin(fr_ref[0:1, :] * (mm(jnp.concatenate([ft[:half], ft[half:]], axis=1), w1_ref[...]) + b1_ref[...]))
    hid = jnp.sin(fr_ref[1:2, :] * (mm(hid, w2_ref[...]) + b2_ref[...]))
    h2 = mm(hid, w3_ref[0])
    k = jnp.concatenate([h2[:, :cw], h2[:, cw:]], axis=0)
    k = k * jnp.exp(-ft[:, 0:1] * dl_ref[...]) * ft[:, LANE - 1:LANE]
    for j in range(k_ref.shape[0]):
        k_ref[j] = k[:, LANE * j:LANE * (j + 1)]

    @pl.when(i == 0)
    def _():
        s_ref[...] = jnp.zeros(s_ref.shape, F32)
    s_ref[...] += jnp.sum(jnp.abs(k), axis=0, keepdims=True)


def _hyena_filter_taps(n_tokens, w1, b1, w2, b2, freq, w3):
    L = n_tokens
    n = 2 * L
    idx = jnp.arange(n)
    m = jnp.where(idx < L, idx, n - idx).astype(F32)
    t = m / max(L - 1, 1)
    bands = jnp.linspace(1e-4, HY_BANDS - 1, HY_BANDS, dtype=F32)
    ang = (2.0 * math.pi / L) * m[:, None] * bands
    n_feat = 2 * HY_BANDS + 1
    feats = jnp.concatenate([t[:, None], jnp.cos(ang), -jnp.sin(ang),
                             jnp.zeros((n, LANE - n_feat - 1), F32),
                             (idx != L).astype(F32)[:, None]], axis=-1)
    two = lambda a: jnp.concatenate([a, a], axis=-1)
    bdiag = lambda a: jnp.concatenate([jnp.pad(a, ((0, 0), (0, a.shape[1]))), jnp.pad(a, ((0, 0), (a.shape[1], 0)))], axis=0)
    w1p = bdiag(jnp.pad(w1, ((0, LANE - n_feat), (0, 0))))
    cw = HY_ORDER * HY_WIDTH
    w3d = jnp.moveaxis(w3.reshape(HY_HIDDEN, 2, cw), 1, 0)
    w3d = jnp.stack([bdiag(w3d[0]), bdiag(w3d[1])])
    w2, freq = bdiag(w2), two(freq)
    deltas = jnp.abs(jnp.linspace(math.log(HY_TARGET) / HY_SLOW_PCT, math.log(HY_TARGET) / HY_FAST_PCT,
                                  HY_WIDTH, dtype=F32))
    dl = jnp.tile(deltas, HY_ORDER)[None, :]
    tr = min(1024, L)
    nt = n // tr
    const = lambda a: pl.BlockSpec(a.shape, lambda i: (0,) * a.ndim)
    b1r, b2r = two(b1)[None, :], two(b2)[None, :]
    return pl.pallas_call(
        _filter_mlp_kernel,
        grid=(nt,),
        in_specs=[pl.BlockSpec((tr, LANE), lambda i: (i, 0)), const(w1p), const(b1r), const(w2), const(b2r),
                  const(freq), pl.BlockSpec((1, 2 * HY_HIDDEN, 2 * cw), lambda i: (i // (nt // 2), 0, 0)), const(dl)],
        out_specs=[pl.BlockSpec((cw // LANE, tr, LANE), lambda i: (0, i, 0)), pl.BlockSpec((1, cw), lambda i: (0, 0))],
        out_shape=[jax.ShapeDtypeStruct((cw // LANE, n, LANE), F32), jax.ShapeDtypeStruct((1, cw), F32)],
        compiler_params=_cparams(("arbitrary",)),
        name="hyena_filter_mlp",
    )(feats, w1p, b1r, w2, b2r, freq, w3d, dl)


def _hyena_long(z, x1, x2, taps, norms, skip):
    b, l, w = z.shape
    nb = 2 * l // LANE
    hf = _hyena_spectrum(taps, 1.0 / norms)
    cst = _dft_consts(nb, nb // 2)
    for o, gate in enumerate((x1, x2)):
        z = _hyena_conv(z, gate, skip[o][None, :], hf, o * (w // LANE), cst)
    return z


def _dense_dft_consts(length):
    n = 2 * length
    h = length + 1
    hp = -(-h // 8) * 8
    k = np.arange(h)[:, None]
    pos = np.arange(n)[None, :]
    ang = 2 * np.pi * ((k * pos) % n) / n
    fwd = np.zeros((2 * hp, n))
    fwd[:h] = np.cos(ang)
    fwd[hp:hp + h] = -np.sin(ang)
    wk = np.where((k == 0) | (k == length), 1.0, 2.0) / n
    inv = np.zeros((length, 2 * hp))
    inv[:, :h] = (wk * np.cos(ang[:, :length])).T
    inv[:, hp:hp + h] = (-wk * np.sin(ang[:, :length])).T
    return dict(hp=hp, full=_np_split(fwd), fwd=_np_split(fwd[:, :length]), inv=_np_split(inv))


def _hyena_short_kernel(z_ref, x1_ref, x2_ref, k_ref, inv_ref, skip_ref, ffh_ref, ffl_ref, fh_ref, fl_ref,
                        fih_ref, fil_ref, o_ref, *, hp):
    z = z_ref[0]
    w = z.shape[1]
    for o, gate_ref in enumerate((x1_ref, x2_ref)):
        cols = slice(o * w, (o + 1) * w)
        hf = _cdot(ffh_ref[...], ffl_ref[...], k_ref[:, cols], passes=3) * inv_ref[:, cols]
        x = _cdot(fh_ref[...], fl_ref[...], z, passes=3)
        xr, xi, hr, hi = x[:hp], x[hp:], hf[:hp], hf[hp:]
        zz = jnp.concatenate([xr * hr - xi * hi, xr * hi + xi * hr], axis=0)
        conv = _cdot(fih_ref[...], fil_ref[...], zz, passes=3)
        z = gate_ref[0] * (conv + z * skip_ref[o:o + 1, :])
    o_ref[0] = z


def _hyena_short(z, x1, x2, taps, norms, skip):
    b, l, w = z.shape
    taps = jnp.moveaxis(taps, 0, 1).reshape(2 * l, -1)
    cst = _dense_dft_consts(l)
    consts = [*cst["full"], *cst["fwd"], *cst["inv"]]
    inv_norm = 1.0 / norms
    const = lambda a: pl.BlockSpec(a.shape, lambda bi: (0,) * a.ndim)
    tile = pl.BlockSpec((1, l, w), lambda bi: (bi, 0, 0))
    return pl.pallas_call(
        functools.partial(_hyena_short_kernel, hp=cst["hp"]),
        grid=(b,),
        in_specs=[tile, tile, tile, const(taps), const(inv_norm), const(skip)] + [const(a) for a in consts],
        out_specs=tile,
        out_shape=jax.ShapeDtypeStruct((b, l, w), F32),
        compiler_params=_cparams(("arbitrary",)),
        name="hyena_short",
    )(z, x1, x2, taps, inv_norm, skip, *consts)


def _rope_tables(rows, rot_dim, head_lanes, rope_off, identity_rows):
    half = rot_dim // 2
    n_freq = rot_dim // 4
    pos = np.arange(rows * GRID_W)
    inv_freq = ROPE_THETA ** (-np.arange(n_freq, dtype=np.float64) / n_freq)
    ang = np.concatenate([(pos // GRID_W)[:, None] * inv_freq, (pos % GRID_W)[:, None] * inv_freq], axis=-1)
    cos, sin = np.cos(ang), np.sin(ang)
    n = pos.shape[0]
    ct = np.ones((n, LANE))
    sa = np.zeros((n, LANE))
    sb = np.zeros((n, LANE))
    for h0 in range(0, LANE, head_lanes):
        lo = h0 + rope_off
        ct[:, lo:lo + half] = cos
        ct[:, lo + half:lo + rot_dim] = cos
        sa[:, lo:lo + half] = -sin
        sb[:, lo + half:lo + rot_dim] = sin
    ident = (np.ones((identity_rows, LANE)), np.zeros((identity_rows, LANE)), np.zeros((identity_rows, LANE)))
    lat = tuple(jnp.asarray(t, F32) for t in (ct, sa, sb))
    ctx = tuple(jnp.asarray(t, F32) for t in ident)
    return lat, ctx


def _pack_w_in(w):
    d = w.shape[0]
    o = 0
    mq = w[:, o:o + MLA_HEADS * (MLA_NOPE + MLA_ROPE)]; o += MLA_HEADS * (MLA_NOPE + MLA_ROPE)
    mckv = w[:, o:o + KV_RANK]; o += KV_RANK
    mkr = w[:, o:o + MLA_ROPE]; o += MLA_ROPE
    sq = w[:, o:o + SWA_HEADS * SWA_HEAD_DIM]; o += SWA_HEADS * SWA_HEAD_DIM
    sk = w[:, o:o + SWA_KV_HEADS * SWA_HEAD_DIM]; o += SWA_KV_HEADS * SWA_HEAD_DIM
    sv = w[:, o:o + SWA_KV_HEADS * SWA_HEAD_DIM]; o += SWA_KV_HEADS * SWA_HEAD_DIM
    hy = w[:, o:o + (HY_ORDER + 1) * HY_WIDTH]; o += (HY_ORDER + 1) * HY_WIDTH
    gt = w[:, o:]
    pad_q = HEAD_PAD - MLA_NOPE - MLA_ROPE
    mq = jnp.pad(mq.reshape(d, MLA_HEADS, MLA_NOPE + MLA_ROPE), ((0, 0), (0, 0), (0, pad_q))).reshape(d, -1)
    mkr = jnp.pad(mkr, ((0, 0), (MLA_NOPE, pad_q)))
    dup = lambda t: jnp.repeat(t.reshape(d, SWA_KV_HEADS, 1, SWA_HEAD_DIM), 2, axis=2).reshape(d, -1)
    return jnp.concatenate([mq, mckv, mkr, sq, dup(sk), dup(sv), hy, gt], axis=1).astype(BF16)


def _pack_w_kv(w):
    r = w.shape[0]
    w = w.reshape(r, MLA_HEADS, MLA_NOPE + MLA_V)
    k = jnp.pad(w[..., :MLA_NOPE], ((0, 0), (0, 0), (0, HEAD_PAD - MLA_NOPE))).reshape(r, -1)
    v = w[..., MLA_NOPE:].reshape(r, -1)
    return jnp.concatenate([k, v], axis=1).astype(BF16)


def kernel(x, c, ctx, c_ctx, w_mod, b_mod, norm_g, w_in, kv_norm_g, w_kv_up, swa_sink, hy_conv_w, hy_conv_b,
           hy_w1, hy_b1, hy_w2, hy_b2, hy_freq, hy_w3, hy_skip, w_branch, w_out, w_up, ffn_conv_w, ffn_conv_b,
           w_down):
    b, s, d = x.shape
    n_ctx = ctx.shape[1]
    depth = w_mod.shape[0]
    assert s % (2 * SWA_BLOCK) == 0 and s % GRID_W == 0, "latent length must tile into 128-token blocks / grid rows"
    assert n_ctx % SWA_BLOCK == 0, "context length must be a multiple of the 128-token block"
    assert d % LANE == 0 and w_in.shape[1] == d
    rows = s // GRID_W
    rope_m, rope_m_ctx = _rope_tables(rows, MLA_ROPE, HEAD_PAD, MLA_NOPE, n_ctx)
    rope_s, rope_s_ctx = _rope_tables(rows, SWA_HEAD_DIM, SWA_HEAD_DIM, 0, n_ctx)

    pad_rows = -(b + 1) % SUBLANES
    cvec = jnp.concatenate([c, c_ctx[None, :], jnp.zeros((pad_rows, d), F32)], axis=0)
    mod_all = _modulation(cvec, w_mod, b_mod)

    x_lat, x_ctx = x, ctx
    for l in range(depth):
        with_ctx = l < depth - 1
        m = mod_all[l].reshape(-1, 6, d)
        lat = [m[:b, k][:, None, :] for k in range(6)]
        cx = [jnp.broadcast_to(m[b, k][None, None, :], (b, 1, d)) for k in range(6)]
        ng = [norm_g[l, k][None, :] for k in range(4)]
        w_pack = _pack_w_in(w_in[l])
        wkv_pack = _pack_w_kv(w_kv_up[l])
        kvg = kv_norm_g[l][None, :]
        cw, cb = hy_conv_w[l], hy_conv_b[l][None, :]

        q, kk, vv, sq, sk, sv, z, x1, x2, gt = _in_proj(
            x_lat, lat[0], lat[1], ng[0], w_pack, wkv_pack, kvg, rope_m, rope_s, cw, cb)
        qc, kkc, vvc, sqc, skc, svc, zc, x1c, x2c, gtc = _in_proj(
            x_ctx, cx[0], cx[1], ng[0], w_pack, wkv_pack, kvg, rope_m_ctx, rope_s_ctx, cw, cb)

        y_a = _mla(q, kkc, vvc, kk, vv)
        y_b = _swa(swa_sink[l], sq, skc, svc, sk, sv)
        hy_mlp = (hy_w1[l], hy_b1[l], hy_w2[l], hy_b2[l], hy_freq[l], hy_w3[l])
        y_c = _hyena_long(z, x1, x2, *_hyena_filter_taps(s, *hy_mlp), hy_skip[l])

        wb = w_branch[l].astype(BF16)
        wo = w_out[l].astype(BF16)
        wu = w_up[l].astype(BF16)
        wd = w_down[l].astype(BF16)
        fcw, fcb = ffn_conv_w[l], ffn_conv_b[l][None, :]

        x_lat = _merge(x_lat, y_a, y_b, y_c, gt, wb, wo, ng[1], lat[2])
        x_lat = _ffn(x_lat, lat[3], lat[4], lat[5], ng[2], ng[3], wu, fcw, fcb, wd)

        if with_ctx:
            yc_a = _mla(qc, kkc, vvc)
            yc_b = _swa(swa_sink[l], sqc, skc, svc)
            yc_c = _hyena_short(zc, x1c, x2c, *_hyena_filter_taps(n_ctx, *hy_mlp), hy_skip[l])
            x_ctx = _merge(x_ctx, yc_a, yc_b, yc_c, gtc, wb, wo, ng[1], cx[2])
            x_ctx = _ffn(x_ctx, cx[3], cx[4], cx[5], ng[2], ng[3], wu, fcw, fcb, wd)
    return x_lat
```

```python
import functools
import math

import numpy as np
import jax
import jax.numpy as jnp
from jax import lax
from jax.experimental import pallas as pl
from jax.experimental.pallas import tpu as pltpu

F32 = jnp.float32
BF16 = jnp.bfloat16

GRID_W = 64
EPS = 1e-6
ROPE_THETA = 10000.0
NEG_INF = -1e30
MLA_HEADS = 8
MLA_NOPE = 64
MLA_ROPE = 32
MLA_V = 64
KV_RANK = 256
MLA_SCALE = (MLA_NOPE + MLA_ROPE) ** -0.5
LOG2E = math.log2(math.e)
SWA_HEADS = 8
SWA_KV_HEADS = 2
SWA_HEAD_DIM = 64
SWA_BLOCK = 128
SWA_SCALE = SWA_HEAD_DIM ** -0.5
HY_WIDTH = 512
HY_ORDER = 2
HY_BANDS = 16
HY_HIDDEN = 64
HY_TARGET = 1e-2
HY_FAST_PCT = 0.3
HY_SLOW_PCT = 1.5
N_BRANCH = 3
BRANCH_WIDTH = 512
LANE = 128
SUBLANES = 8
HEAD_PAD = 128

VMEM_LIMIT = 56 * 1024 * 1024
IN_PROJ_ROWS = 512
FFN_ROWS = 512
FFN_CHUNKS = 1
MERGE_ROWS = 1024

C_Q = 0
C_CKV = C_Q + MLA_HEADS * HEAD_PAD
C_KR = C_CKV + KV_RANK
C_SQ = C_KR + HEAD_PAD
C_SK = C_SQ + SWA_HEADS * SWA_HEAD_DIM
C_SV = C_SK + 2 * SWA_KV_HEADS * SWA_HEAD_DIM
C_HY = C_SV + 2 * SWA_KV_HEADS * SWA_HEAD_DIM
C_GT = C_HY + (HY_ORDER + 1) * HY_WIDTH


def _cparams(sem):
    return pltpu.CompilerParams(dimension_semantics=sem, vmem_limit_bytes=VMEM_LIMIT)


def _rms(xf, g):
    return xf * lax.rsqrt(jnp.mean(xf * xf, axis=-1, keepdims=True) + EPS) * g


def _rope(x, cos, sa, sb, half):
    return x * cos + pltpu.roll(x, LANE - half, 1) * sa + pltpu.roll(x, half, 1) * sb


def _halo_specs(tm, s, d):
    per = tm // SUBLANES
    last = s // SUBLANES - 1
    return (pl.BlockSpec((1, SUBLANES, d), lambda bi, i: (bi, jnp.maximum(i * per - 1, 0), 0)),
            pl.BlockSpec((1, SUBLANES, d), lambda bi, i: (bi, jnp.minimum((i + 1) * per, last), 0)))


def _halo_rows(p, tm, i, nt):
    prev_row = jnp.where(i > 0, p[tm + SUBLANES - 1:tm + SUBLANES, :], 0.0)
    next_row = jnp.where(i < nt - 1, p[tm + SUBLANES:tm + SUBLANES + 1, :], 0.0)
    return prev_row, next_row


def _dwconv3(p, prev_row, next_row, cw_ref, cb_ref):
    tm = p.shape[0]
    rows = lax.broadcasted_iota(jnp.int32, (tm, 1), 0)
    up = jnp.where(rows == 0, prev_row, pltpu.roll(p, 1, 0))
    dn = jnp.where(rows == tm - 1, next_row, pltpu.roll(p, tm - 1, 0))
    return up * cw_ref[0:1, :] + p * cw_ref[1:2, :] + dn * cw_ref[2:3, :] + cb_ref[...]


def _mod_kernel(c_ref, w_ref, b_ref, o_ref):
    c = c_ref[...]
    a = c * jax.nn.sigmoid(c)
    a_hi = a.astype(BF16)
    a_lo = (a - a_hi.astype(F32)).astype(BF16)
    w = w_ref[0]
    w_hi = w.astype(BF16)
    w_lo = (w - w_hi.astype(F32)).astype(BF16)
    acc = jnp.dot(a_hi, w_hi, preferred_element_type=F32)
    acc += jnp.dot(a_lo, w_hi, preferred_element_type=F32)
    acc += jnp.dot(a_hi, w_lo, preferred_element_type=F32)
    o_ref[0] = acc + b_ref[0]


def _modulation(cvec, w_mod, b_mod):
    depth, d, n = w_mod.shape
    rows = cvec.shape[0]
    tn = n // 4 if n % (4 * LANE) == 0 else n
    return pl.pallas_call(
        _mod_kernel,
        grid=(depth, n // tn),
        in_specs=[pl.BlockSpec((rows, d), lambda l, j: (0, 0)),
                  pl.BlockSpec((1, d, tn), lambda l, j: (l, 0, j)),
                  pl.BlockSpec((1, 1, tn), lambda l, j: (l, 0, j))],
        out_specs=pl.BlockSpec((1, rows, tn), lambda l, j: (l, 0, j)),
        out_shape=jax.ShapeDtypeStruct((depth, rows, n), F32),
        compiler_params=_cparams(("arbitrary", "arbitrary")),
        name="adaln_mod",
    )(cvec, w_mod, b_mod.reshape(depth, 1, n))


def _in_proj_kernel(x_ref, xp_ref, xn_ref, sh_ref, sc_ref, g_ref, w_ref, wkv_ref, kvg_ref,
                    cm_ref, sma_ref, smb_ref, cs_ref, ssa_ref, ssb_ref, cw_ref, cb_ref,
                    q_ref, kk_ref, vv_ref, sq_ref, sk_ref, sv_ref, z_ref, x1_ref, x2_ref, gt_ref):
    i = pl.program_id(1)
    nt = pl.num_programs(1)
    g = g_ref[...]
    sh = sh_ref[0]
    sc = sc_ref[0]

    def mod(xf):
        return (_rms(xf, g) * (1.0 + sc) + sh).astype(BF16)

    tm = x_ref.shape[1]
    hb_ext = mod(jnp.concatenate([x_ref[0], xp_ref[0], xn_ref[0]], axis=0))
    hb = hb_ext[:tm]

    def proj(lo, hi, lhs=hb):
        return jnp.dot(lhs, w_ref[:, lo:hi], preferred_element_type=F32)

    cm, sma, smb = cm_ref[...], sma_ref[...], smb_ref[...]
    cs, ssa, ssb = cs_ref[...], ssa_ref[...], ssb_ref[...]

    pq = proj(C_Q, C_CKV)
    for h in range(MLA_HEADS):
        xh = pq[:, HEAD_PAD * h:HEAD_PAD * (h + 1)]
        q_ref[0, :, HEAD_PAD * h:HEAD_PAD * (h + 1)] = (
            _rope(xh, cm, sma, smb, MLA_ROPE // 2) * (MLA_SCALE * LOG2E)).astype(BF16)

    ckv = proj(C_CKV, C_KR)
    cn = _rms(ckv, kvg_ref[...]).astype(BF16)
    kv = jnp.dot(cn, wkv_ref[...], preferred_element_type=F32)
    krr = _rope(proj(C_KR, C_SQ), cm, sma, smb, MLA_ROPE // 2)
    for h in range(MLA_HEADS):
        kk_ref[0, :, HEAD_PAD * h:HEAD_PAD * (h + 1)] = (
            kv[:, HEAD_PAD * h:HEAD_PAD * (h + 1)] + krr).astype(BF16)
    ones = jnp.ones((hb.shape[0], LANE), BF16)
    for j in range(MLA_HEADS // 2):
        v0 = MLA_HEADS * HEAD_PAD + LANE * j
        vv_ref[0, :, 2 * LANE * j:2 * LANE * j + LANE] = kv[:, v0:v0 + LANE].astype(BF16)
        vv_ref[0, :, 2 * LANE * j + LANE:2 * LANE * (j + 1)] = ones

    psq = proj(C_SQ, C_SK)
    for j in range(SWA_HEADS // 2):
        xh = psq[:, LANE * j:LANE * (j + 1)]
        sq_ref[0, :, LANE * j:LANE * (j + 1)] = (
            _rope(xh, cs, ssa, ssb, SWA_HEAD_DIM // 2) * (SWA_SCALE * LOG2E)).astype(BF16)
    psk = proj(C_SK, C_SV)
    for k in range(SWA_KV_HEADS):
        sk_ref[0, :, LANE * k:LANE * (k + 1)] = _rope(
            psk[:, LANE * k:LANE * (k + 1)], cs, ssa, ssb, SWA_HEAD_DIM // 2).astype(BF16)
    sv_ref[0] = proj(C_SV, C_HY).astype(BF16)

    ph = proj(C_HY, C_GT, hb_ext)
    pprev, pnext = _halo_rows(ph, tm, i, nt)
    u = _dwconv3(ph[:tm], pprev, pnext, cw_ref, cb_ref)
    z_ref[0] = u[:, :HY_WIDTH]
    x1_ref[0] = u[:, HY_WIDTH:2 * HY_WIDTH]
    x2_ref[0] = u[:, 2 * HY_WIDTH:]

    gt_ref[0] = jax.nn.sigmoid(proj(C_GT, w_ref.shape[1])).astype(BF16)


def _in_proj(x, sh, sc, g, w, wkv, kvg, rope_m, rope_s, cw, cb):
    b, s, d = x.shape
    tm = min(IN_PROJ_ROWS, s)
    nt = s // tm
    row = lambda width: pl.BlockSpec((1, tm, width), lambda bi, i: (bi, i, 0))
    const2 = lambda a: pl.BlockSpec(a.shape, lambda bi, i: (0, 0), pipeline_mode=pl.Buffered(1))
    tab = pl.BlockSpec((tm, LANE), lambda bi, i: (i, 0))
    in_specs = [
        row(d),
        *_halo_specs(tm, s, d),
        pl.BlockSpec((1, 1, d), lambda bi, i: (bi, 0, 0)),
        pl.BlockSpec((1, 1, d), lambda bi, i: (bi, 0, 0)),
        const2(g), const2(w), const2(wkv), const2(kvg),
        tab, tab, tab, tab, tab, tab,
        const2(cw), const2(cb),
    ]
    widths = [(MLA_HEADS * HEAD_PAD, BF16), (MLA_HEADS * HEAD_PAD, BF16), (MLA_HEADS * LANE, BF16),
              (SWA_HEADS * SWA_HEAD_DIM, BF16), (2 * LANE, BF16), (2 * LANE, BF16),
              (HY_WIDTH, F32), (HY_WIDTH, F32), (HY_WIDTH, F32), (N_BRANCH * d, BF16)]
    return pl.pallas_call(
        _in_proj_kernel,
        grid=(b, nt),
        in_specs=in_specs,
        out_specs=[row(wd) for wd, _ in widths],
        out_shape=[jax.ShapeDtypeStruct((b, s, wd), dt) for wd, dt in widths],
        compiler_params=_cparams(("arbitrary", "arbitrary")),
        name="in_proj",
    )(x, x, x, sh, sc, g, w, wkv, kvg, *rope_m, *rope_s, cw, cb)


def _mla_kernel(*refs, tk, n_lat, rb):
    if n_lat:
        q_ref, kc_ref, vc_ref, kl_ref, vl_ref, o_ref, m_scr, acc_scr, p_scr = refs
    else:
        q_ref, kc_ref, vc_ref, o_ref, m_scr, acc_scr = refs
    tq = q_ref.shape[1]
    nt = (((1,), (1,)), ((), ()))
    blocks = [(e, pl.ds(r * rb, rb)) for e in range(2) for r in range(tq // rb)]

    def scores(e, rows, k_ref, off, n):
        return lax.dot_general(q_ref[0, rows, HEAD_PAD * e:HEAD_PAD * (e + 1)],
                               k_ref[0, pl.ds(off, n), HEAD_PAD * e:HEAD_PAD * (e + 1)], nt,
                               preferred_element_type=F32)

    n_ctx = kc_ref.shape[1]
    for e, rows in blocks:
        s = scores(e, rows, kc_ref, 0, n_ctx)
        m_new = jnp.tile(jnp.max(s, axis=1, keepdims=True), (1, LANE))
        p = jnp.exp2(s - jnp.tile(m_new, (1, n_ctx // LANE))).astype(BF16)
        if n_lat:
            p_scr[e, rows, :n_ctx] = p
        else:
            acc_scr[e, rows, :] = jnp.dot(p, vc_ref[0], preferred_element_type=F32)
        m_scr[e, rows, :] = m_new

    if n_lat:
        def softmax_chunk(off, pv_off):
            for e, rows in blocks:
                s = scores(e, rows, kl_ref, off, tk)
                m_old = m_scr[e, rows, :]
                m_new = jnp.maximum(m_old, jnp.max(s, axis=1, keepdims=True))
                alpha = jnp.tile(jnp.exp2(m_old - m_new), (1, 2))
                if pv_off is None:
                    acc = jnp.dot(p_scr[e, rows, :n_ctx], vc_ref[0], preferred_element_type=F32)
                else:
                    acc = acc_scr[e, rows, :] + jnp.dot(p_scr[e, rows, :], vl_ref[0, pl.ds(pv_off, tk), :],
                                                        preferred_element_type=F32)
                acc_scr[e, rows, :] = alpha * acc
                p_scr[e, rows, :] = jnp.exp2(s - jnp.tile(m_new, (1, tk // LANE))).astype(BF16)
                m_scr[e, rows, :] = m_new

        softmax_chunk(0, None)

        def body(j, carry):
            off = pl.multiple_of(j * tk, tk)
            softmax_chunk(off, pl.multiple_of(off - tk, tk))
            return carry
        lax.fori_loop(1, n_lat, body, 0)
        last = (n_lat - 1) * tk
        for e, rows in blocks:
            acc_scr[e, rows, :] += jnp.dot(p_scr[e, rows, :], vl_ref[0, pl.ds(last, tk), :],
                                           preferred_element_type=F32)
    lane = lax.broadcasted_iota(jnp.int32, (tq, LANE), 1)
    o0 = acc_scr[0, :, :LANE] / acc_scr[0, :, LANE:]
    o1 = acc_scr[1, :, :LANE] / acc_scr[1, :, LANE:]
    o_ref[0] = jnp.where(lane < MLA_V, o0, o1).astype(o_ref.dtype)


def _mla(q, kc, vc, kl=None, vl=None, tq=4096, tk=512, rb=128):
    b, sq, _ = q.shape
    c = kc.shape[1]
    tq = min(tq, sq)
    rb = min(rb, tq)
    hp = MLA_HEADS // 2
    in_specs = [pl.BlockSpec((1, tq, 2 * HEAD_PAD), lambda bi, h, i: (bi, i, h)),
                pl.BlockSpec((1, c, 2 * HEAD_PAD), lambda bi, h, i: (bi, 0, h)),
                pl.BlockSpec((1, c, 2 * LANE), lambda bi, h, i: (bi, 0, h))]
    args = [q, kc, vc]
    n_lat = 0
    if kl is not None:
        s = kl.shape[1]
        tk = min(tk, s)
        n_lat = s // tk
        assert c <= tk, "the context chunk shares the probability buffer of a latent chunk"
        in_specs += [pl.BlockSpec((1, s, 2 * HEAD_PAD), lambda bi, h, i: (bi, 0, h)),
                     pl.BlockSpec((1, s, 2 * LANE), lambda bi, h, i: (bi, 0, h))]
        args += [kl, vl]
    return pl.pallas_call(
        functools.partial(_mla_kernel, tk=tk, n_lat=n_lat, rb=rb),
        grid=(b, hp, sq // tq),
        in_specs=in_specs,
        out_specs=pl.BlockSpec((1, tq, LANE), lambda bi, h, i: (bi, i, h)),
        out_shape=jax.ShapeDtypeStruct((b, sq, MLA_HEADS * MLA_V), BF16),
        scratch_shapes=[pltpu.VMEM((2, tq, LANE), F32), pltpu.VMEM((2, tq, 2 * LANE), F32)]
        + ([pltpu.VMEM((2, tq, tk), BF16)] if n_lat else []),
        compiler_params=_cparams(("arbitrary", "arbitrary", "arbitrary")),
        name="mla_attn",
    )(*args)


def _swa_kernel(*refs, band):
    if band:
        sink_ref, q_ref, kc_ref, vc_ref, kp_ref, k0_ref, kn_ref, vp_ref, v0_ref, vn_ref, o_ref = refs
        kband = jnp.concatenate([kp_ref[0], k0_ref[0], kn_ref[0]], axis=0)
        vband = jnp.concatenate([vp_ref[0], v0_ref[0], vn_ref[0]], axis=0)
    else:
        sink_ref, q_ref, kc_ref, vc_ref, o_ref = refs
    i = pl.program_id(1)
    nb = pl.num_programs(1)
    tq = q_ref.shape[1]
    nsub = tq // SWA_BLOCK
    n_ctx = kc_ref.shape[1]
    n_keys = n_ctx + (3 * SWA_BLOCK if band else 0)
    grp = SWA_HEADS // SWA_KV_HEADS
    rows_all = grp * SWA_BLOCK
    nt = (((1,), (1,)), ((), ()))
    lane = lax.broadcasted_iota(jnp.int32, (SWA_BLOCK, LANE), 1)
    low = lane < SWA_HEAD_DIM
    row = lax.broadcasted_iota(jnp.int32, (rows_all, 1), 0)
    if band:
        r_loc = lax.broadcasted_iota(jnp.int32, (rows_all, n_keys), 0) % SWA_BLOCK
        col = lax.broadcasted_iota(jnp.int32, (rows_all, n_keys), 1) - n_ctx
        prev_bias = jnp.where((col >= 0) & (col < SWA_BLOCK), NEG_INF, 0.0)
        next_bias = jnp.where(col >= 2 * SWA_BLOCK, NEG_INF, 0.0)
        base_bias = (jnp.where(r_loc > col, prev_bias, 0.0)
                     + jnp.where(col - 2 * SWA_BLOCK > r_loc, next_bias, 0.0))
    for r in range(nsub):
        rs = slice(r * SWA_BLOCK, (r + 1) * SWA_BLOCK)
        if band:
            bias = base_bias
            if r == 0:
                bias = jnp.minimum(bias, jnp.where(i == 0, prev_bias, 0.0))
            if r == nsub - 1:
                bias = jnp.minimum(bias, jnp.where(i == nb - 1, next_bias, 0.0))
            bs = slice(r * SWA_BLOCK, (r + 3) * SWA_BLOCK)
        for kvh in range(SWA_KV_HEADS):
            sl = slice(LANE * kvh, LANE * (kvh + 1))
            k, v = kc_ref[0, :, sl], vc_ref[0, :, sl]
            if band:
                k = jnp.concatenate([k, kband[bs, sl]], axis=0)
                v = jnp.concatenate([v, vband[bs, sl]], axis=0)
            v = jnp.concatenate([v, jnp.ones_like(v)], axis=1)
            parts, sink = [], None
            for t in range(grp // 2):
                pair = kvh * (grp // 2) + t
                qp = q_ref[0, rs, LANE * pair:LANE * (pair + 1)]
                parts += [jnp.where(low, qp, jnp.zeros_like(qp)), jnp.where(low, jnp.zeros_like(qp), qp)]
            for h in range(grp):
                sh = sink_ref[kvh * grp + h] * LOG2E
                sink = sh if sink is None else jnp.where(row < h * SWA_BLOCK, sink, sh)
            s = lax.dot_general(jnp.concatenate(parts, axis=0), k, nt, preferred_element_type=F32)
            if band:
                s = s + bias
            m = jnp.maximum(jnp.max(s, axis=1, keepdims=True), sink)
            p = jnp.exp2(s - m).astype(BF16)
            acc = jnp.dot(p, v, preferred_element_type=F32)
            o = acc[:, :LANE] / (acc[:, LANE:] + jnp.exp2(sink - m))
            for t in range(grp // 2):
                pair = kvh * (grp // 2) + t
                lo = o[(2 * t) * SWA_BLOCK:(2 * t + 1) * SWA_BLOCK]
                hi = o[(2 * t + 1) * SWA_BLOCK:(2 * t + 2) * SWA_BLOCK]
                o_ref[0, rs, LANE * pair:LANE * (pair + 1)] = jnp.where(low, lo, hi).astype(o_ref.dtype)


def _swa(sink, q, kc, vc, k=None, v=None, tq=1024):
    b, sq, _ = q.shape
    c = kc.shape[1]
    tq = min(tq, sq)
    nb = sq // tq
    band = k is not None
    blk = lambda rows, width, f: pl.BlockSpec((1, rows, width), f)
    in_specs = [pl.BlockSpec(memory_space=pltpu.SMEM),
                blk(tq, SWA_HEADS * SWA_HEAD_DIM, lambda bi, i: (bi, i, 0)),
                pl.BlockSpec((1, c, 2 * LANE), lambda bi, i: (bi, 0, 0)),
                pl.BlockSpec((1, c, 2 * LANE), lambda bi, i: (bi, 0, 0))]
    args = [sink, q, kc, vc]
    if band:
        per = tq // SWA_BLOCK
        prev = lambda bi, i: (bi, jnp.maximum(i * per - 1, 0), 0)
        cur = lambda bi, i: (bi, i, 0)
        nxt = lambda bi, i: (bi, jnp.minimum((i + 1) * per, sq // SWA_BLOCK - 1), 0)
        in_specs += [blk(SWA_BLOCK, 2 * LANE, prev), blk(tq, 2 * LANE, cur), blk(SWA_BLOCK, 2 * LANE, nxt),
                     blk(SWA_BLOCK, 2 * LANE, prev), blk(tq, 2 * LANE, cur), blk(SWA_BLOCK, 2 * LANE, nxt)]
        args += [k, k, k, v, v, v]
    return pl.pallas_call(
        functools.partial(_swa_kernel, band=band),
        grid=(b, nb),
        in_specs=in_specs,
        out_specs=blk(tq, SWA_HEADS * SWA_HEAD_DIM, lambda bi, i: (bi, i, 0)),
        out_shape=jax.ShapeDtypeStruct((b, sq, SWA_HEADS * SWA_HEAD_DIM), BF16),
        compiler_params=_cparams(("arbitrary", "arbitrary")),
        name="swa_attn",
    )(*args)


def _merge_kernel(x_ref, ya_ref, yb_ref, yc_ref, gt_ref, wb_ref, wo_ref, ng_ref, g1_ref, o_ref):
    d = x_ref.shape[2]
    merged = None
    for k, y_ref in enumerate((ya_ref, yb_ref, yc_ref)):
        t = jnp.dot(y_ref[0].astype(BF16), wb_ref[k], preferred_element_type=F32)
        t = gt_ref[0, :, d * k:d * (k + 1)].astype(F32) * t
        merged = t if merged is None else merged + t
    y = jnp.dot(merged.astype(BF16), wo_ref[...], preferred_element_type=F32)
    o_ref[0] = x_ref[0] + g1_ref[0] * _rms(y, ng_ref[...])


def _merge(x, ya, yb, yc, gt, wb, wo, ng, g1):
    b, s, d = x.shape
    tm = min(MERGE_ROWS, s)
    row = lambda width: pl.BlockSpec((1, tm, width), lambda bi, i: (bi, i, 0))
    return pl.pallas_call(
        _merge_kernel,
        grid=(b, s // tm),
        in_specs=[row(d), row(BRANCH_WIDTH), row(BRANCH_WIDTH), row(BRANCH_WIDTH), row(N_BRANCH * d),
                  pl.BlockSpec(wb.shape, lambda bi, i: (0, 0, 0)),
                  pl.BlockSpec(wo.shape, lambda bi, i: (0, 0)),
                  pl.BlockSpec(ng.shape, lambda bi, i: (0, 0)),
                  pl.BlockSpec((1, 1, d), lambda bi, i: (bi, 0, 0))],
        out_specs=row(d),
        out_shape=jax.ShapeDtypeStruct((b, s, d), F32),
        compiler_params=_cparams(("arbitrary", "arbitrary")),
        name="merge",
    )(x, ya, yb, yc, gt, wb, wo, ng, g1)


def _ffn_kernel(x_ref, xp_ref, xn_ref, sh_ref, sc_ref, g2_ref, ng_in_ref, ng_out_ref,
                wu_ref, cw_ref, cb_ref, wd_ref, o_ref, *, chunk):
    i = pl.program_id(1)
    nt = pl.num_programs(1)
    g = ng_in_ref[...]
    sh = sh_ref[0]
    sc = sc_ref[0]

    def mod(xf):
        return (_rms(xf, g) * (1.0 + sc) + sh).astype(BF16)

    x = x_ref[0]
    tm = x.shape[0]
    hb = mod(jnp.concatenate([x, xp_ref[0], xn_ref[0]], axis=0))
    dff = wd_ref.shape[0]
    f = None
    for c0 in range(0, dff, chunk):
        halves = []
        for off in (c0, dff + c0):
            p = jnp.dot(hb, wu_ref[:, off:off + chunk], preferred_element_type=F32)
            pprev, pnext = _halo_rows(p, tm, i, nt)
            halves.append(_dwconv3(p[:tm], pprev, pnext, cw_ref.at[:, off:off + chunk],
                                   cb_ref.at[:, off:off + chunk]))
        a, bb = halves
        act = (a * jax.nn.sigmoid(a) * bb).astype(BF16)
        t = jnp.dot(act, wd_ref[c0:c0 + chunk, :], preferred_element_type=F32)
        f = t if f is None else f + t
    o_ref[0] = x + g2_ref[0] * _rms(f, ng_out_ref[...])


def _ffn(x, sh, sc, g2, ng_in, ng_out, wu, cw, cb, wd):
    b, s, d = x.shape
    tm = min(FFN_ROWS, s)
    dff = wd.shape[0]
    chunk = dff // FFN_CHUNKS if dff % (FFN_CHUNKS * LANE) == 0 else dff
    row = pl.BlockSpec((1, tm, d), lambda bi, i: (bi, i, 0))
    vec = pl.BlockSpec((1, 1, d), lambda bi, i: (bi, 0, 0))
    const2 = lambda a: pl.BlockSpec(a.shape, lambda bi, i: (0, 0), pipeline_mode=pl.Buffered(1))
    return pl.pallas_call(
        functools.partial(_ffn_kernel, chunk=chunk),
        grid=(b, s // tm),
        in_specs=[row, *_halo_specs(tm, s, d),
                  vec, vec, vec, const2(ng_in), const2(ng_out),
                  const2(wu), const2(cw), const2(cb), const2(wd)],
        out_specs=row,
        out_shape=jax.ShapeDtypeStruct((b, s, d), F32),
        compiler_params=_cparams(("arbitrary", "arbitrary")),
        name="conv_ffn",
    )(x, x, x, sh, sc, g2, ng_in, ng_out, wu, cw, cb, wd)


HY_PASSES = 1
SPEC_PASSES = 1
HY_UNROLL = 8
HY_UNROLL2 = 11


def _split(x):
    hi = x.astype(BF16)
    return hi, (x - hi.astype(F32)).astype(BF16)


def _cdot(c_hi, c_lo, x, passes=HY_PASSES):
    x_hi, x_lo = _split(x)
    acc = jnp.dot(c_hi, x_hi, preferred_element_type=F32)
    if passes > 1:
        acc = acc + jnp.dot(c_lo, x_hi, preferred_element_type=F32)
        acc = acc + jnp.dot(c_hi, x_lo, preferred_element_type=F32)
    return acc


def _np_split(m):
    m = jnp.asarray(m, F32)
    hi = m.astype(BF16)
    return hi, (m - hi.astype(F32)).astype(BF16)


def _dft_consts(n_blocks, nonzero_blocks):
    nb = n_blocks
    n = LANE * nb
    dh = nb // 2 + 1
    dhp = dh + dh % 2
    d = np.arange(dh)[:, None]
    b = np.arange(nonzero_blocks)[None, :]
    ang = 2 * np.pi * ((d * b) % nb) / nb
    f1 = np.zeros((2 * dhp, nonzero_blocks))
    f1[:dh] = np.cos(ang)
    f1[dhp:dhp + dh] = -np.sin(ang)
    a = np.arange(LANE)
    ang2 = 2 * np.pi * ((a[:, None] * a[None, :]) % LANE) / LANE
    ar, ai = np.cos(ang2), -np.sin(ang2)
    f2 = np.block([[ar, -ai], [ai, ar]])
    f2inv = np.block([[ar, ai], [-ai, ar]])
    bo = np.arange(nb // 2)[:, None]
    do = np.arange(dh)[None, :]
    wd = np.where((do == 0) | (do == nb // 2), 1.0, 2.0) / n
    ang3 = 2 * np.pi * ((bo * do) % nb) / nb
    f3 = np.zeros((nb // 2, 2 * dhp))
    f3[:, :dh] = wd * np.cos(ang3)
    f3[:, dhp:dhp + dh] = -wd * np.sin(ang3)
    step_ang = 2 * np.pi * a / n
    tw_step = np.stack([np.broadcast_to(np.cos(step_ang)[:, None], (LANE, LANE)),
                        np.broadcast_to(-np.sin(step_ang)[:, None], (LANE, LANE))])
    eye = np.eye(SUBLANES)
    return dict(dh=dh, dhp=dhp, f1=_np_split(np.kron(f1, eye)), f2=_np_split(f2), f2inv=_np_split(f2inv),
                f3=_np_split(np.kron(f3, eye)), tw_step=jnp.asarray(tw_step, F32))


def _stage1(x_ref, f1_hi, f1_lo, g_scr, passes):
    n_in, n_out = x_ref.shape[0], g_scr.shape[0]

    def body(t, carry):
        r0 = pl.ds(pl.multiple_of(2 * t * SUBLANES, SUBLANES), SUBLANES)
        r1 = pl.ds(pl.multiple_of((2 * t + 1) * SUBLANES, SUBLANES), SUBLANES)
        xa = jnp.concatenate([x_ref[:, r0, :].reshape(n_in * SUBLANES, LANE),
                              x_ref[:, r1, :].reshape(n_in * SUBLANES, LANE)], axis=1)
        g = _cdot(f1_hi, f1_lo, xa, passes)
        g_scr[:, r0, :] = g[:, :LANE].reshape(n_out, SUBLANES, LANE)
        g_scr[:, r1, :] = g[:, LANE:].reshape(n_out, SUBLANES, LANE)
        return carry
    lax.fori_loop(0, LANE // (2 * SUBLANES), body, 0, unroll=HY_UNROLL)


def _twiddle_init(tw_scr):
    tw_scr[0] = jnp.ones((LANE, LANE), F32)
    tw_scr[1] = jnp.zeros((LANE, LANE), F32)


def _twiddle_next(twr, twi, step_ref):
    sr, si = step_ref[0], step_ref[1]
    return twr * sr - twi * si, twr * si + twi * sr


def _twiddle_pair(tw_scr, step_ref):
    t0r, t0i = tw_scr[0], tw_scr[1]
    t1r, t1i = _twiddle_next(t0r, t0i, step_ref)
    t2r, t2i = _twiddle_next(t1r, t1i, step_ref)
    tw_scr[0] = t2r
    tw_scr[1] = t2i
    return jnp.concatenate([t0r, t1r], axis=1), jnp.concatenate([t0i, t1i], axis=1)


def _spectrum_kernel(k_ref, inv_ref, f1h_ref, f1l_ref, f2h_ref, f2l_ref, step_ref, hf_ref, g_scr, tw_scr,
                     *, dh, dhp):
    _stage1(k_ref, f1h_ref[...], f1l_ref[...], g_scr, SPEC_PASSES)
    _twiddle_init(tw_scr)
    inv = inv_ref[...]

    def body(h, carry):
        d = 2 * h
        gr = jnp.concatenate([g_scr[d], g_scr[d + 1]], axis=1)
        gi = jnp.concatenate([g_scr[dhp + d], g_scr[dhp + d + 1]], axis=1)
        twr, twi = _twiddle_pair(tw_scr, step_ref)
        t = jnp.concatenate([gr * twr - gi * twi, gr * twi + gi * twr], axis=0)
        y = _cdot(f2h_ref[...], f2l_ref[...], t, passes=SPEC_PASSES)
        hf_ref[0, d] = y[:LANE, :LANE] * inv
        hf_ref[0, d + 1] = y[:LANE, LANE:] * inv
        hf_ref[1, d] = y[LANE:, :LANE] * inv
        hf_ref[1, d + 1] = y[LANE:, LANE:] * inv
        return carry
    lax.fori_loop(0, dhp // 2, body, 0, unroll=min(HY_UNROLL2, dhp // 2))


def _hyena_spectrum(kern, inv_norm):
    tiles, n, _ = kern.shape
    c = tiles * LANE
    nb = n // LANE
    cst = _dft_consts(nb, nb)
    dh, dhp = cst["dh"], cst["dhp"]
    const = lambda a: pl.BlockSpec(a.shape, lambda j: (0,) * a.ndim, pipeline_mode=pl.Buffered(1))
    consts = [*cst["f1"], *cst["f2"], cst["tw_step"]]
    return pl.pallas_call(
        functools.partial(_spectrum_kernel, dh=dh, dhp=dhp),
        grid=(c // LANE,),
        in_specs=[pl.BlockSpec((None, nb, LANE, LANE), lambda j: (j, 0, 0, 0)),
                  pl.BlockSpec((1, LANE), lambda j: (0, j))]
        + [const(a) for a in consts],
        out_specs=pl.BlockSpec((None, 2, dhp, LANE, LANE), lambda j: (j, 0, 0, 0, 0)),
        out_shape=jax.ShapeDtypeStruct((c // LANE, 2, dhp, LANE, LANE), F32),
        scratch_shapes=[pltpu.VMEM((2 * dhp, LANE, LANE), F32), pltpu.VMEM((2, LANE, LANE), F32)],
        compiler_params=_cparams(("arbitrary",)),
        name="hyena_spectrum",
    )(kern.reshape(tiles, nb, LANE, LANE), inv_norm, *consts)


def _conv_kernel(z_ref, gate_ref, skip_ref, hf_ref, f1h_ref, f1l_ref, f2h_ref, f2l_ref, f2ih_ref, f2il_ref,
                 f3h_ref, f3l_ref, step_ref, o_ref, g_scr, tw_scr, *, dh, dhp):
    nbh = z_ref.shape[1]
    _stage1(z_ref.at[0], f1h_ref[...], f1l_ref[...], g_scr, HY_PASSES)
    _twiddle_init(tw_scr)

    def body(h, carry):
        d = 2 * h
        gr = jnp.concatenate([g_scr[d], g_scr[d + 1]], axis=1)
        gi = jnp.concatenate([g_scr[dhp + d], g_scr[dhp + d + 1]], axis=1)
        twr, twi = _twiddle_pair(tw_scr, step_ref)
        t = jnp.concatenate([gr * twr - gi * twi, gr * twi + gi * twr], axis=0)
        y = _cdot(f2h_ref[...], f2l_ref[...], t)
        yr, yi = y[:LANE], y[LANE:]
        hr = jnp.concatenate([hf_ref[0, d], hf_ref[0, d + 1]], axis=1)
        hi = jnp.concatenate([hf_ref[1, d], hf_ref[1, d + 1]], axis=1)
        zz = jnp.concatenate([yr * hr - yi * hi, yr * hi + yi * hr], axis=0)
        u = _cdot(f2ih_ref[...], f2il_ref[...], zz)
        ur, ui = u[:LANE], u[LANE:]
        vr = ur * twr + ui * twi
        vi = ui * twr - ur * twi
        g_scr[d] = vr[:, :LANE]
        g_scr[d + 1] = vr[:, LANE:]
        g_scr[dhp + d] = vi[:, :LANE]
        g_scr[dhp + d + 1] = vi[:, LANE:]
        return carry
    lax.fori_loop(0, dhp // 2, body, 0, unroll=min(HY_UNROLL2, dhp // 2))

    skip = skip_ref[...]
    f3h, f3l = f3h_ref[...], f3l_ref[...]

    def out_body(t, carry):
        r0 = pl.ds(pl.multiple_of(2 * t * SUBLANES, SUBLANES), SUBLANES)
        r1 = pl.ds(pl.multiple_of((2 * t + 1) * SUBLANES, SUBLANES), SUBLANES)
        ga = jnp.concatenate([g_scr[:, r0, :].reshape(2 * dhp * SUBLANES, LANE),
                              g_scr[:, r1, :].reshape(2 * dhp * SUBLANES, LANE)], axis=1)
        conv = _cdot(f3h, f3l, ga)
        for rows, c in ((r0, conv[:, :LANE]), (r1, conv[:, LANE:])):
            o_ref[0, :, rows, :] = gate_ref[0, :, rows, :] * (
                c.reshape(nbh, SUBLANES, LANE) + z_ref[0, :, rows, :] * skip)
        return carry
    lax.fori_loop(0, LANE // (2 * SUBLANES), out_body, 0, unroll=HY_UNROLL)


def _hyena_conv(z, gate, skip, hf, tile0, cst):
    b, l, w = z.shape
    nbh = l // LANE
    dh, dhp = cst["dh"], cst["dhp"]
    consts = [*cst["f1"], *cst["f2"], *cst["f2inv"], *cst["f3"], cst["tw_step"]]
    const = lambda a: pl.BlockSpec(a.shape, lambda j, bi: (0,) * a.ndim, pipeline_mode=pl.Buffered(1))
    tile = pl.BlockSpec((1, nbh, LANE, LANE), lambda j, bi: (bi, 0, 0, j))
    z, gate = z.reshape(b, nbh, LANE, w), gate.reshape(b, nbh, LANE, w)
    return pl.pallas_call(
        functools.partial(_conv_kernel, dh=dh, dhp=dhp),
        grid=(w // LANE, b),
        in_specs=[tile, tile, pl.BlockSpec((1, LANE), lambda j, bi: (0, j)),
                  pl.BlockSpec((None, 2, dhp, LANE, LANE), lambda j, bi: (j + tile0, 0, 0, 0, 0),
                               pipeline_mode=pl.Buffered(1))]
        + [const(a) for a in consts],
        out_specs=tile,
        out_shape=jax.ShapeDtypeStruct((b, nbh, LANE, w), F32),
        scratch_shapes=[pltpu.VMEM((2 * dhp, LANE, LANE), F32), pltpu.VMEM((2, LANE, LANE), F32)],
        compiler_params=_cparams(("arbitrary", "arbitrary")),
        name="hyena_conv",
    )(z, gate, skip, hf, *consts).reshape(b, l, w)


def _filter_mlp_kernel(ft_ref, w1_ref, b1_ref, w2_ref, b2_ref, fr_ref, w3_ref, dl_ref, k_ref, s_ref):
    i = pl.program_id(0)
    ft = ft_ref[...]

    def mm(x, w):
        x_hi, x_lo = _split(x)
        w_hi, w_lo = _split(w)
        return (jnp.dot(x_hi, w_hi, preferred_element_type=F32) + jnp.dot(x_lo, w_hi, preferred_element_type=F32)
                + jnp.dot(x_hi, w_lo, preferred_element_type=F32))

    half = ft.shape[0] // 2
    cw = dl_ref.shape[1]
    hid = jnp.sin(fr_ref[0:1, :] * (mm(jnp.concatenate([ft[:half], ft[half:]], axis=1), w1_ref[...]) + b1_ref[...]))
    hid = jnp.sin(fr_ref[1:2, :] * (mm(hid, w2_ref[...]) + b2_ref[...]))
    h2 = mm(hid, w3_ref[0])
    k = jnp.concatenate([h2[:, :cw], h2[:, cw:]], axis=0)
    k = k * jnp.exp(-ft[:, 0:1] * dl_ref[...]) * ft[:, LANE - 1:LANE]
    for j in range(k_ref.shape[0]):
        k_ref[j] = k[:, LANE * j:LANE * (j + 1)]

    @pl.when(i == 0)
    def _():
        s_ref[...] = jnp.zeros(s_ref.shape, F32)
    s_ref[...] += jnp.sum(jnp.abs(k), axis=0, keepdims=True)


def _hyena_filter_taps(n_tokens, w1, b1, w2, b2, freq, w3):
    L = n_tokens
    n = 2 * L
    idx = jnp.arange(n)
    m = jnp.where(idx < L, idx, n - idx).astype(F32)
    t = m / max(L - 1, 1)
    bands = jnp.linspace(1e-4, HY_BANDS - 1, HY_BANDS, dtype=F32)
    ang = (2.0 * math.pi / L) * m[:, None] * bands
    n_feat = 2 * HY_BANDS + 1
    feats = jnp.concatenate([t[:, None], jnp.cos(ang), -jnp.sin(ang),
                             jnp.zeros((n, LANE - n_feat - 1), F32),
                             (idx != L).astype(F32)[:, None]], axis=-1)
    two = lambda a: jnp.concatenate([a, a], axis=-1)
    bdiag = lambda a: jnp.concatenate([jnp.pad(a, ((0, 0), (0, a.shape[1]))), jnp.pad(a, ((0, 0), (a.shape[1], 0)))], axis=0)
    w1p = bdiag(jnp.pad(w1, ((0, LANE - n_feat), (0, 0))))
    cw = HY_ORDER * HY_WIDTH
    w3d = jnp.moveaxis(w3.reshape(HY_HIDDEN, 2, cw), 1, 0)
    w3d = jnp.stack([bdiag(w3d[0]), bdiag(w3d[1])])
    w2, freq = bdiag(w2), two(freq)
    deltas = jnp.abs(jnp.linspace(math.log(HY_TARGET) / HY_SLOW_PCT, math.log(HY_TARGET) / HY_FAST_PCT,
                                  HY_WIDTH, dtype=F32))
    dl = jnp.tile(deltas, HY_ORDER)[None, :]
    tr = min(1024, L)
    nt = n // tr
    const = lambda a: pl.BlockSpec(a.shape, lambda i: (0,) * a.ndim)
    b1r, b2r = two(b1)[None, :], two(b2)[None, :]
    return pl.pallas_call(
        _filter_mlp_kernel,
        grid=(nt,),
        in_specs=[pl.BlockSpec((tr, LANE), lambda i: (i, 0)), const(w1p), const(b1r), const(w2), const(b2r),
                  const(freq), pl.BlockSpec((1, 2 * HY_HIDDEN, 2 * cw), lambda i: (i // (nt // 2), 0, 0)), const(dl)],
        out_specs=[pl.BlockSpec((cw // LANE, tr, LANE), lambda i: (0, i, 0)), pl.BlockSpec((1, cw), lambda i: (0, 0))],
        out_shape=[jax.ShapeDtypeStruct((cw // LANE, n, LANE), F32), jax.ShapeDtypeStruct((1, cw), F32)],
        compiler_params=_cparams(("arbitrary",)),
        name="hyena_filter_mlp",
    )(feats, w1p, b1r, w2, b2r, freq, w3d, dl)


def _hyena_long(z, x1, x2, taps, norms, skip):
    b, l, w = z.shape
    nb = 2 * l // LANE
    hf = _hyena_spectrum(taps, 1.0 / norms)
    cst = _dft_consts(nb, nb // 2)
    for o, gate in enumerate((x1, x2)):
        z = _hyena_conv(z, gate, skip[o][None, :], hf, o * (w // LANE), cst)
    return z


def _dense_dft_consts(length):
    n = 2 * length
    h = length + 1
    hp = -(-h // 8) * 8
    k = np.arange(h)[:, None]
    pos = np.arange(n)[None, :]
    ang = 2 * np.pi * ((k * pos) % n) / n
    fwd = np.zeros((2 * hp, n))
    fwd[:h] = np.cos(ang)
    fwd[hp:hp + h] = -np.sin(ang)
    wk = np.where((k == 0) | (k == length), 1.0, 2.0) / n
    inv = np.zeros((length, 2 * hp))
    inv[:, :h] = (wk * np.cos(ang[:, :length])).T
    inv[:, hp:hp + h] = (-wk * np.sin(ang[:, :length])).T
    return dict(hp=hp, full=_np_split(fwd), fwd=_np_split(fwd[:, :length]), inv=_np_split(inv))


def _hyena_short_kernel(z_ref, x1_ref, x2_ref, k_ref, inv_ref, skip_ref, ffh_ref, ffl_ref, fh_ref, fl_ref,
                        fih_ref, fil_ref, o_ref, *, hp):
    z = z_ref[0]
    w = z.shape[1]
    for o, gate_ref in enumerate((x1_ref, x2_ref)):
        cols = slice(o * w, (o + 1) * w)
        hf = _cdot(ffh_ref[...], ffl_ref[...], k_ref[:, cols], passes=3) * inv_ref[:, cols]
        x = _cdot(fh_ref[...], fl_ref[...], z, passes=3)
        xr, xi, hr, hi = x[:hp], x[hp:], hf[:hp], hf[hp:]
        zz = jnp.concatenate([xr * hr - xi * hi, xr * hi + xi * hr], axis=0)
        conv = _cdot(fih_ref[...], fil_ref[...], zz, passes=3)
        z = gate_ref[0] * (conv + z * skip_ref[o:o + 1, :])
    o_ref[0] = z


def _hyena_short(z, x1, x2, taps, norms, skip):
    b, l, w = z.shape
    taps = jnp.moveaxis(taps, 0, 1).reshape(2 * l, -1)
    cst = _dense_dft_consts(l)
    consts = [*cst["full"], *cst["fwd"], *cst["inv"]]
    inv_norm = 1.0 / norms
    const = lambda a: pl.BlockSpec(a.shape, lambda bi: (0,) * a.ndim)
    tile = pl.BlockSpec((1, l, w), lambda bi: (bi, 0, 0))
    return pl.pallas_call(
        functools.partial(_hyena_short_kernel, hp=cst["hp"]),
        grid=(b,),
        in_specs=[tile, tile, tile, const(taps), const(inv_norm), const(skip)] + [const(a) for a in consts],
        out_specs=tile,
        out_shape=jax.ShapeDtypeStruct((b, l, w), F32),
        compiler_params=_cparams(("arbitrary",)),
        name="hyena_short",
    )(z, x1, x2, taps, inv_norm, skip, *consts)


def _rope_tables(rows, rot_dim, head_lanes, rope_off, identity_rows):
    half = rot_dim // 2
    n_freq = rot_dim // 4
    pos = np.arange(rows * GRID_W)
    inv_freq = ROPE_THETA ** (-np.arange(n_freq, dtype=np.float64) / n_freq)
    ang = np.concatenate([(pos // GRID_W)[:, None] * inv_freq, (pos % GRID_W)[:, None] * inv_freq], axis=-1)
    cos, sin = np.cos(ang), np.sin(ang)
    n = pos.shape[0]
    ct = np.ones((n, LANE))
    sa = np.zeros((n, LANE))
    sb = np.zeros((n, LANE))
    for h0 in range(0, LANE, head_lanes):
        lo = h0 + rope_off
        ct[:, lo:lo + half] = cos
        ct[:, lo + half:lo + rot_dim] = cos
        sa[:, lo:lo + half] = -sin
        sb[:, lo + half:lo + rot_dim] = sin
    ident = (np.ones((identity_rows, LANE)), np.zeros((identity_rows, LANE)), np.zeros((identity_rows, LANE)))
    lat = tuple(jnp.asarray(t, F32) for t in (ct, sa, sb))
    ctx = tuple(jnp.asarray(t, F32) for t in ident)
    return lat, ctx


def _pack_w_in(w):
    d = w.shape[0]
    o = 0
    mq = w[:, o:o + MLA_HEADS * (MLA_NOPE + MLA_ROPE)]; o += MLA_HEADS * (MLA_NOPE + MLA_ROPE)
    mckv = w[:, o:o + KV_RANK]; o += KV_RANK
    mkr = w[:, o:o + MLA_ROPE]; o += MLA_ROPE
    sq = w[:, o:o + SWA_HEADS * SWA_HEAD_DIM]; o += SWA_HEADS * SWA_HEAD_DIM
    sk = w[:, o:o + SWA_KV_HEADS * SWA_HEAD_DIM]; o += SWA_KV_HEADS * SWA_HEAD_DIM
    sv = w[:, o:o + SWA_KV_HEADS * SWA_HEAD_DIM]; o += SWA_KV_HEADS * SWA_HEAD_DIM
    hy = w[:, o:o + (HY_ORDER + 1) * HY_WIDTH]; o += (HY_ORDER + 1) * HY_WIDTH
    gt = w[:, o:]
    pad_q = HEAD_PAD - MLA_NOPE - MLA_ROPE
    mq = jnp.pad(mq.reshape(d, MLA_HEADS, MLA_NOPE + MLA_ROPE), ((0, 0), (0, 0), (0, pad_q))).reshape(d, -1)
    mkr = jnp.pad(mkr, ((0, 0), (MLA_NOPE, pad_q)))
    dup = lambda t: jnp.repeat(t.reshape(d, SWA_KV_HEADS, 1, SWA_HEAD_DIM), 2, axis=2).reshape(d, -1)
    return jnp.concatenate([mq, mckv, mkr, sq, dup(sk), dup(sv), hy, gt], axis=1).astype(BF16)


def _pack_w_kv(w):
    r = w.shape[0]
    w = w.reshape(r, MLA_HEADS, MLA_NOPE + MLA_V)
    k = jnp.pad(w[..., :MLA_NOPE], ((0, 0), (0, 0), (0, HEAD_PAD - MLA_NOPE))).reshape(r, -1)
    v = w[..., MLA_NOPE:].reshape(r, -1)
    return jnp.concatenate([k, v], axis=1).astype(BF16)


def kernel(x, c, ctx, c_ctx, w_mod, b_mod, norm_g, w_in, kv_norm_g, w_kv_up, swa_sink, hy_conv_w, hy_conv_b,
           hy_w1, hy_b1, hy_w2, hy_b2, hy_freq, hy_w3, hy_skip, w_branch, w_out, w_up, ffn_conv_w, ffn_conv_b,
           w_down):
    b, s, d = x.shape
    n_ctx = ctx.shape[1]
    depth = w_mod.shape[0]
    assert s % (2 * SWA_BLOCK) == 0 and s % GRID_W == 0, "latent length must tile into 128-token blocks / grid rows"
    assert n_ctx % SWA_BLOCK == 0, "context length must be a multiple of the 128-token block"
    assert d % LANE == 0 and w_in.shape[1] == d
    rows = s // GRID_W
    rope_m, rope_m_ctx = _rope_tables(rows, MLA_ROPE, HEAD_PAD, MLA_NOPE, n_ctx)
    rope_s, rope_s_ctx = _rope_tables(rows, SWA_HEAD_DIM, SWA_HEAD_DIM, 0, n_ctx)

    pad_rows = -(b + 1) % SUBLANES
    cvec = jnp.concatenate([c, c_ctx[None, :], jnp.zeros((pad_rows, d), F32)], axis=0)
    mod_all = _modulation(cvec, w_mod, b_mod)

    x_lat, x_ctx = x, ctx
    for l in range(depth):
        with_ctx = l < depth - 1
        m = mod_all[l].reshape(-1, 6, d)
        lat = [m[:b, k][:, None, :] for k in range(6)]
        cx = [jnp.broadcast_to(m[b, k][None, None, :], (b, 1, d)) for k in range(6)]
        ng = [norm_g[l, k][None, :] for k in range(4)]
        w_pack = _pack_w_in(w_in[l])
        wkv_pack = _pack_w_kv(w_kv_up[l])
        kvg = kv_norm_g[l][None, :]
        cw, cb = hy_conv_w[l], hy_conv_b[l][None, :]

        q, kk, vv, sq, sk, sv, z, x1, x2, gt = _in_proj(
            x_lat, lat[0], lat[1], ng[0], w_pack, wkv_pack, kvg, rope_m, rope_s, cw, cb)
        qc, kkc, vvc, sqc, skc, svc, zc, x1c, x2c, gtc = _in_proj(
            x_ctx, cx[0], cx[1], ng[0], w_pack, wkv_pack, kvg, rope_m_ctx, rope_s_ctx, cw, cb)

        y_a = _mla(q, kkc, vvc, kk, vv)
        y_b = _swa(swa_sink[l], sq, skc, svc, sk, sv)
        hy_mlp = (hy_w1[l], hy_b1[l], hy_w2[l], hy_b2[l], hy_freq[l], hy_w3[l])
        y_c = _hyena_long(z, x1, x2, *_hyena_filter_taps(s, *hy_mlp), hy_skip[l])

        wb = w_branch[l].astype(BF16)
        wo = w_out[l].astype(BF16)
        wu = w_up[l].astype(BF16)
        wd = w_down[l].astype(BF16)
        fcw, fcb = ffn_conv_w[l], ffn_conv_b[l][None, :]

        x_lat = _merge(x_lat, y_a, y_b, y_c, gt, wb, wo, ng[1], lat[2])
        x_lat = _ffn(x_lat, lat[3], lat[4], lat[5], ng[2], ng[3], wu, fcw, fcb, wd)

        if with_ctx:
            yc_a = _mla(qc, kkc, vvc)
            yc_b = _swa(swa_sink[l], sqc, skc, svc)
            yc_c = _hyena_short(zc, x1c, x2c, *_hyena_filter_taps(n_ctx, *hy_mlp), hy_skip[l])
            x_ctx = _merge(x_ctx, yc_a, yc_b, yc_c, gtc, wb, wo, ng[1], cx[2])
            x_ctx = _ffn(x_ctx, cx[3], cx[4], cx[5], ng[2], ng[3], wu, fcw, fcb, wd)
    return x_lat
```

```python
import functools
import math

import numpy as np
import jax
import jax.numpy as jnp
from jax import lax
from jax.experimental import pallas as pl
from jax.experimental.pallas import tpu as pltpu

F32 = jnp.float32
BF16 = jnp.bfloat16

GRID_W = 64
EPS = 1e-6
ROPE_THETA = 10000.0
NEG_INF = -1e30
MLA_HEADS = 8
MLA_NOPE = 64
MLA_ROPE = 32
MLA_V = 64
KV_RANK = 256
MLA_SCALE = (MLA_NOPE + MLA_ROPE) ** -0.5
LOG2E = math.log2(math.e)
SWA_HEADS = 8
SWA_KV_HEADS = 2
SWA_HEAD_DIM = 64
SWA_BLOCK = 128
SWA_SCALE = SWA_HEAD_DIM ** -0.5
HY_WIDTH = 512
HY_ORDER = 2
HY_BANDS = 16
HY_HIDDEN = 64
HY_TARGET = 1e-2
HY_FAST_PCT = 0.3
HY_SLOW_PCT = 1.5
N_BRANCH = 3
BRANCH_WIDTH = 512
LANE = 128
SUBLANES = 8
HEAD_PAD = 128

VMEM_LIMIT = 56 * 1024 * 1024
IN_PROJ_ROWS = 512
FFN_ROWS = 512
FFN_CHUNKS = 1
MERGE_ROWS = 1024

C_Q = 0
C_CKV = C_Q + MLA_HEADS * HEAD_PAD
C_KR = C_CKV + KV_RANK
C_SQ = C_KR + HEAD_PAD
C_SK = C_SQ + SWA_HEADS * SWA_HEAD_DIM
C_SV = C_SK + 2 * SWA_KV_HEADS * SWA_HEAD_DIM
C_HY = C_SV + 2 * SWA_KV_HEADS * SWA_HEAD_DIM
C_GT = C_HY + (HY_ORDER + 1) * HY_WIDTH


def _cparams(sem):
    return pltpu.CompilerParams(dimension_semantics=sem, vmem_limit_bytes=VMEM_LIMIT)


def _rms(xf, g):
    return xf * lax.rsqrt(jnp.mean(xf * xf, axis=-1, keepdims=True) + EPS) * g


def _rope(x, cos, sa, sb, half):
    return x * cos + pltpu.roll(x, LANE - half, 1) * sa + pltpu.roll(x, half, 1) * sb


def _halo_specs(tm, s, d):
    per = tm // SUBLANES
    last = s // SUBLANES - 1
    return (pl.BlockSpec((1, SUBLANES, d), lambda bi, i: (bi, jnp.maximum(i * per - 1, 0), 0)),
            pl.BlockSpec((1, SUBLANES, d), lambda bi, i: (bi, jnp.minimum((i + 1) * per, last), 0)))


def _halo_rows(p, tm, i, nt):
    prev_row = jnp.where(i > 0, p[tm + SUBLANES - 1:tm + SUBLANES, :], 0.0)
    next_row = jnp.where(i < nt - 1, p[tm + SUBLANES:tm + SUBLANES + 1, :], 0.0)
    return prev_row, next_row


def _dwconv3(p, prev_row, next_row, cw_ref, cb_ref):
    tm = p.shape[0]
    rows = lax.broadcasted_iota(jnp.int32, (tm, 1), 0)
    up = jnp.where(rows == 0, prev_row, pltpu.roll(p, 1, 0))
    dn = jnp.where(rows == tm - 1, next_row, pltpu.roll(p, tm - 1, 0))
    return up * cw_ref[0:1, :] + p * cw_ref[1:2, :] + dn * cw_ref[2:3, :] + cb_ref[...]


def _mod_kernel(c_ref, w_ref, b_ref, o_ref):
    c = c_ref[...]
    a = c * jax.nn.sigmoid(c)
    a_hi = a.astype(BF16)
    a_lo = (a - a_hi.astype(F32)).astype(BF16)
    w = w_ref[0]
    w_hi = w.astype(BF16)
    w_lo = (w - w_hi.astype(F32)).astype(BF16)
    acc = jnp.dot(a_hi, w_hi, preferred_element_type=F32)
    acc += jnp.dot(a_lo, w_hi, preferred_element_type=F32)
    acc += jnp.dot(a_hi, w_lo, preferred_element_type=F32)
    o_ref[0] = acc + b_ref[0]


def _modulation(cvec, w_mod, b_mod):
    depth, d, n = w_mod.shape
    rows = cvec.shape[0]
    tn = n // 4 if n % (4 * LANE) == 0 else n
    return pl.pallas_call(
        _mod_kernel,
        grid=(depth, n // tn),
        in_specs=[pl.BlockSpec((rows, d), lambda l, j: (0, 0)),
                  pl.BlockSpec((1, d, tn), lambda l, j: (l, 0, j)),
                  pl.BlockSpec((1, 1, tn), lambda l, j: (l, 0, j))],
        out_specs=pl.BlockSpec((1, rows, tn), lambda l, j: (l, 0, j)),
        out_shape=jax.ShapeDtypeStruct((depth, rows, n), F32),
        compiler_params=_cparams(("arbitrary", "arbitrary")),
        name="adaln_mod",
    )(cvec, w_mod, b_mod.reshape(depth, 1, n))


def _in_proj_kernel(x_ref, xp_ref, xn_ref, sh_ref, sc_ref, g_ref, w_ref, wkv_ref, kvg_ref,
                    cm_ref, sma_ref, smb_ref, cs_ref, ssa_ref, ssb_ref, cw_ref, cb_ref,
                    q_ref, kk_ref, vv_ref, sq_ref, sk_ref, sv_ref, z_ref, x1_ref, x2_ref, gt_ref):
    i = pl.program_id(1)
    nt = pl.num_programs(1)
    g = g_ref[...]
    sh = sh_ref[0]
    sc = sc_ref[0]

    def mod(xf):
        return (_rms(xf, g) * (1.0 + sc) + sh).astype(BF16)

    tm = x_ref.shape[1]
    hb_ext = mod(jnp.concatenate([x_ref[0], xp_ref[0], xn_ref[0]], axis=0))
    hb = hb_ext[:tm]

    def proj(lo, hi, lhs=hb):
        return jnp.dot(lhs, w_ref[:, lo:hi], preferred_element_type=F32)

    cm, sma, smb = cm_ref[...], sma_ref[...], smb_ref[...]
    cs, ssa, ssb = cs_ref[...], ssa_ref[...], ssb_ref[...]

    pq = proj(C_Q, C_CKV)
    for h in range(MLA_HEADS):
        xh = pq[:, HEAD_PAD * h:HEAD_PAD * (h + 1)]
        q_ref[0, :, HEAD_PAD * h:HEAD_PAD * (h + 1)] = (
            _rope(xh, cm, sma, smb, MLA_ROPE // 2) * (MLA_SCALE * LOG2E)).astype(BF16)

    ckv = proj(C_CKV, C_KR)
    cn = _rms(ckv, kvg_ref[...]).astype(BF16)
    kv = jnp.dot(cn, wkv_ref[...], preferred_element_type=F32)
    krr = _rope(proj(C_KR, C_SQ), cm, sma, smb, MLA_ROPE // 2)
    for h in range(MLA_HEADS):
        kk_ref[0, :, HEAD_PAD * h:HEAD_PAD * (h + 1)] = (
            kv[:, HEAD_PAD * h:HEAD_PAD * (h + 1)] + krr).astype(BF16)
    ones = jnp.ones((hb.shape[0], LANE), BF16)
    for j in range(MLA_HEADS // 2):
        v0 = MLA_HEADS * HEAD_PAD + LANE * j
        vv_ref[0, :, 2 * LANE * j:2 * LANE * j + LANE] = kv[:, v0:v0 + LANE].astype(BF16)
        vv_ref[0, :, 2 * LANE * j + LANE:2 * LANE * (j + 1)] = ones

    psq = proj(C_SQ, C_SK)
    for j in range(SWA_HEADS // 2):
        xh = psq[:, LANE * j:LANE * (j + 1)]
        sq_ref[0, :, LANE * j:LANE * (j + 1)] = (
            _rope(xh, cs, ssa, ssb, SWA_HEAD_DIM // 2) * (SWA_SCALE * LOG2E)).astype(BF16)
    psk = proj(C_SK, C_SV)
    for k in range(SWA_KV_HEADS):
        sk_ref[0, :, LANE * k:LANE * (k + 1)] = _rope(
            psk[:, LANE * k:LANE * (k + 1)], cs, ssa, ssb, SWA_HEAD_DIM // 2).astype(BF16)
    sv_ref[0] = proj(C_SV, C_HY).astype(BF16)

    ph = proj(C_HY, C_GT, hb_ext)
    pprev, pnext = _halo_rows(ph, tm, i, nt)
    u = _dwconv3(ph[:tm], pprev, pnext, cw_ref, cb_ref)
    z_ref[0] = u[:, :HY_WIDTH]
    x1_ref[0] = u[:, HY_WIDTH:2 * HY_WIDTH]
    x2_ref[0] = u[:, 2 * HY_WIDTH:]

    gt_ref[0] = jax.nn.sigmoid(proj(C_GT, w_ref.shape[1])).astype(BF16)


def _in_proj(x, sh, sc, g, w, wkv, kvg, rope_m, rope_s, cw, cb):
    b, s, d = x.shape
    tm = min(IN_PROJ_ROWS, s)
    nt = s // tm
    row = lambda width: pl.BlockSpec((1, tm, width), lambda bi, i: (bi, i, 0))
    const2 = lambda a: pl.BlockSpec(a.shape, lambda bi, i: (0, 0), pipeline_mode=pl.Buffered(1))
    tab = pl.BlockSpec((tm, LANE), lambda bi, i: (i, 0))
    in_specs = [
        row(d),
        *_halo_specs(tm, s, d),
        pl.BlockSpec((1, 1, d), lambda bi, i: (bi, 0, 0)),
        pl.BlockSpec((1, 1, d), lambda bi, i: (bi, 0, 0)),
        const2(g), const2(w), const2(wkv), const2(kvg),
        tab, tab, tab, tab, tab, tab,
        const2(cw), const2(cb),
    ]
    widths = [(MLA_HEADS * HEAD_PAD, BF16), (MLA_HEADS * HEAD_PAD, BF16), (MLA_HEADS * LANE, BF16),
              (SWA_HEADS * SWA_HEAD_DIM, BF16), (2 * LANE, BF16), (2 * LANE, BF16),
              (HY_WIDTH, F32), (HY_WIDTH, F32), (HY_WIDTH, F32), (N_BRANCH * d, BF16)]
    return pl.pallas_call(
        _in_proj_kernel,
        grid=(b, nt),
        in_specs=in_specs,
        out_specs=[row(wd) for wd, _ in widths],
        out_shape=[jax.ShapeDtypeStruct((b, s, wd), dt) for wd, dt in widths],
        compiler_params=_cparams(("arbitrary", "arbitrary")),
        name="in_proj",
    )(x, x, x, sh, sc, g, w, wkv, kvg, *rope_m, *rope_s, cw, cb)


def _mla_kernel(*refs, tk, n_lat, rb):
    if n_lat:
        q_ref, kc_ref, vc_ref, kl_ref, vl_ref, o_ref, m_scr, acc_scr, p_scr = refs
    else:
        q_ref, kc_ref, vc_ref, o_ref, m_scr, acc_scr = refs
    tq = q_ref.shape[1]
    nt = (((1,), (1,)), ((), ()))
    blocks = [(e, pl.ds(r * rb, rb)) for e in range(2) for r in range(tq // rb)]

    def scores(e, rows, k_ref, off, n):
        return lax.dot_general(q_ref[0, rows, HEAD_PAD * e:HEAD_PAD * (e + 1)],
                               k_ref[0, pl.ds(off, n), HEAD_PAD * e:HEAD_PAD * (e + 1)], nt,
                               preferred_element_type=F32)

    n_ctx = kc_ref.shape[1]
    for e, rows in blocks:
        s = scores(e, rows, kc_ref, 0, n_ctx)
        m_new = jnp.tile(jnp.max(s, axis=1, keepdims=True), (1, LANE))
        p = jnp.exp2(s - jnp.tile(m_new, (1, n_ctx // LANE))).astype(BF16)
        if n_lat:
            p_scr[e, rows, :n_ctx] = p
        else:
            acc_scr[e, rows, :] = jnp.dot(p, vc_ref[0], preferred_element_type=F32)
        m_scr[e, rows, :] = m_new

    if n_lat:
        def softmax_chunk(off, pv_off):
            for e, rows in blocks:
                s = scores(e, rows, kl_ref, off, tk)
                m_old = m_scr[e, rows, :]
                m_new = jnp.maximum(m_old, jnp.max(s, axis=1, keepdims=True))
                alpha = jnp.tile(jnp.exp2(m_old - m_new), (1, 2))
                if pv_off is None:
                    acc = jnp.dot(p_scr[e, rows, :n_ctx], vc_ref[0], preferred_element_type=F32)
                else:
                    acc = acc_scr[e, rows, :] + jnp.dot(p_scr[e, rows, :], vl_ref[0, pl.ds(pv_off, tk), :],
                                                        preferred_element_type=F32)
                acc_scr[e, rows, :] = alpha * acc
                p_scr[e, rows, :] = jnp.exp2(s - jnp.tile(m_new, (1, tk // LANE))).astype(BF16)
                m_scr[e, rows, :] = m_new

        softmax_chunk(0, None)

        def body(j, carry):
            off = pl.multiple_of(j * tk, tk)
            softmax_chunk(off, pl.multiple_of(off - tk, tk))
            return carry
        lax.fori_loop(1, n_lat, body, 0)
        last = (n_lat - 1) * tk
        for e, rows in blocks:
            acc_scr[e, rows, :] += jnp.dot(p_scr[e, rows, :], vl_ref[0, pl.ds(last, tk), :],
                                           preferred_element_type=F32)
    lane = lax.broadcasted_iota(jnp.int32, (tq, LANE), 1)
    o0 = acc_scr[0, :, :LANE] / acc_scr[0, :, LANE:]
    o1 = acc_scr[1, :, :LANE] / acc_scr[1, :, LANE:]
    o_ref[0] = jnp.where(lane < MLA_V, o0, o1).astype(o_ref.dtype)


def _mla(q, kc, vc, kl=None, vl=None, tq=4096, tk=512, rb=128):
    b, sq, _ = q.shape
    c = kc.shape[1]
    tq = min(tq, sq)
    rb = min(rb, tq)
    hp = MLA_HEADS // 2
    in_specs = [pl.BlockSpec((1, tq, 2 * HEAD_PAD), lambda bi, h, i: (bi, i, h)),
                pl.BlockSpec((1, c, 2 * HEAD_PAD), lambda bi, h, i: (bi, 0, h)),
                pl.BlockSpec((1, c, 2 * LANE), lambda bi, h, i: (bi, 0, h))]
    args = [q, kc, vc]
    n_lat = 0
    if kl is not None:
        s = kl.shape[1]
        tk = min(tk, s)
        n_lat = s // tk
        assert c <= tk, "the context chunk shares the probability buffer of a latent chunk"
        in_specs += [pl.BlockSpec((1, s, 2 * HEAD_PAD), lambda bi, h, i: (bi, 0, h)),
                     pl.BlockSpec((1, s, 2 * LANE), lambda bi, h, i: (bi, 0, h))]
        args += [kl, vl]
    return pl.pallas_call(
        functools.partial(_mla_kernel, tk=tk, n_lat=n_lat, rb=rb),
        grid=(b, hp, sq // tq),
        in_specs=in_specs,
        out_specs=pl.BlockSpec((1, tq, LANE), lambda bi, h, i: (bi, i, h)),
        out_shape=jax.ShapeDtypeStruct((b, sq, MLA_HEADS * MLA_V), BF16),
        scratch_shapes=[pltpu.VMEM((2, tq, LANE), F32), pltpu.VMEM((2, tq, 2 * LANE), F32)]
        + ([pltpu.VMEM((2, tq, tk), BF16)] if n_lat else []),
        compiler_params=_cparams(("arbitrary", "arbitrary", "arbitrary")),
        name="mla_attn",
    )(*args)


def _swa_kernel(*refs, band):
    if band:
        sink_ref, q_ref, kc_ref, vc_ref, kp_ref, k0_ref, kn_ref, vp_ref, v0_ref, vn_ref, o_ref = refs
        kband = jnp.concatenate([kp_ref[0], k0_ref[0], kn_ref[0]], axis=0)
        vband = jnp.concatenate([vp_ref[0], v0_ref[0], vn_ref[0]], axis=0)
    else:
        sink_ref, q_ref, kc_ref, vc_ref, o_ref = refs
    i = pl.program_id(1)
    nb = pl.num_programs(1)
    tq = q_ref.shape[1]
    nsub = tq // SWA_BLOCK
    n_ctx = kc_ref.shape[1]
    n_keys = n_ctx + (3 * SWA_BLOCK if band else 0)
    grp = SWA_HEADS // SWA_KV_HEADS
    rows_all = grp * SWA_BLOCK
    nt = (((1,), (1,)), ((), ()))
    lane = lax.broadcasted_iota(jnp.int32, (SWA_BLOCK, LANE), 1)
    low = lane < SWA_HEAD_DIM
    row = lax.broadcasted_iota(jnp.int32, (rows_all, 1), 0)
    if band:
        r_loc = lax.broadcasted_iota(jnp.int32, (rows_all, n_keys), 0) % SWA_BLOCK
        col = lax.broadcasted_iota(jnp.int32, (rows_all, n_keys), 1) - n_ctx
        prev_bias = jnp.where((col >= 0) & (col < SWA_BLOCK), NEG_INF, 0.0)
        next_bias = jnp.where(col >= 2 * SWA_BLOCK, NEG_INF, 0.0)
        base_bias = (jnp.where(r_loc > col, prev_bias, 0.0)
                     + jnp.where(col - 2 * SWA_BLOCK > r_loc, next_bias, 0.0))
    for r in range(nsub):
        rs = slice(r * SWA_BLOCK, (r + 1) * SWA_BLOCK)
        if band:
            bias = base_bias
            if r == 0:
                bias = jnp.minimum(bias, jnp.where(i == 0, prev_bias, 0.0))
            if r == nsub - 1:
                bias = jnp.minimum(bias, jnp.where(i == nb - 1, next_bias, 0.0))
            bs = slice(r * SWA_BLOCK, (r + 3) * SWA_BLOCK)
        for kvh in range(SWA_KV_HEADS):
            sl = slice(LANE * kvh, LANE * (kvh + 1))
            k, v = kc_ref[0, :, sl], vc_ref[0, :, sl]
            if band:
                k = jnp.concatenate([k, kband[bs, sl]], axis=0)
                v = jnp.concatenate([v, vband[bs, sl]], axis=0)
            v = jnp.concatenate([v, jnp.ones_like(v)], axis=1)
            parts, sink = [], None
            for t in range(grp // 2):
                pair = kvh * (grp // 2) + t
                qp = q_ref[0, rs, LANE * pair:LANE * (pair + 1)]
                parts += [jnp.where(low, qp, jnp.zeros_like(qp)), jnp.where(low, jnp.zeros_like(qp), qp)]
            for h in range(grp):
                sh = sink_ref[kvh * grp + h] * LOG2E
                sink = sh if sink is None else jnp.where(row < h * SWA_BLOCK, sink, sh)
            s = lax.dot_general(jnp.concatenate(parts, axis=0), k, nt, preferred_element_type=F32)
            if band:
                s = s + bias
            m = jnp.maximum(jnp.max(s, axis=1, keepdims=True), sink)
            p = jnp.exp2(s - m).astype(BF16)
            acc = jnp.dot(p, v, preferred_element_type=F32)
            o = acc[:, :LANE] / (acc[:, LANE:] + jnp.exp2(sink - m))
            for t in range(grp // 2):
                pair = kvh * (grp // 2) + t
                lo = o[(2 * t) * SWA_BLOCK:(2 * t + 1) * SWA_BLOCK]
                hi = o[(2 * t + 1) * SWA_BLOCK:(2 * t + 2) * SWA_BLOCK]
                o_ref[0, rs, LANE * pair:LANE * (pair + 1)] = jnp.where(low, lo, hi).astype(o_ref.dtype)


def _swa(sink, q, kc, vc, k=None, v=None, tq=1024):
    b, sq, _ = q.shape
    c = kc.shape[1]
    tq = min(tq, sq)
    nb = sq // tq
    band = k is not None
    blk = lambda rows, width, f: pl.BlockSpec((1, rows, width), f)
    in_specs = [pl.BlockSpec(memory_space=pltpu.SMEM),
                blk(tq, SWA_HEADS * SWA_HEAD_DIM, lambda bi, i: (bi, i, 0)),
                pl.BlockSpec((1, c, 2 * LANE), lambda bi, i: (bi, 0, 0)),
                pl.BlockSpec((1, c, 2 * LANE), lambda bi, i: (bi, 0, 0))]
    args = [sink, q, kc, vc]
    if band:
        per = tq // SWA_BLOCK
        prev = lambda bi, i: (bi, jnp.maximum(i * per - 1, 0), 0)
        cur = lambda bi, i: (bi, i, 0)
        nxt = lambda bi, i: (bi, jnp.minimum((i + 1) * per, sq // SWA_BLOCK - 1), 0)
        in_specs += [blk(SWA_BLOCK, 2 * LANE, prev), blk(tq, 2 * LANE, cur), blk(SWA_BLOCK, 2 * LANE, nxt),
                     blk(SWA_BLOCK, 2 * LANE, prev), blk(tq, 2 * LANE, cur), blk(SWA_BLOCK, 2 * LANE, nxt)]
        args += [k, k, k, v, v, v]
    return pl.pallas_call(
        functools.partial(_swa_kernel, band=band),
        grid=(b, nb),
        in_specs=in_specs,
        out_specs=blk(tq, SWA_HEADS * SWA_HEAD_DIM, lambda bi, i: (bi, i, 0)),
        out_shape=jax.ShapeDtypeStruct((b, sq, SWA_HEADS * SWA_HEAD_DIM), BF16),
        compiler_params=_cparams(("arbitrary", "arbitrary")),
        name="swa_attn",
    )(*args)


def _merge_kernel(x_ref, ya_ref, yb_ref, yc_ref, gt_ref, wb_ref, wo_ref, ng_ref, g1_ref, o_ref):
    d = x_ref.shape[2]
    merged = None
    for k, y_ref in enumerate((ya_ref, yb_ref, yc_ref)):
        t = jnp.dot(y_ref[0].astype(BF16), wb_ref[k], preferred_element_type=F32)
        t = gt_ref[0, :, d * k:d * (k + 1)].astype(F32) * t
        merged = t if merged is None else merged + t
    y = jnp.dot(merged.astype(BF16), wo_ref[...], preferred_element_type=F32)
    o_ref[0] = x_ref[0] + g1_ref[0] * _rms(y, ng_ref[...])


def _merge(x, ya, yb, yc, gt, wb, wo, ng, g1):
    b, s, d = x.shape
    tm = min(MERGE_ROWS, s)
    row = lambda width: pl.BlockSpec((1, tm, width), lambda bi, i: (bi, i, 0))
    return pl.pallas_call(
        _merge_kernel,
        grid=(b, s // tm),
        in_specs=[row(d), row(BRANCH_WIDTH), row(BRANCH_WIDTH), row(BRANCH_WIDTH), row(N_BRANCH * d),
                  pl.BlockSpec(wb.shape, lambda bi, i: (0, 0, 0)),
                  pl.BlockSpec(wo.shape, lambda bi, i: (0, 0)),
                  pl.BlockSpec(ng.shape, lambda bi, i: (0, 0)),
                  pl.BlockSpec((1, 1, d), lambda bi, i: (bi, 0, 0))],
        out_specs=row(d),
        out_shape=jax.ShapeDtypeStruct((b, s, d), F32),
        compiler_params=_cparams(("arbitrary", "arbitrary")),
        name="merge",
    )(x, ya, yb, yc, gt, wb, wo, ng, g1)


def _ffn_kernel(x_ref, xp_ref, xn_ref, sh_ref, sc_ref, g2_ref, ng_in_ref, ng_out_ref,
                wu_ref, cw_ref, cb_ref, wd_ref, o_ref, *, chunk):
    i = pl.program_id(1)
    nt = pl.num_programs(1)
    g = ng_in_ref[...]
    sh = sh_ref[0]
    sc = sc_ref[0]

    def mod(xf):
        return (_rms(xf, g) * (1.0 + sc) + sh).astype(BF16)

    x = x_ref[0]
    tm = x.shape[0]
    hb = mod(jnp.concatenate([x, xp_ref[0], xn_ref[0]], axis=0))
    dff = wd_ref.shape[0]
    f = None
    for c0 in range(0, dff, chunk):
        halves = []
        for off in (c0, dff + c0):
            p = jnp.dot(hb, wu_ref[:, off:off + chunk], preferred_element_type=F32)
            pprev, pnext = _halo_rows(p, tm, i, nt)
            halves.append(_dwconv3(p[:tm], pprev, pnext, cw_ref.at[:, off:off + chunk],
                                   cb_ref.at[:, off:off + chunk]))
        a, bb = halves
        act = (a * jax.nn.sigmoid(a) * bb).astype(BF16)
        t = jnp.dot(act, wd_ref[c0:c0 + chunk, :], preferred_element_type=F32)
        f = t if f is None else f + t
    o_ref[0] = x + g2_ref[0] * _rms(f, ng_out_ref[...])


def _ffn(x, sh, sc, g2, ng_in, ng_out, wu, cw, cb, wd):
    b, s, d = x.shape
    tm = min(FFN_ROWS, s)
    dff = wd.shape[0]
    chunk = dff // FFN_CHUNKS if dff % (FFN_CHUNKS * LANE) == 0 else dff
    row = pl.BlockSpec((1, tm, d), lambda bi, i: (bi, i, 0))
    vec = pl.BlockSpec((1, 1, d), lambda bi, i: (bi, 0, 0))
    const2 = lambda a: pl.BlockSpec(a.shape, lambda bi, i: (0, 0), pipeline_mode=pl.Buffered(1))
    return pl.pallas_call(
        functools.partial(_ffn_kernel, chunk=chunk),
        grid=(b, s // tm),
        in_specs=[row, *_halo_specs(tm, s, d),
                  vec, vec, vec, const2(ng_in), const2(ng_out),
                  const2(wu), const2(cw), const2(cb), const2(wd)],
        out_specs=row,
        out_shape=jax.ShapeDtypeStruct((b, s, d), F32),
        compiler_params=_cparams(("arbitrary", "arbitrary")),
        name="conv_ffn",
    )(x, x, x, sh, sc, g2, ng_in, ng_out, wu, cw, cb, wd)


HY_PASSES = 1
SPEC_PASSES = 1
HY_UNROLL = 8
HY_UNROLL2 = 33


def _split(x):
    hi = x.astype(BF16)
    return hi, (x - hi.astype(F32)).astype(BF16)


def _cdot(c_hi, c_lo, x, passes=HY_PASSES):
    x_hi, x_lo = _split(x)
    acc = jnp.dot(c_hi, x_hi, preferred_element_type=F32)
    if passes > 1:
        acc = acc + jnp.dot(c_lo, x_hi, preferred_element_type=F32)
        acc = acc + jnp.dot(c_hi, x_lo, preferred_element_type=F32)
    return acc


def _np_split(m):
    m = jnp.asarray(m, F32)
    hi = m.astype(BF16)
    return hi, (m - hi.astype(F32)).astype(BF16)


def _dft_consts(n_blocks, nonzero_blocks):
    nb = n_blocks
    n = LANE * nb
    dh = nb // 2 + 1
    dhp = dh + dh % 2
    d = np.arange(dh)[:, None]
    b = np.arange(nonzero_blocks)[None, :]
    ang = 2 * np.pi * ((d * b) % nb) / nb
    f1 = np.zeros((2 * dhp, nonzero_blocks))
    f1[:dh] = np.cos(ang)
    f1[dhp:dhp + dh] = -np.sin(ang)
    a = np.arange(LANE)
    ang2 = 2 * np.pi * ((a[:, None] * a[None, :]) % LANE) / LANE
    ar, ai = np.cos(ang2), -np.sin(ang2)
    f2 = np.block([[ar, -ai], [ai, ar]])
    f2inv = np.block([[ar, ai], [-ai, ar]])
    bo = np.arange(nb // 2)[:, None]
    do = np.arange(dh)[None, :]
    wd = np.where((do == 0) | (do == nb // 2), 1.0, 2.0) / n
    ang3 = 2 * np.pi * ((bo * do) % nb) / nb
    f3 = np.zeros((nb // 2, 2 * dhp))
    f3[:, :dh] = wd * np.cos(ang3)
    f3[:, dhp:dhp + dh] = -wd * np.sin(ang3)
    step_ang = 2 * np.pi * a / n
    tw_step = np.stack([np.broadcast_to(np.cos(step_ang)[:, None], (LANE, LANE)),
                        np.broadcast_to(-np.sin(step_ang)[:, None], (LANE, LANE))])
    eye = np.eye(SUBLANES)
    return dict(dh=dh, dhp=dhp, f1=_np_split(np.kron(f1, eye)), f2=_np_split(f2), f2inv=_np_split(f2inv),
                f3=_np_split(np.kron(f3, eye)), tw_step=jnp.asarray(tw_step, F32))


def _stage1(x_ref, f1_hi, f1_lo, g_scr, passes):
    n_in, n_out = x_ref.shape[0], g_scr.shape[0]

    def body(t, carry):
        r0 = pl.ds(pl.multiple_of(2 * t * SUBLANES, SUBLANES), SUBLANES)
        r1 = pl.ds(pl.multiple_of((2 * t + 1) * SUBLANES, SUBLANES), SUBLANES)
        xa = jnp.concatenate([x_ref[:, r0, :].reshape(n_in * SUBLANES, LANE),
                              x_ref[:, r1, :].reshape(n_in * SUBLANES, LANE)], axis=1)
        g = _cdot(f1_hi, f1_lo, xa, passes)
        g_scr[:, r0, :] = g[:, :LANE].reshape(n_out, SUBLANES, LANE)
        g_scr[:, r1, :] = g[:, LANE:].reshape(n_out, SUBLANES, LANE)
        return carry
    lax.fori_loop(0, LANE // (2 * SUBLANES), body, 0, unroll=HY_UNROLL)


def _twiddle_init(tw_scr):
    tw_scr[0] = jnp.ones((LANE, LANE), F32)
    tw_scr[1] = jnp.zeros((LANE, LANE), F32)


def _twiddle_next(twr, twi, step_ref):
    sr, si = step_ref[0], step_ref[1]
    return twr * sr - twi * si, twr * si + twi * sr


def _twiddle_pair(tw_scr, step_ref):
    t0r, t0i = tw_scr[0], tw_scr[1]
    t1r, t1i = _twiddle_next(t0r, t0i, step_ref)
    t2r, t2i = _twiddle_next(t1r, t1i, step_ref)
    tw_scr[0] = t2r
    tw_scr[1] = t2i
    return jnp.concatenate([t0r, t1r], axis=1), jnp.concatenate([t0i, t1i], axis=1)


def _spectrum_kernel(k_ref, inv_ref, f1h_ref, f1l_ref, f2h_ref, f2l_ref, step_ref, hf_ref, g_scr, tw_scr,
                     *, dh, dhp):
    _stage1(k_ref, f1h_ref[...], f1l_ref[...], g_scr, SPEC_PASSES)
    _twiddle_init(tw_scr)
    inv = inv_ref[...]

    def body(h, carry):
        d = 2 * h
        gr = jnp.concatenate([g_scr[d], g_scr[d + 1]], axis=1)
        gi = jnp.concatenate([g_scr[dhp + d], g_scr[dhp + d + 1]], axis=1)
        twr, twi = _twiddle_pair(tw_scr, step_ref)
        t = jnp.concatenate([gr * twr - gi * twi, gr * twi + gi * twr], axis=0)
        y = _cdot(f2h_ref[...], f2l_ref[...], t, passes=SPEC_PASSES)
        hf_ref[0, d] = y[:LANE, :LANE] * inv
        hf_ref[0, d + 1] = y[:LANE, LANE:] * inv
        hf_ref[1, d] = y[LANE:, :LANE] * inv
        hf_ref[1, d + 1] = y[LANE:, LANE:] * inv
        return carry
    lax.fori_loop(0, dhp // 2, body, 0, unroll=min(HY_UNROLL2, dhp // 2))


def _hyena_spectrum(kern, inv_norm):
    tiles, n, _ = kern.shape
    c = tiles * LANE
    nb = n // LANE
    cst = _dft_consts(nb, nb)
    dh, dhp = cst["dh"], cst["dhp"]
    const = lambda a: pl.BlockSpec(a.shape, lambda j: (0,) * a.ndim, pipeline_mode=pl.Buffered(1))
    consts = [*cst["f1"], *cst["f2"], cst["tw_step"]]
    return pl.pallas_call(
        functools.partial(_spectrum_kernel, dh=dh, dhp=dhp),
        grid=(c // LANE,),
        in_specs=[pl.BlockSpec((None, nb, LANE, LANE), lambda j: (j, 0, 0, 0)),
                  pl.BlockSpec((1, LANE), lambda j: (0, j))]
        + [const(a) for a in consts],
        out_specs=pl.BlockSpec((None, 2, dhp, LANE, LANE), lambda j: (j, 0, 0, 0, 0)),
        out_shape=jax.ShapeDtypeStruct((c // LANE, 2, dhp, LANE, LANE), F32),
        scratch_shapes=[pltpu.VMEM((2 * dhp, LANE, LANE), F32), pltpu.VMEM((2, LANE, LANE), F32)],
        compiler_params=_cparams(("arbitrary",)),
        name="hyena_spectrum",
    )(kern.reshape(tiles, nb, LANE, LANE), inv_norm, *consts)


def _conv_kernel(z_ref, gate_ref, skip_ref, hf_ref, f1h_ref, f1l_ref, f2h_ref, f2l_ref, f2ih_ref, f2il_ref,
                 f3h_ref, f3l_ref, step_ref, o_ref, g_scr, tw_scr, *, dh, dhp):
    nbh = z_ref.shape[1]
    _stage1(z_ref.at[0], f1h_ref[...], f1l_ref[...], g_scr, HY_PASSES)
    _twiddle_init(tw_scr)

    def body(h, carry):
        d = 2 * h
        gr = jnp.concatenate([g_scr[d], g_scr[d + 1]], axis=1)
        gi = jnp.concatenate([g_scr[dhp + d], g_scr[dhp + d + 1]], axis=1)
        twr, twi = _twiddle_pair(tw_scr, step_ref)
        t = jnp.concatenate([gr * twr - gi * twi, gr * twi + gi * twr], axis=0)
        y = _cdot(f2h_ref[...], f2l_ref[...], t)
        yr, yi = y[:LANE], y[LANE:]
        hr = jnp.concatenate([hf_ref[0, d], hf_ref[0, d + 1]], axis=1)
        hi = jnp.concatenate([hf_ref[1, d], hf_ref[1, d + 1]], axis=1)
        zz = jnp.concatenate([yr * hr - yi * hi, yr * hi + yi * hr], axis=0)
        u = _cdot(f2ih_ref[...], f2il_ref[...], zz)
        ur, ui = u[:LANE], u[LANE:]
        vr = ur * twr + ui * twi
        vi = ui * twr - ur * twi
        g_scr[d] = vr[:, :LANE]
        g_scr[d + 1] = vr[:, LANE:]
        g_scr[dhp + d] = vi[:, :LANE]
        g_scr[dhp + d + 1] = vi[:, LANE:]
        return carry
    lax.fori_loop(0, dhp // 2, body, 0, unroll=min(HY_UNROLL2, dhp // 2))

    skip = skip_ref[...]
    f3h, f3l = f3h_ref[...], f3l_ref[...]

    def out_body(t, carry):
        r0 = pl.ds(pl.multiple_of(2 * t * SUBLANES, SUBLANES), SUBLANES)
        r1 = pl.ds(pl.multiple_of((2 * t + 1) * SUBLANES, SUBLANES), SUBLANES)
        ga = jnp.concatenate([g_scr[:, r0, :].reshape(2 * dhp * SUBLANES, LANE),
                              g_scr[:, r1, :].reshape(2 * dhp * SUBLANES, LANE)], axis=1)
        conv = _cdot(f3h, f3l, ga)
        for rows, c in ((r0, conv[:, :LANE]), (r1, conv[:, LANE:])):
            o_ref[0, :, rows, :] = gate_ref[0, :, rows, :] * (
                c.reshape(nbh, SUBLANES, LANE) + z_ref[0, :, rows, :] * skip)
        return carry
    lax.fori_loop(0, LANE // (2 * SUBLANES), out_body, 0, unroll=HY_UNROLL)


def _hyena_conv(z, gate, skip, hf, tile0, cst):
    b, l, w = z.shape
    nbh = l // LANE
    dh, dhp = cst["dh"], cst["dhp"]
    consts = [*cst["f1"], *cst["f2"], *cst["f2inv"], *cst["f3"], cst["tw_step"]]
    const = lambda a: pl.BlockSpec(a.shape, lambda j, bi: (0,) * a.ndim, pipeline_mode=pl.Buffered(1))
    tile = pl.BlockSpec((1, nbh, LANE, LANE), lambda j, bi: (bi, 0, 0, j))
    z, gate = z.reshape(b, nbh, LANE, w), gate.reshape(b, nbh, LANE, w)
    return pl.pallas_call(
        functools.partial(_conv_kernel, dh=dh, dhp=dhp),
        grid=(w // LANE, b),
        in_specs=[tile, tile, pl.BlockSpec((1, LANE), lambda j, bi: (0, j)),
                  pl.BlockSpec((None, 2, dhp, LANE, LANE), lambda j, bi: (j + tile0, 0, 0, 0, 0),
                               pipeline_mode=pl.Buffered(1))]
        + [const(a) for a in consts],
        out_specs=tile,
        out_shape=jax.ShapeDtypeStruct((b, nbh, LANE, w), F32),
        scratch_shapes=[pltpu.VMEM((2 * dhp, LANE, LANE), F32), pltpu.VMEM((2, LANE, LANE), F32)],
        compiler_params=_cparams(("arbitrary", "arbitrary")),
        name="hyena_conv",
    )(z, gate, skip, hf, *consts).reshape(b, l, w)


def _filter_mlp_kernel(ft_ref, w1_ref, b1_ref, w2_ref, b2_ref, fr_ref, w3_ref, dl_ref, k_ref, s_ref):
    i = pl.program_id(0)
    ft = ft_ref[...]

    def mm(x, w):
        x_hi, x_lo = _split(x)
        w_hi, w_lo = _split(w)
        return (jnp.dot(x_hi, w_hi, preferred_element_type=F32) + jnp.dot(x_lo, w_hi, preferred_element_type=F32)
                + jnp.dot(x_hi, w_lo, preferred_element_type=F32))

    half = ft.shape[0] // 2
    cw = dl_ref.shape[1]
    hid = jnp.sin(fr_ref[0:1, :] * (mm(jnp.concatenate([ft[:half], ft[half:]], axis=1), w1_ref[...]) + b1_ref[...]))
    hid = jnp.sin(fr_ref[1:2, :] * (mm(hid, w2_ref[...]) + b2_ref[...]))
    h2 = mm(hid, w3_ref[0])
    k = jnp.concatenate([h2[:, :cw], h2[:, cw:]], axis=0)
    k = k * jnp.exp(-ft[:, 0:1] * dl_ref[...]) * ft[:, LANE - 1:LANE]
    for j in range(k_ref.shape[0]):
        k_ref[j] = k[:, LANE * j:LANE * (j + 1)]

    @pl.when(i == 0)
    def _():
        s_ref[...] = jnp.zeros(s_ref.shape, F32)
    s_ref[...] += jnp.sum(jnp.abs(k), axis=0, keepdims=True)


def _hyena_filter_taps(n_tokens, w1, b1, w2, b2, freq, w3):
    L = n_tokens
    n = 2 * L
    idx = jnp.arange(n)
    m = jnp.where(idx < L, idx, n - idx).astype(F32)
    t = m / max(L - 1, 1)
    bands = jnp.linspace(1e-4, HY_BANDS - 1, HY_BANDS, dtype=F32)
    ang = (2.0 * math.pi / L) * m[:, None] * bands
    n_feat = 2 * HY_BANDS + 1
    feats = jnp.concatenate([t[:, None], jnp.cos(ang), -jnp.sin(ang),
                             jnp.zeros((n, LANE - n_feat - 1), F32),
                             (idx != L).astype(F32)[:, None]], axis=-1)
    two = lambda a: jnp.concatenate([a, a], axis=-1)
    bdiag = lambda a: jnp.concatenate([jnp.pad(a, ((0, 0), (0, a.shape[1]))), jnp.pad(a, ((0, 0), (a.shape[1], 0)))], axis=0)
    w1p = bdiag(jnp.pad(w1, ((0, LANE - n_feat), (0, 0))))
    cw = HY_ORDER * HY_WIDTH
    w3d = jnp.moveaxis(w3.reshape(HY_HIDDEN, 2, cw), 1, 0)
    w3d = jnp.stack([bdiag(w3d[0]), bdiag(w3d[1])])
    w2, freq = bdiag(w2), two(freq)
    deltas = jnp.abs(jnp.linspace(math.log(HY_TARGET) / HY_SLOW_PCT, math.log(HY_TARGET) / HY_FAST_PCT,
                                  HY_WIDTH, dtype=F32))
    dl = jnp.tile(deltas, HY_ORDER)[None, :]
    tr = min(1024, L)
    nt = n // tr
    const = lambda a: pl.BlockSpec(a.shape, lambda i: (0,) * a.ndim)
    b1r, b2r = two(b1)[None, :], two(b2)[None, :]
    return pl.pallas_call(
        _filter_mlp_kernel,
        grid=(nt,),
        in_specs=[pl.BlockSpec((tr, LANE), lambda i: (i, 0)), const(w1p), const(b1r), const(w2), const(b2r),
                  const(freq), pl.BlockSpec((1, 2 * HY_HIDDEN, 2 * cw), lambda i: (i // (nt // 2), 0, 0)), const(dl)],
        out_specs=[pl.BlockSpec((cw // LANE, tr, LANE), lambda i: (0, i, 0)), pl.BlockSpec((1, cw), lambda i: (0, 0))],
        out_shape=[jax.ShapeDtypeStruct((cw // LANE, n, LANE), F32), jax.ShapeDtypeStruct((1, cw), F32)],
        compiler_params=_cparams(("arbitrary",)),
        name="hyena_filter_mlp",
    )(feats, w1p, b1r, w2, b2r, freq, w3d, dl)


def _hyena_long(z, x1, x2, taps, norms, skip):
    b, l, w = z.shape
    nb = 2 * l // LANE
    hf = _hyena_spectrum(taps, 1.0 / norms)
    cst = _dft_consts(nb, nb // 2)
    for o, gate in enumerate((x1, x2)):
        z = _hyena_conv(z, gate, skip[o][None, :], hf, o * (w // LANE), cst)
    return z


def _dense_dft_consts(length):
    n = 2 * length
    h = length + 1
    hp = -(-h // 8) * 8
    k = np.arange(h)[:, None]
    pos = np.arange(n)[None, :]
    ang = 2 * np.pi * ((k * pos) % n) / n
    fwd = np.zeros((2 * hp, n))
    fwd[:h] = np.cos(ang)
    fwd[hp:hp + h] = -np.sin(ang)
    wk = np.where((k == 0) | (k == length), 1.0, 2.0) / n
    inv = np.zeros((length, 2 * hp))
    inv[:, :h] = (wk * np.cos(ang[:, :length])).T
    inv[:, hp:hp + h] = (-wk * np.sin(ang[:, :length])).T
    return dict(hp=hp, full=_np_split(fwd), fwd=_np_split(fwd[:, :length]), inv=_np_split(inv))


def _hyena_short_kernel(z_ref, x1_ref, x2_ref, k_ref, inv_ref, skip_ref, ffh_ref, ffl_ref, fh_ref, fl_ref,
                        fih_ref, fil_ref, o_ref, *, hp):
    z = z_ref[0]
    w = z.shape[1]
    for o, gate_ref in enumerate((x1_ref, x2_ref)):
        cols = slice(o * w, (o + 1) * w)
        hf = _cdot(ffh_ref[...], ffl_ref[...], k_ref[:, cols], passes=3) * inv_ref[:, cols]
        x = _cdot(fh_ref[...], fl_ref[...], z, passes=3)
        xr, xi, hr, hi = x[:hp], x[hp:], hf[:hp], hf[hp:]
        zz = jnp.concatenate([xr * hr - xi * hi, xr * hi + xi * hr], axis=0)
        conv = _cdot(fih_ref[...], fil_ref[...], zz, passes=3)
        z = gate_ref[0] * (conv + z * skip_ref[o:o + 1, :])
    o_ref[0] = z


def _hyena_short(z, x1, x2, taps, norms, skip):
    b, l, w = z.shape
    taps = jnp.moveaxis(taps, 0, 1).reshape(2 * l, -1)
    cst = _dense_dft_consts(l)
    consts = [*cst["full"], *cst["fwd"], *cst["inv"]]
    inv_norm = 1.0 / norms
    const = lambda a: pl.BlockSpec(a.shape, lambda bi: (0,) * a.ndim)
    tile = pl.BlockSpec((1, l, w), lambda bi: (bi, 0, 0))
    return pl.pallas_call(
        functools.partial(_hyena_short_kernel, hp=cst["hp"]),
        grid=(b,),
        in_specs=[tile, tile, tile, const(taps), const(inv_norm), const(skip)] + [const(a) for a in consts],
        out_specs=tile,
        out_shape=jax.ShapeDtypeStruct((b, l, w), F32),
        compiler_params=_cparams(("arbitrary",)),
        name="hyena_short",
    )(z, x1, x2, taps, inv_norm, skip, *consts)


def _rope_tables(rows, rot_dim, head_lanes, rope_off, identity_rows):
    half = rot_dim // 2
    n_freq = rot_dim // 4
    pos = np.arange(rows * GRID_W)
    inv_freq = ROPE_THETA ** (-np.arange(n_freq, dtype=np.float64) / n_freq)
    ang = np.concatenate([(pos // GRID_W)[:, None] * inv_freq, (pos % GRID_W)[:, None] * inv_freq], axis=-1)
    cos, sin = np.cos(ang), np.sin(ang)
    n = pos.shape[0]
    ct = np.ones((n, LANE))
    sa = np.zeros((n, LANE))
    sb = np.zeros((n, LANE))
    for h0 in range(0, LANE, head_lanes):
        lo = h0 + rope_off
        ct[:, lo:lo + half] = cos
        ct[:, lo + half:lo + rot_dim] = cos
        sa[:, lo:lo + half] = -sin
        sb[:, lo + half:lo + rot_dim] = sin
    ident = (np.ones((identity_rows, LANE)), np.zeros((identity_rows, LANE)), np.zeros((identity_rows, LANE)))
    lat = tuple(jnp.asarray(t, F32) for t in (ct, sa, sb))
    ctx = tuple(jnp.asarray(t, F32) for t in ident)
    return lat, ctx


def _pack_w_in(w):
    d = w.shape[0]
    o = 0
    mq = w[:, o:o + MLA_HEADS * (MLA_NOPE + MLA_ROPE)]; o += MLA_HEADS * (MLA_NOPE + MLA_ROPE)
    mckv = w[:, o:o + KV_RANK]; o += KV_RANK
    mkr = w[:, o:o + MLA_ROPE]; o += MLA_ROPE
    sq = w[:, o:o + SWA_HEADS * SWA_HEAD_DIM]; o += SWA_HEADS * SWA_HEAD_DIM
    sk = w[:, o:o + SWA_KV_HEADS * SWA_HEAD_DIM]; o += SWA_KV_HEADS * SWA_HEAD_DIM
    sv = w[:, o:o + SWA_KV_HEADS * SWA_HEAD_DIM]; o += SWA_KV_HEADS * SWA_HEAD_DIM
    hy = w[:, o:o + (HY_ORDER + 1) * HY_WIDTH]; o += (HY_ORDER + 1) * HY_WIDTH
    gt = w[:, o:]
    pad_q = HEAD_PAD - MLA_NOPE - MLA_ROPE
    mq = jnp.pad(mq.reshape(d, MLA_HEADS, MLA_NOPE + MLA_ROPE), ((0, 0), (0, 0), (0, pad_q))).reshape(d, -1)
    mkr = jnp.pad(mkr, ((0, 0), (MLA_NOPE, pad_q)))
    dup = lambda t: jnp.repeat(t.reshape(d, SWA_KV_HEADS, 1, SWA_HEAD_DIM), 2, axis=2).reshape(d, -1)
    return jnp.concatenate([mq, mckv, mkr, sq, dup(sk), dup(sv), hy, gt], axis=1).astype(BF16)


def _pack_w_kv(w):
    r = w.shape[0]
    w = w.reshape(r, MLA_HEADS, MLA_NOPE + MLA_V)
    k = jnp.pad(w[..., :MLA_NOPE], ((0, 0), (0, 0), (0, HEAD_PAD - MLA_NOPE))).reshape(r, -1)
    v = w[..., MLA_NOPE:].reshape(r, -1)
    return jnp.concatenate([k, v], axis=1).astype(BF16)


def kernel(x, c, ctx, c_ctx, w_mod, b_mod, norm_g, w_in, kv_norm_g, w_kv_up, swa_sink, hy_conv_w, hy_conv_b,
           hy_w1, hy_b1, hy_w2, hy_b2, hy_freq, hy_w3, hy_skip, w_branch, w_out, w_up, ffn_conv_w, ffn_conv_b,
           w_down):
    b, s, d = x.shape
    n_ctx = ctx.shape[1]
    depth = w_mod.shape[0]
    assert s % (2 * SWA_BLOCK) == 0 and s % GRID_W == 0, "latent length must tile into 128-token blocks / grid rows"
    assert n_ctx % SWA_BLOCK == 0, "context length must be a multiple of the 128-token block"
    assert d % LANE == 0 and w_in.shape[1] == d
    rows = s // GRID_W
    rope_m, rope_m_ctx = _rope_tables(rows, MLA_ROPE, HEAD_PAD, MLA_NOPE, n_ctx)
    rope_s, rope_s_ctx = _rope_tables(rows, SWA_HEAD_DIM, SWA_HEAD_DIM, 0, n_ctx)

    pad_rows = -(b + 1) % SUBLANES
    cvec = jnp.concatenate([c, c_ctx[None, :], jnp.zeros((pad_rows, d), F32)], axis=0)
    mod_all = _modulation(cvec, w_mod, b_mod)

    x_lat, x_ctx = x, ctx
    for l in range(depth):
        with_ctx = l < depth - 1
        m = mod_all[l].reshape(-1, 6, d)
        lat = [m[:b, k][:, None, :] for k in range(6)]
        cx = [jnp.broadcast_to(m[b, k][None, None, :], (b, 1, d)) for k in range(6)]
        ng = [norm_g[l, k][None, :] for k in range(4)]
        w_pack = _pack_w_in(w_in[l])
        wkv_pack = _pack_w_kv(w_kv_up[l])
        kvg = kv_norm_g[l][None, :]
        cw, cb = hy_conv_w[l], hy_conv_b[l][None, :]

        q, kk, vv, sq, sk, sv, z, x1, x2, gt = _in_proj(
            x_lat, lat[0], lat[1], ng[0], w_pack, wkv_pack, kvg, rope_m, rope_s, cw, cb)
        qc, kkc, vvc, sqc, skc, svc, zc, x1c, x2c, gtc = _in_proj(
            x_ctx, cx[0], cx[1], ng[0], w_pack, wkv_pack, kvg, rope_m_ctx, rope_s_ctx, cw, cb)

        y_a = _mla(q, kkc, vvc, kk, vv)
        y_b = _swa(swa_sink[l], sq, skc, svc, sk, sv)
        hy_mlp = (hy_w1[l], hy_b1[l], hy_w2[l], hy_b2[l], hy_freq[l], hy_w3[l])
        y_c = _hyena_long(z, x1, x2, *_hyena_filter_taps(s, *hy_mlp), hy_skip[l])

        wb = w_branch[l].astype(BF16)
        wo = w_out[l].astype(BF16)
        wu = w_up[l].astype(BF16)
        wd = w_down[l].astype(BF16)
        fcw, fcb = ffn_conv_w[l], ffn_conv_b[l][None, :]

        x_lat = _merge(x_lat, y_a, y_b, y_c, gt, wb, wo, ng[1], lat[2])
        x_lat = _ffn(x_lat, lat[3], lat[4], lat[5], ng[2], ng[3], wu, fcw, fcb, wd)

        if with_ctx:
            yc_a = _mla(qc, kkc, vvc)
            yc_b = _swa(swa_sink[l], sqc, skc, svc)
            yc_c = _hyena_short(zc, x1c, x2c, *_hyena_filter_taps(n_ctx, *hy_mlp), hy_skip[l])
            x_ctx = _merge(x_ctx, yc_a, yc_b, yc_c, gtc, wb, wo, ng[1], cx[2])
            x_ctx = _ffn(x_ctx, cx[3], cx[4], cx[5], ng[2], ng[3], wu, fcw, fcb, wd)
    return x_lat
```

```python
import functools
import math

import numpy as np
import jax
import jax.numpy as jnp
from jax import lax
from jax.experimental import pallas as pl
from jax.experimental.pallas import tpu as pltpu

F32 = jnp.float32
BF16 = jnp.bfloat16

GRID_W = 64
EPS = 1e-6
ROPE_THETA = 10000.0
NEG_INF = -1e30
MLA_HEADS = 8
MLA_NOPE = 64
MLA_ROPE = 32
MLA_V = 64
KV_RANK = 256
MLA_SCALE = (MLA_NOPE + MLA_ROPE) ** -0.5
LOG2E = math.log2(math.e)
SWA_HEADS = 8
SWA_KV_HEADS = 2
SWA_HEAD_DIM = 64
SWA_BLOCK = 128
SWA_SCALE = SWA_HEAD_DIM ** -0.5
HY_WIDTH = 512
HY_ORDER = 2
HY_BANDS = 16
HY_HIDDEN = 64
HY_TARGET = 1e-2
HY_FAST_PCT = 0.3
HY_SLOW_PCT = 1.5
N_BRANCH = 3
BRANCH_WIDTH = 512
LANE = 128
SUBLANES = 8
HEAD_PAD = 128

VMEM_LIMIT = 56 * 1024 * 1024
IN_PROJ_ROWS = 512
FFN_ROWS = 512
FFN_CHUNKS = 1
MERGE_ROWS = 1024
MERGE_SPLIT = 2

C_Q = 0
C_CKV = C_Q + MLA_HEADS * HEAD_PAD
C_KR = C_CKV + KV_RANK
C_SQ = C_KR + HEAD_PAD
C_SK = C_SQ + SWA_HEADS * SWA_HEAD_DIM
C_SV = C_SK + 2 * SWA_KV_HEADS * SWA_HEAD_DIM
C_HY = C_SV + 2 * SWA_KV_HEADS * SWA_HEAD_DIM
C_GT = C_HY + (HY_ORDER + 1) * HY_WIDTH


def _cparams(sem):
    return pltpu.CompilerParams(dimension_semantics=sem, vmem_limit_bytes=VMEM_LIMIT)


def _rms(xf, g):
    return xf * lax.rsqrt(jnp.mean(xf * xf, axis=-1, keepdims=True) + EPS) * g


def _rope(x, cos, sa, sb, half):
    return x * cos + pltpu.roll(x, LANE - half, 1) * sa + pltpu.roll(x, half, 1) * sb


def _halo_specs(tm, s, d):
    per = tm // SUBLANES
    last = s // SUBLANES - 1
    return (pl.BlockSpec((1, SUBLANES, d), lambda bi, i: (bi, jnp.maximum(i * per - 1, 0), 0)),
            pl.BlockSpec((1, SUBLANES, d), lambda bi, i: (bi, jnp.minimum((i + 1) * per, last), 0)))


def _halo_rows(p, tm, i, nt):
    prev_row = jnp.where(i > 0, p[tm + SUBLANES - 1:tm + SUBLANES, :], 0.0)
    next_row = jnp.where(i < nt - 1, p[tm + SUBLANES:tm + SUBLANES + 1, :], 0.0)
    return prev_row, next_row


def _dwconv3(p, prev_row, next_row, cw_ref, cb_ref):
    tm = p.shape[0]
    rows = lax.broadcasted_iota(jnp.int32, (tm, 1), 0)
    up = jnp.where(rows == 0, prev_row, pltpu.roll(p, 1, 0))
    dn = jnp.where(rows == tm - 1, next_row, pltpu.roll(p, tm - 1, 0))
    return up * cw_ref[0:1, :] + p * cw_ref[1:2, :] + dn * cw_ref[2:3, :] + cb_ref[...]


def _mod_kernel(c_ref, w_ref, b_ref, o_ref):
    c = c_ref[...]
    a = c * jax.nn.sigmoid(c)
    a_hi = a.astype(BF16)
    a_lo = (a - a_hi.astype(F32)).astype(BF16)
    w = w_ref[0]
    w_hi = w.astype(BF16)
    w_lo = (w - w_hi.astype(F32)).astype(BF16)
    acc = jnp.dot(a_hi, w_hi, preferred_element_type=F32)
    acc += jnp.dot(a_lo, w_hi, preferred_element_type=F32)
    acc += jnp.dot(a_hi, w_lo, preferred_element_type=F32)
    o_ref[0] = acc + b_ref[0]


def _modulation(cvec, w_mod, b_mod):
    depth, d, n = w_mod.shape
    rows = cvec.shape[0]
    tn = n // 4 if n % (4 * LANE) == 0 else n
    return pl.pallas_call(
        _mod_kernel,
        grid=(depth, n // tn),
        in_specs=[pl.BlockSpec((rows, d), lambda l, j: (0, 0)),
                  pl.BlockSpec((1, d, tn), lambda l, j: (l, 0, j)),
                  pl.BlockSpec((1, 1, tn), lambda l, j: (l, 0, j))],
        out_specs=pl.BlockSpec((1, rows, tn), lambda l, j: (l, 0, j)),
        out_shape=jax.ShapeDtypeStruct((depth, rows, n), F32),
        compiler_params=_cparams(("arbitrary", "arbitrary")),
        name="adaln_mod",
    )(cvec, w_mod, b_mod.reshape(depth, 1, n))


def _in_proj_kernel(x_ref, xp_ref, xn_ref, sh_ref, sc_ref, g_ref, w_ref, wkv_ref, kvg_ref,
                    cm_ref, sma_ref, smb_ref, cs_ref, ssa_ref, ssb_ref, cw_ref, cb_ref,
                    q_ref, kk_ref, vv_ref, sq_ref, sk_ref, sv_ref, z_ref, x1_ref, x2_ref, gt_ref):
    i = pl.program_id(1)
    nt = pl.num_programs(1)
    g = g_ref[...]
    sh = sh_ref[0]
    sc = sc_ref[0]

    def mod(xf):
        return (_rms(xf, g) * (1.0 + sc) + sh).astype(BF16)

    tm = x_ref.shape[1]
    hb_ext = mod(jnp.concatenate([x_ref[0], xp_ref[0], xn_ref[0]], axis=0))
    hb = hb_ext[:tm]

    def proj(lo, hi, lhs=hb):
        return jnp.dot(lhs, w_ref[:, lo:hi], preferred_element_type=F32)

    cm, sma, smb = cm_ref[...], sma_ref[...], smb_ref[...]
    cs, ssa, ssb = cs_ref[...], ssa_ref[...], ssb_ref[...]

    pq = proj(C_Q, C_CKV)
    for h in range(MLA_HEADS):
        xh = pq[:, HEAD_PAD * h:HEAD_PAD * (h + 1)]
        q_ref[0, :, HEAD_PAD * h:HEAD_PAD * (h + 1)] = (
            _rope(xh, cm, sma, smb, MLA_ROPE // 2) * (MLA_SCALE * LOG2E)).astype(BF16)

    ckv = proj(C_CKV, C_KR)
    cn = _rms(ckv, kvg_ref[...]).astype(BF16)
    kv = jnp.dot(cn, wkv_ref[...], preferred_element_type=F32)
    krr = _rope(proj(C_KR, C_SQ), cm, sma, smb, MLA_ROPE // 2)
    for h in range(MLA_HEADS):
        kk_ref[0, :, HEAD_PAD * h:HEAD_PAD * (h + 1)] = (
            kv[:, HEAD_PAD * h:HEAD_PAD * (h + 1)] + krr).astype(BF16)
    ones = jnp.ones((hb.shape[0], LANE), BF16)
    for j in range(MLA_HEADS // 2):
        v0 = MLA_HEADS * HEAD_PAD + LANE * j
        vv_ref[0, :, 2 * LANE * j:2 * LANE * j + LANE] = kv[:, v0:v0 + LANE].astype(BF16)
        vv_ref[0, :, 2 * LANE * j + LANE:2 * LANE * (j + 1)] = ones

    psq = proj(C_SQ, C_SK)
    for j in range(SWA_HEADS // 2):
        xh = psq[:, LANE * j:LANE * (j + 1)]
        sq_ref[0, :, LANE * j:LANE * (j + 1)] = (
            _rope(xh, cs, ssa, ssb, SWA_HEAD_DIM // 2) * (SWA_SCALE * LOG2E)).astype(BF16)
    psk = proj(C_SK, C_SV)
    for k in range(SWA_KV_HEADS):
        sk_ref[0, :, LANE * k:LANE * (k + 1)] = _rope(
            psk[:, LANE * k:LANE * (k + 1)], cs, ssa, ssb, SWA_HEAD_DIM // 2).astype(BF16)
    sv_ref[0] = proj(C_SV, C_HY).astype(BF16)

    ph = proj(C_HY, C_GT, hb_ext)
    pprev, pnext = _halo_rows(ph, tm, i, nt)
    u = _dwconv3(ph[:tm], pprev, pnext, cw_ref, cb_ref)
    z_ref[0] = u[:, :HY_WIDTH]
    x1_ref[0] = u[:, HY_WIDTH:2 * HY_WIDTH]
    x2_ref[0] = u[:, 2 * HY_WIDTH:]

    gt_ref[0] = jax.nn.sigmoid(proj(C_GT, w_ref.shape[1])).astype(BF16)


def _in_proj(x, sh, sc, g, w, wkv, kvg, rope_m, rope_s, cw, cb):
    b, s, d = x.shape
    tm = min(IN_PROJ_ROWS, s)
    nt = s // tm
    row = lambda width: pl.BlockSpec((1, tm, width), lambda bi, i: (bi, i, 0))
    const2 = lambda a: pl.BlockSpec(a.shape, lambda bi, i: (0, 0), pipeline_mode=pl.Buffered(1))
    tab = pl.BlockSpec((tm, LANE), lambda bi, i: (i, 0))
    in_specs = [
        row(d),
        *_halo_specs(tm, s, d),
        pl.BlockSpec((1, 1, d), lambda bi, i: (bi, 0, 0)),
        pl.BlockSpec((1, 1, d), lambda bi, i: (bi, 0, 0)),
        const2(g), const2(w), const2(wkv), const2(kvg),
        tab, tab, tab, tab, tab, tab,
        const2(cw), const2(cb),
    ]
    widths = [(MLA_HEADS * HEAD_PAD, BF16), (MLA_HEADS * HEAD_PAD, BF16), (MLA_HEADS * LANE, BF16),
              (SWA_HEADS * SWA_HEAD_DIM, BF16), (2 * LANE, BF16), (2 * LANE, BF16),
              (HY_WIDTH, F32), (HY_WIDTH, F32), (HY_WIDTH, F32), (N_BRANCH * d, BF16)]
    return pl.pallas_call(
        _in_proj_kernel,
        grid=(b, nt),
        in_specs=in_specs,
        out_specs=[row(wd) for wd, _ in widths],
        out_shape=[jax.ShapeDtypeStruct((b, s, wd), dt) for wd, dt in widths],
        compiler_params=_cparams(("arbitrary", "arbitrary")),
        name="in_proj",
    )(x, x, x, sh, sc, g, w, wkv, kvg, *rope_m, *rope_s, cw, cb)


def _mla_kernel(*refs, tk, n_lat, rb):
    if n_lat:
        q_ref, kc_ref, vc_ref, kl_ref, vl_ref, o_ref, m_scr, acc_scr, p_scr = refs
    else:
        q_ref, kc_ref, vc_ref, o_ref, m_scr, acc_scr = refs
    tq = q_ref.shape[1]
    nt = (((1,), (1,)), ((), ()))
    blocks = [(e, pl.ds(r * rb, rb)) for e in range(2) for r in range(tq // rb)]

    def scores(e, rows, k_ref, off, n):
        return lax.dot_general(q_ref[0, rows, HEAD_PAD * e:HEAD_PAD * (e + 1)],
                               k_ref[0, pl.ds(off, n), HEAD_PAD * e:HEAD_PAD * (e + 1)], nt,
                               preferred_element_type=F32)

    n_ctx = kc_ref.shape[1]
    for e, rows in blocks:
        s = scores(e, rows, kc_ref, 0, n_ctx)
        m_new = jnp.tile(jnp.max(s, axis=1, keepdims=True), (1, LANE))
        p = jnp.exp2(s - jnp.tile(m_new, (1, n_ctx // LANE))).astype(BF16)
        if n_lat:
            p_scr[e, rows, :n_ctx] = p
        else:
            acc_scr[e, rows, :] = jnp.dot(p, vc_ref[0], preferred_element_type=F32)
        m_scr[e, rows, :] = m_new

    if n_lat:
        def softmax_chunk(off, pv_off):
            for e, rows in blocks:
                s = scores(e, rows, kl_ref, off, tk)
                m_old = m_scr[e, rows, :]
                m_new = jnp.maximum(m_old, jnp.max(s, axis=1, keepdims=True))
                alpha = jnp.tile(jnp.exp2(m_old - m_new), (1, 2))
                if pv_off is None:
                    acc = jnp.dot(p_scr[e, rows, :n_ctx], vc_ref[0], preferred_element_type=F32)
                else:
                    acc = acc_scr[e, rows, :] + jnp.dot(p_scr[e, rows, :], vl_ref[0, pl.ds(pv_off, tk), :],
                                                        preferred_element_type=F32)
                acc_scr[e, rows, :] = alpha * acc
                p_scr[e, rows, :] = jnp.exp2(s - jnp.tile(m_new, (1, tk // LANE))).astype(BF16)
                m_scr[e, rows, :] = m_new

        softmax_chunk(0, None)

        def body(j, carry):
            off = pl.multiple_of(j * tk, tk)
            softmax_chunk(off, pl.multiple_of(off - tk, tk))
            return carry
        lax.fori_loop(1, n_lat, body, 0)
        last = (n_lat - 1) * tk
        for e, rows in blocks:
            acc_scr[e, rows, :] += jnp.dot(p_scr[e, rows, :], vl_ref[0, pl.ds(last, tk), :],
                                           preferred_element_type=F32)
    lane = lax.broadcasted_iota(jnp.int32, (tq, LANE), 1)
    o0 = acc_scr[0, :, :LANE] / acc_scr[0, :, LANE:]
    o1 = acc_scr[1, :, :LANE] / acc_scr[1, :, LANE:]
    o_ref[0] = jnp.where(lane < MLA_V, o0, o1).astype(o_ref.dtype)


def _mla(q, kc, vc, kl=None, vl=None, tq=4096, tk=512, rb=128):
    b, sq, _ = q.shape
    c = kc.shape[1]
    tq = min(tq, sq)
    rb = min(rb, tq)
    hp = MLA_HEADS // 2
    in_specs = [pl.BlockSpec((1, tq, 2 * HEAD_PAD), lambda bi, h, i: (bi, i, h)),
                pl.BlockSpec((1, c, 2 * HEAD_PAD), lambda bi, h, i: (bi, 0, h)),
                pl.BlockSpec((1, c, 2 * LANE), lambda bi, h, i: (bi, 0, h))]
    args = [q, kc, vc]
    n_lat = 0
    if kl is not None:
        s = kl.shape[1]
        tk = min(tk, s)
        n_lat = s // tk
        assert c <= tk, "the context chunk shares the probability buffer of a latent chunk"
        in_specs += [pl.BlockSpec((1, s, 2 * HEAD_PAD), lambda bi, h, i: (bi, 0, h)),
                     pl.BlockSpec((1, s, 2 * LANE), lambda bi, h, i: (bi, 0, h))]
        args += [kl, vl]
    return pl.pallas_call(
        functools.partial(_mla_kernel, tk=tk, n_lat=n_lat, rb=rb),
        grid=(b, hp, sq // tq),
        in_specs=in_specs,
        out_specs=pl.BlockSpec((1, tq, LANE), lambda bi, h, i: (bi, i, h)),
        out_shape=jax.ShapeDtypeStruct((b, sq, MLA_HEADS * MLA_V), BF16),
        scratch_shapes=[pltpu.VMEM((2, tq, LANE), F32), pltpu.VMEM((2, tq, 2 * LANE), F32)]
        + ([pltpu.VMEM((2, tq, tk), BF16)] if n_lat else []),
        compiler_params=_cparams(("arbitrary", "arbitrary", "arbitrary")),
        name="mla_attn",
    )(*args)


def _swa_kernel(*refs, band):
    if band:
        sink_ref, q_ref, kc_ref, vc_ref, kp_ref, k0_ref, kn_ref, vp_ref, v0_ref, vn_ref, o_ref = refs
        kband = jnp.concatenate([kp_ref[0], k0_ref[0], kn_ref[0]], axis=0)
        vband = jnp.concatenate([vp_ref[0], v0_ref[0], vn_ref[0]], axis=0)
    else:
        sink_ref, q_ref, kc_ref, vc_ref, o_ref = refs
    i = pl.program_id(1)
    nb = pl.num_programs(1)
    tq = q_ref.shape[1]
    nsub = tq // SWA_BLOCK
    n_ctx = kc_ref.shape[1]
    n_keys = n_ctx + (3 * SWA_BLOCK if band else 0)
    grp = SWA_HEADS // SWA_KV_HEADS
    rows_all = grp * SWA_BLOCK
    nt = (((1,), (1,)), ((), ()))
    lane = lax.broadcasted_iota(jnp.int32, (SWA_BLOCK, LANE), 1)
    low = lane < SWA_HEAD_DIM
    row = lax.broadcasted_iota(jnp.int32, (rows_all, 1), 0)
    if band:
        r_loc = lax.broadcasted_iota(jnp.int32, (rows_all, n_keys), 0) % SWA_BLOCK
        col = lax.broadcasted_iota(jnp.int32, (rows_all, n_keys), 1) - n_ctx
        prev_bias = jnp.where((col >= 0) & (col < SWA_BLOCK), NEG_INF, 0.0)
        next_bias = jnp.where(col >= 2 * SWA_BLOCK, NEG_INF, 0.0)
        base_bias = (jnp.where(r_loc > col, prev_bias, 0.0)
                     + jnp.where(col - 2 * SWA_BLOCK > r_loc, next_bias, 0.0))
    for r in range(nsub):
        rs = slice(r * SWA_BLOCK, (r + 1) * SWA_BLOCK)
        if band:
            bias = base_bias
            if r == 0:
                bias = jnp.minimum(bias, jnp.where(i == 0, prev_bias, 0.0))
            if r == nsub - 1:
                bias = jnp.minimum(bias, jnp.where(i == nb - 1, next_bias, 0.0))
            bs = slice(r * SWA_BLOCK, (r + 3) * SWA_BLOCK)
        for kvh in range(SWA_KV_HEADS):
            sl = slice(LANE * kvh, LANE * (kvh + 1))
            k, v = kc_ref[0, :, sl], vc_ref[0, :, sl]
            if band:
                k = jnp.concatenate([k, kband[bs, sl]], axis=0)
                v = jnp.concatenate([v, vband[bs, sl]], axis=0)
            v = jnp.concatenate([v, jnp.ones_like(v)], axis=1)
            parts, sink = [], None
            for t in range(grp // 2):
                pair = kvh * (grp // 2) + t
                qp = q_ref[0, rs, LANE * pair:LANE * (pair + 1)]
                parts += [jnp.where(low, qp, jnp.zeros_like(qp)), jnp.where(low, jnp.zeros_like(qp), qp)]
            for h in range(grp):
                sh = sink_ref[kvh * grp + h] * LOG2E
                sink = sh if sink is None else jnp.where(row < h * SWA_BLOCK, sink, sh)
            s = lax.dot_general(jnp.concatenate(parts, axis=0), k, nt, preferred_element_type=F32)
            if band:
                s = s + bias
            m = jnp.maximum(jnp.max(s, axis=1, keepdims=True), sink)
            p = jnp.exp2(s - m).astype(BF16)
            acc = jnp.dot(p, v, preferred_element_type=F32)
            o = acc[:, :LANE] / (acc[:, LANE:] + jnp.exp2(sink - m))
            for t in range(grp // 2):
                pair = kvh * (grp // 2) + t
                lo = o[(2 * t) * SWA_BLOCK:(2 * t + 1) * SWA_BLOCK]
                hi = o[(2 * t + 1) * SWA_BLOCK:(2 * t + 2) * SWA_BLOCK]
                o_ref[0, rs, LANE * pair:LANE * (pair + 1)] = jnp.where(low, lo, hi).astype(o_ref.dtype)


def _swa(sink, q, kc, vc, k=None, v=None, tq=1024):
    b, sq, _ = q.shape
    c = kc.shape[1]
    tq = min(tq, sq)
    nb = sq // tq
    band = k is not None
    blk = lambda rows, width, f: pl.BlockSpec((1, rows, width), f)
    in_specs = [pl.BlockSpec(memory_space=pltpu.SMEM),
                blk(tq, SWA_HEADS * SWA_HEAD_DIM, lambda bi, i: (bi, i, 0)),
                pl.BlockSpec((1, c, 2 * LANE), lambda bi, i: (bi, 0, 0)),
                pl.BlockSpec((1, c, 2 * LANE), lambda bi, i: (bi, 0, 0))]
    args = [sink, q, kc, vc]
    if band:
        per = tq // SWA_BLOCK
        prev = lambda bi, i: (bi, jnp.maximum(i * per - 1, 0), 0)
        cur = lambda bi, i: (bi, i, 0)
        nxt = lambda bi, i: (bi, jnp.minimum((i + 1) * per, sq // SWA_BLOCK - 1), 0)
        in_specs += [blk(SWA_BLOCK, 2 * LANE, prev), blk(tq, 2 * LANE, cur), blk(SWA_BLOCK, 2 * LANE, nxt),
                     blk(SWA_BLOCK, 2 * LANE, prev), blk(tq, 2 * LANE, cur), blk(SWA_BLOCK, 2 * LANE, nxt)]
        args += [k, k, k, v, v, v]
    return pl.pallas_call(
        functools.partial(_swa_kernel, band=band),
        grid=(b, nb),
        in_specs=in_specs,
        out_specs=blk(tq, SWA_HEADS * SWA_HEAD_DIM, lambda bi, i: (bi, i, 0)),
        out_shape=jax.ShapeDtypeStruct((b, sq, SWA_HEADS * SWA_HEAD_DIM), BF16),
        compiler_params=_cparams(("arbitrary", "arbitrary")),
        name="swa_attn",
    )(*args)


def _merge_kernel(x_ref, ya_ref, yb_ref, yc_ref, gt_ref, wb_ref, wo_ref, ng_ref, g1_ref, o_ref):
    d = x_ref.shape[2]
    tm = x_ref.shape[1]
    sub = tm // MERGE_SPLIT if tm % (MERGE_SPLIT * SUBLANES) == 0 else tm
    for r0 in range(0, tm, sub):
        rows = slice(r0, r0 + sub)
        merged = None
        for k, y_ref in enumerate((ya_ref, yb_ref, yc_ref)):
            t = jnp.dot(y_ref[0, rows, :].astype(BF16), wb_ref[k], preferred_element_type=F32)
            t = gt_ref[0, rows, d * k:d * (k + 1)].astype(F32) * t
            merged = t if merged is None else merged + t
        y = jnp.dot(merged.astype(BF16), wo_ref[...], preferred_element_type=F32)
        o_ref[0, rows, :] = x_ref[0, rows, :] + g1_ref[0] * _rms(y, ng_ref[...])


def _merge(x, ya, yb, yc, gt, wb, wo, ng, g1):
    b, s, d = x.shape
    tm = min(MERGE_ROWS, s)
    row = lambda width: pl.BlockSpec((1, tm, width), lambda bi, i: (bi, i, 0))
    return pl.pallas_call(
        _merge_kernel,
        grid=(b, s // tm),
        in_specs=[row(d), row(BRANCH_WIDTH), row(BRANCH_WIDTH), row(BRANCH_WIDTH), row(N_BRANCH * d),
                  pl.BlockSpec(wb.shape, lambda bi, i: (0, 0, 0)),
                  pl.BlockSpec(wo.shape, lambda bi, i: (0, 0)),
                  pl.BlockSpec(ng.shape, lambda bi, i: (0, 0)),
                  pl.BlockSpec((1, 1, d), lambda bi, i: (bi, 0, 0))],
        out_specs=row(d),
        out_shape=jax.ShapeDtypeStruct((b, s, d), F32),
        compiler_params=_cparams(("arbitrary", "arbitrary")),
        name="merge",
    )(x, ya, yb, yc, gt, wb, wo, ng, g1)


def _ffn_kernel(x_ref, xp_ref, xn_ref, sh_ref, sc_ref, g2_ref, ng_in_ref, ng_out_ref,
                wu_ref, cw_ref, cb_ref, wd_ref, o_ref, *, chunk):
    i = pl.program_id(1)
    nt = pl.num_programs(1)
    g = ng_in_ref[...]
    sh = sh_ref[0]
    sc = sc_ref[0]

    def mod(xf):
        return (_rms(xf, g) * (1.0 + sc) + sh).astype(BF16)

    x = x_ref[0]
    tm = x.shape[0]
    hb = mod(jnp.concatenate([x, xp_ref[0], xn_ref[0]], axis=0))
    dff = wd_ref.shape[0]
    f = None
    for c0 in range(0, dff, chunk):
        halves = []
        for off in (c0, dff + c0):
            p = jnp.dot(hb, wu_ref[:, off:off + chunk], preferred_element_type=F32)
            pprev, pnext = _halo_rows(p, tm, i, nt)
            halves.append(_dwconv3(p[:tm], pprev, pnext, cw_ref.at[:, off:off + chunk],
                                   cb_ref.at[:, off:off + chunk]))
        a, bb = halves
        act = (a * jax.nn.sigmoid(a) * bb).astype(BF16)
        t = jnp.dot(act, wd_ref[c0:c0 + chunk, :], preferred_element_type=F32)
        f = t if f is None else f + t
    o_ref[0] = x + g2_ref[0] * _rms(f, ng_out_ref[...])


def _ffn(x, sh, sc, g2, ng_in, ng_out, wu, cw, cb, wd):
    b, s, d = x.shape
    tm = min(FFN_ROWS, s)
    dff = wd.shape[0]
    chunk = dff // FFN_CHUNKS if dff % (FFN_CHUNKS * LANE) == 0 else dff
    row = pl.BlockSpec((1, tm, d), lambda bi, i: (bi, i, 0))
    vec = pl.BlockSpec((1, 1, d), lambda bi, i: (bi, 0, 0))
    const2 = lambda a: pl.BlockSpec(a.shape, lambda bi, i: (0, 0), pipeline_mode=pl.Buffered(1))
    return pl.pallas_call(
        functools.partial(_ffn_kernel, chunk=chunk),
        grid=(b, s // tm),
        in_specs=[row, *_halo_specs(tm, s, d),
                  vec, vec, vec, const2(ng_in), const2(ng_out),
                  const2(wu), const2(cw), const2(cb), const2(wd)],
        out_specs=row,
        out_shape=jax.ShapeDtypeStruct((b, s, d), F32),
        compiler_params=_cparams(("arbitrary", "arbitrary")),
        name="conv_ffn",
    )(x, x, x, sh, sc, g2, ng_in, ng_out, wu, cw, cb, wd)


HY_PASSES = 1
SPEC_PASSES = 1
HY_UNROLL = 8
HY_UNROLL2 = 33


def _split(x):
    hi = x.astype(BF16)
    return hi, (x - hi.astype(F32)).astype(BF16)


def _cdot(c_hi, c_lo, x, passes=HY_PASSES):
    x_hi, x_lo = _split(x)
    acc = jnp.dot(c_hi, x_hi, preferred_element_type=F32)
    if passes > 1:
        acc = acc + jnp.dot(c_lo, x_hi, preferred_element_type=F32)
        acc = acc + jnp.dot(c_hi, x_lo, preferred_element_type=F32)
    return acc


def _np_split(m):
    m = jnp.asarray(m, F32)
    hi = m.astype(BF16)
    return hi, (m - hi.astype(F32)).astype(BF16)


def _dft_consts(n_blocks, nonzero_blocks):
    nb = n_blocks
    n = LANE * nb
    dh = nb // 2 + 1
    dhp = dh + dh % 2
    d = np.arange(dh)[:, None]
    b = np.arange(nonzero_blocks)[None, :]
    ang = 2 * np.pi * ((d * b) % nb) / nb
    f1 = np.zeros((2 * dhp, nonzero_blocks))
    f1[:dh] = np.cos(ang)
    f1[dhp:dhp + dh] = -np.sin(ang)
    a = np.arange(LANE)
    ang2 = 2 * np.pi * ((a[:, None] * a[None, :]) % LANE) / LANE
    ar, ai = np.cos(ang2), -np.sin(ang2)
    f2 = np.block([[ar, -ai], [ai, ar]])
    f2inv = np.block([[ar, ai], [-ai, ar]])
    bo = np.arange(nb // 2)[:, None]
    do = np.arange(dh)[None, :]
    wd = np.where((do == 0) | (do == nb // 2), 1.0, 2.0) / n
    ang3 = 2 * np.pi * ((bo * do) % nb) / nb
    f3 = np.zeros((nb // 2, 2 * dhp))
    f3[:, :dh] = wd * np.cos(ang3)
    f3[:, dhp:dhp + dh] = -wd * np.sin(ang3)
    step_ang = 2 * np.pi * a / n
    tw_step = np.stack([np.broadcast_to(np.cos(step_ang)[:, None], (LANE, LANE)),
                        np.broadcast_to(-np.sin(step_ang)[:, None], (LANE, LANE))])
    eye = np.eye(SUBLANES)
    return dict(dh=dh, dhp=dhp, f1=_np_split(np.kron(f1, eye)), f2=_np_split(f2), f2inv=_np_split(f2inv),
                f3=_np_split(np.kron(f3, eye)), tw_step=jnp.asarray(tw_step, F32))


def _stage1(x_ref, f1_hi, f1_lo, g_scr, passes):
    n_in, n_out = x_ref.shape[0], g_scr.shape[0]

    def body(t, carry):
        r0 = pl.ds(pl.multiple_of(2 * t * SUBLANES, SUBLANES), SUBLANES)
        r1 = pl.ds(pl.multiple_of((2 * t + 1) * SUBLANES, SUBLANES), SUBLANES)
        xa = jnp.concatenate([x_ref[:, r0, :].reshape(n_in * SUBLANES, LANE),
                              x_ref[:, r1, :].reshape(n_in * SUBLANES, LANE)], axis=1)
        g = _cdot(f1_hi, f1_lo, xa, passes)
        g_scr[:, r0, :] = g[:, :LANE].reshape(n_out, SUBLANES, LANE)
        g_scr[:, r1, :] = g[:, LANE:].reshape(n_out, SUBLANES, LANE)
        return carry
    lax.fori_loop(0, LANE // (2 * SUBLANES), body, 0, unroll=HY_UNROLL)


def _twiddle_init(tw_scr):
    tw_scr[0] = jnp.ones((LANE, LANE), F32)
    tw_scr[1] = jnp.zeros((LANE, LANE), F32)


def _twiddle_next(twr, twi, step_ref):
    sr, si = step_ref[0], step_ref[1]
    return twr * sr - twi * si, twr * si + twi * sr


def _twiddle_pair(tw_scr, step_ref):
    t0r, t0i = tw_scr[0], tw_scr[1]
    t1r, t1i = _twiddle_next(t0r, t0i, step_ref)
    t2r, t2i = _twiddle_next(t1r, t1i, step_ref)
    tw_scr[0] = t2r
    tw_scr[1] = t2i
    return jnp.concatenate([t0r, t1r], axis=1), jnp.concatenate([t0i, t1i], axis=1)


def _spectrum_kernel(k_ref, inv_ref, f1h_ref, f1l_ref, f2h_ref, f2l_ref, step_ref, hf_ref, g_scr, tw_scr,
                     *, dh, dhp):
    _stage1(k_ref, f1h_ref[...], f1l_ref[...], g_scr, SPEC_PASSES)
    _twiddle_init(tw_scr)
    inv = inv_ref[...]

    def body(h, carry):
        d = 2 * h
        gr = jnp.concatenate([g_scr[d], g_scr[d + 1]], axis=1)
        gi = jnp.concatenate([g_scr[dhp + d], g_scr[dhp + d + 1]], axis=1)
        twr, twi = _twiddle_pair(tw_scr, step_ref)
        t = jnp.concatenate([gr * twr - gi * twi, gr * twi + gi * twr], axis=0)
        y = _cdot(f2h_ref[...], f2l_ref[...], t, passes=SPEC_PASSES)
        hf_ref[0, d] = y[:LANE, :LANE] * inv
        hf_ref[0, d + 1] = y[:LANE, LANE:] * inv
        hf_ref[1, d] = y[LANE:, :LANE] * inv
        hf_ref[1, d + 1] = y[LANE:, LANE:] * inv
        return carry
    lax.fori_loop(0, dhp // 2, body, 0, unroll=min(HY_UNROLL2, dhp // 2))


def _hyena_spectrum(kern, inv_norm):
    tiles, n, _ = kern.shape
    c = tiles * LANE
    nb = n // LANE
    cst = _dft_consts(nb, nb)
    dh, dhp = cst["dh"], cst["dhp"]
    const = lambda a: pl.BlockSpec(a.shape, lambda j: (0,) * a.ndim, pipeline_mode=pl.Buffered(1))
    consts = [*cst["f1"], *cst["f2"], cst["tw_step"]]
    return pl.pallas_call(
        functools.partial(_spectrum_kernel, dh=dh, dhp=dhp),
        grid=(c // LANE,),
        in_specs=[pl.BlockSpec((None, nb, LANE, LANE), lambda j: (j, 0, 0, 0)),
                  pl.BlockSpec((1, LANE), lambda j: (0, j))]
        + [const(a) for a in consts],
        out_specs=pl.BlockSpec((None, 2, dhp, LANE, LANE), lambda j: (j, 0, 0, 0, 0)),
        out_shape=jax.ShapeDtypeStruct((c // LANE, 2, dhp, LANE, LANE), F32),
        scratch_shapes=[pltpu.VMEM((2 * dhp, LANE, LANE), F32), pltpu.VMEM((2, LANE, LANE), F32)],
        compiler_params=_cparams(("arbitrary",)),
        name="hyena_spectrum",
    )(kern.reshape(tiles, nb, LANE, LANE), inv_norm, *consts)


def _conv_kernel(z_ref, gate_ref, skip_ref, hf_ref, f1h_ref, f1l_ref, f2h_ref, f2l_ref, f2ih_ref, f2il_ref,
                 f3h_ref, f3l_ref, step_ref, o_ref, g_scr, tw_scr, *, dh, dhp):
    nbh = z_ref.shape[1]
    _stage1(z_ref.at[0], f1h_ref[...], f1l_ref[...], g_scr, HY_PASSES)
    _twiddle_init(tw_scr)

    def body(h, carry):
        d = 2 * h
        gr = jnp.concatenate([g_scr[d], g_scr[d + 1]], axis=1)
        gi = jnp.concatenate([g_scr[dhp + d], g_scr[dhp + d + 1]], axis=1)
        twr, twi = _twiddle_pair(tw_scr, step_ref)
        t = jnp.concatenate([gr * twr - gi * twi, gr * twi + gi * twr], axis=0)
        y = _cdot(f2h_ref[...], f2l_ref[...], t)
        yr, yi = y[:LANE], y[LANE:]
        hr = jnp.concatenate([hf_ref[0, d], hf_ref[0, d + 1]], axis=1)
        hi = jnp.concatenate([hf_ref[1, d], hf_ref[1, d + 1]], axis=1)
        zz = jnp.concatenate([yr * hr - yi * hi, yr * hi + yi * hr], axis=0)
        u = _cdot(f2ih_ref[...], f2il_ref[...], zz)
        ur, ui = u[:LANE], u[LANE:]
        vr = ur * twr + ui * twi
        vi = ui * twr - ur * twi
        g_scr[d] = vr[:, :LANE]
        g_scr[d + 1] = vr[:, LANE:]
        g_scr[dhp + d] = vi[:, :LANE]
        g_scr[dhp + d + 1] = vi[:, LANE:]
        return carry
    lax.fori_loop(0, dhp // 2, body, 0, unroll=min(HY_UNROLL2, dhp // 2))

    skip = skip_ref[...]
    f3h, f3l = f3h_ref[...], f3l_ref[...]

    def out_body(t, carry):
        r0 = pl.ds(pl.multiple_of(2 * t * SUBLANES, SUBLANES), SUBLANES)
        r1 = pl.ds(pl.multiple_of((2 * t + 1) * SUBLANES, SUBLANES), SUBLANES)
        ga = jnp.concatenate([g_scr[:, r0, :].reshape(2 * dhp * SUBLANES, LANE),
                              g_scr[:, r1, :].reshape(2 * dhp * SUBLANES, LANE)], axis=1)
        conv = _cdot(f3h, f3l, ga)
        for rows, c in ((r0, conv[:, :LANE]), (r1, conv[:, LANE:])):
            o_ref[0, :, rows, :] = gate_ref[0, :, rows, :] * (
                c.reshape(nbh, SUBLANES, LANE) + z_ref[0, :, rows, :] * skip)
        return carry
    lax.fori_loop(0, LANE // (2 * SUBLANES), out_body, 0, unroll=HY_UNROLL)


def _hyena_conv(z, gate, skip, hf, tile0, cst):
    b, l, w = z.shape
    nbh = l // LANE
    dh, dhp = cst["dh"], cst["dhp"]
    consts = [*cst["f1"], *cst["f2"], *cst["f2inv"], *cst["f3"], cst["tw_step"]]
    const = lambda a: pl.BlockSpec(a.shape, lambda j, bi: (0,) * a.ndim, pipeline_mode=pl.Buffered(1))
    tile = pl.BlockSpec((1, nbh, LANE, LANE), lambda j, bi: (bi, 0, 0, j))
    z, gate = z.reshape(b, nbh, LANE, w), gate.reshape(b, nbh, LANE, w)
    return pl.pallas_call(
        functools.partial(_conv_kernel, dh=dh, dhp=dhp),
        grid=(w // LANE, b),
        in_specs=[tile, tile, pl.BlockSpec((1, LANE), lambda j, bi: (0, j)),
                  pl.BlockSpec((None, 2, dhp, LANE, LANE), lambda j, bi: (j + tile0, 0, 0, 0, 0),
                               pipeline_mode=pl.Buffered(1))]
        + [const(a) for a in consts],
        out_specs=tile,
        out_shape=jax.ShapeDtypeStruct((b, nbh, LANE, w), F32),
        scratch_shapes=[pltpu.VMEM((2 * dhp, LANE, LANE), F32), pltpu.VMEM((2, LANE, LANE), F32)],
        compiler_params=_cparams(("arbitrary", "arbitrary")),
        name="hyena_conv",
    )(z, gate, skip, hf, *consts).reshape(b, l, w)


def _filter_mlp_kernel(ft_ref, w1_ref, b1_ref, w2_ref, b2_ref, fr_ref, w3_ref, dl_ref, k_ref, s_ref):
    i = pl.program_id(0)
    ft = ft_ref[...]

    def mm(x, w):
        x_hi, x_lo = _split(x)
        w_hi, w_lo = _split(w)
        return (jnp.dot(x_hi, w_hi, preferred_element_type=F32) + jnp.dot(x_lo, w_hi, preferred_element_type=F32)
                + jnp.dot(x_hi, w_lo, preferred_element_type=F32))

    half = ft.shape[0] // 2
    cw = dl_ref.shape[1]
    hid = jnp.sin(fr_ref[0:1, :] * (mm(jnp.concatenate([ft[:half], ft[half:]], axis=1), w1_ref[...]) + b1_ref[...]))
    hid = jnp.sin(fr_ref[1:2, :] * (mm(hid, w2_ref[...]) + b2_ref[...]))
    h2 = mm(hid, w3_ref[0])
    k = jnp.concatenate([h2[:, :cw], h2[:, cw:]], axis=0)
    k = k * jnp.exp(-ft[:, 0:1] * dl_ref[...]) * ft[:, LANE - 1:LANE]
    for j in range(k_ref.shape[0]):
        k_ref[j] = k[:, LANE * j:LANE * (j + 1)]

    @pl.when(i == 0)
    def _():
        s_ref[...] = jnp.zeros(s_ref.shape, F32)
    s_ref[...] += jnp.sum(jnp.abs(k), axis=0, keepdims=True)


def _hyena_filter_taps(n_tokens, w1, b1, w2, b2, freq, w3):
    L = n_tokens
    n = 2 * L
    idx = jnp.arange(n)
    m = jnp.where(idx < L, idx, n - idx).astype(F32)
    t = m / max(L - 1, 1)
    bands = jnp.linspace(1e-4, HY_BANDS - 1, HY_BANDS, dtype=F32)
    ang = (2.0 * math.pi / L) * m[:, None] * bands
    n_feat = 2 * HY_BANDS + 1
    feats = jnp.concatenate([t[:, None], jnp.cos(ang), -jnp.sin(ang),
                             jnp.zeros((n, LANE - n_feat - 1), F32),
                             (idx != L).astype(F32)[:, None]], axis=-1)
    two = lambda a: jnp.concatenate([a, a], axis=-1)
    bdiag = lambda a: jnp.concatenate([jnp.pad(a, ((0, 0), (0, a.shape[1]))), jnp.pad(a, ((0, 0), (a.shape[1], 0)))], axis=0)
    w1p = bdiag(jnp.pad(w1, ((0, LANE - n_feat), (0, 0))))
    cw = HY_ORDER * HY_WIDTH
    w3d = jnp.moveaxis(w3.reshape(HY_HIDDEN, 2, cw), 1, 0)
    w3d = jnp.stack([bdiag(w3d[0]), bdiag(w3d[1])])
    w2, freq = bdiag(w2), two(freq)
    deltas = jnp.abs(jnp.linspace(math.log(HY_TARGET) / HY_SLOW_PCT, math.log(HY_TARGET) / HY_FAST_PCT,
                                  HY_WIDTH, dtype=F32))
    dl = jnp.tile(deltas, HY_ORDER)[None, :]
    tr = min(1024, L)
    nt = n // tr
    const = lambda a: pl.BlockSpec(a.shape, lambda i: (0,) * a.ndim)
    b1r, b2r = two(b1)[None, :], two(b2)[None, :]
    return pl.pallas_call(
        _filter_mlp_kernel,
        grid=(nt,),
        in_specs=[pl.BlockSpec((tr, LANE), lambda i: (i, 0)), const(w1p), const(b1r), const(w2), const(b2r),
                  const(freq), pl.BlockSpec((1, 2 * HY_HIDDEN, 2 * cw), lambda i: (i // (nt // 2), 0, 0)), const(dl)],
        out_specs=[pl.BlockSpec((cw // LANE, tr, LANE), lambda i: (0, i, 0)), pl.BlockSpec((1, cw), lambda i: (0, 0))],
        out_shape=[jax.ShapeDtypeStruct((cw // LANE, n, LANE), F32), jax.ShapeDtypeStruct((1, cw), F32)],
        compiler_params=_cparams(("arbitrary",)),
        name="hyena_filter_mlp",
    )(feats, w1p, b1r, w2, b2r, freq, w3d, dl)


def _hyena_long(z, x1, x2, taps, norms, skip):
    b, l, w = z.shape
    nb = 2 * l // LANE
    hf = _hyena_spectrum(taps, 1.0 / norms)
    cst = _dft_consts(nb, nb // 2)
    for o, gate in enumerate((x1, x2)):
        z = _hyena_conv(z, gate, skip[o][None, :], hf, o * (w // LANE), cst)
    return z


def _dense_dft_consts(length):
    n = 2 * length
    h = length + 1
    hp = -(-h // 8) * 8
    k = np.arange(h)[:, None]
    pos = np.arange(n)[None, :]
    ang = 2 * np.pi * ((k * pos) % n) / n
    fwd = np.zeros((2 * hp, n))
    fwd[:h] = np.cos(ang)
    fwd[hp:hp + h] = -np.sin(ang)
    wk = np.where((k == 0) | (k == length), 1.0, 2.0) / n
    inv = np.zeros((length, 2 * hp))
    inv[:, :h] = (wk * np.cos(ang[:, :length])).T
    inv[:, hp:hp + h] = (-wk * np.sin(ang[:, :length])).T
    return dict(hp=hp, full=_np_split(fwd), fwd=_np_split(fwd[:, :length]), inv=_np_split(inv))


def _hyena_short_kernel(z_ref, x1_ref, x2_ref, k_ref, inv_ref, skip_ref, ffh_ref, ffl_ref, fh_ref, fl_ref,
                        fih_ref, fil_ref, o_ref, *, hp):
    z = z_ref[0]
    w = z.shape[1]
    for o, gate_ref in enumerate((x1_ref, x2_ref)):
        cols = slice(o * w, (o + 1) * w)
        hf = _cdot(ffh_ref[...], ffl_ref[...], k_ref[:, cols], passes=3) * inv_ref[:, cols]
        x = _cdot(fh_ref[...], fl_ref[...], z, passes=3)
        xr, xi, hr, hi = x[:hp], x[hp:], hf[:hp], hf[hp:]
        zz = jnp.concatenate([xr * hr - xi * hi, xr * hi + xi * hr], axis=0)
        conv = _cdot(fih_ref[...], fil_ref[...], zz, passes=3)
        z = gate_ref[0] * (conv + z * skip_ref[o:o + 1, :])
    o_ref[0] = z


def _hyena_short(z, x1, x2, taps, norms, skip):
    b, l, w = z.shape
    taps = jnp.moveaxis(taps, 0, 1).reshape(2 * l, -1)
    cst = _dense_dft_consts(l)
    consts = [*cst["full"], *cst["fwd"], *cst["inv"]]
    inv_norm = 1.0 / norms
    const = lambda a: pl.BlockSpec(a.shape, lambda bi: (0,) * a.ndim)
    tile = pl.BlockSpec((1, l, w), lambda bi: (bi, 0, 0))
    return pl.pallas_call(
        functools.partial(_hyena_short_kernel, hp=cst["hp"]),
        grid=(b,),
        in_specs=[tile, tile, tile, const(taps), const(inv_norm), const(skip)] + [const(a) for a in consts],
        out_specs=tile,
        out_shape=jax.ShapeDtypeStruct((b, l, w), F32),
        compiler_params=_cparams(("arbitrary",)),
        name="hyena_short",
    )(z, x1, x2, taps, inv_norm, skip, *consts)


def _rope_tables(rows, rot_dim, head_lanes, rope_off, identity_rows):
    half = rot_dim // 2
    n_freq = rot_dim // 4
    pos = np.arange(rows * GRID_W)
    inv_freq = ROPE_THETA ** (-np.arange(n_freq, dtype=np.float64) / n_freq)
    ang = np.concatenate([(pos // GRID_W)[:, None] * inv_freq, (pos % GRID_W)[:, None] * inv_freq], axis=-1)
    cos, sin = np.cos(ang), np.sin(ang)
    n = pos.shape[0]
    ct = np.ones((n, LANE))
    sa = np.zeros((n, LANE))
    sb = np.zeros((n, LANE))
    for h0 in range(0, LANE, head_lanes):
        lo = h0 + rope_off
        ct[:, lo:lo + half] = cos
        ct[:, lo + half:lo + rot_dim] = cos
        sa[:, lo:lo + half] = -sin
        sb[:, lo + half:lo + rot_dim] = sin
    ident = (np.ones((identity_rows, LANE)), np.zeros((identity_rows, LANE)), np.zeros((identity_rows, LANE)))
    lat = tuple(jnp.asarray(t, F32) for t in (ct, sa, sb))
    ctx = tuple(jnp.asarray(t, F32) for t in ident)
    return lat, ctx


def _pack_w_in(w):
    d = w.shape[0]
    o = 0
    mq = w[:, o:o + MLA_HEADS * (MLA_NOPE + MLA_ROPE)]; o += MLA_HEADS * (MLA_NOPE + MLA_ROPE)
    mckv = w[:, o:o + KV_RANK]; o += KV_RANK
    mkr = w[:, o:o + MLA_ROPE]; o += MLA_ROPE
    sq = w[:, o:o + SWA_HEADS * SWA_HEAD_DIM]; o += SWA_HEADS * SWA_HEAD_DIM
    sk = w[:, o:o + SWA_KV_HEADS * SWA_HEAD_DIM]; o += SWA_KV_HEADS * SWA_HEAD_DIM
    sv = w[:, o:o + SWA_KV_HEADS * SWA_HEAD_DIM]; o += SWA_KV_HEADS * SWA_HEAD_DIM
    hy = w[:, o:o + (HY_ORDER + 1) * HY_WIDTH]; o += (HY_ORDER + 1) * HY_WIDTH
    gt = w[:, o:]
    pad_q = HEAD_PAD - MLA_NOPE - MLA_ROPE
    mq = jnp.pad(mq.reshape(d, MLA_HEADS, MLA_NOPE + MLA_ROPE), ((0, 0), (0, 0), (0, pad_q))).reshape(d, -1)
    mkr = jnp.pad(mkr, ((0, 0), (MLA_NOPE, pad_q)))
    dup = lambda t: jnp.repeat(t.reshape(d, SWA_KV_HEADS, 1, SWA_HEAD_DIM), 2, axis=2).reshape(d, -1)
    return jnp.concatenate([mq, mckv, mkr, sq, dup(sk), dup(sv), hy, gt], axis=1).astype(BF16)


def _pack_w_kv(w):
    r = w.shape[0]
    w = w.reshape(r, MLA_HEADS, MLA_NOPE + MLA_V)
    k = jnp.pad(w[..., :MLA_NOPE], ((0, 0), (0, 0), (0, HEAD_PAD - MLA_NOPE))).reshape(r, -1)
    v = w[..., MLA_NOPE:].reshape(r, -1)
    return jnp.concatenate([k, v], axis=1).astype(BF16)


def kernel(x, c, ctx, c_ctx, w_mod, b_mod, norm_g, w_in, kv_norm_g, w_kv_up, swa_sink, hy_conv_w, hy_conv_b,
           hy_w1, hy_b1, hy_w2, hy_b2, hy_freq, hy_w3, hy_skip, w_branch, w_out, w_up, ffn_conv_w, ffn_conv_b,
           w_down):
    b, s, d = x.shape
    n_ctx = ctx.shape[1]
    depth = w_mod.shape[0]
    assert s % (2 * SWA_BLOCK) == 0 and s % GRID_W == 0, "latent length must tile into 128-token blocks / grid rows"
    assert n_ctx % SWA_BLOCK == 0, "context length must be a multiple of the 128-token block"
    assert d % LANE == 0 and w_in.shape[1] == d
    rows = s // GRID_W
    rope_m, rope_m_ctx = _rope_tables(rows, MLA_ROPE, HEAD_PAD, MLA_NOPE, n_ctx)
    rope_s, rope_s_ctx = _rope_tables(rows, SWA_HEAD_DIM, SWA_HEAD_DIM, 0, n_ctx)

    pad_rows = -(b + 1) % SUBLANES
    cvec = jnp.concatenate([c, c_ctx[None, :], jnp.zeros((pad_rows, d), F32)], axis=0)
    mod_all = _modulation(cvec, w_mod, b_mod)

    x_lat, x_ctx = x, ctx
    for l in range(depth):
        with_ctx = l < depth - 1
        m = mod_all[l].reshape(-1, 6, d)
        lat = [m[:b, k][:, None, :] for k in range(6)]
        cx = [jnp.broadcast_to(m[b, k][None, None, :], (b, 1, d)) for k in range(6)]
        ng = [norm_g[l, k][None, :] for k in range(4)]
        w_pack = _pack_w_in(w_in[l])
        wkv_pack = _pack_w_kv(w_kv_up[l])
        kvg = kv_norm_g[l][None, :]
        cw, cb = hy_conv_w[l], hy_conv_b[l][None, :]

        q, kk, vv, sq, sk, sv, z, x1, x2, gt = _in_proj(
            x_lat, lat[0], lat[1], ng[0], w_pack, wkv_pack, kvg, rope_m, rope_s, cw, cb)
        qc, kkc, vvc, sqc, skc, svc, zc, x1c, x2c, gtc = _in_proj(
            x_ctx, cx[0], cx[1], ng[0], w_pack, wkv_pack, kvg, rope_m_ctx, rope_s_ctx, cw, cb)

        y_a = _mla(q, kkc, vvc, kk, vv)
        y_b = _swa(swa_sink[l], sq, skc, svc, sk, sv)
        hy_mlp = (hy_w1[l], hy_b1[l], hy_w2[l], hy_b2[l], hy_freq[l], hy_w3[l])
        y_c = _hyena_long(z, x1, x2, *_hyena_filter_taps(s, *hy_mlp), hy_skip[l])

        wb = w_branch[l].astype(BF16)
        wo = w_out[l].astype(BF16)
        wu = w_up[l].astype(BF16)
        wd = w_down[l].astype(BF16)
        fcw, fcb = ffn_conv_w[l], ffn_conv_b[l][None, :]

        x_lat = _merge(x_lat, y_a, y_b, y_c, gt, wb, wo, ng[1], lat[2])
        x_lat = _ffn(x_lat, lat[3], lat[4], lat[5], ng[2], ng[3], wu, fcw, fcb, wd)

        if with_ctx:
            yc_a = _mla(qc, kkc, vvc)
            yc_b = _swa(swa_sink[l], sqc, skc, svc)
            yc_c = _hyena_short(zc, x1c, x2c, *_hyena_filter_taps(n_ctx, *hy_mlp), hy_skip[l])
            x_ctx = _merge(x_ctx, yc_a, yc_b, yc_c, gtc, wb, wo, ng[1], cx[2])
            x_ctx = _ffn(x_ctx, cx[3], cx[4], cx[5], ng[2], ng[3], wu, fcw, fcb, wd)
    return x_lat
```
